```python
import jax, jax.numpy as jnp
from jax import lax
import numpy as np

D_MODEL = 1024
BATCH = 2
SEQ = 8192
DEPTH = 1
DEC_BATCH = 16
DEC_SEQ = 64
PAST_LEN = 1024

CHUNK = 64
N_HEADS = 8
HEAD_DIM = 64
D_ATTN = N_HEADS * HEAD_DIM
D_RNN = 512
N_RNN_BLOCKS = 8
RNN_BLOCK = D_RNN // N_RNN_BLOCKS
CONV_WIDTH = 4
LRU_C = 8.0
D_FF = 4 * D_MODEL
Q_BLOCK = 128
EPS = 1e-6
SPLITS = (D_ATTN, 2 * D_ATTN, 3 * D_ATTN, 3 * D_ATTN + D_RNN, 3 * D_ATTN + 2 * D_RNN,
          3 * D_ATTN + 2 * D_RNN + D_MODEL)
D_IN = 3 * D_ATTN + 2 * D_RNN + 2 * D_MODEL

kernel_name = "hybrid_stickbreak_rglru_stream_step"


def rms_norm(x, g):
    xf = x.astype(jnp.float32)
    y = xf * lax.rsqrt(jnp.mean(xf * xf, axis=-1, keepdims=True) + EPS)
    return (y * g.astype(jnp.float32)).astype(x.dtype)


def _sb_block(q, k, v, q_pos, k_pos):
    z = jnp.einsum("bqhd,bkhd->bhqk", q, k).astype(jnp.float32) * (HEAD_DIM ** -0.5)
    causal = k_pos[None, :] < q_pos[:, None]
    log_keep = jnp.where(causal, jax.nn.log_sigmoid(-z), 0.0)
    rev = lax.cumsum(log_keep, axis=3, reverse=True)
    after = jnp.concatenate([rev[..., 1:], jnp.zeros_like(rev[..., :1])], axis=-1)
    w = jnp.where(causal, jnp.exp(jax.nn.log_sigmoid(z) + after), 0.0)
    return jnp.einsum("bhqk,bkhd->bqhd", w.astype(v.dtype), v)


def stick_breaking_attention(q, k, v, past_len):
    b, tq = q.shape[0], q.shape[1]
    k_pos = jnp.arange(k.shape[1])
    q_pos = past_len + jnp.arange(tq)
    if tq <= Q_BLOCK:
        return _sb_block(q, k, v, q_pos, k_pos)
    nb = tq // Q_BLOCK
    qb = q.reshape(b, nb, Q_BLOCK, N_HEADS, HEAD_DIM).transpose(1, 0, 2, 3, 4)
    pb = q_pos.reshape(nb, Q_BLOCK)
    out = lax.map(lambda a: _sb_block(a[0], k, v, a[1], k_pos), (qb, pb))
    return out.transpose(1, 0, 2, 3, 4).reshape(b, tq, N_HEADS, HEAD_DIM)


def causal_conv(x, past, w, bias):
    xp = jnp.concatenate([past, x], axis=1)
    t = x.shape[1]
    y = bias + xp[:, 0:t] * w[0]
    for j in range(1, CONV_WIDTH):
        y = y + xp[:, j:j + t] * w[j]
    return y, xp[:, -(CONV_WIDTH - 1):]


def _lin_combine(left, right):
    a1, b1 = left
    a2, b2 = right
    return a1 * a2, a2 * b1 + b2


def rg_lru(x, h0, w_r, b_r, w_i, b_i, lam):
    b, t, _ = x.shape
    xb = x.reshape(b, t, N_RNN_BLOCKS, RNN_BLOCK)
    r = jax.nn.sigmoid(jnp.einsum("btnc,ncd->btnd", xb, w_r).reshape(b, t, D_RNN) + b_r)
    i = jax.nn.sigmoid(jnp.einsum("btnc,ncd->btnd", xb, w_i).reshape(b, t, D_RNN) + b_i)
    log_a = -LRU_C * jax.nn.softplus(-lam.astype(jnp.float32)) * r.astype(jnp.float32)
    a = jnp.exp(log_a)
    u = jnp.sqrt(-jnp.expm1(2.0 * log_a)) * (i * x).astype(jnp.float32)
    a_cum, h = lax.associative_scan(_lin_combine, (a, u), axis=1)
    h = h + a_cum * h0[:, None, :].astype(jnp.float32)
    return h.astype(x.dtype), h[:, -1].astype(x.dtype)


def hybrid_layer(x, k_past, v_past, conv_past, h_past, w_in, g_pre_mix, w_conv, b_conv,
                 w_r, b_r, w_i, b_i, lam, w_a_out, w_b_out, w_o, g_post_mix,
                 g_pre_ffn, w_up, w_down, g_post_ffn):
    b, t, _ = x.shape
    past_len = k_past.shape[1]
    xn = rms_norm(x, g_pre_mix)
    proj = xn @ w_in
    q, k, v, u, g_rnn, g_a, g_b = jnp.split(proj, list(SPLITS), axis=-1)
    q = q.reshape(b, t, N_HEADS, HEAD_DIM)
    k = k.reshape(b, t, N_HEADS, HEAD_DIM)
    v = v.reshape(b, t, N_HEADS, HEAD_DIM)
    k_all = jnp.concatenate([k_past, k], axis=1)
    v_all = jnp.concatenate([v_past, v], axis=1)
    o = stick_breaking_attention(q, k_all, v_all, past_len)
    y_a = o.reshape(b, t, D_ATTN) @ w_a_out
    uc, conv_new = causal_conv(u, conv_past, w_conv, b_conv)
    hseq, h_last = rg_lru(uc, h_past, w_r, b_r, w_i, b_i, lam)
    y_b = (hseq * jax.nn.gelu(g_rnn)) @ w_b_out
    mix = (jax.nn.sigmoid(g_a) * y_a + jax.nn.sigmoid(g_b) * y_b) @ w_o
    x = x + rms_norm(mix, g_post_mix)
    hf = jnp.square(jax.nn.relu(rms_norm(x, g_pre_ffn) @ w_up))
    x = x + rms_norm(hf @ w_down, g_post_ffn)
    return x, k, v, conv_new, h_last


def setup_inputs(seed: int = 0) -> dict:
    key = jax.random.key(seed)
    ks = jax.random.split(key, 32)
    f32 = jnp.float32
    L = DEPTH

    def nrm(k, shape, scale):
        return jax.random.normal(k, shape, f32) * scale

    u = jax.random.uniform(ks[14], (L, D_RNN), f32, 0.9, 0.999)
    s = u ** (1.0 / LRU_C)
    lam = jnp.log(s) - jnp.log1p(-s)
    return {
        "x_prompt": nrm(ks[0], (BATCH, SEQ, D_MODEL), 1.0),
        "x_sample": nrm(ks[1], (DEC_BATCH, DEC_SEQ, D_MODEL), 1.0),
        "cache_k": nrm(ks[2], (L, DEC_BATCH, PAST_LEN, N_HEADS, HEAD_DIM), 1.0),
        "cache_v": nrm(ks[3], (L, DEC_BATCH, PAST_LEN, N_HEADS, HEAD_DIM), 1.0),
        "state_conv": nrm(ks[4], (L, DEC_BATCH, CONV_WIDTH - 1, D_RNN), 1.0),
        "state_h": nrm(ks[5], (L, DEC_BATCH, D_RNN), 0.5),
        "w_in": nrm(ks[6], (L, D_MODEL, D_IN), D_MODEL ** -0.5),
        "g_pre_mix": 1.0 + nrm(ks[7], (L, D_MODEL), 0.01),
        "w_conv": nrm(ks[8], (L, CONV_WIDTH, D_RNN), CONV_WIDTH ** -0.5),
        "b_conv": nrm(ks[9], (L, D_RNN), 0.01),
        "w_r": nrm(ks[10], (L, N_RNN_BLOCKS, RNN_BLOCK, RNN_BLOCK), RNN_BLOCK ** -0.5),
        "b_r": nrm(ks[11], (L, D_RNN), 0.01),
        "w_i": nrm(ks[12], (L, N_RNN_BLOCKS, RNN_BLOCK, RNN_BLOCK), RNN_BLOCK ** -0.5),
        "b_i": nrm(ks[13], (L, D_RNN), 0.01),
        "lam": lam,
        "w_a_out": nrm(ks[15], (L, D_ATTN, D_MODEL), D_ATTN ** -0.5),
        "w_b_out": nrm(ks[16], (L, D_RNN, D_MODEL), D_RNN ** -0.5),
        "w_o": nrm(ks[17], (L, D_MODEL, D_MODEL), D_MODEL ** -0.5),
        "g_post_mix": 1.0 + nrm(ks[18], (L, D_MODEL), 0.01),
        "g_pre_ffn": 1.0 + nrm(ks[19], (L, D_MODEL), 0.01),
        "w_up": nrm(ks[20], (L, D_MODEL, D_FF), D_MODEL ** -0.5),
        "w_down": nrm(ks[21], (L, D_FF, D_MODEL), D_FF ** -0.5),
        "g_post_ffn": 1.0 + nrm(ks[22], (L, D_MODEL), 0.01),
    }


def reference(x_prompt, x_sample, cache_k, cache_v, state_conv, state_h, w_in, g_pre_mix,
              w_conv, b_conv, w_r, b_r, w_i, b_i, lam, w_a_out, w_b_out, w_o, g_post_mix,
              g_pre_ffn, w_up, w_down, g_post_ffn):
    dt = x_prompt.dtype
    kp0 = jnp.zeros((BATCH, 0, N_HEADS, HEAD_DIM), dt)
    cp0 = jnp.zeros((BATCH, CONV_WIDTH - 1, D_RNN), dt)
    hp0 = jnp.zeros((BATCH, D_RNN), dt)
    xp, xs = x_prompt, x_sample
    kps, vps, cps, hps, kss, vss, css, hss = [], [], [], [], [], [], [], []
    for l in range(DEPTH):
        wl = (w_in[l], g_pre_mix[l], w_conv[l], b_conv[l], w_r[l], b_r[l], w_i[l], b_i[l],
              lam[l], w_a_out[l], w_b_out[l], w_o[l], g_post_mix[l], g_pre_ffn[l],
              w_up[l], w_down[l], g_post_ffn[l])
        xp, kp, vp, cp, hp = hybrid_layer(xp, kp0, kp0, cp0, hp0, *wl)
        xs, ksn, vsn, csn, hsn = hybrid_layer(xs, cache_k[l], cache_v[l], state_conv[l],
                                              state_h[l], *wl)
        kps.append(kp); vps.append(vp); cps.append(cp); hps.append(hp)
        kss.append(ksn); vss.append(vsn); css.append(csn); hss.append(hsn)
    return (xp, xs, jnp.stack(kps), jnp.stack(vps), jnp.stack(cps), jnp.stack(hps),
            jnp.stack(kss), jnp.stack(vss), jnp.stack(css), jnp.stack(hss))
```

```python
import functools

import jax
import jax.numpy as jnp
import numpy as np
from jax import lax
from jax.experimental import pallas as pl
from jax.experimental.pallas import tpu as pltpu

F32 = jnp.float32
BF16 = jnp.bfloat16

D_MODEL = 1024
N_HEADS = 8
HEAD_DIM = 64
D_ATTN = N_HEADS * HEAD_DIM
D_RNN = 512
N_RNN_BLOCKS = 8
CONV_WIDTH = 4
LRU_C = 8.0
D_FF = 4 * D_MODEL
EPS = 1e-6

LANES = 128
SUBLANES = 8
KEY_BLOCK = 128
Q_ROWS = 128
FAST_BLOCKS = 3
LOG_KEEP_FLOOR = -104.0
VMEM_LIMIT = 56 * 1024 * 1024


def _rms(xf, g):
    return xf * lax.rsqrt(jnp.mean(xf * xf, axis=-1, keepdims=True) + EPS) * g


def _softplus(x):
    return jnp.maximum(x, 0.0) + jnp.log(1.0 + jnp.exp(-jnp.abs(x)))


def _const_spec(shape):
    nd = len(shape)
    return pl.BlockSpec(shape, lambda *_: (0,) * nd, pipeline_mode=pl.Buffered(1))


def _rows(block_index, size):
    if isinstance(block_index, int):
        return pl.ds(block_index * size, size)
    return pl.ds(pl.multiple_of(block_index * size, size), size)


def _proj_kernel(x_ref, g_ref, w_ref, qb_ref, k_ref, v_ref, kb_ref, vb_ref, u_ref, gr_ref):
    xn = _rms(x_ref[...], g_ref[...]).astype(BF16)

    def proj(c):
        return jnp.dot(xn, w_ref[:, c * D_ATTN:(c + 1) * D_ATTN], preferred_element_type=F32)

    qb_ref[...] = proj(0).astype(BF16)
    k = proj(1)
    k_ref[...] = k
    kb_ref[...] = k.astype(BF16)
    v = proj(2)
    v_ref[...] = v
    vb_ref[...] = v.astype(BF16)
    u_ref[...] = proj(3)
    gr_ref[...] = proj(4)


def _proj(x2d, g, w1, tm):
    n = x2d.shape[0]
    tok = lambda i: (i, 0)
    f32o = jax.ShapeDtypeStruct((n, D_ATTN), F32)
    b16o = jax.ShapeDtypeStruct((n, D_ATTN), BF16)
    ospec = pl.BlockSpec((tm, D_ATTN), tok)
    return pl.pallas_call(
        _proj_kernel,
        grid=(n // tm,),
        in_specs=[pl.BlockSpec((tm, D_MODEL), tok), _const_spec(g.shape), _const_spec(w1.shape)],
        out_specs=[ospec] * 7,
        out_shape=[b16o, f32o, f32o, b16o, b16o, f32o, f32o],
        compiler_params=pltpu.CompilerParams(dimension_semantics=("arbitrary",),
                                             vmem_limit_bytes=VMEM_LIMIT),
        name="proj",
    )(x2d, g, w1)


def _sb_step(qh, kb, vb, carry, mask, tri2):
    z = lax.dot_general(qh, kb, (((1,), (1,)), ((), ())), preferred_element_type=F32)
    sp = _softplus(z)
    if mask is not None:
        sp = jnp.where(mask, sp, 0.0)
    hi = sp.astype(BF16)
    lo = (sp - hi.astype(F32)).astype(BF16)
    r = jnp.dot(jnp.concatenate([hi, lo], axis=1), tri2, preferred_element_type=F32)
    e = jnp.exp(z + r[:, :KEY_BLOCK] + carry)
    if mask is not None:
        e = jnp.where(mask, e, 0.0)
    pv = jnp.dot(e.astype(BF16), vb, preferred_element_type=F32)
    return pv, carry + r[:, KEY_BLOCK:]


def _tri2():
    j = np.arange(KEY_BLOCK)[:, None]
    s = np.arange(KEY_BLOCK)[None, :]
    half = np.concatenate([-(j >= s).astype(np.float32), -np.ones((KEY_BLOCK, LANES), np.float32)], axis=1)
    return jnp.asarray(np.concatenate([half, half], axis=0), dtype=BF16)


def _split_heads(q, lo_half):
    zero = jnp.zeros_like(q)
    return jnp.where(lo_half, q, zero), jnp.where(lo_half, zero, q)


def _walk_back(j0, accs, cs, step_fn):
    def alive(c0, c1):
        return jnp.max(jnp.maximum(c0, c1)) >= LOG_KEEP_FLOOR

    go0 = jnp.logical_and(j0 >= 0, alive(cs[0], cs[1]))

    def body(s):
        j, a0, a1, c0, c1, _ = s
        (p0, c0), (p1, c1) = step_fn(j, c0, c1)
        return j - 1, a0 + p0, a1 + p1, c0, c1, jnp.logical_and(j >= 1, alive(c0, c1))

    _, a0, a1, _, _, _ = lax.while_loop(lambda s: s[-1], body,
                                        (jnp.int32(j0), accs[0], accs[1], cs[0], cs[1], go0))
    return a0, a1


def _attn_prompt_kernel(q_ref, k_ref, v_ref, tri_ref, o_ref):
    t = q_ref.shape[1]
    nq = t // Q_ROWS
    row = lax.broadcasted_iota(jnp.int32, (Q_ROWS, LANES), 0)
    lane = lax.broadcasted_iota(jnp.int32, (Q_ROWS, LANES), 1)
    lo_half = lane < HEAD_DIM
    diag_mask = lane < row
    lane_minus_row = lane - row
    tri2 = tri_ref[...]
    zero = jnp.zeros((Q_ROWS, LANES), F32)

    def kv(j):
        rows = _rows(j, KEY_BLOCK)
        return k_ref[0, rows, :], v_ref[0, rows, :]

    def qblock(i, n_fast):
        rows = _rows(i, Q_ROWS)
        qs = _split_heads(q_ref[0, rows, :], lo_half)
        accs = [zero, zero]
        cs = [zero, zero]
        for d in range(n_fast):
            kb, vb = kv(i - d)
            for h in range(2):
                pv, cs[h] = _sb_step(qs[h], kb, vb, cs[h], diag_mask if d == 0 else None, tri2)
                accs[h] = accs[h] + pv

        def step(j, c0, c1):
            kb, vb = kv(j)
            m = lane_minus_row < (i - j) * KEY_BLOCK
            return (_sb_step(qs[0], kb, vb, c0, m, tri2), _sb_step(qs[1], kb, vb, c1, m, tri2))

        a0, a1 = _walk_back(i - n_fast, accs, cs, step)
        o_ref[0, rows, :] = jnp.where(lo_half, a0, a1).astype(BF16)

    n_head = FAST_BLOCKS - 1
    for i in range(n_head):
        qblock(i, i + 1)

    def group(g, carry):
        qblock(n_head + g, FAST_BLOCKS)
        return carry

    lax.fori_loop(0, nq - n_head, group, 0)


def _attn_prompt(qb, kb, vb, tri2):
    b, t, _ = qb.shape
    spec = pl.BlockSpec((1, t, LANES), lambda i, h: (i, 0, h))
    return pl.pallas_call(
        _attn_prompt_kernel,
        grid=(b, D_ATTN // LANES),
        in_specs=[spec, spec, spec, _const_spec(tri2.shape)],
        out_specs=spec,
        out_shape=jax.ShapeDtypeStruct((b, t, D_ATTN), BF16),
        compiler_params=pltpu.CompilerParams(dimension_semantics=("arbitrary", "arbitrary"),
                                             vmem_limit_bytes=VMEM_LIMIT),
        name="attn_prompt",
    )(qb, kb, vb, tri2)


def _attn_sample_kernel(q_ref, kn_ref, vn_ref, ck_ref, cv_ref, tri_ref, o_ref):
    tq = q_ref.shape[1]
    n_past = ck_ref.shape[1] // KEY_BLOCK
    row = lax.broadcasted_iota(jnp.int32, (tq, LANES), 0)
    lane = lax.broadcasted_iota(jnp.int32, (tq, LANES), 1)
    lo_half = lane < HEAD_DIM
    tri2 = tri_ref[...]
    zero = jnp.zeros((tq, LANES), F32)
    qs = _split_heads(q_ref[0], lo_half)

    def past(j):
        rows = _rows(j, KEY_BLOCK)
        return ck_ref[0, rows, :].astype(BF16), cv_ref[0, rows, :].astype(BF16)

    pad = jnp.zeros((KEY_BLOCK - tq, LANES), BF16)
    blocks = [(jnp.concatenate([kn_ref[0], pad], axis=0), jnp.concatenate([vn_ref[0], pad], axis=0),
               lane < row)]
    n_fast_past = min(FAST_BLOCKS - 1, n_past)
    for d in range(n_fast_past):
        blocks.append(past(n_past - 1 - d) + (None,))

    accs = [zero, zero]
    cs = [zero, zero]
    for kb, vb, m in blocks:
        for h in range(2):
            pv, cs[h] = _sb_step(qs[h], kb, vb, cs[h], m, tri2)
            accs[h] = accs[h] + pv

    def step(j, c0, c1):
        kb, vb = past(j)
        return (_sb_step(qs[0], kb, vb, c0, None, tri2), _sb_step(qs[1], kb, vb, c1, None, tri2))

    a0, a1 = _walk_back(n_past - 1 - n_fast_past, accs, cs, step)
    o_ref[0] = jnp.where(lo_half, a0, a1).astype(BF16)


def _attn_sample(qb, knb, vnb, cache_k, cache_v, tri2):
    s, tq, _ = qb.shape
    past_len = cache_k.shape[1]
    assert past_len % KEY_BLOCK == 0 and tq <= KEY_BLOCK and tq % 16 == 0
    new = pl.BlockSpec((1, tq, LANES), lambda i, h: (i, 0, h))
    old = pl.BlockSpec((1, past_len, LANES), lambda i, h: (i, 0, h))
    return pl.pallas_call(
        _attn_sample_kernel,
        grid=(s, D_ATTN // LANES),
        in_specs=[new, new, new, old, old, _const_spec(tri2.shape)],
        out_specs=new,
        out_shape=jax.ShapeDtypeStruct((s, tq, D_ATTN), BF16),
        compiler_params=pltpu.CompilerParams(dimension_semantics=("arbitrary", "arbitrary"),
                                             vmem_limit_bytes=VMEM_LIMIT),
        name="attn_sample",
    )(qb, knb, vnb, cache_k, cache_v, tri2)


def _gelu_tanh(x):
    return 0.5 * x * (1.0 + jnp.tanh(np.sqrt(2.0 / np.pi).astype(np.float32) * (x + 0.044715 * (x * x * x))))


def _lru_kernel(u_ref, gr_ref, cs_ref, h0_ref, wc_ref, bc_ref, wg_ref, bg_ref, lam_ref,
                hg_ref, cn_ref, hl_ref, ext_ref, a_ref, b_ref, h_ref, hc_ref):
    ti = pl.program_id(1)
    tt = u_ref.shape[1]
    past = CONV_WIDTH - 1

    @pl.when(ti == 0)
    def _():
        ext_ref[0:SUBLANES, :] = jnp.zeros((SUBLANES, D_RNN), F32)
        ext_ref[SUBLANES - past:SUBLANES, :] = cs_ref[0]
        hc_ref[...] = jnp.broadcast_to(h0_ref[0], hc_ref.shape)

    u = u_ref[0]
    ext_ref[SUBLANES:SUBLANES + tt, :] = u
    uc = bc_ref[...] + wc_ref[CONV_WIDTH - 1:CONV_WIDTH, :] * u
    for d in range(1, CONV_WIDTH):
        uc = uc + wc_ref[CONV_WIDTH - 1 - d:CONV_WIDTH - d, :] * ext_ref[SUBLANES - d:SUBLANES - d + tt, :]

    g = jnp.dot(uc.astype(BF16), wg_ref[...], preferred_element_type=F32) + bg_ref[...]
    r = jax.nn.sigmoid(g[:, :D_RNN])
    ig = jax.nn.sigmoid(g[:, D_RNN:])
    log_a = (-LRU_C * _softplus(-lam_ref[...])) * r
    a = jnp.exp(log_a)
    b = jnp.sqrt(-jnp.tanh(log_a) * (a * a + 1.0)) * (ig * uc)

    a3 = a.reshape(tt // SUBLANES, SUBLANES, D_RNN)
    b3 = b.reshape(tt // SUBLANES, SUBLANES, D_RNN)
    sub = lax.broadcasted_iota(jnp.int32, a3.shape, 1)
    k = 1
    while k < SUBLANES:
        has_prev = sub >= k
        b_prev = jnp.where(has_prev, pltpu.roll(b3, k, axis=1), 0.0)
        a_prev = jnp.where(has_prev, pltpu.roll(a3, k, axis=1), 1.0)
        b3 = b3 + a3 * b_prev
        a3 = a3 * a_prev
        k *= 2
    a_ref[...] = a3.reshape(tt, D_RNN)
    b_ref[...] = b3.reshape(tt, D_RNN)

    def group(gi, h_prev):
        rows = _rows(gi, SUBLANES)
        h = b_ref[rows, :] + a_ref[rows, :] * h_prev
        h_ref[rows, :] = h
        return jnp.broadcast_to(h[SUBLANES - 1:SUBLANES, :], h.shape)

    h_last = lax.fori_loop(0, tt // SUBLANES, group, hc_ref[...], unroll=8)
    hc_ref[...] = h_last
    ext_ref[0:SUBLANES, :] = ext_ref[tt:tt + SUBLANES, :]

    hg_ref[0] = (h_ref[...] * _gelu_tanh(gr_ref[0])).astype(BF16)

    @pl.when(ti == pl.num_programs(1) - 1)
    def _():
        cn_ref[0] = ext_ref[SUBLANES - past:SUBLANES, :]
        hl_ref[0] = h_last[0:1, :]


def _lru(u, g_rnn, conv_state, h0, w_conv, b_conv, wg, bg, lam, tt):
    s, t, _ = u.shape
    assert t % tt == 0 and tt % SUBLANES == 0 and tt >= SUBLANES
    past = CONV_WIDTH - 1
    seq = pl.BlockSpec((1, tt, D_RNN), lambda i, j: (i, j, 0))
    st3 = pl.BlockSpec((1, past, D_RNN), lambda i, j: (i, 0, 0))
    st1 = pl.BlockSpec((1, 1, D_RNN), lambda i, j: (i, 0, 0))
    return pl.pallas_call(
        _lru_kernel,
        grid=(s, t // tt),
        in_specs=[seq, seq, st3, st1, _const_spec(w_conv.shape), _const_spec(b_conv.shape),
                  _const_spec(wg.shape), _const_spec(bg.shape), _const_spec(lam.shape)],
        out_specs=[seq, st3, st1],
        out_shape=[jax.ShapeDtypeStruct((s, t, D_RNN), BF16),
                   jax.ShapeDtypeStruct((s, past, D_RNN), F32),
                   jax.ShapeDtypeStruct((s, 1, D_RNN), F32)],
        scratch_shapes=[pltpu.VMEM((tt + SUBLANES, D_RNN), F32), pltpu.VMEM((tt, D_RNN), F32),
                        pltpu.VMEM((tt, D_RNN), F32), pltpu.VMEM((tt, D_RNN), F32),
                        pltpu.VMEM((SUBLANES, D_RNN), F32)],
        compiler_params=pltpu.CompilerParams(dimension_semantics=("arbitrary", "arbitrary"),
                                             vmem_limit_bytes=VMEM_LIMIT),
        name="lru",
    )(u, g_rnn, conv_state, h0, w_conv, b_conv, wg, bg, lam)


FF_CHUNK = 1024


def _out_kernel(x_ref, o_ref, hg_ref, g1_ref, wgab_ref, wa_ref, wb_ref, wo_ref, g2_ref, g3_ref,
                wup_ref, wdn_ref, g4_ref, y_ref):
    x = x_ref[...]
    xn = _rms(x, g1_ref[...]).astype(BF16)
    g_a = jnp.dot(xn, wgab_ref[:, :D_MODEL], preferred_element_type=F32)
    y_a = jnp.dot(o_ref[...], wa_ref[...], preferred_element_type=F32)
    m = jax.nn.sigmoid(g_a) * y_a
    g_b = jnp.dot(xn, wgab_ref[:, D_MODEL:], preferred_element_type=F32)
    y_b = jnp.dot(hg_ref[...], wb_ref[...], preferred_element_type=F32)
    m = m + jax.nn.sigmoid(g_b) * y_b
    mix = jnp.dot(m.astype(BF16), wo_ref[...], preferred_element_type=F32)
    x1 = x + _rms(mix, g2_ref[...])
    f = _rms(x1, g3_ref[...]).astype(BF16)
    acc = None
    for c in range(D_FF // FF_CHUNK):
        cols = slice(c * FF_CHUNK, (c + 1) * FF_CHUNK)
        up = jnp.maximum(jnp.dot(f, wup_ref[:, cols], preferred_element_type=F32), 0.0)
        dn = jnp.dot((up * up).astype(BF16), wdn_ref[cols, :], preferred_element_type=F32)
        acc = dn if acc is None else acc + dn
    y_ref[...] = x1 + _rms(acc, g4_ref[...])


def _out(x2d, o2d, hg2d, g1, wgab, wa, wb, wo, g2, g3, wup, wdn, g4, tm):
    n = x2d.shape[0]
    tok = lambda i: (i, 0)
    consts = (g1, wgab, wa, wb, wo, g2, g3, wup, wdn, g4)
    return pl.pallas_call(
        _out_kernel,
        grid=(n // tm,),
        in_specs=[pl.BlockSpec((tm, D_MODEL), tok), pl.BlockSpec((tm, D_ATTN), tok),
                  pl.BlockSpec((tm, D_RNN), tok)] + [_const_spec(c.shape) for c in consts],
        out_specs=pl.BlockSpec((tm, D_MODEL), tok),
        out_shape=jax.ShapeDtypeStruct((n, D_MODEL), F32),
        compiler_params=pltpu.CompilerParams(dimension_semantics=("arbitrary",),
                                             vmem_limit_bytes=VMEM_LIMIT),
        name="out",
    )(x2d, o2d, hg2d, *consts)


def _block_diag(w):
    n, c, d = w.shape
    return jnp.einsum("ncd,nm->ncmd", w, jnp.eye(n, dtype=w.dtype)).reshape(n * c, n * d)


def _layer(x, cache_k, cache_v, conv_state, h0, wts, tm, tt):
    (w1, g1, w_conv, b_conv, wg, bg, lam, wgab, wa, wb, wo, g2, g3, wup, wdn, g4, tri2) = wts
    s, t, _ = x.shape
    x2d = x.reshape(s * t, D_MODEL)
    qb, k, v, kb, vb, u, g_rnn = _proj(x2d, g1, w1, tm)
    r3 = lambda a: a.reshape(s, t, a.shape[-1])
    if cache_k is None:
        o = _attn_prompt(r3(qb), r3(kb), r3(vb), tri2)
    else:
        o = _attn_sample(r3(qb), r3(kb), r3(vb), cache_k, cache_v, tri2)
    hg, conv_new, h_last = _lru(r3(u), r3(g_rnn), conv_state, h0, w_conv, b_conv, wg, bg, lam, tt)
    y = _out(x2d, o.reshape(s * t, D_ATTN), hg.reshape(s * t, D_RNN), g1, wgab, wa, wb, wo, g2, g3,
             wup, wdn, g4, tm)
    return (y.reshape(s, t, D_MODEL), k.reshape(1, s, t, N_HEADS, HEAD_DIM),
            v.reshape(1, s, t, N_HEADS, HEAD_DIM), conv_new[None], h_last.reshape(1, s, D_RNN))


def kernel(x_prompt, x_sample, cache_k, cache_v, state_conv, state_h, w_in, g_pre_mix, w_conv, b_conv, w_r, b_r, w_i, b_i, lam, w_a_out, w_b_out, w_o, g_post_mix, g_pre_ffn, w_up, w_down, g_post_ffn):
    assert w_in.shape[0] == 1
    row = lambda a: a[0].reshape(1, -1)
    w_in0 = w_in[0]
    n_seq = 3 * D_ATTN + 2 * D_RNN
    w1 = jnp.concatenate([w_in0[:, :D_ATTN] * (HEAD_DIM ** -0.5), w_in0[:, D_ATTN:n_seq]], axis=1).astype(BF16)
    wgab = w_in0[:, n_seq:].astype(BF16)
    wg = jnp.concatenate([_block_diag(w_r[0]), _block_diag(w_i[0])], axis=1).astype(BF16)
    bg = jnp.concatenate([row(b_r), row(b_i)], axis=1)
    wts = (w1, row(g_pre_mix), w_conv[0], row(b_conv), wg, bg, row(lam), wgab,
           w_a_out[0].astype(BF16), w_b_out[0].astype(BF16), w_o[0].astype(BF16), row(g_post_mix),
           row(g_pre_ffn), w_up[0].astype(BF16), w_down[0].astype(BF16), row(g_post_ffn), _tri2())

    bp = x_prompt.shape[0]
    zc = jnp.zeros((bp, CONV_WIDTH - 1, D_RNN), F32)
    zh = jnp.zeros((bp, 1, D_RNN), F32)
    yp, kp, vp, cp, hp = _layer(x_prompt, None, None, zc, zh, wts, tm=512, tt=512)

    bs, ts, _ = x_sample.shape
    past_len = cache_k.shape[2]
    ys, ks, vs, cs, hs = _layer(x_sample, cache_k[0].reshape(bs, past_len, D_ATTN),
                                cache_v[0].reshape(bs, past_len, D_ATTN), state_conv[0],
                                state_h[0].reshape(bs, 1, D_RNN), wts, tm=512, tt=ts)
    return (yp, ys, kp, vp, cp, hp, ks, vs, cs, hs)
```

```python
import functools

import jax
import jax.numpy as jnp
import numpy as np
from jax import lax
from jax.experimental import pallas as pl
from jax.experimental.pallas import tpu as pltpu

F32 = jnp.float32
BF16 = jnp.bfloat16

D_MODEL = 1024
N_HEADS = 8
HEAD_DIM = 64
D_ATTN = N_HEADS * HEAD_DIM
D_RNN = 512
N_RNN_BLOCKS = 8
CONV_WIDTH = 4
LRU_C = 8.0
D_FF = 4 * D_MODEL
EPS = 1e-6

LANES = 128
SUBLANES = 8
KEY_BLOCK = 128
Q_ROWS = 128
FAST_BLOCKS = 3
Q_GROUP = 4
LOG_KEEP_FLOOR = -104.0
VMEM_LIMIT = 56 * 1024 * 1024


def _rms(xf, g):
    return xf * lax.rsqrt(jnp.mean(xf * xf, axis=-1, keepdims=True) + EPS) * g


def _softplus(x):
    return jnp.maximum(x, 0.0) + jnp.log(1.0 + jnp.exp(-jnp.abs(x)))


def _const_spec(shape):
    nd = len(shape)
    return pl.BlockSpec(shape, lambda *_: (0,) * nd, pipeline_mode=pl.Buffered(1))


def _rows(block_index, size):
    if isinstance(block_index, int):
        return pl.ds(block_index * size, size)
    return pl.ds(pl.multiple_of(block_index * size, size), size)


def _proj_kernel(x_ref, g_ref, w_ref, qb_ref, k_ref, v_ref, kb_ref, vb_ref, u_ref, gr_ref):
    xn = _rms(x_ref[...], g_ref[...]).astype(BF16)

    def proj(c):
        return jnp.dot(xn, w_ref[:, c * D_ATTN:(c + 1) * D_ATTN], preferred_element_type=F32)

    qb_ref[...] = proj(0).astype(BF16)
    k = proj(1)
    k_ref[...] = k
    kb_ref[...] = k.astype(BF16)
    v = proj(2)
    v_ref[...] = v
    vb_ref[...] = v.astype(BF16)
    u_ref[...] = proj(3)
    gr_ref[...] = proj(4)


def _proj(x2d, g, w1, tm):
    n = x2d.shape[0]
    tok = lambda i: (i, 0)
    f32o = jax.ShapeDtypeStruct((n, D_ATTN), F32)
    b16o = jax.ShapeDtypeStruct((n, D_ATTN), BF16)
    ospec = pl.BlockSpec((tm, D_ATTN), tok)
    return pl.pallas_call(
        _proj_kernel,
        grid=(n // tm,),
        in_specs=[pl.BlockSpec((tm, D_MODEL), tok), _const_spec(g.shape), _const_spec(w1.shape)],
        out_specs=[ospec] * 7,
        out_shape=[b16o, f32o, f32o, b16o, b16o, f32o, f32o],
        compiler_params=pltpu.CompilerParams(dimension_semantics=("arbitrary",),
                                             vmem_limit_bytes=VMEM_LIMIT),
        name="proj",
    )(x2d, g, w1)


def _sb_scores(q2, kwin, mask):
    n = kwin.shape[0] // KEY_BLOCK
    tq = q2.shape[0] // 2
    z = lax.dot_general(q2, kwin, (((1,), (1,)), ((), ())), preferred_element_type=F32)
    sp = _softplus(z)
    cats = []
    for h in range(2):
        for d in range(n):
            blk = sp[h * tq:(h + 1) * tq, d * KEY_BLOCK:(d + 1) * KEY_BLOCK]
            if d == n - 1 and mask is not None:
                blk = jnp.where(mask, blk, 0.0)
            hi = blk.astype(BF16)
            lo = (blk - hi.astype(F32)).astype(BF16)
            cats.append(jnp.concatenate([hi, lo], axis=1))
    return z, jnp.concatenate(cats, axis=0)


def _sb_cumsum(cat, tri2):
    return jnp.dot(cat, tri2, preferred_element_type=F32)


def _sb_weights(z, r, carries, mask):
    n = z.shape[1] // KEY_BLOCK
    tq = z.shape[0] // 2
    rows, new_carries = [], []
    for h in range(2):
        carry = None if carries is None else carries[h]
        es = [None] * n
        for d in reversed(range(n)):
            rd = r[(h * n + d) * tq:(h * n + d + 1) * tq]
            x = z[h * tq:(h + 1) * tq, d * KEY_BLOCK:(d + 1) * KEY_BLOCK] + rd[:, :KEY_BLOCK]
            if carry is not None:
                x = x + carry
            e = jnp.exp(x)
            if d == n - 1 and mask is not None:
                e = jnp.where(mask, e, 0.0)
            es[d] = e.astype(BF16)
            tot = rd[:, KEY_BLOCK:]
            carry = tot if carry is None else carry + tot
        rows.append(es[0] if n == 1 else jnp.concatenate(es, axis=1))
        new_carries.append(carry)
    return jnp.concatenate(rows, axis=0), new_carries


def _sb_pv(e, vwin):
    tq = e.shape[0] // 2
    pv = jnp.dot(e, vwin, preferred_element_type=F32)
    return pv[:tq], pv[tq:]


def _sb_window(q2, kwin, vwin, carries, mask, tri2):
    z, cat = _sb_scores(q2, kwin, mask)
    e, cs = _sb_weights(z, _sb_cumsum(cat, tri2), carries, mask)
    a0, a1 = _sb_pv(e, vwin)
    return (a0, cs[0]), (a1, cs[1])


def _tri2():
    j = np.arange(KEY_BLOCK)[:, None]
    s = np.arange(KEY_BLOCK)[None, :]
    half = np.concatenate([-(j >= s).astype(np.float32), -np.ones((KEY_BLOCK, LANES), np.float32)], axis=1)
    return jnp.asarray(np.concatenate([half, half], axis=0), dtype=BF16)


def _stack_heads(q, lo_half):
    zero = jnp.zeros_like(q)
    return jnp.concatenate([jnp.where(lo_half, q, zero), jnp.where(lo_half, zero, q)], axis=0)


def _finish(store, j0, outs, step_fn):
    (a0, c0), (a1, c1) = outs
    store(a0, a1)
    if isinstance(j0, int) and j0 < 0:
        return lambda: None

    def alive(x0, x1):
        return jnp.max(jnp.maximum(x0, x1)) >= LOG_KEEP_FLOOR

    go = jnp.logical_and(j0 >= 0, alive(c0, c1))

    def walk():
        @pl.when(go)
        def _():
            def body(s):
                j, b0, b1, d0, d1, _ = s
                (p0, d0), (p1, d1) = step_fn(j, [d0, d1])
                return j - 1, b0 + p0, b1 + p1, d0, d1, jnp.logical_and(j >= 1, alive(d0, d1))

            _, b0, b1, _, _, _ = lax.while_loop(lambda s: s[-1], body,
                                                (jnp.int32(j0), a0, a1, c0, c1, jnp.bool_(True)))
            store(b0, b1)

    return walk


def _sb_chains(chains, tri2):
    scores = [_sb_scores(q2, kwin, mask) for q2, kwin, _, mask, _, _, _ in chains]
    sums = [_sb_cumsum(cat, tri2) for _, cat in scores]
    weights = [_sb_weights(z, r, None, c[3]) for (z, _), r, c in zip(scores, sums, chains)]
    pvs = [_sb_pv(e, c[2]) for (e, _), c in zip(weights, chains)]
    walks = [_finish(c[4], c[6], ((a0, cs[0]), (a1, cs[1])), c[5])
             for c, (a0, a1), (_, cs) in zip(chains, pvs, weights)]
    for walk in walks:
        walk()


def _attn_prompt_kernel(q_ref, k_ref, v_ref, tri_ref, o_ref):
    t = q_ref.shape[1]
    nq = t // Q_ROWS
    row = lax.broadcasted_iota(jnp.int32, (Q_ROWS, LANES), 0)
    lane = lax.broadcasted_iota(jnp.int32, (Q_ROWS, LANES), 1)
    lo_half = lane < HEAD_DIM
    diag_mask = lane < row
    lane_minus_row = lane - row
    tri2 = tri_ref[...]

    def qblocks(blocks, n_fast):
        chains = []
        for i in blocks:
            q2 = _stack_heads(q_ref[0, _rows(i, Q_ROWS), :], lo_half)
            if isinstance(i, int):
                win = pl.ds((i + 1 - n_fast) * KEY_BLOCK, n_fast * KEY_BLOCK)
            else:
                win = pl.ds(pl.multiple_of((i + 1 - n_fast) * KEY_BLOCK, KEY_BLOCK), n_fast * KEY_BLOCK)

            def store(b0, b1, i=i):
                o_ref[0, _rows(i, Q_ROWS), :] = jnp.where(lo_half, b0, b1).astype(BF16)

            def step(j, cs, i=i, q2=q2):
                blk = _rows(j, KEY_BLOCK)
                m = lane_minus_row < (i - j) * KEY_BLOCK
                return _sb_window(q2, k_ref[0, blk, :], v_ref[0, blk, :], cs, m, tri2)

            chains.append((q2, k_ref[0, win, :], v_ref[0, win, :], diag_mask, store, step, i - n_fast))
        _sb_chains(chains, tri2)

    n_head = FAST_BLOCKS - 1
    for i in range(n_head):
        qblocks([i], i + 1)

    start = n_head + (nq - n_head) % Q_GROUP
    if start > n_head:
        qblocks(list(range(n_head, start)), FAST_BLOCKS)

    def group(g, carry):
        qblocks([start + g * Q_GROUP + u for u in range(Q_GROUP)], FAST_BLOCKS)
        return carry

    lax.fori_loop(0, (nq - start) // Q_GROUP, group, 0)


def _attn_prompt(qb, kb, vb, tri2):
    b, t, _ = qb.shape
    spec = pl.BlockSpec((1, t, LANES), lambda i, h: (i, 0, h))
    return pl.pallas_call(
        _attn_prompt_kernel,
        grid=(b, D_ATTN // LANES),
        in_specs=[spec, spec, spec, _const_spec(tri2.shape)],
        out_specs=spec,
        out_shape=jax.ShapeDtypeStruct((b, t, D_ATTN), BF16),
        compiler_params=pltpu.CompilerParams(dimension_semantics=("arbitrary", "arbitrary"),
                                             vmem_limit_bytes=VMEM_LIMIT),
        name="attn_prompt",
    )(qb, kb, vb, tri2)


def _attn_sample_kernel(q_ref, kn_ref, vn_ref, ck_ref, cv_ref, tri_ref, o_ref):
    tq = q_ref.shape[1]
    n_past = ck_ref.shape[1] // KEY_BLOCK
    row = lax.broadcasted_iota(jnp.int32, (tq, LANES), 0)
    lane = lax.broadcasted_iota(jnp.int32, (tq, LANES), 1)
    lo_half = lane < HEAD_DIM
    tri2 = tri_ref[...]

    n_fast_past = min(FAST_BLOCKS - 1, n_past)
    tail = pl.ds((n_past - n_fast_past) * KEY_BLOCK, n_fast_past * KEY_BLOCK)
    pad = jnp.zeros((KEY_BLOCK - tq, LANES), BF16)
    chains = []
    for p in range(q_ref.shape[2] // LANES):
        cols = slice(p * LANES, (p + 1) * LANES)
        q2 = _stack_heads(q_ref[0, :, cols], lo_half)
        kwin = jnp.concatenate([ck_ref[0, tail, cols].astype(BF16), kn_ref[0, :, cols], pad], axis=0)
        vwin = jnp.concatenate([cv_ref[0, tail, cols].astype(BF16), vn_ref[0, :, cols], pad], axis=0)

        def store(a0, a1, cols=cols):
            o_ref[0, :, cols] = jnp.where(lo_half, a0, a1).astype(BF16)

        def step(j, cs, cols=cols, q2=q2):
            blk = _rows(j, KEY_BLOCK)
            return _sb_window(q2, ck_ref[0, blk, cols].astype(BF16), cv_ref[0, blk, cols].astype(BF16),
                              cs, None, tri2)

        chains.append((q2, kwin, vwin, lane < row, store, step, n_past - 1 - n_fast_past))
    _sb_chains(chains, tri2)


def _attn_sample(qb, knb, vnb, cache_k, cache_v, tri2):
    s, tq, _ = qb.shape
    past_len = cache_k.shape[1]
    assert past_len % KEY_BLOCK == 0 and tq <= KEY_BLOCK and tq % 16 == 0
    new = pl.BlockSpec((1, tq, D_ATTN), lambda i: (i, 0, 0))
    old = pl.BlockSpec((1, past_len, D_ATTN), lambda i: (i, 0, 0))
    return pl.pallas_call(
        _attn_sample_kernel,
        grid=(s,),
        in_specs=[new, new, new, old, old, _const_spec(tri2.shape)],
        out_specs=new,
        out_shape=jax.ShapeDtypeStruct((s, tq, D_ATTN), BF16),
        compiler_params=pltpu.CompilerParams(dimension_semantics=("arbitrary",),
                                             vmem_limit_bytes=VMEM_LIMIT),
        name="attn_sample",
    )(qb, knb, vnb, cache_k, cache_v, tri2)


def _gelu_tanh(x):
    return 0.5 * x * (1.0 + jnp.tanh(np.sqrt(2.0 / np.pi).astype(np.float32) * (x + 0.044715 * (x * x * x))))


def _lru_kernel(u_ref, gr_ref, cs_ref, h0_ref, wc_ref, bc_ref, wg_ref, bg_ref, lam_ref,
                hg_ref, cn_ref, hl_ref, ext_ref, a_ref, b_ref, h_ref, hc_ref):
    ti = pl.program_id(1)
    tt = u_ref.shape[1]
    past = CONV_WIDTH - 1

    @pl.when(ti == 0)
    def _():
        ext_ref[0:SUBLANES, :] = jnp.zeros((SUBLANES, D_RNN), F32)
        ext_ref[SUBLANES - past:SUBLANES, :] = cs_ref[0]
        hc_ref[...] = jnp.broadcast_to(h0_ref[0], hc_ref.shape)

    u = u_ref[0]
    ext_ref[SUBLANES:SUBLANES + tt, :] = u
    uc = bc_ref[...] + wc_ref[CONV_WIDTH - 1:CONV_WIDTH, :] * u
    for d in range(1, CONV_WIDTH):
        uc = uc + wc_ref[CONV_WIDTH - 1 - d:CONV_WIDTH - d, :] * ext_ref[SUBLANES - d:SUBLANES - d + tt, :]

    g = jnp.dot(uc.astype(BF16), wg_ref[...], preferred_element_type=F32) + bg_ref[...]
    r = jax.nn.sigmoid(g[:, :D_RNN])
    ig = jax.nn.sigmoid(g[:, D_RNN:])
    log_a = (-LRU_C * _softplus(-lam_ref[...])) * r
    a = jnp.exp(log_a)
    b = jnp.sqrt(-jnp.tanh(log_a) * (a * a + 1.0)) * (ig * uc)

    a3 = a.reshape(tt // SUBLANES, SUBLANES, D_RNN)
    b3 = b.reshape(tt // SUBLANES, SUBLANES, D_RNN)
    sub = lax.broadcasted_iota(jnp.int32, a3.shape, 1)
    k = 1
    while k < SUBLANES:
        has_prev = sub >= k
        b_prev = jnp.where(has_prev, pltpu.roll(b3, k, axis=1), 0.0)
        a_prev = jnp.where(has_prev, pltpu.roll(a3, k, axis=1), 1.0)
        b3 = b3 + a3 * b_prev
        a3 = a3 * a_prev
        k *= 2
    a_ref[...] = a3.reshape(tt, D_RNN)
    b_ref[...] = b3.reshape(tt, D_RNN)

    def group(gi, h_prev):
        rows = _rows(gi, SUBLANES)
        h = b_ref[rows, :] + a_ref[rows, :] * h_prev
        h_ref[rows, :] = h
        return jnp.broadcast_to(h[SUBLANES - 1:SUBLANES, :], h.shape)

    h_last = lax.fori_loop(0, tt // SUBLANES, group, hc_ref[...], unroll=8)
    hc_ref[...] = h_last
    ext_ref[0:SUBLANES, :] = ext_ref[tt:tt + SUBLANES, :]

    hg_ref[0] = (h_ref[...] * _gelu_tanh(gr_ref[0])).astype(BF16)

    @pl.when(ti == pl.num_programs(1) - 1)
    def _():
        cn_ref[0] = ext_ref[SUBLANES - past:SUBLANES, :]
        hl_ref[0] = h_last[0:1, :]


def _lru(u, g_rnn, conv_state, h0, w_conv, b_conv, wg, bg, lam, tt):
    s, t, _ = u.shape
    assert t % tt == 0 and tt % SUBLANES == 0 and tt >= SUBLANES
    past = CONV_WIDTH - 1
    seq = pl.BlockSpec((1, tt, D_RNN), lambda i, j: (i, j, 0))
    st3 = pl.BlockSpec((1, past, D_RNN), lambda i, j: (i, 0, 0))
    st1 = pl.BlockSpec((1, 1, D_RNN), lambda i, j: (i, 0, 0))
    return pl.pallas_call(
        _lru_kernel,
        grid=(s, t // tt),
        in_specs=[seq, seq, st3, st1, _const_spec(w_conv.shape), _const_spec(b_conv.shape),
                  _const_spec(wg.shape), _const_spec(bg.shape), _const_spec(lam.shape)],
        out_specs=[seq, st3, st1],
        out_shape=[jax.ShapeDtypeStruct((s, t, D_RNN), BF16),
                   jax.ShapeDtypeStruct((s, past, D_RNN), F32),
                   jax.ShapeDtypeStruct((s, 1, D_RNN), F32)],
        scratch_shapes=[pltpu.VMEM((tt + SUBLANES, D_RNN), F32), pltpu.VMEM((tt, D_RNN), F32),
                        pltpu.VMEM((tt, D_RNN), F32), pltpu.VMEM((tt, D_RNN), F32),
                        pltpu.VMEM((SUBLANES, D_RNN), F32)],
        compiler_params=pltpu.CompilerParams(dimension_semantics=("arbitrary", "arbitrary"),
                                             vmem_limit_bytes=VMEM_LIMIT),
        name="lru",
    )(u, g_rnn, conv_state, h0, w_conv, b_conv, wg, bg, lam)


FF_CHUNK = 1024


def _out_kernel(x_ref, o_ref, hg_ref, g1_ref, wgab_ref, wa_ref, wb_ref, wo_ref, g2_ref, g3_ref,
                wup_ref, wdn_ref, g4_ref, y_ref):
    x = x_ref[...]
    xn = _rms(x, g1_ref[...]).astype(BF16)
    g_a = jnp.dot(xn, wgab_ref[:, :D_MODEL], preferred_element_type=F32)
    y_a = jnp.dot(o_ref[...], wa_ref[...], preferred_element_type=F32)
    m = jax.nn.sigmoid(g_a) * y_a
    g_b = jnp.dot(xn, wgab_ref[:, D_MODEL:], preferred_element_type=F32)
    y_b = jnp.dot(hg_ref[...], wb_ref[...], preferred_element_type=F32)
    m = m + jax.nn.sigmoid(g_b) * y_b
    mix = jnp.dot(m.astype(BF16), wo_ref[...], preferred_element_type=F32)
    x1 = x + _rms(mix, g2_ref[...])
    f = _rms(x1, g3_ref[...]).astype(BF16)
    acc = None
    for c in range(D_FF // FF_CHUNK):
        cols = slice(c * FF_CHUNK, (c + 1) * FF_CHUNK)
        up = jnp.maximum(jnp.dot(f, wup_ref[:, cols], preferred_element_type=F32), 0.0)
        dn = jnp.dot((up * up).astype(BF16), wdn_ref[cols, :], preferred_element_type=F32)
        acc = dn if acc is None else acc + dn
    y_ref[...] = x1 + _rms(acc, g4_ref[...])


def _out(x2d, o2d, hg2d, g1, wgab, wa, wb, wo, g2, g3, wup, wdn, g4, tm):
    n = x2d.shape[0]
    tok = lambda i: (i, 0)
    consts = (g1, wgab, wa, wb, wo, g2, g3, wup, wdn, g4)
    return pl.pallas_call(
        _out_kernel,
        grid=(n // tm,),
        in_specs=[pl.BlockSpec((tm, D_MODEL), tok), pl.BlockSpec((tm, D_ATTN), tok),
                  pl.BlockSpec((tm, D_RNN), tok)] + [_const_spec(c.shape) for c in consts],
        out_specs=pl.BlockSpec((tm, D_MODEL), tok),
        out_shape=jax.ShapeDtypeStruct((n, D_MODEL), F32),
        compiler_params=pltpu.CompilerParams(dimension_semantics=("arbitrary",),
                                             vmem_limit_bytes=VMEM_LIMIT),
        name="out",
    )(x2d, o2d, hg2d, *consts)


def _block_diag(w):
    n, c, d = w.shape
    return jnp.einsum("ncd,nm->ncmd", w, jnp.eye(n, dtype=w.dtype)).reshape(n * c, n * d)


def _layer(x, cache_k, cache_v, conv_state, h0, wts, tm, tt):
    (w1, g1, w_conv, b_conv, wg, bg, lam, wgab, wa, wb, wo, g2, g3, wup, wdn, g4, tri2) = wts
    s, t, _ = x.shape
    x2d = x.reshape(s * t, D_MODEL)
    qb, k, v, kb, vb, u, g_rnn = _proj(x2d, g1, w1, tm)
    r3 = lambda a: a.reshape(s, t, a.shape[-1])
    if cache_k is None:
        o = _attn_prompt(r3(qb), r3(kb), r3(vb), tri2)
    else:
        o = _attn_sample(r3(qb), r3(kb), r3(vb), cache_k, cache_v, tri2)
    hg, conv_new, h_last = _lru(r3(u), r3(g_rnn), conv_state, h0, w_conv, b_conv, wg, bg, lam, tt)
    y = _out(x2d, o.reshape(s * t, D_ATTN), hg.reshape(s * t, D_RNN), g1, wgab, wa, wb, wo, g2, g3,
             wup, wdn, g4, tm)
    return (y.reshape(s, t, D_MODEL), k.reshape(1, s, t, N_HEADS, HEAD_DIM),
            v.reshape(1, s, t, N_HEADS, HEAD_DIM), conv_new[None], h_last.reshape(1, s, D_RNN))


def kernel(x_prompt, x_sample, cache_k, cache_v, state_conv, state_h, w_in, g_pre_mix, w_conv, b_conv, w_r, b_r, w_i, b_i, lam, w_a_out, w_b_out, w_o, g_post_mix, g_pre_ffn, w_up, w_down, g_post_ffn):
    assert w_in.shape[0] == 1
    row = lambda a: a[0].reshape(1, -1)
    w_in0 = w_in[0]
    n_seq = 3 * D_ATTN + 2 * D_RNN
    w1 = jnp.concatenate([w_in0[:, :D_ATTN] * (HEAD_DIM ** -0.5), w_in0[:, D_ATTN:n_seq]], axis=1).astype(BF16)
    wgab = w_in0[:, n_seq:].astype(BF16)
    wg = jnp.concatenate([_block_diag(w_r[0]), _block_diag(w_i[0])], axis=1).astype(BF16)
    bg = jnp.concatenate([row(b_r), row(b_i)], axis=1)
    wts = (w1, row(g_pre_mix), w_conv[0], row(b_conv), wg, bg, row(lam), wgab,
           w_a_out[0].astype(BF16), w_b_out[0].astype(BF16), w_o[0].astype(BF16), row(g_post_mix),
           row(g_pre_ffn), w_up[0].astype(BF16), w_down[0].astype(BF16), row(g_post_ffn), _tri2())

    bp = x_prompt.shape[0]
    zc = jnp.zeros((bp, CONV_WIDTH - 1, D_RNN), F32)
    zh = jnp.zeros((bp, 1, D_RNN), F32)
    yp, kp, vp, cp, hp = _layer(x_prompt, None, None, zc, zh, wts, tm=512, tt=512)

    bs, ts, _ = x_sample.shape
    past_len = cache_k.shape[2]
    ys, ks, vs, cs, hs = _layer(x_sample, cache_k[0].reshape(bs, past_len, D_ATTN),
                                cache_v[0].reshape(bs, past_len, D_ATTN), state_conv[0],
                                state_h[0].reshape(bs, 1, D_RNN), wts, tm=512, tt=ts)
    return (yp, ys, kp, vp, cp, hp, ks, vs, cs, hs)
```

```python
import functools

import jax
import jax.numpy as jnp
import numpy as np
from jax import lax
from jax.experimental import pallas as pl
from jax.experimental.pallas import tpu as pltpu

F32 = jnp.float32
BF16 = jnp.bfloat16

D_MODEL = 1024
N_HEADS = 8
HEAD_DIM = 64
D_ATTN = N_HEADS * HEAD_DIM
D_RNN = 512
N_RNN_BLOCKS = 8
CONV_WIDTH = 4
LRU_C = 8.0
D_FF = 4 * D_MODEL
EPS = 1e-6

LANES = 128
SUBLANES = 8
KEY_BLOCK = 128
Q_ROWS = 128
FAST_BLOCKS = 3
Q_GROUP = 4
TOKEN_TILE = 512
LOG_KEEP_FLOOR = -104.0
VMEM_LIMIT = 56 * 1024 * 1024

NT_DIMS = (((1,), (1,)), ((), ()))


def _rms(xf, g):
    return xf * lax.rsqrt(jnp.mean(xf * xf, axis=-1, keepdims=True) + EPS) * g


def _softplus(x):
    return jnp.maximum(x, 0.0) + jnp.log(1.0 + jnp.exp(-jnp.abs(x)))


def _sigmoid(x):
    return 0.5 * jnp.tanh(0.5 * x) + 0.5


def _const_spec(shape):
    nd = len(shape)
    return pl.BlockSpec(shape, lambda *_: (0,) * nd, pipeline_mode=pl.Buffered(1))


def _rows(block_index, size):
    if isinstance(block_index, int):
        return pl.ds(block_index * size, size)
    return pl.ds(pl.multiple_of(block_index * size, size), size)


def _proj_kernel(transposed, x_ref, g_ref, wq_ref, wug_ref, wkv_ref,
                 qb_ref, k_ref, v_ref, kb_ref, vb_ref, u_ref, gr_ref):
    xn = _rms(x_ref[...], g_ref[...]).astype(BF16)
    qb_ref[...] = jnp.dot(xn, wq_ref[...], preferred_element_type=F32).astype(BF16)
    ug = jnp.dot(xn, wug_ref[...], preferred_element_type=F32)
    u_ref[...] = ug[:, :D_RNN]
    gr_ref[...] = ug[:, D_RNN:]
    if transposed:
        kv = lax.dot_general(wkv_ref[...], xn, NT_DIMS, preferred_element_type=F32)
        k, v = kv[:D_ATTN], kv[D_ATTN:]
        k_ref[0] = k
        v_ref[0] = v
        for j in range(kb_ref.shape[1]):
            cols = slice(j * KEY_BLOCK, (j + 1) * KEY_BLOCK)
            kb_ref[0, j] = k[:, cols].astype(BF16)
            vb_ref[0, j] = v[:, cols].astype(BF16)
    else:
        kv = jnp.dot(xn, wkv_ref[...], preferred_element_type=F32)
        k, v = kv[:, :D_ATTN], kv[:, D_ATTN:]
        k_ref[...] = k
        v_ref[...] = v
        kb_ref[...] = k.astype(BF16)
        vb_ref[...] = v.astype(BF16)


def _proj(x, g, wq, wug, wkv, transposed):
    s, t, _ = x.shape
    n = s * t
    tm = TOKEN_TILE
    assert n % tm == 0
    tok = lambda i: (i, 0)
    row_out = pl.BlockSpec((tm, D_ATTN), tok)
    f32o = jax.ShapeDtypeStruct((n, D_ATTN), F32)
    b16o = jax.ShapeDtypeStruct((n, D_ATTN), BF16)
    if transposed:
        assert t % tm == 0 and tm % KEY_BLOCK == 0
        per = t // tm
        kv_spec = pl.BlockSpec((1, D_ATTN, tm), lambda i: (i // per, 0, i % per))
        kvb_spec = pl.BlockSpec((1, tm // KEY_BLOCK, D_ATTN, KEY_BLOCK), lambda i: (i // per, i % per, 0, 0))
        kv_shape = jax.ShapeDtypeStruct((s, D_ATTN, t), F32)
        kvb_shape = jax.ShapeDtypeStruct((s, t // KEY_BLOCK, D_ATTN, KEY_BLOCK), BF16)
    else:
        kv_spec = kvb_spec = row_out
        kv_shape, kvb_shape = f32o, b16o
    return pl.pallas_call(
        functools.partial(_proj_kernel, transposed),
        grid=(n // tm,),
        in_specs=[pl.BlockSpec((tm, D_MODEL), tok), _const_spec(g.shape), _const_spec(wq.shape),
                  _const_spec(wug.shape), _const_spec(wkv.shape)],
        out_specs=[row_out, kv_spec, kv_spec, kvb_spec, kvb_spec, row_out, row_out],
        out_shape=[b16o, kv_shape, kv_shape, kvb_shape, kvb_shape, f32o, f32o],
        compiler_params=pltpu.CompilerParams(dimension_semantics=("arbitrary",),
                                             vmem_limit_bytes=VMEM_LIMIT),
        name="proj",
    )(x.reshape(n, D_MODEL), g, wq, wug, wkv)


def _sb_split(z, mask):
    n = z.shape[1] // KEY_BLOCK
    tq = z.shape[0] // 2
    sp = _softplus(z)
    cats = []
    for h in range(2):
        for d in range(n):
            blk = sp[h * tq:(h + 1) * tq, d * KEY_BLOCK:(d + 1) * KEY_BLOCK]
            if d == n - 1 and mask is not None:
                blk = jnp.where(mask, blk, 0.0)
            hi = blk.astype(BF16)
            lo = (blk - hi.astype(F32)).astype(BF16)
            cats.append(jnp.concatenate([hi, lo], axis=1))
    return jnp.concatenate(cats, axis=0)


def _sb_cumsum(cat, tri2):
    return jnp.dot(cat, tri2, preferred_element_type=F32)


def _sb_weights(z, r, carries, mask):
    n = z.shape[1] // KEY_BLOCK
    tq = z.shape[0] // 2
    rows, new_carries = [], []
    for h in range(2):
        carry = None if carries is None else carries[h]
        es = [None] * n
        for d in reversed(range(n)):
            rd = r[(h * n + d) * tq:(h * n + d + 1) * tq]
            x = z[h * tq:(h + 1) * tq, d * KEY_BLOCK:(d + 1) * KEY_BLOCK] + rd[:, :KEY_BLOCK]
            if carry is not None:
                x = x + carry
            e = jnp.exp(x)
            if d == n - 1 and mask is not None:
                e = jnp.where(mask, e, 0.0)
            es[d] = e.astype(BF16)
            tot = rd[:, KEY_BLOCK:]
            carry = tot if carry is None else carry + tot
        rows.append(es[0] if n == 1 else jnp.concatenate(es, axis=1))
        new_carries.append(carry)
    return jnp.concatenate(rows, axis=0), new_carries


def _tri2():
    j = np.arange(KEY_BLOCK)[:, None]
    s = np.arange(KEY_BLOCK)[None, :]
    half = np.concatenate([-(j >= s).astype(np.float32), -np.ones((KEY_BLOCK, LANES), np.float32)], axis=1)
    return jnp.asarray(np.concatenate([half, half], axis=0), dtype=BF16)


def _stack_heads(q, lo_half):
    zero = jnp.zeros_like(q)
    return jnp.concatenate([jnp.where(lo_half, q, zero), jnp.where(lo_half, zero, q)], axis=0)


def _sb_window(z_fn, pv_fn, carries, mask, tri2):
    z = z_fn()
    e, cs = _sb_weights(z, _sb_cumsum(_sb_split(z, mask), tri2), carries, mask)
    pv = pv_fn(e)
    tq = pv.shape[0] // 2
    return (pv[:tq], cs[0]), (pv[tq:], cs[1])


def _finish(store, j0, outs, step_fn):
    (a0, c0), (a1, c1) = outs
    store(a0, a1)
    if isinstance(j0, int) and j0 < 0:
        return lambda: None

    def alive(x0, x1):
        return jnp.max(jnp.maximum(x0, x1)) >= LOG_KEEP_FLOOR

    go = jnp.logical_and(j0 >= 0, alive(c0, c1))

    def walk():
        @pl.when(go)
        def _():
            def body(s):
                j, b0, b1, d0, d1, _ = s
                (p0, d0), (p1, d1) = step_fn(j, [d0, d1])
                return j - 1, b0 + p0, b1 + p1, d0, d1, jnp.logical_and(j >= 1, alive(d0, d1))

            _, b0, b1, _, _, _ = lax.while_loop(lambda s: s[-1], body,
                                                (jnp.int32(j0), a0, a1, c0, c1, jnp.bool_(True)))
            store(b0, b1)

    return walk


def _sb_chains(chains, tri2):
    zs = [c[0]() for c in chains]
    sums = [_sb_cumsum(_sb_split(z, c[2]), tri2) for z, c in zip(zs, chains)]
    weights = [_sb_weights(z, r, None, c[2]) for z, r, c in zip(zs, sums, chains)]
    walks = []
    for c, (e, cs) in zip(chains, weights):
        pv = c[1](e)
        tq = pv.shape[0] // 2
        walks.append(_finish(c[3], c[5], ((pv[:tq], cs[0]), (pv[tq:], cs[1])), c[4]))
    for walk in walks:
        walk()


def _attn_prompt_kernel(q_ref, kt_ref, vt_ref, tri_ref, o_ref):
    t = q_ref.shape[1]
    nq = t // Q_ROWS
    row = lax.broadcasted_iota(jnp.int32, (Q_ROWS, LANES), 0)
    lane = lax.broadcasted_iota(jnp.int32, (Q_ROWS, LANES), 1)
    lo_half = lane < HEAD_DIM
    diag_mask = lane < row
    lane_minus_row = lane - row
    tri2 = tri_ref[...]

    def window(ref, first, n):
        return jnp.concatenate([ref[0, first + d] for d in range(n)], axis=1)

    def qblocks(blocks, n_fast):
        chains = []
        for i in blocks:
            q2 = _stack_heads(q_ref[0, _rows(i, Q_ROWS), :], lo_half)
            first = i + 1 - n_fast

            def z_fn(q2=q2, first=first):
                return jnp.dot(q2, window(kt_ref, first, n_fast), preferred_element_type=F32)

            def pv_fn(e, first=first):
                return lax.dot_general(e, window(vt_ref, first, n_fast), NT_DIMS, preferred_element_type=F32)

            def store(b0, b1, i=i):
                o_ref[0, _rows(i, Q_ROWS), :] = jnp.where(lo_half, b0, b1).astype(BF16)

            def step(j, cs, i=i, q2=q2):
                m = lane_minus_row < (i - j) * KEY_BLOCK
                return _sb_window(lambda: jnp.dot(q2, kt_ref[0, j], preferred_element_type=F32),
                                  lambda e: lax.dot_general(e, vt_ref[0, j], NT_DIMS, preferred_element_type=F32),
                                  cs, m, tri2)

            chains.append((z_fn, pv_fn, diag_mask, store, step, i - n_fast))
        _sb_chains(chains, tri2)

    n_head = FAST_BLOCKS - 1
    for i in range(n_head):
        qblocks([i], i + 1)

    start = n_head + (nq - n_head) % Q_GROUP
    if start > n_head:
        qblocks(list(range(n_head, start)), FAST_BLOCKS)

    def group(g, carry):
        qblocks([start + g * Q_GROUP + u for u in range(Q_GROUP)], FAST_BLOCKS)
        return carry

    lax.fori_loop(0, (nq - start) // Q_GROUP, group, 0)


def _attn_prompt(qb, ktb, vtb, tri2):
    b, t, _ = qb.shape
    assert t % Q_ROWS == 0 and Q_ROWS == KEY_BLOCK
    qspec = pl.BlockSpec((1, t, LANES), lambda i, h: (i, 0, h))
    kspec = pl.BlockSpec((1, t // KEY_BLOCK, LANES, KEY_BLOCK), lambda i, h: (i, 0, h, 0))
    return pl.pallas_call(
        _attn_prompt_kernel,
        grid=(b, D_ATTN // LANES),
        in_specs=[qspec, kspec, kspec, _const_spec(tri2.shape)],
        out_specs=qspec,
        out_shape=jax.ShapeDtypeStruct((b, t, D_ATTN), BF16),
        compiler_params=pltpu.CompilerParams(dimension_semantics=("arbitrary", "arbitrary"),
                                             vmem_limit_bytes=VMEM_LIMIT),
        name="attn_prompt",
    )(qb, ktb, vtb, tri2)


def _attn_sample_kernel(n_tail, q_ref, kn_ref, vn_ref, ckt_ref, cvt_ref, ck_hbm, cv_hbm, tri_ref, o_ref,
                        kbuf, vbuf):
    s = pl.program_id(0)
    tq = q_ref.shape[1]
    n_past = ck_hbm.shape[2] // KEY_BLOCK
    row = lax.broadcasted_iota(jnp.int32, (tq, LANES), 0)
    lane = lax.broadcasted_iota(jnp.int32, (tq, LANES), 1)
    lo_half = lane < HEAD_DIM
    tri2 = tri_ref[...]

    pad = jnp.zeros((KEY_BLOCK - tq, LANES), BF16)
    chains = []
    for p in range(q_ref.shape[2] // LANES):
        cols = slice(p * LANES, (p + 1) * LANES)
        q2 = _stack_heads(q_ref[0, :, cols], lo_half)

        def z_fn(q2=q2, cols=cols):
            z_new = lax.dot_general(q2, jnp.concatenate([kn_ref[0, :, cols], pad], axis=0), NT_DIMS,
                                    preferred_element_type=F32)
            if n_tail == 0:
                return z_new
            z_past = jnp.dot(q2, ckt_ref[0, cols, :].astype(BF16), preferred_element_type=F32)
            return jnp.concatenate([z_past, z_new], axis=1)

        def pv_fn(e, cols=cols):
            pv = jnp.dot(e[:, n_tail * KEY_BLOCK:], jnp.concatenate([vn_ref[0, :, cols], pad], axis=0),
                         preferred_element_type=F32)
            if n_tail == 0:
                return pv
            return pv + lax.dot_general(e[:, :n_tail * KEY_BLOCK], cvt_ref[0, cols, :].astype(BF16), NT_DIMS,
                                        preferred_element_type=F32)

        def store(a0, a1, cols=cols):
            o_ref[0, :, cols] = jnp.where(lo_half, a0, a1).astype(BF16)

        def step(j, cs, cols=cols, q2=q2, p=p):
            keys = pl.ds(pl.multiple_of(j * KEY_BLOCK, KEY_BLOCK), KEY_BLOCK)
            pltpu.sync_copy(ck_hbm.at[s, pl.ds(p * LANES, LANES), keys], kbuf)
            pltpu.sync_copy(cv_hbm.at[s, pl.ds(p * LANES, LANES), keys], vbuf)
            return _sb_window(lambda: jnp.dot(q2, kbuf[...].astype(BF16), preferred_element_type=F32),
                              lambda e: lax.dot_general(e, vbuf[...].astype(BF16), NT_DIMS,
                                                        preferred_element_type=F32),
                              cs, None, tri2)

        chains.append((z_fn, pv_fn, lane < row, store, step, n_past - 1 - n_tail))
    _sb_chains(chains, tri2)


def _attn_sample(qb, knb, vnb, cache_kt, cache_vt, tri2):
    s, tq, _ = qb.shape
    past_len = cache_kt.shape[2]
    assert past_len % KEY_BLOCK == 0 and tq <= KEY_BLOCK and tq % 16 == 0
    n_tail = min(FAST_BLOCKS - 1, past_len // KEY_BLOCK)
    assert n_tail > 0 and past_len % (n_tail * KEY_BLOCK) == 0
    new = pl.BlockSpec((1, tq, D_ATTN), lambda i: (i, 0, 0))
    tail = pl.BlockSpec((1, D_ATTN, n_tail * KEY_BLOCK), lambda i: (i, 0, past_len // (n_tail * KEY_BLOCK) - 1))
    hbm = pl.BlockSpec(memory_space=pl.ANY)
    return pl.pallas_call(
        functools.partial(_attn_sample_kernel, n_tail),
        grid=(s,),
        in_specs=[new, new, new, tail, tail, hbm, hbm, _const_spec(tri2.shape)],
        out_specs=new,
        out_shape=jax.ShapeDtypeStruct((s, tq, D_ATTN), BF16),
        scratch_shapes=[pltpu.VMEM((LANES, KEY_BLOCK), F32), pltpu.VMEM((LANES, KEY_BLOCK), F32)],
        compiler_params=pltpu.CompilerParams(dimension_semantics=("arbitrary",),
                                             vmem_limit_bytes=VMEM_LIMIT),
        name="attn_sample",
    )(qb, knb, vnb, cache_kt, cache_vt, cache_kt, cache_vt, tri2)


def _gelu_tanh(x):
    return 0.5 * x * (1.0 + jnp.tanh(np.sqrt(2.0 / np.pi).astype(np.float32) * (x + 0.044715 * (x * x * x))))


def _lru_kernel(u_ref, gr_ref, cs_ref, h0_ref, wc_ref, bc_ref, wg_ref, bg_ref, lam_ref,
                hg_ref, cn_ref, hl_ref, ext_ref, a_ref, b_ref, h_ref, hc_ref):
    ti = pl.program_id(1)
    tt = u_ref.shape[1]
    past = CONV_WIDTH - 1

    @pl.when(ti == 0)
    def _():
        ext_ref[0:SUBLANES, :] = jnp.zeros((SUBLANES, D_RNN), F32)
        ext_ref[SUBLANES - past:SUBLANES, :] = cs_ref[0]
        hc_ref[...] = jnp.broadcast_to(h0_ref[0], hc_ref.shape)

    u = u_ref[0]
    ext_ref[SUBLANES:SUBLANES + tt, :] = u
    uc = bc_ref[...] + wc_ref[CONV_WIDTH - 1:CONV_WIDTH, :] * u
    for d in range(1, CONV_WIDTH):
        uc = uc + wc_ref[CONV_WIDTH - 1 - d:CONV_WIDTH - d, :] * ext_ref[SUBLANES - d:SUBLANES - d + tt, :]

    g = jnp.dot(uc.astype(BF16), wg_ref[...], preferred_element_type=F32) + bg_ref[...]
    r = _sigmoid(g[:, :D_RNN])
    ig = _sigmoid(g[:, D_RNN:])
    log_a = (-LRU_C * _softplus(-lam_ref[...])) * r
    a = jnp.exp(log_a)
    m2 = -jnp.tanh(log_a) * (a * a + 1.0)
    b = jnp.where(m2 > 0.0, m2 * lax.rsqrt(m2), 0.0) * (ig * uc)

    a3 = a.reshape(tt // SUBLANES, SUBLANES, D_RNN)
    b3 = b.reshape(tt // SUBLANES, SUBLANES, D_RNN)
    sub = lax.broadcasted_iota(jnp.int32, a3.shape, 1)
    k = 1
    while k < SUBLANES:
        has_prev = sub >= k
        b_prev = jnp.where(has_prev, pltpu.roll(b3, k, axis=1), 0.0)
        a_prev = jnp.where(has_prev, pltpu.roll(a3, k, axis=1), 1.0)
        b3 = b3 + a3 * b_prev
        a3 = a3 * a_prev
        k *= 2
    a_ref[...] = a3.reshape(tt, D_RNN)
    b_ref[...] = b3.reshape(tt, D_RNN)

    def group(gi, h_prev):
        rows = _rows(gi, SUBLANES)
        h = b_ref[rows, :] + a_ref[rows, :] * h_prev
        h_ref[rows, :] = h
        return jnp.broadcast_to(h[SUBLANES - 1:SUBLANES, :], h.shape)

    h_last = lax.fori_loop(0, tt // SUBLANES, group, hc_ref[...], unroll=8)
    hc_ref[...] = h_last
    ext_ref[0:SUBLANES, :] = ext_ref[tt:tt + SUBLANES, :]

    hg_ref[0] = (h_ref[...] * _gelu_tanh(gr_ref[0])).astype(BF16)

    @pl.when(ti == pl.num_programs(1) - 1)
    def _():
        cn_ref[0] = ext_ref[SUBLANES - past:SUBLANES, :]
        hl_ref[0] = h_last[0:1, :]


def _lru(u, g_rnn, conv_state, h0, w_conv, b_conv, wg, bg, lam):
    s, t, _ = u.shape
    tt = min(t, TOKEN_TILE)
    assert t % tt == 0 and tt % SUBLANES == 0
    past = CONV_WIDTH - 1
    seq = pl.BlockSpec((1, tt, D_RNN), lambda i, j: (i, j, 0))
    st3 = pl.BlockSpec((1, past, D_RNN), lambda i, j: (i, 0, 0))
    st1 = pl.BlockSpec((1, 1, D_RNN), lambda i, j: (i, 0, 0))
    return pl.pallas_call(
        _lru_kernel,
        grid=(s, t // tt),
        in_specs=[seq, seq, st3, st1, _const_spec(w_conv.shape), _const_spec(b_conv.shape),
                  _const_spec(wg.shape), _const_spec(bg.shape), _const_spec(lam.shape)],
        out_specs=[seq, st3, st1],
        out_shape=[jax.ShapeDtypeStruct((s, t, D_RNN), BF16),
                   jax.ShapeDtypeStruct((s, past, D_RNN), F32),
                   jax.ShapeDtypeStruct((s, 1, D_RNN), F32)],
        scratch_shapes=[pltpu.VMEM((tt + SUBLANES, D_RNN), F32), pltpu.VMEM((tt, D_RNN), F32),
                        pltpu.VMEM((tt, D_RNN), F32), pltpu.VMEM((tt, D_RNN), F32),
                        pltpu.VMEM((SUBLANES, D_RNN), F32)],
        compiler_params=pltpu.CompilerParams(dimension_semantics=("arbitrary", "arbitrary"),
                                             vmem_limit_bytes=VMEM_LIMIT),
        name="lru",
    )(u, g_rnn, conv_state, h0, w_conv, b_conv, wg, bg, lam)


FF_CHUNK = 1024


def _out_kernel(x_ref, o_ref, hg_ref, g1_ref, wgab_ref, wa_ref, wb_ref, wo_ref, g2_ref, g3_ref,
                wup_ref, wdn_ref, g4_ref, y_ref):
    x = x_ref[...]
    xn = _rms(x, g1_ref[...]).astype(BF16)
    g_a = jnp.dot(xn, wgab_ref[:, :D_MODEL], preferred_element_type=F32)
    y_a = jnp.dot(o_ref[...], wa_ref[...], preferred_element_type=F32)
    m = jax.nn.sigmoid(g_a) * y_a
    g_b = jnp.dot(xn, wgab_ref[:, D_MODEL:], preferred_element_type=F32)
    y_b = jnp.dot(hg_ref[...], wb_ref[...], preferred_element_type=F32)
    m = m + jax.nn.sigmoid(g_b) * y_b
    mix = jnp.dot(m.astype(BF16), wo_ref[...], preferred_element_type=F32)
    x1 = x + _rms(mix, g2_ref[...])
    f = _rms(x1, g3_ref[...]).astype(BF16)
    acc = None
    for c in range(D_FF // FF_CHUNK):
        cols = slice(c * FF_CHUNK, (c + 1) * FF_CHUNK)
        up = jnp.maximum(jnp.dot(f, wup_ref[:, cols], preferred_element_type=F32), 0.0)
        dn = jnp.dot((up * up).astype(BF16), wdn_ref[cols, :], preferred_element_type=F32)
        acc = dn if acc is None else acc + dn
    y_ref[...] = x1 + _rms(acc, g4_ref[...])


def _out(x2d, o2d, hg2d, g1, wgab, wa, wb, wo, g2, g3, wup, wdn, g4):
    n = x2d.shape[0]
    tm = TOKEN_TILE
    assert n % tm == 0
    tok = lambda i: (i, 0)
    consts = (g1, wgab, wa, wb, wo, g2, g3, wup, wdn, g4)
    return pl.pallas_call(
        _out_kernel,
        grid=(n // tm,),
        in_specs=[pl.BlockSpec((tm, D_MODEL), tok), pl.BlockSpec((tm, D_ATTN), tok),
                  pl.BlockSpec((tm, D_RNN), tok)] + [_const_spec(c.shape) for c in consts],
        out_specs=pl.BlockSpec((tm, D_MODEL), tok),
        out_shape=jax.ShapeDtypeStruct((n, D_MODEL), F32),
        compiler_params=pltpu.CompilerParams(dimension_semantics=("arbitrary",),
                                             vmem_limit_bytes=VMEM_LIMIT),
        name="out",
    )(x2d, o2d, hg2d, *consts)


def _block_diag(w):
    n, c, d = w.shape
    return jnp.einsum("ncd,nm->ncmd", w, jnp.eye(n, dtype=w.dtype)).reshape(n * c, n * d)


def _heads_last(kt):
    s, _, t = kt.shape
    return kt.reshape(s, N_HEADS, HEAD_DIM, t).transpose(0, 3, 1, 2)[None]


def _time_last(cache):
    s, t = cache.shape[:2]
    return cache.transpose(0, 2, 3, 1).reshape(s, D_ATTN, t)


def _layer(x, caches, conv_state, h0, wts):
    (wq, wug, wkv, wkv_t, g1, w_conv, b_conv, wg, bg, lam, wgab, wa, wb, wo, g2, g3, wup, wdn, g4, tri2) = wts
    s, t, _ = x.shape
    x2d = x.reshape(s * t, D_MODEL)
    r3 = lambda a: a.reshape(s, t, a.shape[-1])
    if caches is None:
        qb, kt, vt, ktb, vtb, u, g_rnn = _proj(x, g1, wq, wug, wkv_t, transposed=True)
        o = _attn_prompt(r3(qb), ktb, vtb, tri2)
        k5, v5 = _heads_last(kt), _heads_last(vt)
    else:
        qb, k, v, kb, vb, u, g_rnn = _proj(x, g1, wq, wug, wkv, transposed=False)
        o = _attn_sample(r3(qb), r3(kb), r3(vb), _time_last(caches[0]), _time_last(caches[1]), tri2)
        k5, v5 = k.reshape(1, s, t, N_HEADS, HEAD_DIM), v.reshape(1, s, t, N_HEADS, HEAD_DIM)
    hg, conv_new, h_last = _lru(r3(u), r3(g_rnn), conv_state, h0, w_conv, b_conv, wg, bg, lam)
    y = _out(x2d, o.reshape(s * t, D_ATTN), hg.reshape(s * t, D_RNN), g1, wgab, wa, wb, wo, g2, g3,
             wup, wdn, g4)
    return y.reshape(s, t, D_MODEL), k5, v5, conv_new[None], h_last.reshape(1, s, D_RNN)


def kernel(x_prompt, x_sample, cache_k, cache_v, state_conv, state_h, w_in, g_pre_mix, w_conv, b_conv, w_r, b_r, w_i, b_i, lam, w_a_out, w_b_out, w_o, g_post_mix, g_pre_ffn, w_up, w_down, g_post_ffn):
    assert w_in.shape[0] == 1
    row = lambda a: a[0].reshape(1, -1)
    w_in0 = w_in[0]
    n_seq = 3 * D_ATTN + 2 * D_RNN
    wq = (w_in0[:, :D_ATTN] * (HEAD_DIM ** -0.5)).astype(BF16)
    wkv = w_in0[:, D_ATTN:3 * D_ATTN].astype(BF16)
    wug = w_in0[:, 3 * D_ATTN:n_seq].astype(BF16)
    wgab = w_in0[:, n_seq:].astype(BF16)
    wg = jnp.concatenate([_block_diag(w_r[0]), _block_diag(w_i[0])], axis=1).astype(BF16)
    bg = jnp.concatenate([row(b_r), row(b_i)], axis=1)
    wts = (wq, wug, wkv, wkv.T, row(g_pre_mix), w_conv[0], row(b_conv), wg, bg, row(lam), wgab,
           w_a_out[0].astype(BF16), w_b_out[0].astype(BF16), w_o[0].astype(BF16), row(g_post_mix),
           row(g_pre_ffn), w_up[0].astype(BF16), w_down[0].astype(BF16), row(g_post_ffn), _tri2())

    bp = x_prompt.shape[0]
    zc = jnp.zeros((bp, CONV_WIDTH - 1, D_RNN), F32)
    zh = jnp.zeros((bp, 1, D_RNN), F32)
    yp, kp, vp, cp, hp = _layer(x_prompt, None, zc, zh, wts)

    bs = x_sample.shape[0]
    ys, ks, vs, cs, hs = _layer(x_sample, (cache_k[0], cache_v[0]), state_conv[0],
                                state_h[0].reshape(bs, 1, D_RNN), wts)
    return (yp, ys, kp, vp, cp, hp, ks, vs, cs, hs)
```

```python
import functools

import jax
import jax.numpy as jnp
import numpy as np
from jax import lax
from jax.experimental import pallas as pl
from jax.experimental.pallas import tpu as pltpu

F32 = jnp.float32
BF16 = jnp.bfloat16

D_MODEL = 1024
N_HEADS = 8
HEAD_DIM = 64
D_ATTN = N_HEADS * HEAD_DIM
D_RNN = 512
N_RNN_BLOCKS = 8
CONV_WIDTH = 4
LRU_C = 8.0
D_FF = 4 * D_MODEL
EPS = 1e-6

LANES = 128
SUBLANES = 8
KEY_BLOCK = 128
Q_ROWS = 128
FAST_BLOCKS = 3
Q_GROUP = 4
TOKEN_TILE = 512
LRU_SLICES = 2
PROJ_CHUNK = 256
LOG_KEEP_FLOOR = -104.0
VMEM_LIMIT = 56 * 1024 * 1024

NT_DIMS = (((1,), (1,)), ((), ()))


def _rms(xf, g):
    return xf * lax.rsqrt(jnp.mean(xf * xf, axis=-1, keepdims=True) + EPS) * g


def _softplus(x):
    return jnp.maximum(x, 0.0) + jnp.log(1.0 + jnp.exp(-jnp.abs(x)))


def _sigmoid(x):
    return 0.5 * jnp.tanh(0.5 * x) + 0.5


def _const_spec(shape):
    nd = len(shape)
    return pl.BlockSpec(shape, lambda *_: (0,) * nd, pipeline_mode=pl.Buffered(1))


def _rows(block_index, size):
    if isinstance(block_index, int):
        return pl.ds(block_index * size, size)
    return pl.ds(pl.multiple_of(block_index * size, size), size)


def _gelu_tanh(x):
    return 0.5 * x * (1.0 + jnp.tanh(np.sqrt(2.0 / np.pi).astype(np.float32) * (x + 0.044715 * (x * x * x))))


def _lru_seed(first, cs_ref, h0_ref, ext_ref, hc_ref):
    @pl.when(first)
    def _():
        ext_ref[0:SUBLANES, :] = jnp.zeros((SUBLANES, D_RNN), F32)
        ext_ref[SUBLANES - (CONV_WIDTH - 1):SUBLANES, :] = cs_ref[0]
        hc_ref[...] = jnp.broadcast_to(h0_ref[0], hc_ref.shape)


def _lru_tile(u, g_rnn, wc_ref, bc_ref, wg_ref, bg_ref, lam_ref, ext_ref, hc_ref):
    tt = u.shape[0]
    ext_ref[SUBLANES:SUBLANES + tt, :] = u
    uc = bc_ref[...] + wc_ref[CONV_WIDTH - 1:CONV_WIDTH, :] * u
    for d in range(1, CONV_WIDTH):
        uc = uc + wc_ref[CONV_WIDTH - 1 - d:CONV_WIDTH - d, :] * ext_ref[SUBLANES - d:SUBLANES - d + tt, :]

    g = jnp.dot(uc.astype(BF16), wg_ref[...], preferred_element_type=F32) + bg_ref[...]
    r = _sigmoid(g[:, :D_RNN])
    ig = _sigmoid(g[:, D_RNN:])
    log_a = (-LRU_C * _softplus(-lam_ref[...])) * r
    a = jnp.exp(log_a)
    m2 = -jnp.tanh(log_a) * (a * a + 1.0)
    b = jnp.where(m2 > 0.0, m2 * lax.rsqrt(m2), 0.0) * (ig * uc)

    n_groups = tt // SUBLANES
    a3 = a.reshape(n_groups, SUBLANES, D_RNN)
    b3 = b.reshape(n_groups, SUBLANES, D_RNN)
    sub = lax.broadcasted_iota(jnp.int32, a3.shape, 1)
    k = 1
    while k < SUBLANES:
        has_prev = sub >= k
        b_prev = jnp.where(has_prev, pltpu.roll(b3, k, axis=1), 0.0)
        a_prev = jnp.where(has_prev, pltpu.roll(a3, k, axis=1), 1.0)
        b3 = b3 + a3 * b_prev
        a3 = a3 * a_prev
        k *= 2

    h_prev = hc_ref[...]
    hs = []
    for gi in range(n_groups):
        h = b3[gi] + a3[gi] * h_prev
        hs.append(h)
        h_prev = jnp.broadcast_to(h[SUBLANES - 1:SUBLANES, :], h.shape)
    hc_ref[...] = h_prev
    ext_ref[0:SUBLANES, :] = ext_ref[tt:tt + SUBLANES, :]
    return (jnp.concatenate(hs, axis=0) * _gelu_tanh(g_rnn)).astype(BF16), h_prev


def _lru_kernel(u_ref, gr_ref, cs_ref, h0_ref, wc_ref, bc_ref, wg_ref, bg_ref, lam_ref,
                hg_ref, cn_ref, hl_ref, ext_ref, hc_ref):
    ti = pl.program_id(1)
    _lru_seed(ti == 0, cs_ref, h0_ref, ext_ref, hc_ref)
    hg, h_last = _lru_tile(u_ref[0], gr_ref[0], wc_ref, bc_ref, wg_ref, bg_ref, lam_ref, ext_ref, hc_ref)
    hg_ref[0] = hg

    @pl.when(ti == pl.num_programs(1) - 1)
    def _():
        cn_ref[0] = ext_ref[SUBLANES - (CONV_WIDTH - 1):SUBLANES, :]
        hl_ref[0] = h_last[0:1, :]


def _lru_state_specs(index_map):
    past = CONV_WIDTH - 1
    return (pl.BlockSpec((1, past, D_RNN), index_map), pl.BlockSpec((1, 1, D_RNN), index_map))


def _lru_scratch(tt):
    return [pltpu.VMEM((tt + SUBLANES, D_RNN), F32), pltpu.VMEM((SUBLANES, D_RNN), F32)]


def _lru(u, g_rnn, conv_state, h0, lru_w):
    s, t, _ = u.shape
    tt = min(t, TOKEN_TILE)
    assert t % tt == 0 and tt % SUBLANES == 0
    seq = pl.BlockSpec((1, tt, D_RNN), lambda i, j: (i, j, 0))
    st3, st1 = _lru_state_specs(lambda i, j: (i, 0, 0))
    return pl.pallas_call(
        _lru_kernel,
        grid=(s, t // tt),
        in_specs=[seq, seq, st3, st1] + [_const_spec(w.shape) for w in lru_w],
        out_specs=[seq, st3, st1],
        out_shape=[jax.ShapeDtypeStruct((s, t, D_RNN), BF16),
                   jax.ShapeDtypeStruct((s, CONV_WIDTH - 1, D_RNN), F32),
                   jax.ShapeDtypeStruct((s, 1, D_RNN), F32)],
        scratch_shapes=_lru_scratch(tt),
        compiler_params=pltpu.CompilerParams(dimension_semantics=("arbitrary", "arbitrary"),
                                             vmem_limit_bytes=VMEM_LIMIT),
        name="lru",
    )(u, g_rnn, conv_state, h0, *lru_w)


def _proj_kernel(x_ref, g_ref, wq_ref, wug_ref, wkv_ref, qb_ref, k_ref, v_ref, kb_ref, vb_ref, u_ref, gr_ref):
    xn = _rms(x_ref[...], g_ref[...]).astype(BF16)
    qb_ref[...] = jnp.dot(xn, wq_ref[...], preferred_element_type=F32).astype(BF16)
    ug = jnp.dot(xn, wug_ref[...], preferred_element_type=F32)
    u_ref[...] = ug[:, :D_RNN]
    gr_ref[...] = ug[:, D_RNN:]
    kv = jnp.dot(xn, wkv_ref[...], preferred_element_type=F32)
    k, v = kv[:, :D_ATTN], kv[:, D_ATTN:]
    k_ref[...] = k
    v_ref[...] = v
    kb_ref[...] = k.astype(BF16)
    vb_ref[...] = v.astype(BF16)


def _proj(x, g, wq, wug, wkv):
    s, t, _ = x.shape
    n = s * t
    tm = TOKEN_TILE
    assert n % tm == 0
    tok = lambda i: (i, 0)
    row_out = pl.BlockSpec((tm, D_ATTN), tok)
    f32o = jax.ShapeDtypeStruct((n, D_ATTN), F32)
    b16o = jax.ShapeDtypeStruct((n, D_ATTN), BF16)
    return pl.pallas_call(
        _proj_kernel,
        grid=(n // tm,),
        in_specs=[pl.BlockSpec((tm, D_MODEL), tok), _const_spec(g.shape), _const_spec(wq.shape),
                  _const_spec(wug.shape), _const_spec(wkv.shape)],
        out_specs=[row_out] * 7,
        out_shape=[b16o, f32o, f32o, b16o, b16o, f32o, f32o],
        compiler_params=pltpu.CompilerParams(dimension_semantics=("arbitrary",),
                                             vmem_limit_bytes=VMEM_LIMIT),
        name="proj",
    )(x.reshape(n, D_MODEL), g, wq, wug, wkv)


def _proj_lru_kernel(per, x_ref, g_ref, wq_ref, wug_ref, wkvt_ref, cs_ref, h0_ref, wc_ref, bc_ref, wg_ref,
                     bg_ref, lam_ref, qb_ref, kt_ref, vt_ref, ktb_ref, vtb_ref, hg_ref, cn_ref, hl_ref,
                     ext_ref, hc_ref):
    ti = pl.program_id(0) % per
    _lru_seed(ti == 0, cs_ref, h0_ref, ext_ref, hc_ref)
    xn = _rms(x_ref[...], g_ref[...]).astype(BF16)
    tm = xn.shape[0]

    def q_chunk(c):
        cols = slice(c * PROJ_CHUNK, (c + 1) * PROJ_CHUNK)
        qb_ref[:, cols] = jnp.dot(xn, wq_ref[:, cols], preferred_element_type=F32).astype(BF16)

    def kv_chunk(c):
        toks = slice(c * PROJ_CHUNK, (c + 1) * PROJ_CHUNK)
        kv = lax.dot_general(wkvt_ref[...], xn[toks], NT_DIMS, preferred_element_type=F32)
        k, v = kv[:D_ATTN], kv[D_ATTN:]
        kt_ref[0, :, toks] = k
        vt_ref[0, :, toks] = v
        for j in range(PROJ_CHUNK // KEY_BLOCK):
            cols = slice(j * KEY_BLOCK, (j + 1) * KEY_BLOCK)
            ktb_ref[0, c * (PROJ_CHUNK // KEY_BLOCK) + j] = k[:, cols].astype(BF16)
            vtb_ref[0, c * (PROJ_CHUNK // KEY_BLOCK) + j] = v[:, cols].astype(BF16)

    chunks = []
    for c in range(max(tm, D_ATTN) // PROJ_CHUNK):
        if c < D_ATTN // PROJ_CHUNK:
            chunks.append(functools.partial(q_chunk, c))
        if c < tm // PROJ_CHUNK:
            chunks.append(functools.partial(kv_chunk, c))

    ug = jnp.dot(xn, wug_ref[...], preferred_element_type=F32)
    rows_per = tm // LRU_SLICES
    for p in range(LRU_SLICES):
        rows = slice(p * rows_per, (p + 1) * rows_per)
        hg, h_last = _lru_tile(ug[rows, :D_RNN], ug[rows, D_RNN:], wc_ref, bc_ref, wg_ref, bg_ref, lam_ref,
                               ext_ref, hc_ref)
        hg_ref[rows, :] = hg
        for chunk in chunks[p * len(chunks) // LRU_SLICES:(p + 1) * len(chunks) // LRU_SLICES]:
            chunk()

    @pl.when(ti == per - 1)
    def _():
        cn_ref[0] = ext_ref[SUBLANES - (CONV_WIDTH - 1):SUBLANES, :]
        hl_ref[0] = h_last[0:1, :]


def _proj_lru(x, g, wq, wug, wkvt, conv_state, h0, lru_w):
    s, t, _ = x.shape
    n = s * t
    tm = TOKEN_TILE
    assert t % tm == 0 and tm % KEY_BLOCK == 0
    per = t // tm
    tok = lambda i: (i, 0)
    row_out = pl.BlockSpec((tm, D_ATTN), tok)
    b16o = jax.ShapeDtypeStruct((n, D_ATTN), BF16)
    kv_spec = pl.BlockSpec((1, D_ATTN, tm), lambda i: (i // per, 0, i % per))
    kvb_spec = pl.BlockSpec((1, tm // KEY_BLOCK, D_ATTN, KEY_BLOCK), lambda i: (i // per, i % per, 0, 0))
    kv_shape = jax.ShapeDtypeStruct((s, D_ATTN, t), F32)
    kvb_shape = jax.ShapeDtypeStruct((s, t // KEY_BLOCK, D_ATTN, KEY_BLOCK), BF16)
    st3, st1 = _lru_state_specs(lambda i: (i // per, 0, 0))
    consts = (g, wq, wug, wkvt)
    return pl.pallas_call(
        functools.partial(_proj_lru_kernel, per),
        grid=(n // tm,),
        in_specs=[pl.BlockSpec((tm, D_MODEL), tok)] + [_const_spec(c.shape) for c in consts] + [st3, st1]
        + [_const_spec(w.shape) for w in lru_w],
        out_specs=[row_out, kv_spec, kv_spec, kvb_spec, kvb_spec, row_out, st3, st1],
        out_shape=[b16o, kv_shape, kv_shape, kvb_shape, kvb_shape, b16o,
                   jax.ShapeDtypeStruct((s, CONV_WIDTH - 1, D_RNN), F32),
                   jax.ShapeDtypeStruct((s, 1, D_RNN), F32)],
        scratch_shapes=_lru_scratch(tm // LRU_SLICES),
        compiler_params=pltpu.CompilerParams(dimension_semantics=("arbitrary",),
                                             vmem_limit_bytes=VMEM_LIMIT),
        name="proj_lru",
    )(x.reshape(n, D_MODEL), *consts, conv_state, h0, *lru_w)


def _sb_split(z, mask):
    n = z.shape[1] // KEY_BLOCK
    tq = z.shape[0] // 2
    sp = _softplus(z)
    cats = []
    for h in range(2):
        for d in range(n):
            blk = sp[h * tq:(h + 1) * tq, d * KEY_BLOCK:(d + 1) * KEY_BLOCK]
            if d == n - 1 and mask is not None:
                blk = jnp.where(mask, blk, 0.0)
            hi = blk.astype(BF16)
            lo = (blk - hi.astype(F32)).astype(BF16)
            cats.append(jnp.concatenate([hi, lo], axis=1))
    return jnp.concatenate(cats, axis=0)


def _sb_cumsum(cat, tri2):
    return jnp.dot(cat, tri2, preferred_element_type=F32)


def _sb_weights(z, r, carries, mask):
    n = z.shape[1] // KEY_BLOCK
    tq = z.shape[0] // 2
    rows, new_carries = [], []
    for h in range(2):
        carry = None if carries is None else carries[h]
        es = [None] * n
        for d in reversed(range(n)):
            rd = r[(h * n + d) * tq:(h * n + d + 1) * tq]
            x = z[h * tq:(h + 1) * tq, d * KEY_BLOCK:(d + 1) * KEY_BLOCK] + rd[:, :KEY_BLOCK]
            if carry is not None:
                x = x + carry
            e = jnp.exp(x)
            if d == n - 1 and mask is not None:
                e = jnp.where(mask, e, 0.0)
            es[d] = e.astype(BF16)
            tot = rd[:, KEY_BLOCK:]
            carry = tot if carry is None else carry + tot
        rows.append(es[0] if n == 1 else jnp.concatenate(es, axis=1))
        new_carries.append(carry)
    return jnp.concatenate(rows, axis=0), new_carries


def _tri2():
    j = np.arange(KEY_BLOCK)[:, None]
    s = np.arange(KEY_BLOCK)[None, :]
    half = np.concatenate([-(j >= s).astype(np.float32), -np.ones((KEY_BLOCK, LANES), np.float32)], axis=1)
    return jnp.asarray(np.concatenate([half, half], axis=0), dtype=BF16)


def _stack_heads(q, lo_half):
    zero = jnp.zeros_like(q)
    return jnp.concatenate([jnp.where(lo_half, q, zero), jnp.where(lo_half, zero, q)], axis=0)


def _sb_window(z_fn, pv_fn, carries, mask, tri2):
    z = z_fn()
    e, cs = _sb_weights(z, _sb_cumsum(_sb_split(z, mask), tri2), carries, mask)
    pv = pv_fn(e)
    tq = pv.shape[0] // 2
    return (pv[:tq], cs[0]), (pv[tq:], cs[1])


def _finish(store, j0, outs, step_fn):
    (a0, c0), (a1, c1) = outs
    store(a0, a1)
    if isinstance(j0, int) and j0 < 0:
        return lambda: None

    def alive(x0, x1):
        return jnp.max(jnp.maximum(x0, x1)) >= LOG_KEEP_FLOOR

    go = jnp.logical_and(j0 >= 0, alive(c0, c1))

    def walk():
        @pl.when(go)
        def _():
            def body(s):
                j, b0, b1, d0, d1, _ = s
                (p0, d0), (p1, d1) = step_fn(j, [d0, d1])
                return j - 1, b0 + p0, b1 + p1, d0, d1, jnp.logical_and(j >= 1, alive(d0, d1))

            _, b0, b1, _, _, _ = lax.while_loop(lambda s: s[-1], body,
                                                (jnp.int32(j0), a0, a1, c0, c1, jnp.bool_(True)))
            store(b0, b1)

    return walk


def _sb_chains(chains, tri2):
    zs = [c[0]() for c in chains]
    sums = [_sb_cumsum(_sb_split(z, c[2]), tri2) for z, c in zip(zs, chains)]
    weights = [_sb_weights(z, r, None, c[2]) for z, r, c in zip(zs, sums, chains)]
    walks = []
    for c, (e, cs) in zip(chains, weights):
        pv = c[1](e)
        tq = pv.shape[0] // 2
        walks.append(_finish(c[3], c[5], ((pv[:tq], cs[0]), (pv[tq:], cs[1])), c[4]))
    for walk in walks:
        walk()


def _attn_prompt_kernel(q_ref, kt_ref, vt_ref, tri_ref, o_ref):
    t = q_ref.shape[1]
    nq = t // Q_ROWS
    row = lax.broadcasted_iota(jnp.int32, (Q_ROWS, LANES), 0)
    lane = lax.broadcasted_iota(jnp.int32, (Q_ROWS, LANES), 1)
    lo_half = lane < HEAD_DIM
    diag_mask = lane < row
    lane_minus_row = lane - row
    tri2 = tri_ref[...]

    def window(ref, first, n):
        return jnp.concatenate([ref[0, first + d] for d in range(n)], axis=1)

    def qblocks(blocks, n_fast):
        chains = []
        for i in blocks:
            q2 = _stack_heads(q_ref[0, _rows(i, Q_ROWS), :], lo_half)
            first = i + 1 - n_fast

            def z_fn(q2=q2, first=first):
                return jnp.dot(q2, window(kt_ref, first, n_fast), preferred_element_type=F32)

            def pv_fn(e, first=first):
                return lax.dot_general(e, window(vt_ref, first, n_fast), NT_DIMS, preferred_element_type=F32)

            def store(b0, b1, i=i):
                o_ref[0, _rows(i, Q_ROWS), :] = jnp.where(lo_half, b0, b1).astype(BF16)

            def step(j, cs, i=i, q2=q2):
                m = lane_minus_row < (i - j) * KEY_BLOCK
                return _sb_window(lambda: jnp.dot(q2, kt_ref[0, j], preferred_element_type=F32),
                                  lambda e: lax.dot_general(e, vt_ref[0, j], NT_DIMS, preferred_element_type=F32),
                                  cs, m, tri2)

            chains.append((z_fn, pv_fn, diag_mask, store, step, i - n_fast))
        _sb_chains(chains, tri2)

    n_head = FAST_BLOCKS - 1
    for i in range(n_head):
        qblocks([i], i + 1)

    start = n_head + (nq - n_head) % Q_GROUP
    if start > n_head:
        qblocks(list(range(n_head, start)), FAST_BLOCKS)

    def group(g, carry):
        qblocks([start + g * Q_GROUP + u for u in range(Q_GROUP)], FAST_BLOCKS)
        return carry

    lax.fori_loop(0, (nq - start) // Q_GROUP, group, 0)


def _attn_prompt(qb, ktb, vtb, tri2):
    b, t, _ = qb.shape
    assert t % Q_ROWS == 0 and Q_ROWS == KEY_BLOCK
    qspec = pl.BlockSpec((1, t, LANES), lambda i, h: (i, 0, h))
    kspec = pl.BlockSpec((1, t // KEY_BLOCK, LANES, KEY_BLOCK), lambda i, h: (i, 0, h, 0))
    return pl.pallas_call(
        _attn_prompt_kernel,
        grid=(b, D_ATTN // LANES),
        in_specs=[qspec, kspec, kspec, _const_spec(tri2.shape)],
        out_specs=qspec,
        out_shape=jax.ShapeDtypeStruct((b, t, D_ATTN), BF16),
        compiler_params=pltpu.CompilerParams(dimension_semantics=("arbitrary", "arbitrary"),
                                             vmem_limit_bytes=VMEM_LIMIT),
        name="attn_prompt",
    )(qb, ktb, vtb, tri2)


def _attn_sample_kernel(n_tail, q_ref, kn_ref, vn_ref, ckt_ref, cvt_ref, ck_hbm, cv_hbm, tri_ref, o_ref,
                        kbuf, vbuf):
    s = pl.program_id(0)
    tq = q_ref.shape[1]
    n_past = ck_hbm.shape[2] // KEY_BLOCK
    row = lax.broadcasted_iota(jnp.int32, (tq, LANES), 0)
    lane = lax.broadcasted_iota(jnp.int32, (tq, LANES), 1)
    lo_half = lane < HEAD_DIM
    tri2 = tri_ref[...]

    pad = jnp.zeros((KEY_BLOCK - tq, LANES), BF16)
    chains = []
    for p in range(q_ref.shape[2] // LANES):
        cols = slice(p * LANES, (p + 1) * LANES)
        q2 = _stack_heads(q_ref[0, :, cols], lo_half)

        def z_fn(q2=q2, cols=cols):
            z_new = lax.dot_general(q2, jnp.concatenate([kn_ref[0, :, cols], pad], axis=0), NT_DIMS,
                                    preferred_element_type=F32)
            if n_tail == 0:
                return z_new
            z_past = jnp.dot(q2, ckt_ref[0, cols, :].astype(BF16), preferred_element_type=F32)
            return jnp.concatenate([z_past, z_new], axis=1)

        def pv_fn(e, cols=cols):
            pv = jnp.dot(e[:, n_tail * KEY_BLOCK:], jnp.concatenate([vn_ref[0, :, cols], pad], axis=0),
                         preferred_element_type=F32)
            if n_tail == 0:
                return pv
            return pv + lax.dot_general(e[:, :n_tail * KEY_BLOCK], cvt_ref[0, cols, :].astype(BF16), NT_DIMS,
                                        preferred_element_type=F32)

        def store(a0, a1, cols=cols):
            o_ref[0, :, cols] = jnp.where(lo_half, a0, a1).astype(BF16)

        def step(j, cs, cols=cols, q2=q2, p=p):
            keys = pl.ds(pl.multiple_of(j * KEY_BLOCK, KEY_BLOCK), KEY_BLOCK)
            pltpu.sync_copy(ck_hbm.at[s, pl.ds(p * LANES, LANES), keys], kbuf)
            pltpu.sync_copy(cv_hbm.at[s, pl.ds(p * LANES, LANES), keys], vbuf)
            return _sb_window(lambda: jnp.dot(q2, kbuf[...].astype(BF16), preferred_element_type=F32),
                              lambda e: lax.dot_general(e, vbuf[...].astype(BF16), NT_DIMS,
                                                        preferred_element_type=F32),
                              cs, None, tri2)

        chains.append((z_fn, pv_fn, lane < row, store, step, n_past - 1 - n_tail))
    _sb_chains(chains, tri2)


def _attn_sample(qb, knb, vnb, cache_kt, cache_vt, tri2):
    s, tq, _ = qb.shape
    past_len = cache_kt.shape[2]
    assert past_len % KEY_BLOCK == 0 and tq <= KEY_BLOCK and tq % 16 == 0
    n_tail = min(FAST_BLOCKS - 1, past_len // KEY_BLOCK)
    assert n_tail > 0 and past_len % (n_tail * KEY_BLOCK) == 0
    new = pl.BlockSpec((1, tq, D_ATTN), lambda i: (i, 0, 0))
    tail = pl.BlockSpec((1, D_ATTN, n_tail * KEY_BLOCK), lambda i: (i, 0, past_len // (n_tail * KEY_BLOCK) - 1))
    hbm = pl.BlockSpec(memory_space=pl.ANY)
    return pl.pallas_call(
        functools.partial(_attn_sample_kernel, n_tail),
        grid=(s,),
        in_specs=[new, new, new, tail, tail, hbm, hbm, _const_spec(tri2.shape)],
        out_specs=new,
        out_shape=jax.ShapeDtypeStruct((s, tq, D_ATTN), BF16),
        scratch_shapes=[pltpu.VMEM((LANES, KEY_BLOCK), F32), pltpu.VMEM((LANES, KEY_BLOCK), F32)],
        compiler_params=pltpu.CompilerParams(dimension_semantics=("arbitrary",),
                                             vmem_limit_bytes=VMEM_LIMIT),
        name="attn_sample",
    )(qb, knb, vnb, cache_kt, cache_vt, cache_kt, cache_vt, tri2)


FF_CHUNK = 1024


def _out_kernel(x_ref, o_ref, hg_ref, g1_ref, wgab_ref, wa_ref, wb_ref, wo_ref, g2_ref, g3_ref,
                wup_ref, wdn_ref, g4_ref, y_ref):
    x = x_ref[...]
    xn = _rms(x, g1_ref[...]).astype(BF16)
    g_a = jnp.dot(xn, wgab_ref[:, :D_MODEL], preferred_element_type=F32)
    y_a = jnp.dot(o_ref[...], wa_ref[...], preferred_element_type=F32)
    m = jax.nn.sigmoid(g_a) * y_a
    g_b = jnp.dot(xn, wgab_ref[:, D_MODEL:], preferred_element_type=F32)
    y_b = jnp.dot(hg_ref[...], wb_ref[...], preferred_element_type=F32)
    m = m + jax.nn.sigmoid(g_b) * y_b
    mix = jnp.dot(m.astype(BF16), wo_ref[...], preferred_element_type=F32)
    x1 = x + _rms(mix, g2_ref[...])
    f = _rms(x1, g3_ref[...]).astype(BF16)
    acc = None
    for c in range(D_FF // FF_CHUNK):
        cols = slice(c * FF_CHUNK, (c + 1) * FF_CHUNK)
        up = jnp.maximum(jnp.dot(f, wup_ref[:, cols], preferred_element_type=F32), 0.0)
        dn = jnp.dot((up * up).astype(BF16), wdn_ref[cols, :], preferred_element_type=F32)
        acc = dn if acc is None else acc + dn
    y_ref[...] = x1 + _rms(acc, g4_ref[...])


def _out(x2d, o2d, hg2d, g1, wgab, wa, wb, wo, g2, g3, wup, wdn, g4):
    n = x2d.shape[0]
    tm = TOKEN_TILE
    assert n % tm == 0
    tok = lambda i: (i, 0)
    consts = (g1, wgab, wa, wb, wo, g2, g3, wup, wdn, g4)
    return pl.pallas_call(
        _out_kernel,
        grid=(n // tm,),
        in_specs=[pl.BlockSpec((tm, D_MODEL), tok), pl.BlockSpec((tm, D_ATTN), tok),
                  pl.BlockSpec((tm, D_RNN), tok)] + [_const_spec(c.shape) for c in consts],
        out_specs=pl.BlockSpec((tm, D_MODEL), tok),
        out_shape=jax.ShapeDtypeStruct((n, D_MODEL), F32),
        compiler_params=pltpu.CompilerParams(dimension_semantics=("arbitrary",),
                                             vmem_limit_bytes=VMEM_LIMIT),
        name="out",
    )(x2d, o2d, hg2d, *consts)


def _block_diag(w):
    n, c, d = w.shape
    return jnp.einsum("ncd,nm->ncmd", w, jnp.eye(n, dtype=w.dtype)).reshape(n * c, n * d)


def _heads_last(kt):
    s, _, t = kt.shape
    return kt.reshape(s, N_HEADS, HEAD_DIM, t).transpose(0, 3, 1, 2)[None]


def _time_last(cache):
    s, t = cache.shape[:2]
    return cache.transpose(0, 2, 3, 1).reshape(s, D_ATTN, t)


def _layer(x, caches, conv_state, h0, wts):
    (wq, wug, wkv, wkv_t, g1, lru_w, wgab, wa, wb, wo, g2, g3, wup, wdn, g4, tri2) = wts
    s, t, _ = x.shape
    x2d = x.reshape(s * t, D_MODEL)
    r3 = lambda a: a.reshape(s, t, a.shape[-1])
    if caches is None:
        qb, kt, vt, ktb, vtb, hg, conv_new, h_last = _proj_lru(x, g1, wq, wug, wkv_t, conv_state, h0, lru_w)
        o = _attn_prompt(r3(qb), ktb, vtb, tri2)
        k5, v5 = _heads_last(kt), _heads_last(vt)
    else:
        qb, k, v, kb, vb, u, g_rnn = _proj(x, g1, wq, wug, wkv)
        o = _attn_sample(r3(qb), r3(kb), r3(vb), _time_last(caches[0]), _time_last(caches[1]), tri2)
        k5, v5 = k.reshape(1, s, t, N_HEADS, HEAD_DIM), v.reshape(1, s, t, N_HEADS, HEAD_DIM)
        hg, conv_new, h_last = _lru(r3(u), r3(g_rnn), conv_state, h0, lru_w)
    y = _out(x2d, o.reshape(s * t, D_ATTN), hg.reshape(s * t, D_RNN), g1, wgab, wa, wb, wo, g2, g3,
             wup, wdn, g4)
    return y.reshape(s, t, D_MODEL), k5, v5, conv_new[None], h_last.reshape(1, s, D_RNN)


def kernel(x_prompt, x_sample, cache_k, cache_v, state_conv, state_h, w_in, g_pre_mix, w_conv, b_conv, w_r, b_r, w_i, b_i, lam, w_a_out, w_b_out, w_o, g_post_mix, g_pre_ffn, w_up, w_down, g_post_ffn):
    assert w_in.shape[0] == 1
    row = lambda a: a[0].reshape(1, -1)
    w_in0 = w_in[0]
    n_seq = 3 * D_ATTN + 2 * D_RNN
    wq = (w_in0[:, :D_ATTN] * (HEAD_DIM ** -0.5)).astype(BF16)
    wkv = w_in0[:, D_ATTN:3 * D_ATTN].astype(BF16)
    wug = w_in0[:, 3 * D_ATTN:n_seq].astype(BF16)
    wgab = w_in0[:, n_seq:].astype(BF16)
    wg = jnp.concatenate([_block_diag(w_r[0]), _block_diag(w_i[0])], axis=1).astype(BF16)
    bg = jnp.concatenate([row(b_r), row(b_i)], axis=1)
    lru_w = (w_conv[0], row(b_conv), wg, bg, row(lam))
    wts = (wq, wug, wkv, wkv.T, row(g_pre_mix), lru_w, wgab,
           w_a_out[0].astype(BF16), w_b_out[0].astype(BF16), w_o[0].astype(BF16), row(g_post_mix),
           row(g_pre_ffn), w_up[0].astype(BF16), w_down[0].astype(BF16), row(g_post_ffn), _tri2())

    bp = x_prompt.shape[0]
    zc = jnp.zeros((bp, CONV_WIDTH - 1, D_RNN), F32)
    zh = jnp.zeros((bp, 1, D_RNN), F32)
    yp, kp, vp, cp, hp = _layer(x_prompt, None, zc, zh, wts)

    bs = x_sample.shape[0]
    ys, ks, vs, cs, hs = _layer(x_sample, (cache_k[0], cache_v[0]), state_conv[0],
                                state_h[0].reshape(bs, 1, D_RNN), wts)
    return (yp, ys, kp, vp, cp, hp, ks, vs, cs, hs)
```

```python
import functools

import jax
import jax.numpy as jnp
import numpy as np
from jax import lax
from jax.experimental import pallas as pl
from jax.experimental.pallas import tpu as pltpu

F32 = jnp.float32
BF16 = jnp.bfloat16

D_MODEL = 1024
N_HEADS = 8
HEAD_DIM = 64
D_ATTN = N_HEADS * HEAD_DIM
D_RNN = 512
N_RNN_BLOCKS = 8
CONV_WIDTH = 4
LRU_C = 8.0
D_FF = 4 * D_MODEL
EPS = 1e-6

LANES = 128
SUBLANES = 8
KEY_BLOCK = 128
Q_ROWS = 128
FAST_BLOCKS = 3
Q_GROUP = 4
TOKEN_TILE = 512
SCAN_SLICES = 16
MXU_N = 256
LOG_KEEP_FLOOR = -104.0
VMEM_LIMIT = 56 * 1024 * 1024

NT_DIMS = (((1,), (1,)), ((), ()))


def _rms(xf, g):
    return xf * lax.rsqrt(jnp.mean(xf * xf, axis=-1, keepdims=True) + EPS) * g


def _softplus(x):
    return jnp.maximum(x, 0.0) + jnp.log(1.0 + jnp.exp(-jnp.abs(x)))


def _sigmoid(x):
    return 0.5 * jnp.tanh(0.5 * x) + 0.5


def _const_spec(shape):
    nd = len(shape)
    return pl.BlockSpec(shape, lambda *_: (0,) * nd, pipeline_mode=pl.Buffered(1))


def _rows(block_index, size):
    if isinstance(block_index, int):
        return pl.ds(block_index * size, size)
    return pl.ds(pl.multiple_of(block_index * size, size), size)


def _gelu_tanh(x):
    return 0.5 * x * (1.0 + jnp.tanh(np.sqrt(2.0 / np.pi).astype(np.float32) * (x + 0.044715 * (x * x * x))))


def _conv_seed(first, cs_ref, ext_ref):
    @pl.when(first)
    def _():
        ext_ref[0:SUBLANES, :] = jnp.zeros((SUBLANES, D_RNN), F32)
        ext_ref[SUBLANES - (CONV_WIDTH - 1):SUBLANES, :] = cs_ref[0]


def _scan_seed(first, h0_ref, hc_ref):
    @pl.when(first)
    def _():
        hc_ref[...] = jnp.broadcast_to(h0_ref[0], hc_ref.shape)


def _lru_conv(u, wc_ref, bc_ref, ext_ref):
    tt = u.shape[0]
    ext_ref[SUBLANES:SUBLANES + tt, :] = u
    uc = bc_ref[...] + wc_ref[CONV_WIDTH - 1:CONV_WIDTH, :] * u
    for d in range(1, CONV_WIDTH):
        uc = uc + wc_ref[CONV_WIDTH - 1 - d:CONV_WIDTH - d, :] * ext_ref[SUBLANES - d:SUBLANES - d + tt, :]
    ext_ref[0:SUBLANES, :] = ext_ref[tt:tt + SUBLANES, :]
    return uc


def _lru_gates(uc, wg_ref, bg_ref):
    return jnp.dot(uc.astype(BF16), wg_ref[...], preferred_element_type=F32) + bg_ref[...]


def _lru_scan(g, uc, g_rnn, lam_ref, h_prev):
    tt = uc.shape[0]
    r = _sigmoid(g[:, :D_RNN])
    ig = _sigmoid(g[:, D_RNN:])
    log_a = (-LRU_C * _softplus(-lam_ref[...])) * r
    a = jnp.exp(log_a)
    m2 = -jnp.tanh(log_a) * (a * a + 1.0)
    b = jnp.where(m2 > 0.0, m2 * lax.rsqrt(m2), 0.0) * (ig * uc)

    n_groups = tt // SUBLANES
    a3 = a.reshape(n_groups, SUBLANES, D_RNN)
    b3 = b.reshape(n_groups, SUBLANES, D_RNN)
    sub = lax.broadcasted_iota(jnp.int32, a3.shape, 1)
    k = 1
    while k < SUBLANES:
        has_prev = sub >= k
        b_prev = jnp.where(has_prev, pltpu.roll(b3, k, axis=1), 0.0)
        a_prev = jnp.where(has_prev, pltpu.roll(a3, k, axis=1), 1.0)
        b3 = b3 + a3 * b_prev
        a3 = a3 * a_prev
        k *= 2

    hs = []
    for gi in range(n_groups):
        h = b3[gi] + a3[gi] * h_prev
        hs.append(h)
        h_prev = jnp.broadcast_to(h[SUBLANES - 1:SUBLANES, :], h.shape)
    return (jnp.concatenate(hs, axis=0) * _gelu_tanh(g_rnn)).astype(BF16), h_prev


def _lru_kernel(u_ref, gr_ref, cs_ref, h0_ref, wc_ref, bc_ref, wg_ref, bg_ref, lam_ref,
                hg_ref, cn_ref, hl_ref, ext_ref, hc_ref):
    ti = pl.program_id(1)
    _conv_seed(ti == 0, cs_ref, ext_ref)
    _scan_seed(ti == 0, h0_ref, hc_ref)
    uc = _lru_conv(u_ref[0], wc_ref, bc_ref, ext_ref)
    hg, h_last = _lru_scan(_lru_gates(uc, wg_ref, bg_ref), uc, gr_ref[0], lam_ref, hc_ref[...])
    hc_ref[...] = h_last
    hg_ref[0] = hg

    @pl.when(ti == pl.num_programs(1) - 1)
    def _():
        cn_ref[0] = ext_ref[SUBLANES - (CONV_WIDTH - 1):SUBLANES, :]
        hl_ref[0] = h_last[0:1, :]


def _lru_state_specs(index_map):
    past = CONV_WIDTH - 1
    return (pl.BlockSpec((1, past, D_RNN), index_map), pl.BlockSpec((1, 1, D_RNN), index_map))


def _lru_scratch(tt):
    return [pltpu.VMEM((tt + SUBLANES, D_RNN), F32), pltpu.VMEM((SUBLANES, D_RNN), F32)]


def _lru(u, g_rnn, conv_state, h0, lru_w):
    s, t, _ = u.shape
    tt = min(t, TOKEN_TILE)
    assert t % tt == 0 and tt % SUBLANES == 0
    seq = pl.BlockSpec((1, tt, D_RNN), lambda i, j: (i, j, 0))
    st3, st1 = _lru_state_specs(lambda i, j: (i, 0, 0))
    return pl.pallas_call(
        _lru_kernel,
        grid=(s, t // tt),
        in_specs=[seq, seq, st3, st1] + [_const_spec(w.shape) for w in lru_w],
        out_specs=[seq, st3, st1],
        out_shape=[jax.ShapeDtypeStruct((s, t, D_RNN), BF16),
                   jax.ShapeDtypeStruct((s, CONV_WIDTH - 1, D_RNN), F32),
                   jax.ShapeDtypeStruct((s, 1, D_RNN), F32)],
        scratch_shapes=_lru_scratch(tt),
        compiler_params=pltpu.CompilerParams(dimension_semantics=("arbitrary", "arbitrary"),
                                             vmem_limit_bytes=VMEM_LIMIT),
        name="lru",
    )(u, g_rnn, conv_state, h0, *lru_w)


def _proj_kernel(x_ref, g_ref, wq_ref, wug_ref, wkv_ref, qb_ref, k_ref, v_ref, kb_ref, vb_ref, u_ref, gr_ref):
    xn = _rms(x_ref[...], g_ref[...]).astype(BF16)
    qb_ref[...] = jnp.dot(xn, wq_ref[...], preferred_element_type=F32).astype(BF16)
    ug = jnp.dot(xn, wug_ref[...], preferred_element_type=F32)
    u_ref[...] = ug[:, :D_RNN]
    gr_ref[...] = ug[:, D_RNN:]
    kv = jnp.dot(xn, wkv_ref[...], preferred_element_type=F32)
    k, v = kv[:, :D_ATTN], kv[:, D_ATTN:]
    k_ref[...] = k
    v_ref[...] = v
    kb_ref[...] = k.astype(BF16)
    vb_ref[...] = v.astype(BF16)


def _proj(x, g, wq, wug, wkv):
    s, t, _ = x.shape
    n = s * t
    tm = TOKEN_TILE
    assert n % tm == 0
    tok = lambda i: (i, 0)
    row_out = pl.BlockSpec((tm, D_ATTN), tok)
    f32o = jax.ShapeDtypeStruct((n, D_ATTN), F32)
    b16o = jax.ShapeDtypeStruct((n, D_ATTN), BF16)
    return pl.pallas_call(
        _proj_kernel,
        grid=(n // tm,),
        in_specs=[pl.BlockSpec((tm, D_MODEL), tok), _const_spec(g.shape), _const_spec(wq.shape),
                  _const_spec(wug.shape), _const_spec(wkv.shape)],
        out_specs=[row_out] * 7,
        out_shape=[b16o, f32o, f32o, b16o, b16o, f32o, f32o],
        compiler_params=pltpu.CompilerParams(dimension_semantics=("arbitrary",),
                                             vmem_limit_bytes=VMEM_LIMIT),
        name="proj",
    )(x.reshape(n, D_MODEL), g, wq, wug, wkv)


def _proj_lru_kernel(per, n_tiles, x_ref, g_ref, wq_ref, wug_ref, wkvt_ref, cs_ref, h0_ref, wc_ref, bc_ref,
                     wg_ref, bg_ref, lam_ref, qb_ref, kt_ref, vt_ref, ktb_ref, vtb_ref, hg_ref, cn_ref, hl_ref,
                     ext_ref, hc_ref, xn_ref, uc_ref, gr_ref, gt_ref):
    i = pl.program_id(0)
    dot_tile, scan_tile = i - 1, i - 2
    dot_live = jnp.logical_and(dot_tile >= 0, dot_tile < n_tiles)
    scan_live = jnp.logical_and(scan_tile >= 0, scan_tile < n_tiles)
    _conv_seed(jnp.logical_and(dot_live, dot_tile % per == 0), cs_ref, ext_ref)
    _scan_seed(jnp.logical_and(scan_live, scan_tile % per == 0), h0_ref, hc_ref)

    @pl.when(i == 0)
    def _():
        xn_ref[...] = jnp.zeros(xn_ref.shape, BF16)
        hc_ref[...] = jnp.zeros(hc_ref.shape, F32)
        ext_ref[...] = jnp.zeros(ext_ref.shape, F32)
        for ref in (uc_ref, gr_ref, gt_ref):
            ref[...] = jnp.zeros(ref.shape, F32)

    tm = x_ref.shape[0]
    sc, dt = i % 2, (i + 1) % 2
    xn = xn_ref[dt]
    state = [hc_ref[...]]
    rows_per = tm // SCAN_SLICES
    u_parts = {}
    uc_b = [None]

    def scan_slice(p):
        rows = pl.ds(p * rows_per, rows_per)
        hg, state[0] = _lru_scan(gt_ref[sc, rows, :], uc_ref[sc, rows, :], gr_ref[sc, rows, :], lam_ref, state[0])
        hg_ref[rows, :] = hg

    def rms_piece(r):
        rows = pl.ds(r * (tm // 4), tm // 4)
        xn_ref[sc, rows, :] = _rms(x_ref[rows, :], g_ref[...]).astype(BF16)

    def ug_chunk(c):
        cols = slice(c * MXU_N, (c + 1) * MXU_N)
        r = jnp.dot(xn, wug_ref[:, cols], preferred_element_type=F32)
        if c < D_RNN // MXU_N:
            u_parts[c] = r
        else:
            gr_ref[dt, :, c * MXU_N - D_RNN:(c + 1) * MXU_N - D_RNN] = r

    def conv():
        uc = _lru_conv(jnp.concatenate([u_parts[c] for c in sorted(u_parts)], axis=1), wc_ref, bc_ref, ext_ref)
        uc_ref[dt] = uc
        uc_b[0] = uc.astype(BF16)

    def q_chunk(c):
        cols = slice(c * MXU_N, (c + 1) * MXU_N)
        qb_ref[:, cols] = jnp.dot(xn, wq_ref[:, cols], preferred_element_type=F32).astype(BF16)

    def kv_chunk(c):
        toks = slice(c * MXU_N, (c + 1) * MXU_N)
        kv = lax.dot_general(wkvt_ref[...], xn[toks], NT_DIMS, preferred_element_type=F32)
        k, v = kv[:D_ATTN], kv[D_ATTN:]
        kt_ref[0, :, toks] = k
        vt_ref[0, :, toks] = v
        for j in range(MXU_N // KEY_BLOCK):
            cols = slice(j * KEY_BLOCK, (j + 1) * KEY_BLOCK)
            ktb_ref[0, c * (MXU_N // KEY_BLOCK) + j] = k[:, cols].astype(BF16)
            vtb_ref[0, c * (MXU_N // KEY_BLOCK) + j] = v[:, cols].astype(BF16)

    def gates_chunk(c):
        cols = slice(c * MXU_N, (c + 1) * MXU_N)
        gt_ref[dt, :, cols] = jnp.dot(uc_b[0], wg_ref[:, cols], preferred_element_type=F32) + bg_ref[:, cols]

    S, P = scan_slice, functools.partial
    program = [
        P(ug_chunk, 0), P(S, 0), P(S, 1), P(ug_chunk, 1), P(S, 2), P(S, 3), P(ug_chunk, 2), conv,
        P(ug_chunk, 3), P(S, 4), P(S, 5), P(q_chunk, 0), P(S, 6), P(S, 7),
        P(q_chunk, 1), P(rms_piece, 0), P(rms_piece, 1), P(S, 8),
        P(kv_chunk, 0), P(S, 9), P(S, 10), P(S, 11), P(rms_piece, 2),
        P(kv_chunk, 1), P(S, 12), P(S, 13), P(S, 14), P(rms_piece, 3),
        P(gates_chunk, 0), P(S, 15), P(gates_chunk, 1), P(gates_chunk, 2), P(gates_chunk, 3),
    ]
    for piece in program:
        piece()
    h_last = state[0]
    hc_ref[...] = h_last

    @pl.when(jnp.logical_and(dot_live, dot_tile % per == per - 1))
    def _():
        cn_ref[0] = ext_ref[SUBLANES - (CONV_WIDTH - 1):SUBLANES, :]

    @pl.when(jnp.logical_and(scan_live, scan_tile % per == per - 1))
    def _():
        hl_ref[0] = h_last[0:1, :]


def _proj_lru(x, g, wq, wug, wkvt, conv_state, h0, lru_w):
    s, t, _ = x.shape
    n = s * t
    tm = TOKEN_TILE
    assert t % tm == 0 and tm % KEY_BLOCK == 0
    per = t // tm
    n_tiles = n // tm
    tile = lambda i, lag: jnp.clip(i - lag, 0, n_tiles - 1)
    tok_in = pl.BlockSpec((tm, D_MODEL), lambda i: (tile(i, 0), 0))
    q_out = pl.BlockSpec((tm, D_ATTN), lambda i: (tile(i, 1), 0))
    hg_out = pl.BlockSpec((tm, D_RNN), lambda i: (tile(i, 2), 0))
    b16o = jax.ShapeDtypeStruct((n, D_ATTN), BF16)
    kv_spec = pl.BlockSpec((1, D_ATTN, tm), lambda i: (tile(i, 1) // per, 0, tile(i, 1) % per))
    kvb_spec = pl.BlockSpec((1, tm // KEY_BLOCK, D_ATTN, KEY_BLOCK),
                            lambda i: (tile(i, 1) // per, tile(i, 1) % per, 0, 0))
    kv_shape = jax.ShapeDtypeStruct((s, D_ATTN, t), F32)
    kvb_shape = jax.ShapeDtypeStruct((s, t // KEY_BLOCK, D_ATTN, KEY_BLOCK), BF16)
    cs_spec, _ = _lru_state_specs(lambda i: (tile(i, 1) // per, 0, 0))
    _, h_spec = _lru_state_specs(lambda i: (tile(i, 2) // per, 0, 0))
    consts = (g, wq, wug, wkvt)
    return pl.pallas_call(
        functools.partial(_proj_lru_kernel, per, n_tiles),
        grid=(n_tiles + 2,),
        in_specs=[tok_in] + [_const_spec(c.shape) for c in consts] + [cs_spec, h_spec]
        + [_const_spec(w.shape) for w in lru_w],
        out_specs=[q_out, kv_spec, kv_spec, kvb_spec, kvb_spec, hg_out, cs_spec, h_spec],
        out_shape=[b16o, kv_shape, kv_shape, kvb_shape, kvb_shape, b16o,
                   jax.ShapeDtypeStruct((s, CONV_WIDTH - 1, D_RNN), F32),
                   jax.ShapeDtypeStruct((s, 1, D_RNN), F32)],
        scratch_shapes=_lru_scratch(tm) + [pltpu.VMEM((2, tm, D_MODEL), BF16), pltpu.VMEM((2, tm, D_RNN), F32),
                                           pltpu.VMEM((2, tm, D_RNN), F32), pltpu.VMEM((2, tm, 2 * D_RNN), F32)],
        compiler_params=pltpu.CompilerParams(dimension_semantics=("arbitrary",),
                                             vmem_limit_bytes=VMEM_LIMIT),
        name="proj_lru",
    )(x.reshape(n, D_MODEL), *consts, conv_state, h0, *lru_w)


def _sb_split(z, mask):
    n = z.shape[1] // KEY_BLOCK
    tq = z.shape[0] // 2
    sp = _softplus(z)
    cats = []
    for h in range(2):
        for d in range(n):
            blk = sp[h * tq:(h + 1) * tq, d * KEY_BLOCK:(d + 1) * KEY_BLOCK]
            if d == n - 1 and mask is not None:
                blk = jnp.where(mask, blk, 0.0)
            hi = blk.astype(BF16)
            lo = (blk - hi.astype(F32)).astype(BF16)
            cats.append(jnp.concatenate([hi, lo], axis=1))
    return jnp.concatenate(cats, axis=0)


def _sb_cumsum(cat, tri2):
    return jnp.dot(cat, tri2, preferred_element_type=F32)


def _sb_weights(z, r, carries, mask):
    n = z.shape[1] // KEY_BLOCK
    tq = z.shape[0] // 2
    rows, new_carries = [], []
    for h in range(2):
        carry = None if carries is None else carries[h]
        es = [None] * n
        for d in reversed(range(n)):
            rd = r[(h * n + d) * tq:(h * n + d + 1) * tq]
            x = z[h * tq:(h + 1) * tq, d * KEY_BLOCK:(d + 1) * KEY_BLOCK] + rd[:, :KEY_BLOCK]
            if carry is not None:
                x = x + carry
            e = jnp.exp(x)
            if d == n - 1 and mask is not None:
                e = jnp.where(mask, e, 0.0)
            es[d] = e.astype(BF16)
            tot = rd[:, KEY_BLOCK:]
            carry = tot if carry is None else carry + tot
        rows.append(es[0] if n == 1 else jnp.concatenate(es, axis=1))
        new_carries.append(carry)
    return jnp.concatenate(rows, axis=0), new_carries


def _tri2():
    j = np.arange(KEY_BLOCK)[:, None]
    s = np.arange(KEY_BLOCK)[None, :]
    half = np.concatenate([-(j >= s).astype(np.float32), -np.ones((KEY_BLOCK, LANES), np.float32)], axis=1)
    return jnp.asarray(np.concatenate([half, half], axis=0), dtype=BF16)


def _stack_heads(q, lo_half):
    zero = jnp.zeros_like(q)
    return jnp.concatenate([jnp.where(lo_half, q, zero), jnp.where(lo_half, zero, q)], axis=0)


def _sb_window(z_fn, pv_fn, carries, mask, tri2):
    z = z_fn()
    e, cs = _sb_weights(z, _sb_cumsum(_sb_split(z, mask), tri2), carries, mask)
    pv = pv_fn(e)
    tq = pv.shape[0] // 2
    return (pv[:tq], cs[0]), (pv[tq:], cs[1])


def _finish(store, j0, outs, step_fn):
    (a0, c0), (a1, c1) = outs
    store(a0, a1)
    if isinstance(j0, int) and j0 < 0:
        return lambda: None

    def alive(x0, x1):
        return jnp.max(jnp.maximum(x0, x1)) >= LOG_KEEP_FLOOR

    go = jnp.logical_and(j0 >= 0, alive(c0, c1))

    def walk():
        @pl.when(go)
        def _():
            def body(s):
                j, b0, b1, d0, d1, _ = s
                (p0, d0), (p1, d1) = step_fn(j, [d0, d1])
                return j - 1, b0 + p0, b1 + p1, d0, d1, jnp.logical_and(j >= 1, alive(d0, d1))

            _, b0, b1, _, _, _ = lax.while_loop(lambda s: s[-1], body,
                                                (jnp.int32(j0), a0, a1, c0, c1, jnp.bool_(True)))
            store(b0, b1)

    return walk


def _sb_chains(chains, tri2):
    zs = [c[0]() for c in chains]
    sums = [_sb_cumsum(_sb_split(z, c[2]), tri2) for z, c in zip(zs, chains)]
    weights = [_sb_weights(z, r, None, c[2]) for z, r, c in zip(zs, sums, chains)]
    walks = []
    for c, (e, cs) in zip(chains, weights):
        pv = c[1](e)
        tq = pv.shape[0] // 2
        walks.append(_finish(c[3], c[5], ((pv[:tq], cs[0]), (pv[tq:], cs[1])), c[4]))
    for walk in walks:
        walk()


def _attn_prompt_kernel(q_ref, kt_ref, vt_ref, tri_ref, o_ref):
    t = q_ref.shape[1]
    nq = t // Q_ROWS
    row = lax.broadcasted_iota(jnp.int32, (Q_ROWS, LANES), 0)
    lane = lax.broadcasted_iota(jnp.int32, (Q_ROWS, LANES), 1)
    lo_half = lane < HEAD_DIM
    diag_mask = lane < row
    lane_minus_row = lane - row
    tri2 = tri_ref[...]

    def window(ref, first, n):
        return jnp.concatenate([ref[0, first + d] for d in range(n)], axis=1)

    def qblocks(blocks, n_fast):
        chains = []
        for i in blocks:
            q2 = _stack_heads(q_ref[0, _rows(i, Q_ROWS), :], lo_half)
            first = i + 1 - n_fast

            def z_fn(q2=q2, first=first):
                return jnp.dot(q2, window(kt_ref, first, n_fast), preferred_element_type=F32)

            def pv_fn(e, first=first):
                return lax.dot_general(e, window(vt_ref, first, n_fast), NT_DIMS, preferred_element_type=F32)

            def store(b0, b1, i=i):
                o_ref[0, _rows(i, Q_ROWS), :] = jnp.where(lo_half, b0, b1).astype(BF16)

            def step(j, cs, i=i, q2=q2):
                m = lane_minus_row < (i - j) * KEY_BLOCK
                return _sb_window(lambda: jnp.dot(q2, kt_ref[0, j], preferred_element_type=F32),
                                  lambda e: lax.dot_general(e, vt_ref[0, j], NT_DIMS, preferred_element_type=F32),
                                  cs, m, tri2)

            chains.append((z_fn, pv_fn, diag_mask, store, step, i - n_fast))
        _sb_chains(chains, tri2)

    n_head = FAST_BLOCKS - 1
    for i in range(n_head):
        qblocks([i], i + 1)

    start = n_head + (nq - n_head) % Q_GROUP
    if start > n_head:
        qblocks(list(range(n_head, start)), FAST_BLOCKS)

    def group(g, carry):
        qblocks([start + g * Q_GROUP + u for u in range(Q_GROUP)], FAST_BLOCKS)
        return carry

    lax.fori_loop(0, (nq - start) // Q_GROUP, group, 0)


def _attn_prompt(qb, ktb, vtb, tri2):
    b, t, _ = qb.shape
    assert t % Q_ROWS == 0 and Q_ROWS == KEY_BLOCK
    qspec = pl.BlockSpec((1, t, LANES), lambda i, h: (i, 0, h))
    kspec = pl.BlockSpec((1, t // KEY_BLOCK, LANES, KEY_BLOCK), lambda i, h: (i, 0, h, 0))
    return pl.pallas_call(
        _attn_prompt_kernel,
        grid=(b, D_ATTN // LANES),
        in_specs=[qspec, kspec, kspec, _const_spec(tri2.shape)],
        out_specs=qspec,
        out_shape=jax.ShapeDtypeStruct((b, t, D_ATTN), BF16),
        compiler_params=pltpu.CompilerParams(dimension_semantics=("arbitrary", "arbitrary"),
                                             vmem_limit_bytes=VMEM_LIMIT),
        name="attn_prompt",
    )(qb, ktb, vtb, tri2)


def _attn_sample_kernel(n_tail, q_ref, kn_ref, vn_ref, ckt_ref, cvt_ref, ck_hbm, cv_hbm, tri_ref, o_ref,
                        kbuf, vbuf):
    s = pl.program_id(0)
    tq = q_ref.shape[1]
    n_past = ck_hbm.shape[2] // KEY_BLOCK
    row = lax.broadcasted_iota(jnp.int32, (tq, LANES), 0)
    lane = lax.broadcasted_iota(jnp.int32, (tq, LANES), 1)
    lo_half = lane < HEAD_DIM
    tri2 = tri_ref[...]

    pad = jnp.zeros((KEY_BLOCK - tq, LANES), BF16)
    chains = []
    for p in range(q_ref.shape[2] // LANES):
        cols = slice(p * LANES, (p + 1) * LANES)
        q2 = _stack_heads(q_ref[0, :, cols], lo_half)

        def z_fn(q2=q2, cols=cols):
            z_new = lax.dot_general(q2, jnp.concatenate([kn_ref[0, :, cols], pad], axis=0), NT_DIMS,
                                    preferred_element_type=F32)
            if n_tail == 0:
                return z_new
            z_past = jnp.dot(q2, ckt_ref[0, cols, :].astype(BF16), preferred_element_type=F32)
            return jnp.concatenate([z_past, z_new], axis=1)

        def pv_fn(e, cols=cols):
            pv = jnp.dot(e[:, n_tail * KEY_BLOCK:], jnp.concatenate([vn_ref[0, :, cols], pad], axis=0),
                         preferred_element_type=F32)
            if n_tail == 0:
                return pv
            return pv + lax.dot_general(e[:, :n_tail * KEY_BLOCK], cvt_ref[0, cols, :].astype(BF16), NT_DIMS,
                                        preferred_element_type=F32)

        def store(a0, a1, cols=cols):
            o_ref[0, :, cols] = jnp.where(lo_half, a0, a1).astype(BF16)

        def step(j, cs, cols=cols, q2=q2, p=p):
            keys = pl.ds(pl.multiple_of(j * KEY_BLOCK, KEY_BLOCK), KEY_BLOCK)
            pltpu.sync_copy(ck_hbm.at[s, pl.ds(p * LANES, LANES), keys], kbuf)
            pltpu.sync_copy(cv_hbm.at[s, pl.ds(p * LANES, LANES), keys], vbuf)
            return _sb_window(lambda: jnp.dot(q2, kbuf[...].astype(BF16), preferred_element_type=F32),
                              lambda e: lax.dot_general(e, vbuf[...].astype(BF16), NT_DIMS,
                                                        preferred_element_type=F32),
                              cs, None, tri2)

        chains.append((z_fn, pv_fn, lane < row, store, step, n_past - 1 - n_tail))
    _sb_chains(chains, tri2)


def _attn_sample(qb, knb, vnb, cache_kt, cache_vt, tri2):
    s, tq, _ = qb.shape
    past_len = cache_kt.shape[2]
    assert past_len % KEY_BLOCK == 0 and tq <= KEY_BLOCK and tq % 16 == 0
    n_tail = min(FAST_BLOCKS - 1, past_len // KEY_BLOCK)
    assert n_tail > 0 and past_len % (n_tail * KEY_BLOCK) == 0
    new = pl.BlockSpec((1, tq, D_ATTN), lambda i: (i, 0, 0))
    tail = pl.BlockSpec((1, D_ATTN, n_tail * KEY_BLOCK), lambda i: (i, 0, past_len // (n_tail * KEY_BLOCK) - 1))
    hbm = pl.BlockSpec(memory_space=pl.ANY)
    return pl.pallas_call(
        functools.partial(_attn_sample_kernel, n_tail),
        grid=(s,),
        in_specs=[new, new, new, tail, tail, hbm, hbm, _const_spec(tri2.shape)],
        out_specs=new,
        out_shape=jax.ShapeDtypeStruct((s, tq, D_ATTN), BF16),
        scratch_shapes=[pltpu.VMEM((LANES, KEY_BLOCK), F32), pltpu.VMEM((LANES, KEY_BLOCK), F32)],
        compiler_params=pltpu.CompilerParams(dimension_semantics=("arbitrary",),
                                             vmem_limit_bytes=VMEM_LIMIT),
        name="attn_sample",
    )(qb, knb, vnb, cache_kt, cache_vt, cache_kt, cache_vt, tri2)


FF_CHUNK = 1024


def _out_kernel(x_ref, o_ref, hg_ref, g1_ref, wgab_ref, wa_ref, wb_ref, wo_ref, g2_ref, g3_ref,
                wup_ref, wdn_ref, g4_ref, y_ref):
    x = x_ref[...]
    xn = _rms(x, g1_ref[...]).astype(BF16)
    g_a = jnp.dot(xn, wgab_ref[:, :D_MODEL], preferred_element_type=F32)
    y_a = jnp.dot(o_ref[...], wa_ref[...], preferred_element_type=F32)
    m = jax.nn.sigmoid(g_a) * y_a
    g_b = jnp.dot(xn, wgab_ref[:, D_MODEL:], preferred_element_type=F32)
    y_b = jnp.dot(hg_ref[...], wb_ref[...], preferred_element_type=F32)
    m = m + jax.nn.sigmoid(g_b) * y_b
    mix = jnp.dot(m.astype(BF16), wo_ref[...], preferred_element_type=F32)
    x1 = x + _rms(mix, g2_ref[...])
    f = _rms(x1, g3_ref[...]).astype(BF16)
    acc = None
    for c in range(D_FF // FF_CHUNK):
        cols = slice(c * FF_CHUNK, (c + 1) * FF_CHUNK)
        up = jnp.maximum(jnp.dot(f, wup_ref[:, cols], preferred_element_type=F32), 0.0)
        dn = jnp.dot((up * up).astype(BF16), wdn_ref[cols, :], preferred_element_type=F32)
        acc = dn if acc is None else acc + dn
    y_ref[...] = x1 + _rms(acc, g4_ref[...])


def _out(x2d, o2d, hg2d, g1, wgab, wa, wb, wo, g2, g3, wup, wdn, g4):
    n = x2d.shape[0]
    tm = TOKEN_TILE
    assert n % tm == 0
    tok = lambda i: (i, 0)
    consts = (g1, wgab, wa, wb, wo, g2, g3, wup, wdn, g4)
    return pl.pallas_call(
        _out_kernel,
        grid=(n // tm,),
        in_specs=[pl.BlockSpec((tm, D_MODEL), tok), pl.BlockSpec((tm, D_ATTN), tok),
                  pl.BlockSpec((tm, D_RNN), tok)] + [_const_spec(c.shape) for c in consts],
        out_specs=pl.BlockSpec((tm, D_MODEL), tok),
        out_shape=jax.ShapeDtypeStruct((n, D_MODEL), F32),
        compiler_params=pltpu.CompilerParams(dimension_semantics=("arbitrary",),
                                             vmem_limit_bytes=VMEM_LIMIT),
        name="out",
    )(x2d, o2d, hg2d, *consts)


def _block_diag(w):
    n, c, d = w.shape
    return jnp.einsum("ncd,nm->ncmd", w, jnp.eye(n, dtype=w.dtype)).reshape(n * c, n * d)


def _heads_last(kt):
    s, _, t = kt.shape
    return kt.reshape(s, N_HEADS, HEAD_DIM, t).transpose(0, 3, 1, 2)[None]


def _time_last(cache):
    s, t = cache.shape[:2]
    return cache.transpose(0, 2, 3, 1).reshape(s, D_ATTN, t)


def _layer(x, caches, conv_state, h0, wts):
    (wq, wug, wkv, wkv_t, g1, lru_w, wgab, wa, wb, wo, g2, g3, wup, wdn, g4, tri2) = wts
    s, t, _ = x.shape
    x2d = x.reshape(s * t, D_MODEL)
    r3 = lambda a: a.reshape(s, t, a.shape[-1])
    if caches is None:
        qb, kt, vt, ktb, vtb, hg, conv_new, h_last = _proj_lru(x, g1, wq, wug, wkv_t, conv_state, h0, lru_w)
        o = _attn_prompt(r3(qb), ktb, vtb, tri2)
        k5, v5 = _heads_last(kt), _heads_last(vt)
    else:
        qb, k, v, kb, vb, u, g_rnn = _proj(x, g1, wq, wug, wkv)
        o = _attn_sample(r3(qb), r3(kb), r3(vb), _time_last(caches[0]), _time_last(caches[1]), tri2)
        k5, v5 = k.reshape(1, s, t, N_HEADS, HEAD_DIM), v.reshape(1, s, t, N_HEADS, HEAD_DIM)
        hg, conv_new, h_last = _lru(r3(u), r3(g_rnn), conv_state, h0, lru_w)
    y = _out(x2d, o.reshape(s * t, D_ATTN), hg.reshape(s * t, D_RNN), g1, wgab, wa, wb, wo, g2, g3,
             wup, wdn, g4)
    return y.reshape(s, t, D_MODEL), k5, v5, conv_new[None], h_last.reshape(1, s, D_RNN)


def kernel(x_prompt, x_sample, cache_k, cache_v, state_conv, state_h, w_in, g_pre_mix, w_conv, b_conv, w_r, b_r, w_i, b_i, lam, w_a_out, w_b_out, w_o, g_post_mix, g_pre_ffn, w_up, w_down, g_post_ffn):
    assert w_in.shape[0] == 1
    row = lambda a: a[0].reshape(1, -1)
    w_in0 = w_in[0]
    n_seq = 3 * D_ATTN + 2 * D_RNN
    wq = (w_in0[:, :D_ATTN] * (HEAD_DIM ** -0.5)).astype(BF16)
    wkv = w_in0[:, D_ATTN:3 * D_ATTN].astype(BF16)
    wug = w_in0[:, 3 * D_ATTN:n_seq].astype(BF16)
    wgab = w_in0[:, n_seq:].astype(BF16)
    wg = jnp.concatenate([_block_diag(w_r[0]), _block_diag(w_i[0])], axis=1).astype(BF16)
    bg = jnp.concatenate([row(b_r), row(b_i)], axis=1)
    lru_w = (w_conv[0], row(b_conv), wg, bg, row(lam))
    wts = (wq, wug, wkv, wkv.T, row(g_pre_mix), lru_w, wgab,
           w_a_out[0].astype(BF16), w_b_out[0].astype(BF16), w_o[0].astype(BF16), row(g_post_mix),
           row(g_pre_ffn), w_up[0].astype(BF16), w_down[0].astype(BF16), row(g_post_ffn), _tri2())

    bp = x_prompt.shape[0]
    zc = jnp.zeros((bp, CONV_WIDTH - 1, D_RNN), F32)
    zh = jnp.zeros((bp, 1, D_RNN), F32)
    yp, kp, vp, cp, hp = _layer(x_prompt, None, zc, zh, wts)

    bs = x_sample.shape[0]
    ys, ks, vs, cs, hs = _layer(x_sample, (cache_k[0], cache_v[0]), state_conv[0],
                                state_h[0].reshape(bs, 1, D_RNN), wts)
    return (yp, ys, kp, vp, cp, hp, ks, vs, cs, hs)
```

```python
import functools

import jax
import jax.numpy as jnp
import numpy as np
from jax import lax
from jax.experimental import pallas as pl
from jax.experimental.pallas import tpu as pltpu

F32 = jnp.float32
BF16 = jnp.bfloat16

D_MODEL = 1024
N_HEADS = 8
HEAD_DIM = 64
D_ATTN = N_HEADS * HEAD_DIM
D_RNN = 512
N_RNN_BLOCKS = 8
CONV_WIDTH = 4
LRU_C = 8.0
D_FF = 4 * D_MODEL
EPS = 1e-6

LANES = 128
SUBLANES = 8
KEY_BLOCK = 128
Q_ROWS = 128
FAST_BLOCKS = 3
Q_GROUP = 4
TOKEN_TILE = 512
SCAN_SLICES = 16
MXU_N = 256
LOG_KEEP_FLOOR = -104.0
VMEM_LIMIT = 56 * 1024 * 1024

NT_DIMS = (((1,), (1,)), ((), ()))


def _rms(xf, g):
    return xf * lax.rsqrt(jnp.mean(xf * xf, axis=-1, keepdims=True) + EPS) * g


def _softplus(x):
    return jnp.maximum(x, 0.0) + jnp.log(1.0 + jnp.exp(-jnp.abs(x)))


def _sigmoid(x):
    return 0.5 * jnp.tanh(0.5 * x) + 0.5


def _const_spec(shape):
    nd = len(shape)
    return pl.BlockSpec(shape, lambda *_: (0,) * nd, pipeline_mode=pl.Buffered(1))


def _rows(block_index, size):
    if isinstance(block_index, int):
        return pl.ds(block_index * size, size)
    return pl.ds(pl.multiple_of(block_index * size, size), size)


def _gelu_tanh(x):
    return 0.5 * x * (1.0 + jnp.tanh(np.sqrt(2.0 / np.pi).astype(np.float32) * (x + 0.044715 * (x * x * x))))


def _conv_seed(first, cs_ref, ext_ref):
    @pl.when(first)
    def _():
        ext_ref[0:SUBLANES, :] = jnp.zeros((SUBLANES, D_RNN), F32)
        ext_ref[SUBLANES - (CONV_WIDTH - 1):SUBLANES, :] = cs_ref[0]


def _scan_seed(first, h0_ref, hc_ref):
    @pl.when(first)
    def _():
        hc_ref[...] = jnp.broadcast_to(h0_ref[0], hc_ref.shape)


def _lru_conv(u, wc_ref, bc_ref, ext_ref):
    tt = u.shape[0]
    ext_ref[SUBLANES:SUBLANES + tt, :] = u
    uc = bc_ref[...] + wc_ref[CONV_WIDTH - 1:CONV_WIDTH, :] * u
    for d in range(1, CONV_WIDTH):
        uc = uc + wc_ref[CONV_WIDTH - 1 - d:CONV_WIDTH - d, :] * ext_ref[SUBLANES - d:SUBLANES - d + tt, :]
    ext_ref[0:SUBLANES, :] = ext_ref[tt:tt + SUBLANES, :]
    return uc


def _lru_gates(uc, wg_ref, bg_ref):
    return jnp.dot(uc.astype(BF16), wg_ref[...], preferred_element_type=F32) + bg_ref[...]


def _lru_scan(g, uc, g_rnn, lam_ref, h_prev):
    tt = uc.shape[0]
    r = _sigmoid(g[:, :D_RNN])
    ig = _sigmoid(g[:, D_RNN:])
    log_a = (-LRU_C * _softplus(-lam_ref[...])) * r
    a = jnp.exp(log_a)
    m2 = -jnp.tanh(log_a) * (a * a + 1.0)
    b = jnp.where(m2 > 0.0, m2 * lax.rsqrt(m2), 0.0) * (ig * uc)

    n_groups = tt // SUBLANES
    a3 = a.reshape(n_groups, SUBLANES, D_RNN)
    b3 = b.reshape(n_groups, SUBLANES, D_RNN)
    sub = lax.broadcasted_iota(jnp.int32, a3.shape, 1)
    k = 1
    while k < SUBLANES:
        has_prev = sub >= k
        b_prev = jnp.where(has_prev, pltpu.roll(b3, k, axis=1), 0.0)
        a_prev = jnp.where(has_prev, pltpu.roll(a3, k, axis=1), 1.0)
        b3 = b3 + a3 * b_prev
        a3 = a3 * a_prev
        k *= 2

    hs = []
    for gi in range(n_groups):
        h = b3[gi] + a3[gi] * h_prev
        hs.append(h)
        h_prev = jnp.broadcast_to(h[SUBLANES - 1:SUBLANES, :], h.shape)
    return (jnp.concatenate(hs, axis=0) * _gelu_tanh(g_rnn)).astype(BF16), h_prev


def _lru_kernel(u_ref, gr_ref, cs_ref, h0_ref, wc_ref, bc_ref, wg_ref, bg_ref, lam_ref,
                hg_ref, cn_ref, hl_ref, ext_ref, hc_ref):
    ti = pl.program_id(1)
    _conv_seed(ti == 0, cs_ref, ext_ref)
    _scan_seed(ti == 0, h0_ref, hc_ref)
    uc = _lru_conv(u_ref[0], wc_ref, bc_ref, ext_ref)
    hg, h_last = _lru_scan(_lru_gates(uc, wg_ref, bg_ref), uc, gr_ref[0], lam_ref, hc_ref[...])
    hc_ref[...] = h_last
    hg_ref[...] = hg

    @pl.when(ti == pl.num_programs(1) - 1)
    def _():
        cn_ref[0] = ext_ref[SUBLANES - (CONV_WIDTH - 1):SUBLANES, :]
        hl_ref[0] = h_last[0:1, :]


def _lru_state_specs(index_map):
    past = CONV_WIDTH - 1
    return (pl.BlockSpec((1, past, D_RNN), index_map), pl.BlockSpec((1, 1, D_RNN), index_map))


def _lru_scratch(tt):
    return [pltpu.VMEM((tt + SUBLANES, D_RNN), F32), pltpu.VMEM((SUBLANES, D_RNN), F32)]


def _lru(u, g_rnn, conv_state, h0, lru_w):
    s, t, _ = u.shape
    tt = min(t, TOKEN_TILE)
    assert t % tt == 0 and tt % SUBLANES == 0
    seq = pl.BlockSpec((1, tt, D_RNN), lambda i, j: (i, j, 0))
    st3, st1 = _lru_state_specs(lambda i, j: (i, 0, 0))
    return pl.pallas_call(
        _lru_kernel,
        grid=(s, t // tt),
        in_specs=[seq, seq, st3, st1] + [_const_spec(w.shape) for w in lru_w],
        out_specs=[pl.BlockSpec((tt, D_RNN), lambda i, j: (i * (t // tt) + j, 0)), st3, st1],
        out_shape=[jax.ShapeDtypeStruct((s * t, D_RNN), BF16),
                   jax.ShapeDtypeStruct((s, CONV_WIDTH - 1, D_RNN), F32),
                   jax.ShapeDtypeStruct((s, 1, D_RNN), F32)],
        scratch_shapes=_lru_scratch(tt),
        compiler_params=pltpu.CompilerParams(dimension_semantics=("arbitrary", "arbitrary"),
                                             vmem_limit_bytes=VMEM_LIMIT),
        name="lru",
    )(u, g_rnn, conv_state, h0, *lru_w)


def _proj_kernel(x_ref, g_ref, wq_ref, wug_ref, wkv_ref, qb_ref, k_ref, v_ref, kb_ref, vb_ref, u_ref, gr_ref):
    xn = _rms(x_ref[...], g_ref[...]).astype(BF16)
    qb_ref[...] = jnp.dot(xn, wq_ref[...], preferred_element_type=F32).astype(BF16)
    ug = jnp.dot(xn, wug_ref[...], preferred_element_type=F32)
    u_ref[...] = ug[:, :D_RNN]
    gr_ref[...] = ug[:, D_RNN:]
    kv = jnp.dot(xn, wkv_ref[...], preferred_element_type=F32)
    k, v = kv[:, :D_ATTN], kv[:, D_ATTN:]
    k_ref[...] = k
    v_ref[...] = v
    kb_ref[...] = k.astype(BF16)
    vb_ref[...] = v.astype(BF16)


def _proj(x, g, wq, wug, wkv):
    s, t, _ = x.shape
    n = s * t
    tm = TOKEN_TILE
    assert n % tm == 0
    tok = lambda i: (i, 0)
    row_out = pl.BlockSpec((tm, D_ATTN), tok)
    f32o = jax.ShapeDtypeStruct((n, D_ATTN), F32)
    b16o = jax.ShapeDtypeStruct((n, D_ATTN), BF16)
    return pl.pallas_call(
        _proj_kernel,
        grid=(n // tm,),
        in_specs=[pl.BlockSpec((tm, D_MODEL), tok), _const_spec(g.shape), _const_spec(wq.shape),
                  _const_spec(wug.shape), _const_spec(wkv.shape)],
        out_specs=[row_out] * 7,
        out_shape=[b16o, f32o, f32o, b16o, b16o, f32o, f32o],
        compiler_params=pltpu.CompilerParams(dimension_semantics=("arbitrary",),
                                             vmem_limit_bytes=VMEM_LIMIT),
        name="proj",
    )(x.reshape(n, D_MODEL), g, wq, wug, wkv)


def _proj_lru_kernel(per, n_tiles, x_ref, g_ref, wq_ref, wug_ref, wkvt_ref, cs_ref, h0_ref, wc_ref, bc_ref,
                     wg_ref, bg_ref, lam_ref, qb_ref, kt_ref, vt_ref, ktb_ref, vtb_ref, hg_ref, cn_ref, hl_ref,
                     ext_ref, hc_ref, xn_ref, uc_ref, gr_ref, gt_ref):
    i = pl.program_id(0)
    dot_tile, scan_tile = i - 1, i - 2
    dot_live = jnp.logical_and(dot_tile >= 0, dot_tile < n_tiles)
    scan_live = jnp.logical_and(scan_tile >= 0, scan_tile < n_tiles)
    _conv_seed(jnp.logical_and(dot_live, dot_tile % per == 0), cs_ref, ext_ref)
    _scan_seed(jnp.logical_and(scan_live, scan_tile % per == 0), h0_ref, hc_ref)

    @pl.when(i == 0)
    def _():
        xn_ref[...] = jnp.zeros(xn_ref.shape, BF16)
        hc_ref[...] = jnp.zeros(hc_ref.shape, F32)
        ext_ref[...] = jnp.zeros(ext_ref.shape, F32)
        for ref in (uc_ref, gr_ref, gt_ref):
            ref[...] = jnp.zeros(ref.shape, F32)

    tm = x_ref.shape[0]
    sc, dt = i % 2, (i + 1) % 2
    xn = xn_ref[dt]
    state = [hc_ref[...]]
    rows_per = tm // SCAN_SLICES
    u_parts = {}
    uc_b = [None]

    def scan_slice(p):
        rows = pl.ds(p * rows_per, rows_per)
        hg, state[0] = _lru_scan(gt_ref[sc, rows, :], uc_ref[sc, rows, :], gr_ref[sc, rows, :], lam_ref, state[0])
        hg_ref[rows, :] = hg

    def rms_piece(r):
        rows = pl.ds(r * (tm // 4), tm // 4)
        xn_ref[sc, rows, :] = _rms(x_ref[rows, :], g_ref[...]).astype(BF16)

    def ug_chunk(c):
        cols = slice(c * MXU_N, (c + 1) * MXU_N)
        r = jnp.dot(xn, wug_ref[:, cols], preferred_element_type=F32)
        if c < D_RNN // MXU_N:
            u_parts[c] = r
        else:
            gr_ref[dt, :, c * MXU_N - D_RNN:(c + 1) * MXU_N - D_RNN] = r

    def conv():
        uc = _lru_conv(jnp.concatenate([u_parts[c] for c in sorted(u_parts)], axis=1), wc_ref, bc_ref, ext_ref)
        uc_ref[dt] = uc
        uc_b[0] = uc.astype(BF16)

    def q_chunk(c):
        cols = slice(c * MXU_N, (c + 1) * MXU_N)
        qb_ref[:, cols] = jnp.dot(xn, wq_ref[:, cols], preferred_element_type=F32).astype(BF16)

    def kv_chunk(c):
        toks = slice(c * MXU_N, (c + 1) * MXU_N)
        kv = lax.dot_general(wkvt_ref[...], xn[toks], NT_DIMS, preferred_element_type=F32)
        k, v = kv[:D_ATTN], kv[D_ATTN:]
        kt_ref[0, :, toks] = k
        vt_ref[0, :, toks] = v
        for j in range(MXU_N // KEY_BLOCK):
            cols = slice(j * KEY_BLOCK, (j + 1) * KEY_BLOCK)
            ktb_ref[0, c * (MXU_N // KEY_BLOCK) + j] = k[:, cols].astype(BF16)
            vtb_ref[0, c * (MXU_N // KEY_BLOCK) + j] = v[:, cols].astype(BF16)

    def gates_chunk(c):
        cols = slice(c * MXU_N, (c + 1) * MXU_N)
        gt_ref[dt, :, cols] = jnp.dot(uc_b[0], wg_ref[:, cols], preferred_element_type=F32) + bg_ref[:, cols]

    S, P = scan_slice, functools.partial
    program = [
        P(ug_chunk, 0), P(S, 0), P(S, 1), P(ug_chunk, 1), P(S, 2), P(S, 3), P(ug_chunk, 2), conv,
        P(ug_chunk, 3), P(S, 4), P(S, 5), P(q_chunk, 0), P(S, 6), P(S, 7),
        P(q_chunk, 1), P(rms_piece, 0), P(rms_piece, 1), P(S, 8),
        P(kv_chunk, 0), P(S, 9), P(S, 10), P(S, 11), P(rms_piece, 2),
        P(kv_chunk, 1), P(S, 12), P(S, 13), P(S, 14), P(rms_piece, 3),
        P(gates_chunk, 0), P(S, 15), P(gates_chunk, 1), P(gates_chunk, 2), P(gates_chunk, 3),
    ]
    for piece in program:
        piece()
    h_last = state[0]
    hc_ref[...] = h_last

    @pl.when(jnp.logical_and(dot_live, dot_tile % per == per - 1))
    def _():
        cn_ref[0] = ext_ref[SUBLANES - (CONV_WIDTH - 1):SUBLANES, :]

    @pl.when(jnp.logical_and(scan_live, scan_tile % per == per - 1))
    def _():
        hl_ref[0] = h_last[0:1, :]


def _proj_lru(x, g, wq, wug, wkvt, conv_state, h0, lru_w):
    s, t, _ = x.shape
    n = s * t
    tm = TOKEN_TILE
    assert t % tm == 0 and tm % KEY_BLOCK == 0
    per = t // tm
    n_tiles = n // tm
    tile = lambda i, lag: jnp.clip(i - lag, 0, n_tiles - 1)
    tok_in = pl.BlockSpec((tm, D_MODEL), lambda i: (tile(i, 0), 0))
    q_out = pl.BlockSpec((tm, D_ATTN), lambda i: (tile(i, 1), 0))
    hg_out = pl.BlockSpec((tm, D_RNN), lambda i: (tile(i, 2), 0))
    b16o = jax.ShapeDtypeStruct((n, D_ATTN), BF16)
    kv_spec = pl.BlockSpec((1, D_ATTN, tm), lambda i: (tile(i, 1) // per, 0, tile(i, 1) % per))
    kvb_spec = pl.BlockSpec((1, tm // KEY_BLOCK, D_ATTN, KEY_BLOCK),
                            lambda i: (tile(i, 1) // per, tile(i, 1) % per, 0, 0))
    kv_shape = jax.ShapeDtypeStruct((s, D_ATTN, t), F32)
    kvb_shape = jax.ShapeDtypeStruct((s, t // KEY_BLOCK, D_ATTN, KEY_BLOCK), BF16)
    cs_spec, _ = _lru_state_specs(lambda i: (tile(i, 1) // per, 0, 0))
    _, h_spec = _lru_state_specs(lambda i: (tile(i, 2) // per, 0, 0))
    consts = (g, wq, wug, wkvt)
    return pl.pallas_call(
        functools.partial(_proj_lru_kernel, per, n_tiles),
        grid=(n_tiles + 2,),
        in_specs=[tok_in] + [_const_spec(c.shape) for c in consts] + [cs_spec, h_spec]
        + [_const_spec(w.shape) for w in lru_w],
        out_specs=[q_out, kv_spec, kv_spec, kvb_spec, kvb_spec, hg_out, cs_spec, h_spec],
        out_shape=[b16o, kv_shape, kv_shape, kvb_shape, kvb_shape, b16o,
                   jax.ShapeDtypeStruct((s, CONV_WIDTH - 1, D_RNN), F32),
                   jax.ShapeDtypeStruct((s, 1, D_RNN), F32)],
        scratch_shapes=_lru_scratch(tm) + [pltpu.VMEM((2, tm, D_MODEL), BF16), pltpu.VMEM((2, tm, D_RNN), F32),
                                           pltpu.VMEM((2, tm, D_RNN), F32), pltpu.VMEM((2, tm, 2 * D_RNN), F32)],
        compiler_params=pltpu.CompilerParams(dimension_semantics=("arbitrary",),
                                             vmem_limit_bytes=VMEM_LIMIT),
        name="proj_lru",
    )(x.reshape(n, D_MODEL), *consts, conv_state, h0, *lru_w)


def _sb_split(z, mask):
    n = z.shape[1] // KEY_BLOCK
    tq = z.shape[0] // 2
    sp = _softplus(z)
    cats = []
    for h in range(2):
        for d in range(n):
            blk = sp[h * tq:(h + 1) * tq, d * KEY_BLOCK:(d + 1) * KEY_BLOCK]
            if d == n - 1 and mask is not None:
                blk = jnp.where(mask, blk, 0.0)
            hi = blk.astype(BF16)
            lo = (blk - hi.astype(F32)).astype(BF16)
            cats.append(jnp.concatenate([hi, lo], axis=1))
    return jnp.concatenate(cats, axis=0)


def _sb_cumsum(cat, tri2):
    return jnp.dot(cat, tri2, preferred_element_type=F32)


def _sb_weights(z, r, carries, mask):
    n = z.shape[1] // KEY_BLOCK
    tq = z.shape[0] // 2
    rows, new_carries = [], []
    for h in range(2):
        carry = None if carries is None else carries[h]
        es = [None] * n
        for d in reversed(range(n)):
            rd = r[(h * n + d) * tq:(h * n + d + 1) * tq]
            x = z[h * tq:(h + 1) * tq, d * KEY_BLOCK:(d + 1) * KEY_BLOCK] + rd[:, :KEY_BLOCK]
            if carry is not None:
                x = x + carry
            e = jnp.exp(x)
            if d == n - 1 and mask is not None:
                e = jnp.where(mask, e, 0.0)
            es[d] = e.astype(BF16)
            tot = rd[:, KEY_BLOCK:]
            carry = tot if carry is None else carry + tot
        rows.append(es[0] if n == 1 else jnp.concatenate(es, axis=1))
        new_carries.append(carry)
    return jnp.concatenate(rows, axis=0), new_carries


def _tri2():
    j = np.arange(KEY_BLOCK)[:, None]
    s = np.arange(KEY_BLOCK)[None, :]
    half = np.concatenate([-(j >= s).astype(np.float32), -np.ones((KEY_BLOCK, LANES), np.float32)], axis=1)
    return jnp.asarray(np.concatenate([half, half], axis=0), dtype=BF16)


def _stack_heads(q, lo_half):
    zero = jnp.zeros_like(q)
    return jnp.concatenate([jnp.where(lo_half, q, zero), jnp.where(lo_half, zero, q)], axis=0)


def _sb_window(z_fn, pv_fn, carries, mask, tri2):
    z = z_fn()
    e, cs = _sb_weights(z, _sb_cumsum(_sb_split(z, mask), tri2), carries, mask)
    pv = pv_fn(e)
    tq = pv.shape[0] // 2
    return (pv[:tq], cs[0]), (pv[tq:], cs[1])


def _finish(store, j0, outs, step_fn):
    (a0, c0), (a1, c1) = outs
    store(a0, a1)
    if isinstance(j0, int) and j0 < 0:
        return lambda: None

    def alive(x0, x1):
        return jnp.max(jnp.maximum(x0, x1)) >= LOG_KEEP_FLOOR

    go = jnp.logical_and(j0 >= 0, alive(c0, c1))

    def walk():
        @pl.when(go)
        def _():
            def body(s):
                j, b0, b1, d0, d1, _ = s
                (p0, d0), (p1, d1) = step_fn(j, [d0, d1])
                return j - 1, b0 + p0, b1 + p1, d0, d1, jnp.logical_and(j >= 1, alive(d0, d1))

            _, b0, b1, _, _, _ = lax.while_loop(lambda s: s[-1], body,
                                                (jnp.int32(j0), a0, a1, c0, c1, jnp.bool_(True)))
            store(b0, b1)

    return walk


def _sb_chains(chains, tri2):
    zs = [c[0]() for c in chains]
    sums = [_sb_cumsum(_sb_split(z, c[2]), tri2) for z, c in zip(zs, chains)]
    weights = [_sb_weights(z, r, None, c[2]) for z, r, c in zip(zs, sums, chains)]
    walks = []
    for c, (e, cs) in zip(chains, weights):
        pv = c[1](e)
        tq = pv.shape[0] // 2
        walks.append(_finish(c[3], c[5], ((pv[:tq], cs[0]), (pv[tq:], cs[1])), c[4]))
    for walk in walks:
        walk()


def _attn_prompt_kernel(q_ref, kt_ref, vt_ref, tri_ref, o_ref):
    t = q_ref.shape[1]
    nq = t // Q_ROWS
    row = lax.broadcasted_iota(jnp.int32, (Q_ROWS, LANES), 0)
    lane = lax.broadcasted_iota(jnp.int32, (Q_ROWS, LANES), 1)
    lo_half = lane < HEAD_DIM
    diag_mask = lane < row
    lane_minus_row = lane - row
    tri2 = tri_ref[...]

    def window(ref, first, n):
        return jnp.concatenate([ref[0, first + d] for d in range(n)], axis=1)

    def qblocks(blocks, n_fast):
        chains = []
        for i in blocks:
            q2 = _stack_heads(q_ref[0, _rows(i, Q_ROWS), :], lo_half)
            first = i + 1 - n_fast

            def z_fn(q2=q2, first=first):
                return jnp.dot(q2, window(kt_ref, first, n_fast), preferred_element_type=F32)

            def pv_fn(e, first=first):
                return lax.dot_general(e, window(vt_ref, first, n_fast), NT_DIMS, preferred_element_type=F32)

            def store(b0, b1, i=i):
                o_ref[_rows(i, Q_ROWS), :] = jnp.where(lo_half, b0, b1).astype(BF16)

            def step(j, cs, i=i, q2=q2):
                m = lane_minus_row < (i - j) * KEY_BLOCK
                return _sb_window(lambda: jnp.dot(q2, kt_ref[0, j], preferred_element_type=F32),
                                  lambda e: lax.dot_general(e, vt_ref[0, j], NT_DIMS, preferred_element_type=F32),
                                  cs, m, tri2)

            chains.append((z_fn, pv_fn, diag_mask, store, step, i - n_fast))
        _sb_chains(chains, tri2)

    n_head = FAST_BLOCKS - 1
    for i in range(n_head):
        qblocks([i], i + 1)

    start = n_head + (nq - n_head) % Q_GROUP
    if start > n_head:
        qblocks(list(range(n_head, start)), FAST_BLOCKS)

    def group(g, carry):
        qblocks([start + g * Q_GROUP + u for u in range(Q_GROUP)], FAST_BLOCKS)
        return carry

    lax.fori_loop(0, (nq - start) // Q_GROUP, group, 0)


def _attn_prompt(qb, ktb, vtb, tri2):
    b, t, _ = qb.shape
    assert t % Q_ROWS == 0 and Q_ROWS == KEY_BLOCK
    qspec = pl.BlockSpec((1, t, LANES), lambda i, h: (i, 0, h))
    kspec = pl.BlockSpec((1, t // KEY_BLOCK, LANES, KEY_BLOCK), lambda i, h: (i, 0, h, 0))
    return pl.pallas_call(
        _attn_prompt_kernel,
        grid=(b, D_ATTN // LANES),
        in_specs=[qspec, kspec, kspec, _const_spec(tri2.shape)],
        out_specs=pl.BlockSpec((t, LANES), lambda i, h: (i, h)),
        out_shape=jax.ShapeDtypeStruct((b * t, D_ATTN), BF16),
        compiler_params=pltpu.CompilerParams(dimension_semantics=("arbitrary", "arbitrary"),
                                             vmem_limit_bytes=VMEM_LIMIT),
        name="attn_prompt",
    )(qb, ktb, vtb, tri2)


def _attn_sample_kernel(n_tail, q_ref, kn_ref, vn_ref, ckt_ref, cvt_ref, ck_hbm, cv_hbm, tri_ref, o_ref,
                        kbuf, vbuf):
    s = pl.program_id(0)
    tq = q_ref.shape[1]
    n_past = ck_hbm.shape[2] // KEY_BLOCK
    row = lax.broadcasted_iota(jnp.int32, (tq, LANES), 0)
    lane = lax.broadcasted_iota(jnp.int32, (tq, LANES), 1)
    lo_half = lane < HEAD_DIM
    tri2 = tri_ref[...]

    pad = jnp.zeros((KEY_BLOCK - tq, LANES), BF16)
    chains = []
    for p in range(q_ref.shape[2] // LANES):
        cols = slice(p * LANES, (p + 1) * LANES)
        q2 = _stack_heads(q_ref[0, :, cols], lo_half)

        def z_fn(q2=q2, cols=cols):
            z_new = lax.dot_general(q2, jnp.concatenate([kn_ref[0, :, cols], pad], axis=0), NT_DIMS,
                                    preferred_element_type=F32)
            if n_tail == 0:
                return z_new
            z_past = jnp.dot(q2, ckt_ref[0, cols, :].astype(BF16), preferred_element_type=F32)
            return jnp.concatenate([z_past, z_new], axis=1)

        def pv_fn(e, cols=cols):
            pv = jnp.dot(e[:, n_tail * KEY_BLOCK:], jnp.concatenate([vn_ref[0, :, cols], pad], axis=0),
                         preferred_element_type=F32)
            if n_tail == 0:
                return pv
            return pv + lax.dot_general(e[:, :n_tail * KEY_BLOCK], cvt_ref[0, cols, :].astype(BF16), NT_DIMS,
                                        preferred_element_type=F32)

        def store(a0, a1, cols=cols):
            o_ref[:, cols] = jnp.where(lo_half, a0, a1).astype(BF16)

        def step(j, cs, cols=cols, q2=q2, p=p):
            keys = pl.ds(pl.multiple_of(j * KEY_BLOCK, KEY_BLOCK), KEY_BLOCK)
            pltpu.sync_copy(ck_hbm.at[s, pl.ds(p * LANES, LANES), keys], kbuf)
            pltpu.sync_copy(cv_hbm.at[s, pl.ds(p * LANES, LANES), keys], vbuf)
            return _sb_window(lambda: jnp.dot(q2, kbuf[...].astype(BF16), preferred_element_type=F32),
                              lambda e: lax.dot_general(e, vbuf[...].astype(BF16), NT_DIMS,
                                                        preferred_element_type=F32),
                              cs, None, tri2)

        chains.append((z_fn, pv_fn, lane < row, store, step, n_past - 1 - n_tail))
    _sb_chains(chains, tri2)


def _attn_sample(qb, knb, vnb, cache_kt, cache_vt, tri2):
    s, tq, _ = qb.shape
    past_len = cache_kt.shape[2]
    assert past_len % KEY_BLOCK == 0 and tq <= KEY_BLOCK and tq % 16 == 0
    n_tail = min(FAST_BLOCKS - 1, past_len // KEY_BLOCK)
    assert n_tail > 0 and past_len % (n_tail * KEY_BLOCK) == 0
    new = pl.BlockSpec((1, tq, D_ATTN), lambda i: (i, 0, 0))
    tail = pl.BlockSpec((1, D_ATTN, n_tail * KEY_BLOCK), lambda i: (i, 0, past_len // (n_tail * KEY_BLOCK) - 1))
    hbm = pl.BlockSpec(memory_space=pl.ANY)
    return pl.pallas_call(
        functools.partial(_attn_sample_kernel, n_tail),
        grid=(s,),
        in_specs=[new, new, new, tail, tail, hbm, hbm, _const_spec(tri2.shape)],
        out_specs=pl.BlockSpec((tq, D_ATTN), lambda i: (i, 0)),
        out_shape=jax.ShapeDtypeStruct((s * tq, D_ATTN), BF16),
        scratch_shapes=[pltpu.VMEM((LANES, KEY_BLOCK), F32), pltpu.VMEM((LANES, KEY_BLOCK), F32)],
        compiler_params=pltpu.CompilerParams(dimension_semantics=("arbitrary",),
                                             vmem_limit_bytes=VMEM_LIMIT),
        name="attn_sample",
    )(qb, knb, vnb, cache_kt, cache_vt, cache_kt, cache_vt, tri2)


FF_CHUNK = 1024


def _out_kernel(n_first, xa_ref, xb_ref, oa_ref, ob_ref, hga_ref, hgb_ref, g1_ref, wgab_ref, wa_ref, wb_ref,
                wo_ref, g2_ref, g3_ref, wup_ref, wdn_ref, g4_ref, ya_ref, yb_ref):
    first = pl.program_id(0) < n_first
    pick = lambda a_ref, b_ref: jnp.where(first, a_ref[...], b_ref[...])
    x = pick(xa_ref, xb_ref)
    xn = _rms(x, g1_ref[...]).astype(BF16)
    g_a = jnp.dot(xn, wgab_ref[:, :D_MODEL], preferred_element_type=F32)
    y_a = jnp.dot(pick(oa_ref, ob_ref), wa_ref[...], preferred_element_type=F32)
    m = jax.nn.sigmoid(g_a) * y_a
    g_b = jnp.dot(xn, wgab_ref[:, D_MODEL:], preferred_element_type=F32)
    y_b = jnp.dot(pick(hga_ref, hgb_ref), wb_ref[...], preferred_element_type=F32)
    m = m + jax.nn.sigmoid(g_b) * y_b
    mix = jnp.dot(m.astype(BF16), wo_ref[...], preferred_element_type=F32)
    x1 = x + _rms(mix, g2_ref[...])
    f = _rms(x1, g3_ref[...]).astype(BF16)
    acc = None
    for c in range(D_FF // FF_CHUNK):
        cols = slice(c * FF_CHUNK, (c + 1) * FF_CHUNK)
        up = jnp.maximum(jnp.dot(f, wup_ref[:, cols], preferred_element_type=F32), 0.0)
        dn = jnp.dot((up * up).astype(BF16), wdn_ref[cols, :], preferred_element_type=F32)
        acc = dn if acc is None else acc + dn
    y = x1 + _rms(acc, g4_ref[...])

    @pl.when(first)
    def _():
        ya_ref[...] = y

    @pl.when(jnp.logical_not(first))
    def _():
        yb_ref[...] = y


def _out(xa, xb, oa, ob, hga, hgb, g1, wgab, wa, wb, wo, g2, g3, wup, wdn, g4):
    na, nb = xa.shape[0], xb.shape[0]
    tm = TOKEN_TILE
    assert na % tm == 0 and nb % tm == 0
    n_first = na // tm
    a_tok = lambda i: (jnp.minimum(i, n_first - 1), 0)
    b_tok = lambda i: (jnp.maximum(i - n_first, 0), 0)
    consts = (g1, wgab, wa, wb, wo, g2, g3, wup, wdn, g4)
    return pl.pallas_call(
        functools.partial(_out_kernel, n_first),
        grid=((na + nb) // tm,),
        in_specs=[pl.BlockSpec((tm, D_MODEL), a_tok), pl.BlockSpec((tm, D_MODEL), b_tok),
                  pl.BlockSpec((tm, D_ATTN), a_tok), pl.BlockSpec((tm, D_ATTN), b_tok),
                  pl.BlockSpec((tm, D_RNN), a_tok), pl.BlockSpec((tm, D_RNN), b_tok)]
        + [_const_spec(c.shape) for c in consts],
        out_specs=[pl.BlockSpec((tm, D_MODEL), a_tok), pl.BlockSpec((tm, D_MODEL), b_tok)],
        out_shape=[jax.ShapeDtypeStruct((na, D_MODEL), F32), jax.ShapeDtypeStruct((nb, D_MODEL), F32)],
        compiler_params=pltpu.CompilerParams(dimension_semantics=("arbitrary",),
                                             vmem_limit_bytes=VMEM_LIMIT),
        name="out",
    )(xa, xb, oa, ob, hga, hgb, *consts)


def _block_diag(w):
    n, c, d = w.shape
    return jnp.einsum("ncd,nm->ncmd", w, jnp.eye(n, dtype=w.dtype)).reshape(n * c, n * d)


def _heads_last(kt):
    s, _, t = kt.shape
    return kt.reshape(s, N_HEADS, HEAD_DIM, t).transpose(0, 3, 1, 2)[None]


def _time_last(cache):
    s, t = cache.shape[:2]
    return cache.transpose(0, 2, 3, 1).reshape(s, D_ATTN, t)


def kernel(x_prompt, x_sample, cache_k, cache_v, state_conv, state_h, w_in, g_pre_mix, w_conv, b_conv, w_r, b_r, w_i, b_i, lam, w_a_out, w_b_out, w_o, g_post_mix, g_pre_ffn, w_up, w_down, g_post_ffn):
    assert w_in.shape[0] == 1
    row = lambda a: a[0].reshape(1, -1)
    w_in0 = w_in[0]
    n_seq = 3 * D_ATTN + 2 * D_RNN
    wq = (w_in0[:, :D_ATTN] * (HEAD_DIM ** -0.5)).astype(BF16)
    wkv = w_in0[:, D_ATTN:3 * D_ATTN].astype(BF16)
    wug = w_in0[:, 3 * D_ATTN:n_seq].astype(BF16)
    wgab = w_in0[:, n_seq:].astype(BF16)
    wg = jnp.concatenate([_block_diag(w_r[0]), _block_diag(w_i[0])], axis=1).astype(BF16)
    bg = jnp.concatenate([row(b_r), row(b_i)], axis=1)
    lru_w = (w_conv[0], row(b_conv), wg, bg, row(lam))
    g1 = row(g_pre_mix)
    tri2 = _tri2()

    bp, tp, _ = x_prompt.shape
    bs, ts, _ = x_sample.shape
    n_p, n_s = bp * tp, bs * ts

    zc = jnp.zeros((bp, CONV_WIDTH - 1, D_RNN), F32)
    zh = jnp.zeros((bp, 1, D_RNN), F32)
    qb, kt, vt, ktb, vtb, hg_p, cp, hp = _proj_lru(x_prompt, g1, wq, wug, wkv.T, zc, zh, lru_w)
    o_p = _attn_prompt(qb.reshape(bp, tp, D_ATTN), ktb, vtb, tri2)

    s3 = lambda a: a.reshape(bs, ts, a.shape[-1])
    qs, ks, vs, ksb, vsb, us, grs = _proj(x_sample, g1, wq, wug, wkv)
    o_s = _attn_sample(s3(qs), s3(ksb), s3(vsb), _time_last(cache_k[0]), _time_last(cache_v[0]), tri2)
    hg_s, cs, hs = _lru(s3(us), s3(grs), state_conv[0], state_h[0].reshape(bs, 1, D_RNN), lru_w)

    yp, ys = _out(x_prompt.reshape(n_p, D_MODEL), x_sample.reshape(n_s, D_MODEL), o_p, o_s, hg_p, hg_s, g1, wgab,
                  w_a_out[0].astype(BF16), w_b_out[0].astype(BF16), w_o[0].astype(BF16), row(g_post_mix),
                  row(g_pre_ffn), w_up[0].astype(BF16), w_down[0].astype(BF16), row(g_post_ffn))
    return (yp.reshape(bp, tp, D_MODEL), ys.reshape(bs, ts, D_MODEL), _heads_last(kt), _heads_last(vt),
            cp[None], hp.reshape(1, bp, D_RNN), ks.reshape(1, bs, ts, N_HEADS, HEAD_DIM),
            vs.reshape(1, bs, ts, N_HEADS, HEAD_DIM), cs[None], hs.reshape(1, bs, D_RNN))
```

```python
import functools

import jax
import jax.numpy as jnp
import numpy as np
from jax import lax
from jax.experimental import pallas as pl
from jax.experimental.pallas import tpu as pltpu

F32 = jnp.float32
BF16 = jnp.bfloat16

D_MODEL = 1024
N_HEADS = 8
HEAD_DIM = 64
D_ATTN = N_HEADS * HEAD_DIM
D_RNN = 512
N_RNN_BLOCKS = 8
CONV_WIDTH = 4
LRU_C = 8.0
D_FF = 4 * D_MODEL
EPS = 1e-6

LANES = 128
SUBLANES = 8
KEY_BLOCK = 128
Q_ROWS = 128
FAST_BLOCKS = 3
Q_GROUP = 4
TOKEN_TILE = 512
SCAN_SLICES = 16
MXU_N = 256
LOG_KEEP_FLOOR = -104.0
VMEM_LIMIT = 56 * 1024 * 1024

NT_DIMS = (((1,), (1,)), ((), ()))


def _rms(xf, g):
    return xf * lax.rsqrt(jnp.mean(xf * xf, axis=-1, keepdims=True) + EPS) * g


def _softplus(x):
    return jnp.maximum(x, 0.0) + jnp.log(1.0 + jnp.exp(-jnp.abs(x)))


def _sigmoid(x):
    return 0.5 * jnp.tanh(0.5 * x) + 0.5


def _const_spec(shape):
    nd = len(shape)
    return pl.BlockSpec(shape, lambda *_: (0,) * nd, pipeline_mode=pl.Buffered(1))


def _rows(block_index, size):
    if isinstance(block_index, int):
        return pl.ds(block_index * size, size)
    return pl.ds(pl.multiple_of(block_index * size, size), size)


def _gelu_tanh(x):
    return 0.5 * x * (1.0 + jnp.tanh(np.sqrt(2.0 / np.pi).astype(np.float32) * (x + 0.044715 * (x * x * x))))


def _conv_seed(first, cs_ref, ext_ref):
    @pl.when(first)
    def _():
        ext_ref[0:SUBLANES, :] = jnp.zeros((SUBLANES, D_RNN), F32)
        ext_ref[SUBLANES - (CONV_WIDTH - 1):SUBLANES, :] = cs_ref[0]


def _scan_seed(first, h0_ref, hc_ref):
    @pl.when(first)
    def _():
        hc_ref[...] = jnp.broadcast_to(h0_ref[0], hc_ref.shape)


def _lru_conv(u, wc_ref, bc_ref, ext_ref):
    tt = u.shape[0]
    ext_ref[SUBLANES:SUBLANES + tt, :] = u
    uc = bc_ref[...] + wc_ref[CONV_WIDTH - 1:CONV_WIDTH, :] * u
    for d in range(1, CONV_WIDTH):
        uc = uc + wc_ref[CONV_WIDTH - 1 - d:CONV_WIDTH - d, :] * ext_ref[SUBLANES - d:SUBLANES - d + tt, :]
    ext_ref[0:SUBLANES, :] = ext_ref[tt:tt + SUBLANES, :]
    return uc


def _lru_gates(uc, wg_ref, bg_ref):
    return jnp.dot(uc.astype(BF16), wg_ref[...], preferred_element_type=F32) + bg_ref[...]


def _lru_scan(g, uc, g_rnn, lam_ref, h_prev):
    tt = uc.shape[0]
    r = _sigmoid(g[:, :D_RNN])
    ig = _sigmoid(g[:, D_RNN:])
    log_a = (-LRU_C * _softplus(-lam_ref[...])) * r
    a = jnp.exp(log_a)
    m2 = -jnp.tanh(log_a) * (a * a + 1.0)
    b = jnp.where(m2 > 0.0, m2 * lax.rsqrt(m2), 0.0) * (ig * uc)

    n_groups = tt // SUBLANES
    a3 = a.reshape(n_groups, SUBLANES, D_RNN)
    b3 = b.reshape(n_groups, SUBLANES, D_RNN)
    sub = lax.broadcasted_iota(jnp.int32, a3.shape, 1)
    k = 1
    while k < SUBLANES:
        has_prev = sub >= k
        b_prev = jnp.where(has_prev, pltpu.roll(b3, k, axis=1), 0.0)
        a_prev = jnp.where(has_prev, pltpu.roll(a3, k, axis=1), 1.0)
        b3 = b3 + a3 * b_prev
        a3 = a3 * a_prev
        k *= 2

    hs = []
    for gi in range(n_groups):
        h = b3[gi] + a3[gi] * h_prev
        hs.append(h)
        h_prev = jnp.broadcast_to(h[SUBLANES - 1:SUBLANES, :], h.shape)
    return (jnp.concatenate(hs, axis=0) * _gelu_tanh(g_rnn)).astype(BF16), h_prev


def _lru_kernel(u_ref, gr_ref, cs_ref, h0_ref, wc_ref, bc_ref, wg_ref, bg_ref, lam_ref,
                hg_ref, cn_ref, hl_ref, ext_ref, hc_ref):
    ti = pl.program_id(1)
    _conv_seed(ti == 0, cs_ref, ext_ref)
    _scan_seed(ti == 0, h0_ref, hc_ref)
    uc = _lru_conv(u_ref[0], wc_ref, bc_ref, ext_ref)
    hg, h_last = _lru_scan(_lru_gates(uc, wg_ref, bg_ref), uc, gr_ref[0], lam_ref, hc_ref[...])
    hc_ref[...] = h_last
    hg_ref[...] = hg

    @pl.when(ti == pl.num_programs(1) - 1)
    def _():
        cn_ref[0] = ext_ref[SUBLANES - (CONV_WIDTH - 1):SUBLANES, :]
        hl_ref[0] = h_last[0:1, :]


def _lru_state_specs(index_map):
    past = CONV_WIDTH - 1
    return (pl.BlockSpec((1, past, D_RNN), index_map), pl.BlockSpec((1, 1, D_RNN), index_map))


def _lru_scratch(tt):
    return [pltpu.VMEM((tt + SUBLANES, D_RNN), F32), pltpu.VMEM((SUBLANES, D_RNN), F32)]


def _lru(u, g_rnn, conv_state, h0, lru_w):
    s, t, _ = u.shape
    tt = min(t, TOKEN_TILE)
    assert t % tt == 0 and tt % SUBLANES == 0
    seq = pl.BlockSpec((1, tt, D_RNN), lambda i, j: (i, j, 0))
    st3, st1 = _lru_state_specs(lambda i, j: (i, 0, 0))
    return pl.pallas_call(
        _lru_kernel,
        grid=(s, t // tt),
        in_specs=[seq, seq, st3, st1] + [_const_spec(w.shape) for w in lru_w],
        out_specs=[pl.BlockSpec((tt, D_RNN), lambda i, j: (i * (t // tt) + j, 0)), st3, st1],
        out_shape=[jax.ShapeDtypeStruct((s * t, D_RNN), BF16),
                   jax.ShapeDtypeStruct((s, CONV_WIDTH - 1, D_RNN), F32),
                   jax.ShapeDtypeStruct((s, 1, D_RNN), F32)],
        scratch_shapes=_lru_scratch(tt),
        compiler_params=pltpu.CompilerParams(dimension_semantics=("arbitrary", "arbitrary"),
                                             vmem_limit_bytes=VMEM_LIMIT),
        name="lru",
    )(u, g_rnn, conv_state, h0, *lru_w)


def _proj_kernel(x_ref, g_ref, wq_ref, wug_ref, wkv_ref, qb_ref, k_ref, v_ref, kb_ref, vb_ref, u_ref, gr_ref):
    xn = _rms(x_ref[...], g_ref[...]).astype(BF16)
    qb_ref[...] = jnp.dot(xn, wq_ref[...], preferred_element_type=F32).astype(BF16)
    ug = jnp.dot(xn, wug_ref[...], preferred_element_type=F32)
    u_ref[...] = ug[:, :D_RNN]
    gr_ref[...] = ug[:, D_RNN:]
    kv = jnp.dot(xn, wkv_ref[...], preferred_element_type=F32)
    k, v = kv[:, :D_ATTN], kv[:, D_ATTN:]
    k_ref[...] = k
    v_ref[...] = v
    kb_ref[...] = k.astype(BF16)
    vb_ref[...] = v.astype(BF16)


def _proj(x, g, wq, wug, wkv):
    s, t, _ = x.shape
    n = s * t
    tm = TOKEN_TILE
    assert n % tm == 0
    tok = lambda i: (i, 0)
    row_out = pl.BlockSpec((tm, D_ATTN), tok)
    f32o = jax.ShapeDtypeStruct((n, D_ATTN), F32)
    b16o = jax.ShapeDtypeStruct((n, D_ATTN), BF16)
    return pl.pallas_call(
        _proj_kernel,
        grid=(n // tm,),
        in_specs=[pl.BlockSpec((tm, D_MODEL), tok), _const_spec(g.shape), _const_spec(wq.shape),
                  _const_spec(wug.shape), _const_spec(wkv.shape)],
        out_specs=[row_out] * 7,
        out_shape=[b16o, f32o, f32o, b16o, b16o, f32o, f32o],
        compiler_params=pltpu.CompilerParams(dimension_semantics=("arbitrary",),
                                             vmem_limit_bytes=VMEM_LIMIT),
        name="proj",
    )(x.reshape(n, D_MODEL), g, wq, wug, wkv)


def _proj_lru_kernel(per, n_tiles, x_ref, g_ref, wq_ref, wug_ref, wkvt_ref, cs_ref, h0_ref, wc_ref, bc_ref,
                     wg_ref, bg_ref, lam_ref, qb_ref, kt_ref, vt_ref, ktb_ref, vtb_ref, hg_ref, cn_ref, hl_ref,
                     ext_ref, hc_ref, xn_ref, uc_ref, gr_ref, gt_ref):
    i = pl.program_id(0)
    dot_tile, scan_tile = i - 1, i - 2
    dot_live = jnp.logical_and(dot_tile >= 0, dot_tile < n_tiles)
    scan_live = jnp.logical_and(scan_tile >= 0, scan_tile < n_tiles)
    _conv_seed(jnp.logical_and(dot_live, dot_tile % per == 0), cs_ref, ext_ref)
    _scan_seed(jnp.logical_and(scan_live, scan_tile % per == 0), h0_ref, hc_ref)

    @pl.when(i == 0)
    def _():
        xn_ref[...] = jnp.zeros(xn_ref.shape, BF16)
        hc_ref[...] = jnp.zeros(hc_ref.shape, F32)
        ext_ref[...] = jnp.zeros(ext_ref.shape, F32)
        for ref in (uc_ref, gr_ref, gt_ref):
            ref[...] = jnp.zeros(ref.shape, F32)

    tm = x_ref.shape[0]
    sc, dt = i % 2, (i + 1) % 2
    xn = xn_ref[dt]
    state = [hc_ref[...]]
    rows_per = tm // SCAN_SLICES
    u_parts = {}
    uc_b = [None]

    def scan_slice(p):
        rows = pl.ds(p * rows_per, rows_per)
        hg, state[0] = _lru_scan(gt_ref[sc, rows, :], uc_ref[sc, rows, :], gr_ref[sc, rows, :], lam_ref, state[0])
        hg_ref[rows, :] = hg

    def rms_piece(r):
        rows = pl.ds(r * (tm // 4), tm // 4)
        xn_ref[sc, rows, :] = _rms(x_ref[rows, :], g_ref[...]).astype(BF16)

    def ug_chunk(c):
        cols = slice(c * MXU_N, (c + 1) * MXU_N)
        r = jnp.dot(xn, wug_ref[:, cols], preferred_element_type=F32)
        if c < D_RNN // MXU_N:
            u_parts[c] = r
        else:
            gr_ref[dt, :, c * MXU_N - D_RNN:(c + 1) * MXU_N - D_RNN] = r

    def conv():
        uc = _lru_conv(jnp.concatenate([u_parts[c] for c in sorted(u_parts)], axis=1), wc_ref, bc_ref, ext_ref)
        uc_ref[dt] = uc
        uc_b[0] = uc.astype(BF16)

    def q_chunk(c):
        cols = slice(c * MXU_N, (c + 1) * MXU_N)
        qb_ref[:, cols] = jnp.dot(xn, wq_ref[:, cols], preferred_element_type=F32).astype(BF16)

    def kv_chunk(c):
        toks = slice(c * MXU_N, (c + 1) * MXU_N)
        kv = lax.dot_general(wkvt_ref[...], xn[toks], NT_DIMS, preferred_element_type=F32)
        k, v = kv[:D_ATTN], kv[D_ATTN:]
        kt_ref[0, :, toks] = k
        vt_ref[0, :, toks] = v
        for j in range(MXU_N // KEY_BLOCK):
            cols = slice(j * KEY_BLOCK, (j + 1) * KEY_BLOCK)
            ktb_ref[0, c * (MXU_N // KEY_BLOCK) + j] = k[:, cols].astype(BF16)
            vtb_ref[0, c * (MXU_N // KEY_BLOCK) + j] = v[:, cols].astype(BF16)

    def gates_chunk(c):
        cols = slice(c * MXU_N, (c + 1) * MXU_N)
        gt_ref[dt, :, cols] = jnp.dot(uc_b[0], wg_ref[:, cols], preferred_element_type=F32) + bg_ref[:, cols]

    S, P = scan_slice, functools.partial
    program = [
        P(ug_chunk, 0), P(S, 0), P(S, 1), P(ug_chunk, 1), P(S, 2), P(S, 3), P(ug_chunk, 2), conv,
        P(ug_chunk, 3), P(S, 4), P(S, 5), P(q_chunk, 0), P(S, 6), P(S, 7),
        P(q_chunk, 1), P(rms_piece, 0), P(rms_piece, 1), P(S, 8),
        P(kv_chunk, 0), P(S, 9), P(S, 10), P(S, 11), P(rms_piece, 2),
        P(kv_chunk, 1), P(S, 12), P(S, 13), P(S, 14), P(rms_piece, 3),
        P(gates_chunk, 0), P(S, 15), P(gates_chunk, 1), P(gates_chunk, 2), P(gates_chunk, 3),
    ]
    for piece in program:
        piece()
    h_last = state[0]
    hc_ref[...] = h_last

    @pl.when(jnp.logical_and(dot_live, dot_tile % per == per - 1))
    def _():
        cn_ref[0] = ext_ref[SUBLANES - (CONV_WIDTH - 1):SUBLANES, :]

    @pl.when(jnp.logical_and(scan_live, scan_tile % per == per - 1))
    def _():
        hl_ref[0] = h_last[0:1, :]


def _proj_lru(x, g, wq, wug, wkvt, conv_state, h0, lru_w):
    s, t, _ = x.shape
    n = s * t
    tm = TOKEN_TILE
    assert t % tm == 0 and tm % KEY_BLOCK == 0
    per = t // tm
    n_tiles = n // tm
    tile = lambda i, lag: jnp.clip(i - lag, 0, n_tiles - 1)
    tok_in = pl.BlockSpec((tm, D_MODEL), lambda i: (tile(i, 0), 0))
    q_out = pl.BlockSpec((tm, D_ATTN), lambda i: (tile(i, 1), 0))
    hg_out = pl.BlockSpec((tm, D_RNN), lambda i: (tile(i, 2), 0))
    b16o = jax.ShapeDtypeStruct((n, D_ATTN), BF16)
    kv_spec = pl.BlockSpec((1, D_ATTN, tm), lambda i: (tile(i, 1) // per, 0, tile(i, 1) % per))
    kvb_spec = pl.BlockSpec((1, tm // KEY_BLOCK, D_ATTN, KEY_BLOCK),
                            lambda i: (tile(i, 1) // per, tile(i, 1) % per, 0, 0))
    kv_shape = jax.ShapeDtypeStruct((s, D_ATTN, t), F32)
    kvb_shape = jax.ShapeDtypeStruct((s, t // KEY_BLOCK, D_ATTN, KEY_BLOCK), BF16)
    cs_spec, _ = _lru_state_specs(lambda i: (tile(i, 1) // per, 0, 0))
    _, h_spec = _lru_state_specs(lambda i: (tile(i, 2) // per, 0, 0))
    consts = (g, wq, wug, wkvt)
    return pl.pallas_call(
        functools.partial(_proj_lru_kernel, per, n_tiles),
        grid=(n_tiles + 2,),
        in_specs=[tok_in] + [_const_spec(c.shape) for c in consts] + [cs_spec, h_spec]
        + [_const_spec(w.shape) for w in lru_w],
        out_specs=[q_out, kv_spec, kv_spec, kvb_spec, kvb_spec, hg_out, cs_spec, h_spec],
        out_shape=[b16o, kv_shape, kv_shape, kvb_shape, kvb_shape, b16o,
                   jax.ShapeDtypeStruct((s, CONV_WIDTH - 1, D_RNN), F32),
                   jax.ShapeDtypeStruct((s, 1, D_RNN), F32)],
        scratch_shapes=_lru_scratch(tm) + [pltpu.VMEM((2, tm, D_MODEL), BF16), pltpu.VMEM((2, tm, D_RNN), F32),
                                           pltpu.VMEM((2, tm, D_RNN), F32), pltpu.VMEM((2, tm, 2 * D_RNN), F32)],
        compiler_params=pltpu.CompilerParams(dimension_semantics=("arbitrary",),
                                             vmem_limit_bytes=VMEM_LIMIT),
        name="proj_lru",
    )(x.reshape(n, D_MODEL), *consts, conv_state, h0, *lru_w)


def _sb_split(z, mask):
    n = z.shape[1] // KEY_BLOCK
    tq = z.shape[0] // 2
    sp = _softplus(z)
    cats = []
    for h in range(2):
        for d in range(n):
            blk = sp[h * tq:(h + 1) * tq, d * KEY_BLOCK:(d + 1) * KEY_BLOCK]
            if d == n - 1 and mask is not None:
                blk = jnp.where(mask, blk, 0.0)
            hi = blk.astype(BF16)
            lo = (blk - hi.astype(F32)).astype(BF16)
            cats.append(jnp.concatenate([hi, lo], axis=1))
    return jnp.concatenate(cats, axis=0)


def _sb_cumsum(cat, tri2):
    return jnp.dot(cat, tri2, preferred_element_type=F32)


def _sb_weights(z, r, carries, mask):
    n = z.shape[1] // KEY_BLOCK
    tq = z.shape[0] // 2
    rows, new_carries = [], []
    for h in range(2):
        carry = None if carries is None else carries[h]
        es = [None] * n
        for d in reversed(range(n)):
            rd = r[(h * n + d) * tq:(h * n + d + 1) * tq]
            x = z[h * tq:(h + 1) * tq, d * KEY_BLOCK:(d + 1) * KEY_BLOCK] + rd[:, :KEY_BLOCK]
            if carry is not None:
                x = x + carry
            e = jnp.exp(x)
            if d == n - 1 and mask is not None:
                e = jnp.where(mask, e, 0.0)
            es[d] = e.astype(BF16)
            tot = rd[:, KEY_BLOCK:]
            carry = tot if carry is None else carry + tot
        rows.append(es[0] if n == 1 else jnp.concatenate(es, axis=1))
        new_carries.append(carry)
    return jnp.concatenate(rows, axis=0), new_carries


def _tri2():
    j = np.arange(KEY_BLOCK)[:, None]
    s = np.arange(KEY_BLOCK)[None, :]
    half = np.concatenate([-(j >= s).astype(np.float32), -np.ones((KEY_BLOCK, LANES), np.float32)], axis=1)
    return jnp.asarray(np.concatenate([half, half], axis=0), dtype=BF16)


def _stack_heads(q, lo_half):
    zero = jnp.zeros_like(q)
    return jnp.concatenate([jnp.where(lo_half, q, zero), jnp.where(lo_half, zero, q)], axis=0)


def _sb_window(z_fn, pv_fn, carries, mask, tri2):
    z = z_fn()
    e, cs = _sb_weights(z, _sb_cumsum(_sb_split(z, mask), tri2), carries, mask)
    pv = pv_fn(e)
    tq = pv.shape[0] // 2
    return (pv[:tq], cs[0]), (pv[tq:], cs[1])


def _finish(store, j0, outs, step_fn):
    (a0, c0), (a1, c1) = outs
    store(a0, a1)
    if isinstance(j0, int) and j0 < 0:
        return lambda: None

    def alive(x0, x1):
        return jnp.max(jnp.maximum(x0, x1)) >= LOG_KEEP_FLOOR

    go = jnp.logical_and(j0 >= 0, alive(c0, c1))

    def walk():
        @pl.when(go)
        def _():
            def body(s):
                j, b0, b1, d0, d1, _ = s
                (p0, d0), (p1, d1) = step_fn(j, [d0, d1])
                return j - 1, b0 + p0, b1 + p1, d0, d1, jnp.logical_and(j >= 1, alive(d0, d1))

            _, b0, b1, _, _, _ = lax.while_loop(lambda s: s[-1], body,
                                                (jnp.int32(j0), a0, a1, c0, c1, jnp.bool_(True)))
            store(b0, b1)

    return walk


def _sb_chains(chains, tri2):
    zs = [c[0]() for c in chains]
    sums = [_sb_cumsum(_sb_split(z, c[2]), tri2) for z, c in zip(zs, chains)]
    weights = [_sb_weights(z, r, None, c[2]) for z, r, c in zip(zs, sums, chains)]
    walks = []
    for c, (e, cs) in zip(chains, weights):
        pv = c[1](e)
        tq = pv.shape[0] // 2
        walks.append(_finish(c[3], c[5], ((pv[:tq], cs[0]), (pv[tq:], cs[1])), c[4]))
    for walk in walks:
        walk()


def _attn_prompt_kernel(q_ref, kt_ref, vt_ref, tri_ref, o_ref):
    t = q_ref.shape[1]
    nq = t // Q_ROWS
    row = lax.broadcasted_iota(jnp.int32, (Q_ROWS, LANES), 0)
    lane = lax.broadcasted_iota(jnp.int32, (Q_ROWS, LANES), 1)
    lo_half = lane < HEAD_DIM
    diag_mask = lane < row
    lane_minus_row = lane - row
    tri2 = tri_ref[...]

    def window(ref, first, n):
        return jnp.concatenate([ref[0, first + d] for d in range(n)], axis=1)

    def qblocks(blocks, n_fast):
        chains = []
        for i in blocks:
            q2 = _stack_heads(q_ref[0, _rows(i, Q_ROWS), :], lo_half)
            first = i + 1 - n_fast

            def z_fn(q2=q2, first=first):
                return jnp.dot(q2, window(kt_ref, first, n_fast), preferred_element_type=F32)

            def pv_fn(e, first=first):
                return lax.dot_general(e, window(vt_ref, first, n_fast), NT_DIMS, preferred_element_type=F32)

            def store(b0, b1, i=i):
                o_ref[_rows(i, Q_ROWS), :] = jnp.where(lo_half, b0, b1).astype(BF16)

            def step(j, cs, i=i, q2=q2):
                m = lane_minus_row < (i - j) * KEY_BLOCK
                return _sb_window(lambda: jnp.dot(q2, kt_ref[0, j], preferred_element_type=F32),
                                  lambda e: lax.dot_general(e, vt_ref[0, j], NT_DIMS, preferred_element_type=F32),
                                  cs, m, tri2)

            chains.append((z_fn, pv_fn, diag_mask, store, step, i - n_fast))
        _sb_chains(chains, tri2)

    n_head = FAST_BLOCKS - 1
    for i in range(n_head):
        qblocks([i], i + 1)

    start = n_head + (nq - n_head) % Q_GROUP
    if start > n_head:
        qblocks(list(range(n_head, start)), FAST_BLOCKS)

    def group(g, carry):
        qblocks([start + g * Q_GROUP + u for u in range(Q_GROUP)], FAST_BLOCKS)
        return carry

    lax.fori_loop(0, (nq - start) // Q_GROUP, group, 0)


def _attn_prompt(qb, ktb, vtb, tri2):
    b, t, _ = qb.shape
    assert t % Q_ROWS == 0 and Q_ROWS == KEY_BLOCK
    qspec = pl.BlockSpec((1, t, LANES), lambda i, h: (i, 0, h))
    kspec = pl.BlockSpec((1, t // KEY_BLOCK, LANES, KEY_BLOCK), lambda i, h: (i, 0, h, 0))
    return pl.pallas_call(
        _attn_prompt_kernel,
        grid=(b, D_ATTN // LANES),
        in_specs=[qspec, kspec, kspec, _const_spec(tri2.shape)],
        out_specs=pl.BlockSpec((t, LANES), lambda i, h: (i, h)),
        out_shape=jax.ShapeDtypeStruct((b * t, D_ATTN), BF16),
        compiler_params=pltpu.CompilerParams(dimension_semantics=("arbitrary", "arbitrary"),
                                             vmem_limit_bytes=VMEM_LIMIT),
        name="attn_prompt",
    )(qb, ktb, vtb, tri2)


def _attn_sample_kernel(n_tail, q_ref, kn_ref, vn_ref, ckt_ref, cvt_ref, ck_hbm, cv_hbm, tri_ref, o_ref,
                        kbuf, vbuf):
    s = pl.program_id(0)
    tq = q_ref.shape[1]
    n_past = ck_hbm.shape[2] // KEY_BLOCK
    row = lax.broadcasted_iota(jnp.int32, (tq, LANES), 0)
    lane = lax.broadcasted_iota(jnp.int32, (tq, LANES), 1)
    lo_half = lane < HEAD_DIM
    tri2 = tri_ref[...]

    pad = jnp.zeros((KEY_BLOCK - tq, LANES), BF16)
    chains = []
    for p in range(q_ref.shape[2] // LANES):
        cols = slice(p * LANES, (p + 1) * LANES)
        q2 = _stack_heads(q_ref[0, :, cols], lo_half)

        def z_fn(q2=q2, cols=cols):
            z_new = lax.dot_general(q2, jnp.concatenate([kn_ref[0, :, cols], pad], axis=0), NT_DIMS,
                                    preferred_element_type=F32)
            if n_tail == 0:
                return z_new
            z_past = jnp.dot(q2, ckt_ref[0, cols, :].astype(BF16), preferred_element_type=F32)
            return jnp.concatenate([z_past, z_new], axis=1)

        def pv_fn(e, cols=cols):
            pv = jnp.dot(e[:, n_tail * KEY_BLOCK:], jnp.concatenate([vn_ref[0, :, cols], pad], axis=0),
                         preferred_element_type=F32)
            if n_tail == 0:
                return pv
            return pv + lax.dot_general(e[:, :n_tail * KEY_BLOCK], cvt_ref[0, cols, :].astype(BF16), NT_DIMS,
                                        preferred_element_type=F32)

        def store(a0, a1, cols=cols):
            o_ref[:, cols] = jnp.where(lo_half, a0, a1).astype(BF16)

        def step(j, cs, cols=cols, q2=q2, p=p):
            keys = pl.ds(pl.multiple_of(j * KEY_BLOCK, KEY_BLOCK), KEY_BLOCK)
            pltpu.sync_copy(ck_hbm.at[s, pl.ds(p * LANES, LANES), keys], kbuf)
            pltpu.sync_copy(cv_hbm.at[s, pl.ds(p * LANES, LANES), keys], vbuf)
            return _sb_window(lambda: jnp.dot(q2, kbuf[...].astype(BF16), preferred_element_type=F32),
                              lambda e: lax.dot_general(e, vbuf[...].astype(BF16), NT_DIMS,
                                                        preferred_element_type=F32),
                              cs, None, tri2)

        chains.append((z_fn, pv_fn, lane < row, store, step, n_past - 1 - n_tail))
    _sb_chains(chains, tri2)


def _attn_sample(qb, knb, vnb, cache_kt, cache_vt, tri2):
    s, tq, _ = qb.shape
    past_len = cache_kt.shape[2]
    assert past_len % KEY_BLOCK == 0 and tq <= KEY_BLOCK and tq % 16 == 0
    n_tail = min(FAST_BLOCKS - 1, past_len // KEY_BLOCK)
    assert n_tail > 0 and past_len % (n_tail * KEY_BLOCK) == 0
    new = pl.BlockSpec((1, tq, D_ATTN), lambda i: (i, 0, 0))
    tail = pl.BlockSpec((1, D_ATTN, n_tail * KEY_BLOCK), lambda i: (i, 0, past_len // (n_tail * KEY_BLOCK) - 1))
    hbm = pl.BlockSpec(memory_space=pl.ANY)
    return pl.pallas_call(
        functools.partial(_attn_sample_kernel, n_tail),
        grid=(s,),
        in_specs=[new, new, new, tail, tail, hbm, hbm, _const_spec(tri2.shape)],
        out_specs=pl.BlockSpec((tq, D_ATTN), lambda i: (i, 0)),
        out_shape=jax.ShapeDtypeStruct((s * tq, D_ATTN), BF16),
        scratch_shapes=[pltpu.VMEM((LANES, KEY_BLOCK), F32), pltpu.VMEM((LANES, KEY_BLOCK), F32)],
        compiler_params=pltpu.CompilerParams(dimension_semantics=("arbitrary",),
                                             vmem_limit_bytes=VMEM_LIMIT),
        name="attn_sample",
    )(qb, knb, vnb, cache_kt, cache_vt, cache_kt, cache_vt, tri2)


FF_CHUNK = 1024


def _out_kernel(n_first, xa_ref, xb_ref, oa_ref, ob_ref, hga_ref, hgb_ref, g1_ref, wgab_ref, wa_ref, wb_ref,
                wo_ref, g2_ref, g3_ref, wup_ref, wdn_ref, g4_ref, ya_ref, yb_ref):
    first = pl.program_id(0) < n_first
    tm = xa_ref.shape[0]
    halves = [pl.ds(0, tm // 2), pl.ds(tm // 2, tm // 2)]
    n_ff = D_FF // FF_CHUNK
    st = [{}, {}]

    def pick(a_ref, b_ref, h):
        return jnp.where(first, a_ref[halves[h], :], b_ref[halves[h], :])

    def norm_in(h):
        st[h]["x"] = pick(xa_ref, xb_ref, h)
        st[h]["xn"] = _rms(st[h]["x"], g1_ref[...]).astype(BF16)

    def attn_gate(h):
        g_a = jnp.dot(st[h]["xn"], wgab_ref[:, :D_MODEL], preferred_element_type=F32)
        y_a = jnp.dot(pick(oa_ref, ob_ref, h), wa_ref[...], preferred_element_type=F32)
        st[h]["m"] = jax.nn.sigmoid(g_a) * y_a

    def rnn_gate(h):
        g_b = jnp.dot(st[h]["xn"], wgab_ref[:, D_MODEL:], preferred_element_type=F32)
        y_b = jnp.dot(pick(hga_ref, hgb_ref, h), wb_ref[...], preferred_element_type=F32)
        st[h]["m"] = st[h]["m"] + jax.nn.sigmoid(g_b) * y_b

    def mix(h):
        st[h]["mix"] = jnp.dot(st[h]["m"].astype(BF16), wo_ref[...], preferred_element_type=F32)

    def norm_mid(h):
        st[h]["x1"] = st[h]["x"] + _rms(st[h]["mix"], g2_ref[...])
        st[h]["f"] = _rms(st[h]["x1"], g3_ref[...]).astype(BF16)

    def ffn(h, c):
        cols = slice(c * FF_CHUNK, (c + 1) * FF_CHUNK)
        up = jnp.maximum(jnp.dot(st[h]["f"], wup_ref[:, cols], preferred_element_type=F32), 0.0)
        dn = jnp.dot((up * up).astype(BF16), wdn_ref[cols, :], preferred_element_type=F32)
        st[h]["acc"] = dn if c == 0 else st[h]["acc"] + dn

    def norm_out(h):
        yb_ref[halves[h], :] = st[h]["x1"] + _rms(st[h]["acc"], g4_ref[...])

    norm_in(0)
    norm_in(1)
    attn_gate(0)
    rnn_gate(0)
    mix(0)
    attn_gate(1)
    norm_mid(0)
    rnn_gate(1)
    mix(1)
    ffn(0, 0)
    norm_mid(1)
    for c in range(1, n_ff):
        ffn(0, c)
    ffn(1, 0)
    norm_out(0)
    for c in range(1, n_ff):
        ffn(1, c)
    norm_out(1)

    @pl.when(first)
    def _():
        ya_ref[...] = yb_ref[...]


def _out(xa, xb, oa, ob, hga, hgb, g1, wgab, wa, wb, wo, g2, g3, wup, wdn, g4):
    na, nb = xa.shape[0], xb.shape[0]
    tm = TOKEN_TILE
    assert na % tm == 0 and nb % tm == 0
    n_first = na // tm
    a_tok = lambda i: (jnp.minimum(i, n_first - 1), 0)
    b_tok = lambda i: (jnp.maximum(i - n_first, 0), 0)
    consts = (g1, wgab, wa, wb, wo, g2, g3, wup, wdn, g4)
    return pl.pallas_call(
        functools.partial(_out_kernel, n_first),
        grid=((na + nb) // tm,),
        in_specs=[pl.BlockSpec((tm, D_MODEL), a_tok), pl.BlockSpec((tm, D_MODEL), b_tok),
                  pl.BlockSpec((tm, D_ATTN), a_tok), pl.BlockSpec((tm, D_ATTN), b_tok),
                  pl.BlockSpec((tm, D_RNN), a_tok), pl.BlockSpec((tm, D_RNN), b_tok)]
        + [_const_spec(c.shape) for c in consts],
        out_specs=[pl.BlockSpec((tm, D_MODEL), a_tok), pl.BlockSpec((tm, D_MODEL), b_tok)],
        out_shape=[jax.ShapeDtypeStruct((na, D_MODEL), F32), jax.ShapeDtypeStruct((nb, D_MODEL), F32)],
        compiler_params=pltpu.CompilerParams(dimension_semantics=("arbitrary",),
                                             vmem_limit_bytes=VMEM_LIMIT),
        name="out",
    )(xa, xb, oa, ob, hga, hgb, *consts)


def _block_diag(w):
    n, c, d = w.shape
    return jnp.einsum("ncd,nm->ncmd", w, jnp.eye(n, dtype=w.dtype)).reshape(n * c, n * d)


def _heads_last(kt):
    s, _, t = kt.shape
    return kt.reshape(s, N_HEADS, HEAD_DIM, t).transpose(0, 3, 1, 2)[None]


def _time_last(cache):
    s, t = cache.shape[:2]
    return cache.transpose(0, 2, 3, 1).reshape(s, D_ATTN, t)


def kernel(x_prompt, x_sample, cache_k, cache_v, state_conv, state_h, w_in, g_pre_mix, w_conv, b_conv, w_r, b_r, w_i, b_i, lam, w_a_out, w_b_out, w_o, g_post_mix, g_pre_ffn, w_up, w_down, g_post_ffn):
    assert w_in.shape[0] == 1
    row = lambda a: a[0].reshape(1, -1)
    w_in0 = w_in[0]
    n_seq = 3 * D_ATTN + 2 * D_RNN
    wq = (w_in0[:, :D_ATTN] * (HEAD_DIM ** -0.5)).astype(BF16)
    wkv = w_in0[:, D_ATTN:3 * D_ATTN].astype(BF16)
    wug = w_in0[:, 3 * D_ATTN:n_seq].astype(BF16)
    wgab = w_in0[:, n_seq:].astype(BF16)
    wg = jnp.concatenate([_block_diag(w_r[0]), _block_diag(w_i[0])], axis=1).astype(BF16)
    bg = jnp.concatenate([row(b_r), row(b_i)], axis=1)
    lru_w = (w_conv[0], row(b_conv), wg, bg, row(lam))
    g1 = row(g_pre_mix)
    tri2 = _tri2()

    bp, tp, _ = x_prompt.shape
    bs, ts, _ = x_sample.shape
    n_p, n_s = bp * tp, bs * ts

    zc = jnp.zeros((bp, CONV_WIDTH - 1, D_RNN), F32)
    zh = jnp.zeros((bp, 1, D_RNN), F32)
    qb, kt, vt, ktb, vtb, hg_p, cp, hp = _proj_lru(x_prompt, g1, wq, wug, wkv.T, zc, zh, lru_w)
    o_p = _attn_prompt(qb.reshape(bp, tp, D_ATTN), ktb, vtb, tri2)

    s3 = lambda a: a.reshape(bs, ts, a.shape[-1])
    qs, ks, vs, ksb, vsb, us, grs = _proj(x_sample, g1, wq, wug, wkv)
    o_s = _attn_sample(s3(qs), s3(ksb), s3(vsb), _time_last(cache_k[0]), _time_last(cache_v[0]), tri2)
    hg_s, cs, hs = _lru(s3(us), s3(grs), state_conv[0], state_h[0].reshape(bs, 1, D_RNN), lru_w)

    ys, yp = _out(x_sample.reshape(n_s, D_MODEL), x_prompt.reshape(n_p, D_MODEL), o_s, o_p, hg_s, hg_p, g1, wgab,
                  w_a_out[0].astype(BF16), w_b_out[0].astype(BF16), w_o[0].astype(BF16), row(g_post_mix),
                  row(g_pre_ffn), w_up[0].astype(BF16), w_down[0].astype(BF16), row(g_post_ffn))
    return (yp.reshape(bp, tp, D_MODEL), ys.reshape(bs, ts, D_MODEL), _heads_last(kt), _heads_last(vt),
            cp[None], hp.reshape(1, bp, D_RNN), ks.reshape(1, bs, ts, N_HEADS, HEAD_DIM),
            vs.reshape(1, bs, ts, N_HEADS, HEAD_DIM), cs[None], hs.reshape(1, bs, D_RNN))
```

```python
import functools

import jax
import jax.numpy as jnp
import numpy as np
from jax import lax
from jax.experimental import pallas as pl
from jax.experimental.pallas import tpu as pltpu

F32 = jnp.float32
BF16 = jnp.bfloat16

D_MODEL = 1024
N_HEADS = 8
HEAD_DIM = 64
D_ATTN = N_HEADS * HEAD_DIM
D_RNN = 512
N_RNN_BLOCKS = 8
CONV_WIDTH = 4
LRU_C = 8.0
D_FF = 4 * D_MODEL
EPS = 1e-6

LANES = 128
SUBLANES = 8
KEY_BLOCK = 128
Q_ROWS = 128
FAST_BLOCKS = 3
Q_GROUP = 4
TOKEN_TILE = 512
SCAN_SLICES = 16
MXU_N = 256
LOG_KEEP_FLOOR = -104.0
VMEM_LIMIT = 56 * 1024 * 1024

NT_DIMS = (((1,), (1,)), ((), ()))
LOG2_E = 1.4426950408889634


def _rms(xf, g):
    return xf * lax.rsqrt(jnp.mean(xf * xf, axis=-1, keepdims=True) + EPS) * g


def _softplus(x):
    return jnp.maximum(x, 0.0) + jnp.log(1.0 + jnp.exp2(jnp.abs(x) * -LOG2_E))


def _sigmoid(x):
    return 0.5 * jnp.tanh(0.5 * x) + 0.5


def _const_spec(shape):
    nd = len(shape)
    return pl.BlockSpec(shape, lambda *_: (0,) * nd, pipeline_mode=pl.Buffered(1))


def _rows(block_index, size):
    if isinstance(block_index, int):
        return pl.ds(block_index * size, size)
    return pl.ds(pl.multiple_of(block_index * size, size), size)


def _gelu_tanh(x):
    return 0.5 * x * (1.0 + jnp.tanh(np.sqrt(2.0 / np.pi).astype(np.float32) * (x + 0.044715 * (x * x * x))))


def _conv_seed(first, cs_ref, ext_ref):
    @pl.when(first)
    def _():
        ext_ref[0:SUBLANES, :] = jnp.zeros((SUBLANES, D_RNN), F32)
        ext_ref[SUBLANES - (CONV_WIDTH - 1):SUBLANES, :] = cs_ref[0]


def _scan_seed(first, h0_ref, hc_ref):
    @pl.when(first)
    def _():
        hc_ref[...] = jnp.broadcast_to(h0_ref[0], hc_ref.shape)


def _lru_conv(u, wc_ref, bc_ref, ext_ref):
    tt = u.shape[0]
    ext_ref[SUBLANES:SUBLANES + tt, :] = u
    uc = bc_ref[...] + wc_ref[CONV_WIDTH - 1:CONV_WIDTH, :] * u
    for d in range(1, CONV_WIDTH):
        uc = uc + wc_ref[CONV_WIDTH - 1 - d:CONV_WIDTH - d, :] * ext_ref[SUBLANES - d:SUBLANES - d + tt, :]
    ext_ref[0:SUBLANES, :] = ext_ref[tt:tt + SUBLANES, :]
    return uc


def _lru_gates(uc, wg_ref, bg_ref):
    return jnp.dot(uc.astype(BF16), wg_ref[...], preferred_element_type=F32) + bg_ref[...]


def _lru_scan(g, uc, g_rnn, lam_ref, h_prev):
    tt = uc.shape[0]
    r = _sigmoid(g[:, :D_RNN])
    ig = _sigmoid(g[:, D_RNN:])
    log_a = (-LRU_C * _softplus(-lam_ref[...])) * r
    a = jnp.exp(log_a)
    m2 = -jnp.tanh(log_a) * (a * a + 1.0)
    b = jnp.where(m2 > 0.0, m2 * lax.rsqrt(m2), 0.0) * (ig * uc)

    n_groups = tt // SUBLANES
    a3 = a.reshape(n_groups, SUBLANES, D_RNN)
    b3 = b.reshape(n_groups, SUBLANES, D_RNN)
    sub = lax.broadcasted_iota(jnp.int32, a3.shape, 1)
    k = 1
    while k < SUBLANES:
        has_prev = sub >= k
        b_prev = jnp.where(has_prev, pltpu.roll(b3, k, axis=1), 0.0)
        a_prev = jnp.where(has_prev, pltpu.roll(a3, k, axis=1), 1.0)
        b3 = b3 + a3 * b_prev
        a3 = a3 * a_prev
        k *= 2

    hs = []
    for gi in range(n_groups):
        h = b3[gi] + a3[gi] * h_prev
        hs.append(h)
        h_prev = jnp.broadcast_to(h[SUBLANES - 1:SUBLANES, :], h.shape)
    return (jnp.concatenate(hs, axis=0) * _gelu_tanh(g_rnn)).astype(BF16), h_prev


def _lru_state_specs(index_map):
    past = CONV_WIDTH - 1
    return (pl.BlockSpec((1, past, D_RNN), index_map), pl.BlockSpec((1, 1, D_RNN), index_map))


def _lru_scratch(tt):
    return [pltpu.VMEM((tt + SUBLANES, D_RNN), F32), pltpu.VMEM((SUBLANES, D_RNN), F32)]


def _proj_kernel(t, x_ref, g_ref, wq_ref, wug_ref, wkv_ref, cs_ref, h0_ref, wc_ref, bc_ref, wg_ref, bg_ref, lam_ref,
                 qb_ref, k_ref, v_ref, kb_ref, vb_ref, hg_ref, cn_ref, hl_ref, ext_ref):
    n_streams = x_ref.shape[0] // t
    past = CONV_WIDTH - 1
    xn = _rms(x_ref[...], g_ref[...]).astype(BF16)
    ug = jnp.dot(xn, wug_ref[...], preferred_element_type=F32)
    qb_ref[...] = jnp.dot(xn, wq_ref[...], preferred_element_type=F32).astype(BF16)
    kv = jnp.dot(xn, wkv_ref[...], preferred_element_type=F32)
    k, v = kv[:, :D_ATTN], kv[:, D_ATTN:]
    k_ref[...] = k
    v_ref[...] = v
    kb_ref[...] = k.astype(BF16)
    vb_ref[...] = v.astype(BF16)

    ucs = []
    for s in range(n_streams):
        ext = ext_ref.at[s]
        ext[0:SUBLANES, :] = jnp.zeros((SUBLANES, D_RNN), F32)
        ext[SUBLANES - past:SUBLANES, :] = cs_ref[s]
        ucs.append(_lru_conv(ug[s * t:(s + 1) * t, :D_RNN], wc_ref, bc_ref, ext))
        cn_ref[s] = ext[SUBLANES - past:SUBLANES, :]
    uc = jnp.concatenate(ucs, axis=0)
    gates = _lru_gates(uc, wg_ref, bg_ref)
    for s in range(n_streams):
        rows = slice(s * t, (s + 1) * t)
        hg, h_last = _lru_scan(gates[rows], uc[rows], ug[rows, D_RNN:], lam_ref,
                               jnp.broadcast_to(h0_ref[s], (SUBLANES, D_RNN)))
        hg_ref[rows, :] = hg
        hl_ref[s] = h_last[0:1, :]


def _proj(x, g, wq, wug, wkv, conv_state, h0, lru_w):
    s, t, _ = x.shape
    n = s * t
    tm = TOKEN_TILE
    assert n % tm == 0 and tm % t == 0 and t % SUBLANES == 0
    per_tile = tm // t
    tok = lambda i: (i, 0)
    row_out = pl.BlockSpec((tm, D_ATTN), tok)
    f32o = jax.ShapeDtypeStruct((n, D_ATTN), F32)
    b16o = jax.ShapeDtypeStruct((n, D_ATTN), BF16)
    st3 = pl.BlockSpec((per_tile, CONV_WIDTH - 1, D_RNN), lambda i: (i, 0, 0))
    st1 = pl.BlockSpec((per_tile, 1, D_RNN), lambda i: (i, 0, 0))
    consts = (g, wq, wug, wkv)
    return pl.pallas_call(
        functools.partial(_proj_kernel, t),
        grid=(n // tm,),
        in_specs=[pl.BlockSpec((tm, D_MODEL), tok)] + [_const_spec(c.shape) for c in consts] + [st3, st1]
        + [_const_spec(w.shape) for w in lru_w],
        out_specs=[row_out] * 6 + [st3, st1],
        out_shape=[b16o, f32o, f32o, b16o, b16o, b16o,
                   jax.ShapeDtypeStruct((s, CONV_WIDTH - 1, D_RNN), F32),
                   jax.ShapeDtypeStruct((s, 1, D_RNN), F32)],
        scratch_shapes=[pltpu.VMEM((per_tile, t + SUBLANES, D_RNN), F32)],
        compiler_params=pltpu.CompilerParams(dimension_semantics=("arbitrary",),
                                             vmem_limit_bytes=VMEM_LIMIT),
        name="proj",
    )(x.reshape(n, D_MODEL), *consts, conv_state, h0, *lru_w)


def _proj_lru_kernel(per, n_tiles, x_ref, g_ref, wq_ref, wug_ref, wkvt_ref, cs_ref, h0_ref, wc_ref, bc_ref,
                     wg_ref, bg_ref, lam_ref, qb_ref, kt_ref, vt_ref, ktb_ref, vtb_ref, hg_ref, cn_ref, hl_ref,
                     ext_ref, hc_ref, xn_ref, uc_ref, gr_ref, gt_ref):
    i = pl.program_id(0)
    dot_tile, scan_tile = i - 1, i - 2
    dot_live = jnp.logical_and(dot_tile >= 0, dot_tile < n_tiles)
    scan_live = jnp.logical_and(scan_tile >= 0, scan_tile < n_tiles)
    _conv_seed(jnp.logical_and(dot_live, dot_tile % per == 0), cs_ref, ext_ref)
    _scan_seed(jnp.logical_and(scan_live, scan_tile % per == 0), h0_ref, hc_ref)

    @pl.when(i == 0)
    def _():
        xn_ref[...] = jnp.zeros(xn_ref.shape, BF16)
        hc_ref[...] = jnp.zeros(hc_ref.shape, F32)
        ext_ref[...] = jnp.zeros(ext_ref.shape, F32)
        for ref in (uc_ref, gr_ref, gt_ref):
            ref[...] = jnp.zeros(ref.shape, F32)

    tm = x_ref.shape[0]
    sc, dt = i % 2, (i + 1) % 2
    xn = xn_ref[dt]
    state = [hc_ref[...]]
    rows_per = tm // SCAN_SLICES
    u_parts = {}
    uc_b = [None]

    def scan_slice(p):
        rows = pl.ds(p * rows_per, rows_per)
        hg, state[0] = _lru_scan(gt_ref[sc, rows, :], uc_ref[sc, rows, :], gr_ref[sc, rows, :], lam_ref, state[0])
        hg_ref[rows, :] = hg

    def rms_piece(r):
        rows = pl.ds(r * (tm // 4), tm // 4)
        xn_ref[sc, rows, :] = _rms(x_ref[rows, :], g_ref[...]).astype(BF16)

    def ug_chunk(c):
        cols = slice(c * MXU_N, (c + 1) * MXU_N)
        r = jnp.dot(xn, wug_ref[:, cols], preferred_element_type=F32)
        if c < D_RNN // MXU_N:
            u_parts[c] = r
        else:
            gr_ref[dt, :, c * MXU_N - D_RNN:(c + 1) * MXU_N - D_RNN] = r

    def conv():
        uc = _lru_conv(jnp.concatenate([u_parts[c] for c in sorted(u_parts)], axis=1), wc_ref, bc_ref, ext_ref)
        uc_ref[dt] = uc
        uc_b[0] = uc.astype(BF16)

    def q_chunk(c):
        cols = slice(c * MXU_N, (c + 1) * MXU_N)
        qb_ref[:, cols] = jnp.dot(xn, wq_ref[:, cols], preferred_element_type=F32).astype(BF16)

    def kv_chunk(c):
        toks = slice(c * MXU_N, (c + 1) * MXU_N)
        kv = lax.dot_general(wkvt_ref[...], xn[toks], NT_DIMS, preferred_element_type=F32)
        k, v = kv[:D_ATTN], kv[D_ATTN:]
        kt_ref[0, :, toks] = k
        vt_ref[0, :, toks] = v
        for j in range(MXU_N // KEY_BLOCK):
            cols = slice(j * KEY_BLOCK, (j + 1) * KEY_BLOCK)
            ktb_ref[0, c * (MXU_N // KEY_BLOCK) + j] = k[:, cols].astype(BF16)
            vtb_ref[0, c * (MXU_N // KEY_BLOCK) + j] = v[:, cols].astype(BF16)

    def gates_chunk(c):
        cols = slice(c * MXU_N, (c + 1) * MXU_N)
        gt_ref[dt, :, cols] = jnp.dot(uc_b[0], wg_ref[:, cols], preferred_element_type=F32) + bg_ref[:, cols]

    S, P = scan_slice, functools.partial
    program = [
        P(ug_chunk, 0), P(S, 0), P(S, 1), P(ug_chunk, 1), P(S, 2), P(S, 3), P(ug_chunk, 2), conv,
        P(ug_chunk, 3), P(S, 4), P(S, 5), P(q_chunk, 0), P(S, 6), P(S, 7),
        P(q_chunk, 1), P(rms_piece, 0), P(rms_piece, 1), P(S, 8),
        P(kv_chunk, 0), P(S, 9), P(S, 10), P(S, 11), P(rms_piece, 2),
        P(kv_chunk, 1), P(S, 12), P(S, 13), P(S, 14), P(rms_piece, 3),
        P(gates_chunk, 0), P(S, 15), P(gates_chunk, 1), P(gates_chunk, 2), P(gates_chunk, 3),
    ]
    for piece in program:
        piece()
    h_last = state[0]
    hc_ref[...] = h_last

    @pl.when(jnp.logical_and(dot_live, dot_tile % per == per - 1))
    def _():
        cn_ref[0] = ext_ref[SUBLANES - (CONV_WIDTH - 1):SUBLANES, :]

    @pl.when(jnp.logical_and(scan_live, scan_tile % per == per - 1))
    def _():
        hl_ref[0] = h_last[0:1, :]


def _proj_lru(x, g, wq, wug, wkvt, conv_state, h0, lru_w):
    s, t, _ = x.shape
    n = s * t
    tm = TOKEN_TILE
    assert t % tm == 0 and tm % KEY_BLOCK == 0
    per = t // tm
    n_tiles = n // tm
    tile = lambda i, lag: jnp.clip(i - lag, 0, n_tiles - 1)
    tok_in = pl.BlockSpec((tm, D_MODEL), lambda i: (tile(i, 0), 0))
    q_out = pl.BlockSpec((tm, D_ATTN), lambda i: (tile(i, 1), 0))
    hg_out = pl.BlockSpec((tm, D_RNN), lambda i: (tile(i, 2), 0))
    b16o = jax.ShapeDtypeStruct((n, D_ATTN), BF16)
    kv_spec = pl.BlockSpec((1, D_ATTN, tm), lambda i: (tile(i, 1) // per, 0, tile(i, 1) % per))
    kvb_spec = pl.BlockSpec((1, tm // KEY_BLOCK, D_ATTN, KEY_BLOCK),
                            lambda i: (tile(i, 1) // per, tile(i, 1) % per, 0, 0))
    kv_shape = jax.ShapeDtypeStruct((s, D_ATTN, t), F32)
    kvb_shape = jax.ShapeDtypeStruct((s, t // KEY_BLOCK, D_ATTN, KEY_BLOCK), BF16)
    cs_spec, _ = _lru_state_specs(lambda i: (tile(i, 1) // per, 0, 0))
    _, h_spec = _lru_state_specs(lambda i: (tile(i, 2) // per, 0, 0))
    consts = (g, wq, wug, wkvt)
    return pl.pallas_call(
        functools.partial(_proj_lru_kernel, per, n_tiles),
        grid=(n_tiles + 2,),
        in_specs=[tok_in] + [_const_spec(c.shape) for c in consts] + [cs_spec, h_spec]
        + [_const_spec(w.shape) for w in lru_w],
        out_specs=[q_out, kv_spec, kv_spec, kvb_spec, kvb_spec, hg_out, cs_spec, h_spec],
        out_shape=[b16o, kv_shape, kv_shape, kvb_shape, kvb_shape, b16o,
                   jax.ShapeDtypeStruct((s, CONV_WIDTH - 1, D_RNN), F32),
                   jax.ShapeDtypeStruct((s, 1, D_RNN), F32)],
        scratch_shapes=_lru_scratch(tm) + [pltpu.VMEM((2, tm, D_MODEL), BF16), pltpu.VMEM((2, tm, D_RNN), F32),
                                           pltpu.VMEM((2, tm, D_RNN), F32), pltpu.VMEM((2, tm, 2 * D_RNN), F32)],
        compiler_params=pltpu.CompilerParams(dimension_semantics=("arbitrary",),
                                             vmem_limit_bytes=VMEM_LIMIT),
        name="proj_lru",
    )(x.reshape(n, D_MODEL), *consts, conv_state, h0, *lru_w)


def _sb_split(z, mask):
    n = z.shape[1] // KEY_BLOCK
    tq = z.shape[0] // 2
    sp = _softplus(z)
    cats = []
    for h in range(2):
        for d in range(n):
            blk = sp[h * tq:(h + 1) * tq, d * KEY_BLOCK:(d + 1) * KEY_BLOCK]
            if d == n - 1 and mask is not None:
                blk = jnp.where(mask, blk, 0.0)
            hi = blk.astype(BF16)
            lo = (blk - hi.astype(F32)).astype(BF16)
            cats.append(jnp.concatenate([hi, lo], axis=1))
    return jnp.concatenate(cats, axis=0)


def _sb_cumsum(cat, tri2):
    return jnp.dot(cat, tri2, preferred_element_type=F32)


def _sb_weights(z, r, carries, mask):
    n = z.shape[1] // KEY_BLOCK
    tq = z.shape[0] // 2
    rows, new_carries = [], []
    for h in range(2):
        carry = None if carries is None else carries[h]
        es = [None] * n
        for d in reversed(range(n)):
            rd = r[(h * n + d) * tq:(h * n + d + 1) * tq]
            x = z[h * tq:(h + 1) * tq, d * KEY_BLOCK:(d + 1) * KEY_BLOCK] + rd[:, :KEY_BLOCK]
            if carry is not None:
                x = x + carry
            e = jnp.exp(x)
            if d == n - 1 and mask is not None:
                e = jnp.where(mask, e, 0.0)
            es[d] = e.astype(BF16)
            tot = rd[:, KEY_BLOCK:]
            carry = tot if carry is None else carry + tot
        rows.append(es[0] if n == 1 else jnp.concatenate(es, axis=1))
        new_carries.append(carry)
    return jnp.concatenate(rows, axis=0), new_carries


def _tri2():
    j = np.arange(KEY_BLOCK)[:, None]
    s = np.arange(KEY_BLOCK)[None, :]
    half = np.concatenate([-(j >= s).astype(np.float32), -np.ones((KEY_BLOCK, LANES), np.float32)], axis=1)
    return jnp.asarray(np.concatenate([half, half], axis=0), dtype=BF16)


def _stack_heads(q, lo_half):
    zero = jnp.zeros_like(q)
    return jnp.concatenate([jnp.where(lo_half, q, zero), jnp.where(lo_half, zero, q)], axis=0)


def _sb_window(z_fn, pv_fn, carries, mask, tri2):
    z = z_fn()
    e, cs = _sb_weights(z, _sb_cumsum(_sb_split(z, mask), tri2), carries, mask)
    pv = pv_fn(e)
    tq = pv.shape[0] // 2
    return (pv[:tq], cs[0]), (pv[tq:], cs[1])


def _finish(store, j0, outs, step_fn):
    (a0, c0), (a1, c1) = outs
    store(a0, a1)
    if isinstance(j0, int) and j0 < 0:
        return lambda: None

    def alive(x0, x1):
        return jnp.max(jnp.maximum(x0, x1)) >= LOG_KEEP_FLOOR

    go = jnp.logical_and(j0 >= 0, alive(c0, c1))

    def walk():
        @pl.when(go)
        def _():
            def body(s):
                j, b0, b1, d0, d1, _ = s
                (p0, d0), (p1, d1) = step_fn(j, [d0, d1])
                return j - 1, b0 + p0, b1 + p1, d0, d1, jnp.logical_and(j >= 1, alive(d0, d1))

            _, b0, b1, _, _, _ = lax.while_loop(lambda s: s[-1], body,
                                                (jnp.int32(j0), a0, a1, c0, c1, jnp.bool_(True)))
            store(b0, b1)

    return walk


def _sb_chains(chains, tri2):
    zs = [c[0]() for c in chains]
    sums = [_sb_cumsum(_sb_split(z, c[2]), tri2) for z, c in zip(zs, chains)]
    weights = [_sb_weights(z, r, None, c[2]) for z, r, c in zip(zs, sums, chains)]
    walks = []
    for c, (e, cs) in zip(chains, weights):
        pv = c[1](e)
        tq = pv.shape[0] // 2
        walks.append(_finish(c[3], c[5], ((pv[:tq], cs[0]), (pv[tq:], cs[1])), c[4]))
    for walk in walks:
        walk()


def _attn_prompt_kernel(q_ref, kt_ref, vt_ref, tri_ref, o_ref):
    t = q_ref.shape[1]
    nq = t // Q_ROWS
    row = lax.broadcasted_iota(jnp.int32, (Q_ROWS, LANES), 0)
    lane = lax.broadcasted_iota(jnp.int32, (Q_ROWS, LANES), 1)
    lo_half = lane < HEAD_DIM
    diag_mask = lane < row
    lane_minus_row = lane - row
    tri2 = tri_ref[...]

    def window(ref, first, n):
        return jnp.concatenate([ref[0, first + d] for d in range(n)], axis=1)

    def qblocks(blocks, n_fast):
        chains = []
        for i in blocks:
            q2 = _stack_heads(q_ref[0, _rows(i, Q_ROWS), :], lo_half)
            first = i + 1 - n_fast

            def z_fn(q2=q2, first=first):
                return jnp.dot(q2, window(kt_ref, first, n_fast), preferred_element_type=F32)

            def pv_fn(e, first=first):
                return lax.dot_general(e, window(vt_ref, first, n_fast), NT_DIMS, preferred_element_type=F32)

            def store(b0, b1, i=i):
                o_ref[_rows(i, Q_ROWS), :] = jnp.where(lo_half, b0, b1).astype(BF16)

            def step(j, cs, i=i, q2=q2):
                m = lane_minus_row < (i - j) * KEY_BLOCK
                return _sb_window(lambda: jnp.dot(q2, kt_ref[0, j], preferred_element_type=F32),
                                  lambda e: lax.dot_general(e, vt_ref[0, j], NT_DIMS, preferred_element_type=F32),
                                  cs, m, tri2)

            chains.append((z_fn, pv_fn, diag_mask, store, step, i - n_fast))
        _sb_chains(chains, tri2)

    n_head = FAST_BLOCKS - 1
    for i in range(n_head):
        qblocks([i], i + 1)

    start = n_head + (nq - n_head) % Q_GROUP
    if start > n_head:
        qblocks(list(range(n_head, start)), FAST_BLOCKS)

    def group(g, carry):
        qblocks([start + g * Q_GROUP + u for u in range(Q_GROUP)], FAST_BLOCKS)
        return carry

    lax.fori_loop(0, (nq - start) // Q_GROUP, group, 0)


def _attn_prompt(qb, ktb, vtb, tri2):
    b, t, _ = qb.shape
    assert t % Q_ROWS == 0 and Q_ROWS == KEY_BLOCK
    qspec = pl.BlockSpec((1, t, LANES), lambda i, h: (i, 0, h))
    kspec = pl.BlockSpec((1, t // KEY_BLOCK, LANES, KEY_BLOCK), lambda i, h: (i, 0, h, 0))
    return pl.pallas_call(
        _attn_prompt_kernel,
        grid=(b, D_ATTN // LANES),
        in_specs=[qspec, kspec, kspec, _const_spec(tri2.shape)],
        out_specs=pl.BlockSpec((t, LANES), lambda i, h: (i, h)),
        out_shape=jax.ShapeDtypeStruct((b * t, D_ATTN), BF16),
        compiler_params=pltpu.CompilerParams(dimension_semantics=("arbitrary", "arbitrary"),
                                             vmem_limit_bytes=VMEM_LIMIT),
        name="attn_prompt",
    )(qb, ktb, vtb, tri2)


def _attn_sample_kernel(n_tail, q_ref, kn_ref, vn_ref, ckt_ref, cvt_ref, ck_hbm, cv_hbm, tri_ref, o_ref,
                        kbuf, vbuf):
    s = pl.program_id(0)
    tq = q_ref.shape[1]
    n_past = ck_hbm.shape[2] // KEY_BLOCK
    row = lax.broadcasted_iota(jnp.int32, (tq, LANES), 0)
    lane = lax.broadcasted_iota(jnp.int32, (tq, LANES), 1)
    lo_half = lane < HEAD_DIM
    tri2 = tri_ref[...]

    pad = jnp.zeros((KEY_BLOCK - tq, LANES), BF16)
    chains = []
    for p in range(q_ref.shape[2] // LANES):
        cols = slice(p * LANES, (p + 1) * LANES)
        q2 = _stack_heads(q_ref[0, :, cols], lo_half)

        def z_fn(q2=q2, cols=cols):
            z_new = lax.dot_general(q2, jnp.concatenate([kn_ref[0, :, cols], pad], axis=0), NT_DIMS,
                                    preferred_element_type=F32)
            if n_tail == 0:
                return z_new
            z_past = jnp.dot(q2, ckt_ref[0, cols, :].astype(BF16), preferred_element_type=F32)
            return jnp.concatenate([z_past, z_new], axis=1)

        def pv_fn(e, cols=cols):
            pv = jnp.dot(e[:, n_tail * KEY_BLOCK:], jnp.concatenate([vn_ref[0, :, cols], pad], axis=0),
                         preferred_element_type=F32)
            if n_tail == 0:
                return pv
            return pv + lax.dot_general(e[:, :n_tail * KEY_BLOCK], cvt_ref[0, cols, :].astype(BF16), NT_DIMS,
                                        preferred_element_type=F32)

        def store(a0, a1, cols=cols):
            o_ref[:, cols] = jnp.where(lo_half, a0, a1).astype(BF16)

        def step(j, cs, cols=cols, q2=q2, p=p):
            keys = pl.ds(pl.multiple_of(j * KEY_BLOCK, KEY_BLOCK), KEY_BLOCK)
            pltpu.sync_copy(ck_hbm.at[s, pl.ds(p * LANES, LANES), keys], kbuf)
            pltpu.sync_copy(cv_hbm.at[s, pl.ds(p * LANES, LANES), keys], vbuf)
            return _sb_window(lambda: jnp.dot(q2, kbuf[...].astype(BF16), preferred_element_type=F32),
                              lambda e: lax.dot_general(e, vbuf[...].astype(BF16), NT_DIMS,
                                                        preferred_element_type=F32),
                              cs, None, tri2)

        chains.append((z_fn, pv_fn, lane < row, store, step, n_past - 1 - n_tail))
    _sb_chains(chains, tri2)


def _attn_sample(qb, knb, vnb, cache_kt, cache_vt, tri2):
    s, tq, _ = qb.shape
    past_len = cache_kt.shape[2]
    assert past_len % KEY_BLOCK == 0 and tq <= KEY_BLOCK and tq % 16 == 0
    n_tail = min(FAST_BLOCKS - 1, past_len // KEY_BLOCK)
    assert n_tail > 0 and past_len % (n_tail * KEY_BLOCK) == 0
    new = pl.BlockSpec((1, tq, D_ATTN), lambda i: (i, 0, 0))
    tail = pl.BlockSpec((1, D_ATTN, n_tail * KEY_BLOCK), lambda i: (i, 0, past_len // (n_tail * KEY_BLOCK) - 1))
    hbm = pl.BlockSpec(memory_space=pl.ANY)
    return pl.pallas_call(
        functools.partial(_attn_sample_kernel, n_tail),
        grid=(s,),
        in_specs=[new, new, new, tail, tail, hbm, hbm, _const_spec(tri2.shape)],
        out_specs=pl.BlockSpec((tq, D_ATTN), lambda i: (i, 0)),
        out_shape=jax.ShapeDtypeStruct((s * tq, D_ATTN), BF16),
        scratch_shapes=[pltpu.VMEM((LANES, KEY_BLOCK), F32), pltpu.VMEM((LANES, KEY_BLOCK), F32)],
        compiler_params=pltpu.CompilerParams(dimension_semantics=("arbitrary",),
                                             vmem_limit_bytes=VMEM_LIMIT),
        name="attn_sample",
    )(qb, knb, vnb, cache_kt, cache_vt, cache_kt, cache_vt, tri2)


FF_CHUNK = 1024


def _out_kernel(n_first, xa_ref, xb_ref, oa_ref, ob_ref, hga_ref, hgb_ref, g1_ref, wgab_ref, wa_ref, wb_ref,
                wo_ref, g2_ref, g3_ref, wup_ref, wdn_ref, g4_ref, ya_ref, yb_ref):
    first = pl.program_id(0) < n_first
    tm = xa_ref.shape[0]
    halves = [pl.ds(0, tm // 2), pl.ds(tm // 2, tm // 2)]
    n_ff = D_FF // FF_CHUNK
    st = [{}, {}]

    def pick(a_ref, b_ref, h):
        return jnp.where(first, a_ref[halves[h], :], b_ref[halves[h], :])

    def norm_in(h):
        st[h]["x"] = pick(xa_ref, xb_ref, h)
        st[h]["xn"] = _rms(st[h]["x"], g1_ref[...]).astype(BF16)

    def attn_gate(h):
        g_a = jnp.dot(st[h]["xn"], wgab_ref[:, :D_MODEL], preferred_element_type=F32)
        y_a = jnp.dot(pick(oa_ref, ob_ref, h), wa_ref[...], preferred_element_type=F32)
        st[h]["m"] = jax.nn.sigmoid(g_a) * y_a

    def rnn_gate(h):
        g_b = jnp.dot(st[h]["xn"], wgab_ref[:, D_MODEL:], preferred_element_type=F32)
        y_b = jnp.dot(pick(hga_ref, hgb_ref, h), wb_ref[...], preferred_element_type=F32)
        st[h]["m"] = st[h]["m"] + jax.nn.sigmoid(g_b) * y_b

    def mix(h):
        st[h]["mix"] = jnp.dot(st[h]["m"].astype(BF16), wo_ref[...], preferred_element_type=F32)

    def norm_mid(h):
        st[h]["x1"] = st[h]["x"] + _rms(st[h]["mix"], g2_ref[...])
        st[h]["f"] = _rms(st[h]["x1"], g3_ref[...]).astype(BF16)

    def ffn(h, c):
        cols = slice(c * FF_CHUNK, (c + 1) * FF_CHUNK)
        up = jnp.maximum(jnp.dot(st[h]["f"], wup_ref[:, cols], preferred_element_type=F32), 0.0)
        dn = jnp.dot((up * up).astype(BF16), wdn_ref[cols, :], preferred_element_type=F32)
        st[h]["acc"] = dn if c == 0 else st[h]["acc"] + dn

    def norm_out(h):
        yb_ref[halves[h], :] = st[h]["x1"] + _rms(st[h]["acc"], g4_ref[...])

    norm_in(0)
    norm_in(1)
    attn_gate(0)
    rnn_gate(0)
    mix(0)
    attn_gate(1)
    norm_mid(0)
    rnn_gate(1)
    mix(1)
    ffn(0, 0)
    norm_mid(1)
    for c in range(1, n_ff):
        ffn(0, c)
    ffn(1, 0)
    norm_out(0)
    for c in range(1, n_ff):
        ffn(1, c)
    norm_out(1)

    @pl.when(first)
    def _():
        ya_ref[...] = yb_ref[...]


def _out(xa, xb, oa, ob, hga, hgb, g1, wgab, wa, wb, wo, g2, g3, wup, wdn, g4):
    na, nb = xa.shape[0], xb.shape[0]
    tm = TOKEN_TILE
    assert na % tm == 0 and nb % tm == 0
    n_first = na // tm
    a_tok = lambda i: (jnp.minimum(i, n_first - 1), 0)
    b_tok = lambda i: (jnp.maximum(i - n_first, 0), 0)
    consts = (g1, wgab, wa, wb, wo, g2, g3, wup, wdn, g4)
    return pl.pallas_call(
        functools.partial(_out_kernel, n_first),
        grid=((na + nb) // tm,),
        in_specs=[pl.BlockSpec((tm, D_MODEL), a_tok), pl.BlockSpec((tm, D_MODEL), b_tok),
                  pl.BlockSpec((tm, D_ATTN), a_tok), pl.BlockSpec((tm, D_ATTN), b_tok),
                  pl.BlockSpec((tm, D_RNN), a_tok), pl.BlockSpec((tm, D_RNN), b_tok)]
        + [_const_spec(c.shape) for c in consts],
        out_specs=[pl.BlockSpec((tm, D_MODEL), a_tok), pl.BlockSpec((tm, D_MODEL), b_tok)],
        out_shape=[jax.ShapeDtypeStruct((na, D_MODEL), F32), jax.ShapeDtypeStruct((nb, D_MODEL), F32)],
        compiler_params=pltpu.CompilerParams(dimension_semantics=("arbitrary",),
                                             vmem_limit_bytes=VMEM_LIMIT),
        name="out",
    )(xa, xb, oa, ob, hga, hgb, *consts)


def _block_diag(w):
    n, c, d = w.shape
    return jnp.einsum("ncd,nm->ncmd", w, jnp.eye(n, dtype=w.dtype)).reshape(n * c, n * d)


def _heads_last(kt):
    s, _, t = kt.shape
    return kt.reshape(s, N_HEADS, HEAD_DIM, t).transpose(0, 3, 1, 2)[None]


def _time_last(cache):
    s, t = cache.shape[:2]
    return cache.transpose(0, 2, 3, 1).reshape(s, D_ATTN, t)


def kernel(x_prompt, x_sample, cache_k, cache_v, state_conv, state_h, w_in, g_pre_mix, w_conv, b_conv, w_r, b_r, w_i, b_i, lam, w_a_out, w_b_out, w_o, g_post_mix, g_pre_ffn, w_up, w_down, g_post_ffn):
    assert w_in.shape[0] == 1
    row = lambda a: a[0].reshape(1, -1)
    w_in0 = w_in[0]
    n_seq = 3 * D_ATTN + 2 * D_RNN
    wq = (w_in0[:, :D_ATTN] * (HEAD_DIM ** -0.5)).astype(BF16)
    wkv = w_in0[:, D_ATTN:3 * D_ATTN].astype(BF16)
    wug = w_in0[:, 3 * D_ATTN:n_seq].astype(BF16)
    wgab = w_in0[:, n_seq:].astype(BF16)
    wg = jnp.concatenate([_block_diag(w_r[0]), _block_diag(w_i[0])], axis=1).astype(BF16)
    bg = jnp.concatenate([row(b_r), row(b_i)], axis=1)
    lru_w = (w_conv[0], row(b_conv), wg, bg, row(lam))
    g1 = row(g_pre_mix)
    tri2 = _tri2()

    bp, tp, _ = x_prompt.shape
    bs, ts, _ = x_sample.shape
    n_p, n_s = bp * tp, bs * ts

    zc = jnp.zeros((bp, CONV_WIDTH - 1, D_RNN), F32)
    zh = jnp.zeros((bp, 1, D_RNN), F32)
    qb, kt, vt, ktb, vtb, hg_p, cp, hp = _proj_lru(x_prompt, g1, wq, wug, wkv.T, zc, zh, lru_w)
    o_p = _attn_prompt(qb.reshape(bp, tp, D_ATTN), ktb, vtb, tri2)

    s3 = lambda a: a.reshape(bs, ts, a.shape[-1])
    qs, ks, vs, ksb, vsb, hg_s, cs, hs = _proj(x_sample, g1, wq, wug, wkv, state_conv[0],
                                               state_h[0].reshape(bs, 1, D_RNN), lru_w)
    o_s = _attn_sample(s3(qs), s3(ksb), s3(vsb), _time_last(cache_k[0]), _time_last(cache_v[0]), tri2)

    ys, yp = _out(x_sample.reshape(n_s, D_MODEL), x_prompt.reshape(n_p, D_MODEL), o_s, o_p, hg_s, hg_p, g1, wgab,
                  w_a_out[0].astype(BF16), w_b_out[0].astype(BF16), w_o[0].astype(BF16), row(g_post_mix),
                  row(g_pre_ffn), w_up[0].astype(BF16), w_down[0].astype(BF16), row(g_post_ffn))
    return (yp.reshape(bp, tp, D_MODEL), ys.reshape(bs, ts, D_MODEL), _heads_last(kt), _heads_last(vt),
            cp[None], hp.reshape(1, bp, D_RNN), ks.reshape(1, bs, ts, N_HEADS, HEAD_DIM),
            vs.reshape(1, bs, ts, N_HEADS, HEAD_DIM), cs[None], hs.reshape(1, bs, D_RNN))
```

```python
import functools

import jax
import jax.numpy as jnp
import numpy as np
from jax import lax
from jax.experimental import pallas as pl
from jax.experimental.pallas import tpu as pltpu

F32 = jnp.float32
BF16 = jnp.bfloat16

D_MODEL = 1024
N_HEADS = 8
HEAD_DIM = 64
D_ATTN = N_HEADS * HEAD_DIM
D_RNN = 512
N_RNN_BLOCKS = 8
CONV_WIDTH = 4
LRU_C = 8.0
D_FF = 4 * D_MODEL
EPS = 1e-6

LANES = 128
SUBLANES = 8
KEY_BLOCK = 128
Q_ROWS = 128
FAST_BLOCKS = 3
Q_GROUP = 4
TOKEN_TILE = 512
SCAN_SLICES = 16
MXU_N = 256
LOG_KEEP_FLOOR = -104.0
VMEM_LIMIT = 56 * 1024 * 1024

NT_DIMS = (((1,), (1,)), ((), ()))
LOG2_E = 1.4426950408889634


def _rms(xf, g):
    return xf * lax.rsqrt(jnp.mean(xf * xf, axis=-1, keepdims=True) + EPS) * g


def _softplus(x):
    return jnp.maximum(x, 0.0) + jnp.log(1.0 + jnp.exp2(jnp.abs(x) * -LOG2_E))


def _sigmoid(x):
    return 0.5 * jnp.tanh(0.5 * x) + 0.5


def _const_spec(shape):
    nd = len(shape)
    return pl.BlockSpec(shape, lambda *_: (0,) * nd, pipeline_mode=pl.Buffered(1))


COL_Q, COL_K, COL_V, COL_U, COL_GR, COL_GA, COL_GB = 0, 1, 2, 3, 4, 5, 7
Q_SCALE = HEAD_DIM ** -0.5


def _w_in_cols(j):
    return pl.BlockSpec((D_MODEL, D_ATTN), lambda *_: (0, j), pipeline_mode=pl.Buffered(1))


def _rows(block_index, size):
    if isinstance(block_index, int):
        return pl.ds(block_index * size, size)
    return pl.ds(pl.multiple_of(block_index * size, size), size)


def _gelu_tanh(x):
    return 0.5 * x * (1.0 + jnp.tanh(np.sqrt(2.0 / np.pi).astype(np.float32) * (x + 0.044715 * (x * x * x))))


def _conv_seed(first, cs_ref, ext_ref):
    @pl.when(first)
    def _():
        ext_ref[0:SUBLANES, :] = jnp.zeros((SUBLANES, D_RNN), F32)
        ext_ref[SUBLANES - (CONV_WIDTH - 1):SUBLANES, :] = cs_ref[0]


def _scan_seed(first, h0_ref, hc_ref):
    @pl.when(first)
    def _():
        hc_ref[...] = jnp.broadcast_to(h0_ref[0], hc_ref.shape)


def _lru_conv(u, wc_ref, bc_ref, ext_ref):
    tt = u.shape[0]
    ext_ref[SUBLANES:SUBLANES + tt, :] = u
    uc = bc_ref[...] + wc_ref[CONV_WIDTH - 1:CONV_WIDTH, :] * u
    for d in range(1, CONV_WIDTH):
        uc = uc + wc_ref[CONV_WIDTH - 1 - d:CONV_WIDTH - d, :] * ext_ref[SUBLANES - d:SUBLANES - d + tt, :]
    ext_ref[0:SUBLANES, :] = ext_ref[tt:tt + SUBLANES, :]
    return uc


def _lru_gates(uc, wg_ref, bg_ref):
    return jnp.dot(uc.astype(BF16), wg_ref[...], preferred_element_type=F32) + bg_ref[...]


def _lru_scan(g, uc, g_rnn, lam_ref, h_prev):
    tt = uc.shape[0]
    r = _sigmoid(g[:, :D_RNN])
    ig = _sigmoid(g[:, D_RNN:])
    log_a = (-LRU_C * _softplus(-lam_ref[...])) * r
    a = jnp.exp(log_a)
    m2 = -jnp.tanh(log_a) * (a * a + 1.0)
    b = jnp.where(m2 > 0.0, m2 * lax.rsqrt(m2), 0.0) * (ig * uc)

    n_groups = tt // SUBLANES
    a3 = a.reshape(n_groups, SUBLANES, D_RNN)
    b3 = b.reshape(n_groups, SUBLANES, D_RNN)
    sub = lax.broadcasted_iota(jnp.int32, a3.shape, 1)
    k = 1
    while k < SUBLANES:
        has_prev = sub >= k
        b_prev = jnp.where(has_prev, pltpu.roll(b3, k, axis=1), 0.0)
        a_prev = jnp.where(has_prev, pltpu.roll(a3, k, axis=1), 1.0)
        b3 = b3 + a3 * b_prev
        a3 = a3 * a_prev
        k *= 2

    hs = []
    for gi in range(n_groups):
        h = b3[gi] + a3[gi] * h_prev
        hs.append(h)
        h_prev = jnp.broadcast_to(h[SUBLANES - 1:SUBLANES, :], h.shape)
    return (jnp.concatenate(hs, axis=0) * _gelu_tanh(g_rnn)).astype(BF16), h_prev


def _lru_state_specs(index_map):
    past = CONV_WIDTH - 1
    return (pl.BlockSpec((1, past, D_RNN), index_map), pl.BlockSpec((1, 1, D_RNN), index_map))


def _lru_scratch(tt):
    return [pltpu.VMEM((tt + SUBLANES, D_RNN), F32), pltpu.VMEM((SUBLANES, D_RNN), F32)]


def _proj_kernel(t, x_ref, g_ref, wq_ref, wk_ref, wv_ref, wu_ref, wgr_ref, cs_ref, h0_ref, wc_ref, bc_ref, wg_ref,
                 bg_ref, lam_ref, qb_ref, k_ref, v_ref, kb_ref, vb_ref, hg_ref, cn_ref, hl_ref, ext_ref):
    n_streams = x_ref.shape[0] // t
    past = CONV_WIDTH - 1
    xn = _rms(x_ref[...], g_ref[...]).astype(BF16)
    proj = lambda w_ref: jnp.dot(xn, w_ref[...], preferred_element_type=F32)
    u, g_rnn = proj(wu_ref), proj(wgr_ref)
    qb_ref[...] = (proj(wq_ref) * Q_SCALE).astype(BF16)
    k, v = proj(wk_ref), proj(wv_ref)
    k_ref[...] = k
    v_ref[...] = v
    kb_ref[...] = k.astype(BF16)
    vb_ref[...] = v.astype(BF16)

    ucs = []
    for s in range(n_streams):
        ext = ext_ref.at[s]
        ext[0:SUBLANES, :] = jnp.zeros((SUBLANES, D_RNN), F32)
        ext[SUBLANES - past:SUBLANES, :] = cs_ref[s]
        ucs.append(_lru_conv(u[s * t:(s + 1) * t], wc_ref, bc_ref, ext))
        cn_ref[s] = ext[SUBLANES - past:SUBLANES, :]
    uc = jnp.concatenate(ucs, axis=0)
    gates = _lru_gates(uc, wg_ref, bg_ref)
    for s in range(n_streams):
        rows = slice(s * t, (s + 1) * t)
        hg, h_last = _lru_scan(gates[rows], uc[rows], g_rnn[rows], lam_ref,
                               jnp.broadcast_to(h0_ref[s], (SUBLANES, D_RNN)))
        hg_ref[rows, :] = hg
        hl_ref[s] = h_last[0:1, :]


def _proj(x, g, w_in, conv_state, h0, lru_w):
    s, t, _ = x.shape
    n = s * t
    tm = TOKEN_TILE
    assert n % tm == 0 and tm % t == 0 and t % SUBLANES == 0
    per_tile = tm // t
    tok = lambda i: (i, 0)
    row_out = pl.BlockSpec((tm, D_ATTN), tok)
    f32o = jax.ShapeDtypeStruct((n, D_ATTN), F32)
    b16o = jax.ShapeDtypeStruct((n, D_ATTN), BF16)
    st3 = pl.BlockSpec((per_tile, CONV_WIDTH - 1, D_RNN), lambda i: (i, 0, 0))
    st1 = pl.BlockSpec((per_tile, 1, D_RNN), lambda i: (i, 0, 0))
    w_cols = (COL_Q, COL_K, COL_V, COL_U, COL_GR)
    return pl.pallas_call(
        functools.partial(_proj_kernel, t),
        grid=(n // tm,),
        in_specs=[pl.BlockSpec((tm, D_MODEL), tok), _const_spec(g.shape)] + [_w_in_cols(j) for j in w_cols]
        + [st3, st1] + [_const_spec(w.shape) for w in lru_w],
        out_specs=[row_out] * 6 + [st3, st1],
        out_shape=[b16o, f32o, f32o, b16o, b16o, b16o,
                   jax.ShapeDtypeStruct((s, CONV_WIDTH - 1, D_RNN), F32),
                   jax.ShapeDtypeStruct((s, 1, D_RNN), F32)],
        scratch_shapes=[pltpu.VMEM((per_tile, t + SUBLANES, D_RNN), F32)],
        compiler_params=pltpu.CompilerParams(dimension_semantics=("arbitrary",),
                                             vmem_limit_bytes=VMEM_LIMIT),
        name="proj",
    )(x.reshape(n, D_MODEL), g, *([w_in] * len(w_cols)), conv_state, h0, *lru_w)


def _proj_lru_kernel(per, n_tiles, x_ref, g_ref, wq_ref, wu_ref, wgr_ref, wkvt_ref, cs_ref, h0_ref, wc_ref, bc_ref,
                     wg_ref, bg_ref, lam_ref, qb_ref, kt_ref, vt_ref, ktb_ref, vtb_ref, hg_ref, cn_ref, hl_ref,
                     ext_ref, hc_ref, xn_ref, uc_ref, gr_ref, gt_ref):
    i = pl.program_id(0)
    dot_tile, scan_tile = i - 1, i - 2
    dot_live = jnp.logical_and(dot_tile >= 0, dot_tile < n_tiles)
    scan_live = jnp.logical_and(scan_tile >= 0, scan_tile < n_tiles)
    _conv_seed(jnp.logical_and(dot_live, dot_tile % per == 0), cs_ref, ext_ref)
    _scan_seed(jnp.logical_and(scan_live, scan_tile % per == 0), h0_ref, hc_ref)

    @pl.when(i == 0)
    def _():
        xn_ref[...] = jnp.zeros(xn_ref.shape, BF16)
        hc_ref[...] = jnp.zeros(hc_ref.shape, F32)
        ext_ref[...] = jnp.zeros(ext_ref.shape, F32)
        for ref in (uc_ref, gr_ref, gt_ref):
            ref[...] = jnp.zeros(ref.shape, F32)

    tm = x_ref.shape[0]
    sc, dt = i % 2, (i + 1) % 2
    xn = xn_ref[dt]
    state = [hc_ref[...]]
    rows_per = tm // SCAN_SLICES
    u_parts = {}
    uc_b = [None]

    def scan_slice(p):
        rows = pl.ds(p * rows_per, rows_per)
        hg, state[0] = _lru_scan(gt_ref[sc, rows, :], uc_ref[sc, rows, :], gr_ref[sc, rows, :], lam_ref, state[0])
        hg_ref[rows, :] = hg

    def rms_piece(r):
        rows = pl.ds(r * (tm // 4), tm // 4)
        xn_ref[sc, rows, :] = _rms(x_ref[rows, :], g_ref[...]).astype(BF16)

    def ug_chunk(c):
        is_u = c < D_RNN // MXU_N
        cols = slice(c * MXU_N % D_RNN, c * MXU_N % D_RNN + MXU_N)
        r = jnp.dot(xn, (wu_ref if is_u else wgr_ref)[:, cols], preferred_element_type=F32)
        if is_u:
            u_parts[c] = r
        else:
            gr_ref[dt, :, cols] = r

    def conv():
        uc = _lru_conv(jnp.concatenate([u_parts[c] for c in sorted(u_parts)], axis=1), wc_ref, bc_ref, ext_ref)
        uc_ref[dt] = uc
        uc_b[0] = uc.astype(BF16)

    def q_chunk(c):
        cols = slice(c * MXU_N, (c + 1) * MXU_N)
        qb_ref[:, cols] = (jnp.dot(xn, wq_ref[:, cols], preferred_element_type=F32) * Q_SCALE).astype(BF16)

    def kv_chunk(c):
        toks = slice(c * MXU_N, (c + 1) * MXU_N)
        kv = lax.dot_general(wkvt_ref[...], xn[toks], NT_DIMS, preferred_element_type=F32)
        k, v = kv[:D_ATTN], kv[D_ATTN:]
        kt_ref[0, :, toks] = k
        vt_ref[0, :, toks] = v
        for j in range(MXU_N // KEY_BLOCK):
            cols = slice(j * KEY_BLOCK, (j + 1) * KEY_BLOCK)
            ktb_ref[0, c * (MXU_N // KEY_BLOCK) + j] = k[:, cols].astype(BF16)
            vtb_ref[0, c * (MXU_N // KEY_BLOCK) + j] = v[:, cols].astype(BF16)

    def gates_chunk(c):
        cols = slice(c * MXU_N, (c + 1) * MXU_N)
        gt_ref[dt, :, cols] = jnp.dot(uc_b[0], wg_ref[:, cols], preferred_element_type=F32) + bg_ref[:, cols]

    S, P = scan_slice, functools.partial
    program = [
        P(ug_chunk, 0), P(S, 0), P(S, 1), P(ug_chunk, 1), P(S, 2), P(S, 3), P(ug_chunk, 2), conv,
        P(ug_chunk, 3), P(S, 4), P(S, 5), P(q_chunk, 0), P(S, 6), P(S, 7),
        P(q_chunk, 1), P(rms_piece, 0), P(rms_piece, 1), P(S, 8),
        P(kv_chunk, 0), P(S, 9), P(S, 10), P(S, 11), P(rms_piece, 2),
        P(kv_chunk, 1), P(S, 12), P(S, 13), P(S, 14), P(rms_piece, 3),
        P(gates_chunk, 0), P(S, 15), P(gates_chunk, 1), P(gates_chunk, 2), P(gates_chunk, 3),
    ]
    for piece in program:
        piece()
    h_last = state[0]
    hc_ref[...] = h_last

    @pl.when(jnp.logical_and(dot_live, dot_tile % per == per - 1))
    def _():
        cn_ref[0] = ext_ref[SUBLANES - (CONV_WIDTH - 1):SUBLANES, :]

    @pl.when(jnp.logical_and(scan_live, scan_tile % per == per - 1))
    def _():
        hl_ref[0] = h_last[0:1, :]


def _proj_lru(x, g, w_in, wkvt, conv_state, h0, lru_w):
    s, t, _ = x.shape
    n = s * t
    tm = TOKEN_TILE
    assert t % tm == 0 and tm % KEY_BLOCK == 0
    per = t // tm
    n_tiles = n // tm
    tile = lambda i, lag: jnp.clip(i - lag, 0, n_tiles - 1)
    tok_in = pl.BlockSpec((tm, D_MODEL), lambda i: (tile(i, 0), 0))
    q_out = pl.BlockSpec((tm, D_ATTN), lambda i: (tile(i, 1), 0))
    hg_out = pl.BlockSpec((tm, D_RNN), lambda i: (tile(i, 2), 0))
    b16o = jax.ShapeDtypeStruct((n, D_ATTN), BF16)
    kv_spec = pl.BlockSpec((1, D_ATTN, tm), lambda i: (tile(i, 1) // per, 0, tile(i, 1) % per))
    kvb_spec = pl.BlockSpec((1, tm // KEY_BLOCK, D_ATTN, KEY_BLOCK),
                            lambda i: (tile(i, 1) // per, tile(i, 1) % per, 0, 0))
    kv_shape = jax.ShapeDtypeStruct((s, D_ATTN, t), F32)
    kvb_shape = jax.ShapeDtypeStruct((s, t // KEY_BLOCK, D_ATTN, KEY_BLOCK), BF16)
    cs_spec, _ = _lru_state_specs(lambda i: (tile(i, 1) // per, 0, 0))
    _, h_spec = _lru_state_specs(lambda i: (tile(i, 2) // per, 0, 0))
    w_cols = (COL_Q, COL_U, COL_GR)
    return pl.pallas_call(
        functools.partial(_proj_lru_kernel, per, n_tiles),
        grid=(n_tiles + 2,),
        in_specs=[tok_in, _const_spec(g.shape)] + [_w_in_cols(j) for j in w_cols] + [_const_spec(wkvt.shape)]
        + [cs_spec, h_spec] + [_const_spec(w.shape) for w in lru_w],
        out_specs=[q_out, kv_spec, kv_spec, kvb_spec, kvb_spec, hg_out, cs_spec, h_spec],
        out_shape=[b16o, kv_shape, kv_shape, kvb_shape, kvb_shape, b16o,
                   jax.ShapeDtypeStruct((s, CONV_WIDTH - 1, D_RNN), F32),
                   jax.ShapeDtypeStruct((s, 1, D_RNN), F32)],
        scratch_shapes=_lru_scratch(tm) + [pltpu.VMEM((2, tm, D_MODEL), BF16), pltpu.VMEM((2, tm, D_RNN), F32),
                                           pltpu.VMEM((2, tm, D_RNN), F32), pltpu.VMEM((2, tm, 2 * D_RNN), F32)],
        compiler_params=pltpu.CompilerParams(dimension_semantics=("arbitrary",),
                                             vmem_limit_bytes=VMEM_LIMIT),
        name="proj_lru",
    )(x.reshape(n, D_MODEL), g, *([w_in] * len(w_cols)), wkvt, conv_state, h0, *lru_w)


def _sb_split(z, mask):
    n = z.shape[1] // KEY_BLOCK
    tq = z.shape[0] // 2
    sp = _softplus(z)
    cats = []
    for h in range(2):
        for d in range(n):
            blk = sp[h * tq:(h + 1) * tq, d * KEY_BLOCK:(d + 1) * KEY_BLOCK]
            if d == n - 1 and mask is not None:
                blk = jnp.where(mask, blk, 0.0)
            hi = blk.astype(BF16)
            lo = (blk - hi.astype(F32)).astype(BF16)
            cats.append(jnp.concatenate([hi, lo], axis=1))
    return jnp.concatenate(cats, axis=0)


def _sb_cumsum(cat, tri2):
    return jnp.dot(cat, tri2, preferred_element_type=F32)


def _sb_weights(z, r, carries, mask):
    n = z.shape[1] // KEY_BLOCK
    tq = z.shape[0] // 2
    rows, new_carries = [], []
    for h in range(2):
        carry = None if carries is None else carries[h]
        es = [None] * n
        for d in reversed(range(n)):
            rd = r[(h * n + d) * tq:(h * n + d + 1) * tq]
            x = z[h * tq:(h + 1) * tq, d * KEY_BLOCK:(d + 1) * KEY_BLOCK] + rd[:, :KEY_BLOCK]
            if carry is not None:
                x = x + carry
            e = jnp.exp(x)
            if d == n - 1 and mask is not None:
                e = jnp.where(mask, e, 0.0)
            es[d] = e.astype(BF16)
            tot = rd[:, KEY_BLOCK:]
            carry = tot if carry is None else carry + tot
        rows.append(es[0] if n == 1 else jnp.concatenate(es, axis=1))
        new_carries.append(carry)
    return jnp.concatenate(rows, axis=0), new_carries


def _tri2():
    j = np.arange(KEY_BLOCK)[:, None]
    s = np.arange(KEY_BLOCK)[None, :]
    half = np.concatenate([-(j >= s).astype(np.float32), -np.ones((KEY_BLOCK, LANES), np.float32)], axis=1)
    return jnp.asarray(np.concatenate([half, half], axis=0), dtype=BF16)


def _stack_heads(q, lo_half):
    zero = jnp.zeros_like(q)
    return jnp.concatenate([jnp.where(lo_half, q, zero), jnp.where(lo_half, zero, q)], axis=0)


def _sb_window(z_fn, pv_fn, carries, mask, tri2):
    z = z_fn()
    e, cs = _sb_weights(z, _sb_cumsum(_sb_split(z, mask), tri2), carries, mask)
    pv = pv_fn(e)
    tq = pv.shape[0] // 2
    return (pv[:tq], cs[0]), (pv[tq:], cs[1])


def _finish(store, j0, outs, step_fn):
    (a0, c0), (a1, c1) = outs
    store(a0, a1)
    if isinstance(j0, int) and j0 < 0:
        return lambda: None

    def alive(x0, x1):
        return jnp.max(jnp.maximum(x0, x1)) >= LOG_KEEP_FLOOR

    go = jnp.logical_and(j0 >= 0, alive(c0, c1))

    def walk():
        @pl.when(go)
        def _():
            def body(s):
                j, b0, b1, d0, d1, _ = s
                (p0, d0), (p1, d1) = step_fn(j, [d0, d1])
                return j - 1, b0 + p0, b1 + p1, d0, d1, jnp.logical_and(j >= 1, alive(d0, d1))

            _, b0, b1, _, _, _ = lax.while_loop(lambda s: s[-1], body,
                                                (jnp.int32(j0), a0, a1, c0, c1, jnp.bool_(True)))
            store(b0, b1)

    return walk


def _sb_chains(chains, tri2):
    zs = [c[0]() for c in chains]
    sums = [_sb_cumsum(_sb_split(z, c[2]), tri2) for z, c in zip(zs, chains)]
    weights = [_sb_weights(z, r, None, c[2]) for z, r, c in zip(zs, sums, chains)]
    walks = []
    for c, (e, cs) in zip(chains, weights):
        pv = c[1](e)
        tq = pv.shape[0] // 2
        walks.append(_finish(c[3], c[5], ((pv[:tq], cs[0]), (pv[tq:], cs[1])), c[4]))
    for walk in walks:
        walk()


def _attn_prompt_kernel(q_ref, kt_ref, vt_ref, tri_ref, o_ref):
    t = q_ref.shape[1]
    nq = t // Q_ROWS
    row = lax.broadcasted_iota(jnp.int32, (Q_ROWS, LANES), 0)
    lane = lax.broadcasted_iota(jnp.int32, (Q_ROWS, LANES), 1)
    lo_half = lane < HEAD_DIM
    diag_mask = lane < row
    lane_minus_row = lane - row
    tri2 = tri_ref[...]

    def window(ref, first, n):
        return jnp.concatenate([ref[0, first + d] for d in range(n)], axis=1)

    def qblocks(blocks, n_fast):
        chains = []
        for i in blocks:
            q2 = _stack_heads(q_ref[0, _rows(i, Q_ROWS), :], lo_half)
            first = i + 1 - n_fast

            def z_fn(q2=q2, first=first):
                return jnp.dot(q2, window(kt_ref, first, n_fast), preferred_element_type=F32)

            def pv_fn(e, first=first):
                return lax.dot_general(e, window(vt_ref, first, n_fast), NT_DIMS, preferred_element_type=F32)

            def store(b0, b1, i=i):
                o_ref[_rows(i, Q_ROWS), :] = jnp.where(lo_half, b0, b1).astype(BF16)

            def step(j, cs, i=i, q2=q2):
                m = lane_minus_row < (i - j) * KEY_BLOCK
                return _sb_window(lambda: jnp.dot(q2, kt_ref[0, j], preferred_element_type=F32),
                                  lambda e: lax.dot_general(e, vt_ref[0, j], NT_DIMS, preferred_element_type=F32),
                                  cs, m, tri2)

            chains.append((z_fn, pv_fn, diag_mask, store, step, i - n_fast))
        _sb_chains(chains, tri2)

    n_head = FAST_BLOCKS - 1
    for i in range(n_head):
        qblocks([i], i + 1)

    start = n_head + (nq - n_head) % Q_GROUP
    if start > n_head:
        qblocks(list(range(n_head, start)), FAST_BLOCKS)

    def group(g, carry):
        qblocks([start + g * Q_GROUP + u for u in range(Q_GROUP)], FAST_BLOCKS)
        return carry

    lax.fori_loop(0, (nq - start) // Q_GROUP, group, 0)


def _attn_prompt(qb, ktb, vtb, tri2):
    b, t, _ = qb.shape
    assert t % Q_ROWS == 0 and Q_ROWS == KEY_BLOCK
    qspec = pl.BlockSpec((1, t, LANES), lambda i, h: (i, 0, h))
    kspec = pl.BlockSpec((1, t // KEY_BLOCK, LANES, KEY_BLOCK), lambda i, h: (i, 0, h, 0))
    return pl.pallas_call(
        _attn_prompt_kernel,
        grid=(b, D_ATTN // LANES),
        in_specs=[qspec, kspec, kspec, _const_spec(tri2.shape)],
        out_specs=pl.BlockSpec((t, LANES), lambda i, h: (i, h)),
        out_shape=jax.ShapeDtypeStruct((b * t, D_ATTN), BF16),
        compiler_params=pltpu.CompilerParams(dimension_semantics=("arbitrary", "arbitrary"),
                                             vmem_limit_bytes=VMEM_LIMIT),
        name="attn_prompt",
    )(qb, ktb, vtb, tri2)


def _attn_sample_kernel(n_tail, q_ref, kn_ref, vn_ref, ckt_ref, cvt_ref, ck_hbm, cv_hbm, tri_ref, o_ref,
                        kbuf, vbuf):
    s = pl.program_id(0)
    tq = q_ref.shape[1]
    n_past = ck_hbm.shape[2] // KEY_BLOCK
    row = lax.broadcasted_iota(jnp.int32, (tq, LANES), 0)
    lane = lax.broadcasted_iota(jnp.int32, (tq, LANES), 1)
    lo_half = lane < HEAD_DIM
    tri2 = tri_ref[...]

    pad = jnp.zeros((KEY_BLOCK - tq, LANES), BF16)
    chains = []
    for p in range(q_ref.shape[2] // LANES):
        cols = slice(p * LANES, (p + 1) * LANES)
        q2 = _stack_heads(q_ref[0, :, cols], lo_half)

        def z_fn(q2=q2, cols=cols):
            z_new = lax.dot_general(q2, jnp.concatenate([kn_ref[0, :, cols], pad], axis=0), NT_DIMS,
                                    preferred_element_type=F32)
            if n_tail == 0:
                return z_new
            z_past = jnp.dot(q2, ckt_ref[0, cols, :].astype(BF16), preferred_element_type=F32)
            return jnp.concatenate([z_past, z_new], axis=1)

        def pv_fn(e, cols=cols):
            pv = jnp.dot(e[:, n_tail * KEY_BLOCK:], jnp.concatenate([vn_ref[0, :, cols], pad], axis=0),
                         preferred_element_type=F32)
            if n_tail == 0:
                return pv
            return pv + lax.dot_general(e[:, :n_tail * KEY_BLOCK], cvt_ref[0, cols, :].astype(BF16), NT_DIMS,
                                        preferred_element_type=F32)

        def store(a0, a1, cols=cols):
            o_ref[:, cols] = jnp.where(lo_half, a0, a1).astype(BF16)

        def step(j, cs, cols=cols, q2=q2, p=p):
            keys = pl.ds(pl.multiple_of(j * KEY_BLOCK, KEY_BLOCK), KEY_BLOCK)
            pltpu.sync_copy(ck_hbm.at[s, pl.ds(p * LANES, LANES), keys], kbuf)
            pltpu.sync_copy(cv_hbm.at[s, pl.ds(p * LANES, LANES), keys], vbuf)
            return _sb_window(lambda: jnp.dot(q2, kbuf[...].astype(BF16), preferred_element_type=F32),
                              lambda e: lax.dot_general(e, vbuf[...].astype(BF16), NT_DIMS,
                                                        preferred_element_type=F32),
                              cs, None, tri2)

        chains.append((z_fn, pv_fn, lane < row, store, step, n_past - 1 - n_tail))
    _sb_chains(chains, tri2)


def _attn_sample(qb, knb, vnb, cache_kt, cache_vt, tri2):
    s, tq, _ = qb.shape
    past_len = cache_kt.shape[2]
    assert past_len % KEY_BLOCK == 0 and tq <= KEY_BLOCK and tq % 16 == 0
    n_tail = min(FAST_BLOCKS - 1, past_len // KEY_BLOCK)
    assert n_tail > 0 and past_len % (n_tail * KEY_BLOCK) == 0
    new = pl.BlockSpec((1, tq, D_ATTN), lambda i: (i, 0, 0))
    tail = pl.BlockSpec((1, D_ATTN, n_tail * KEY_BLOCK), lambda i: (i, 0, past_len // (n_tail * KEY_BLOCK) - 1))
    hbm = pl.BlockSpec(memory_space=pl.ANY)
    return pl.pallas_call(
        functools.partial(_attn_sample_kernel, n_tail),
        grid=(s,),
        in_specs=[new, new, new, tail, tail, hbm, hbm, _const_spec(tri2.shape)],
        out_specs=pl.BlockSpec((tq, D_ATTN), lambda i: (i, 0)),
        out_shape=jax.ShapeDtypeStruct((s * tq, D_ATTN), BF16),
        scratch_shapes=[pltpu.VMEM((LANES, KEY_BLOCK), F32), pltpu.VMEM((LANES, KEY_BLOCK), F32)],
        compiler_params=pltpu.CompilerParams(dimension_semantics=("arbitrary",),
                                             vmem_limit_bytes=VMEM_LIMIT),
        name="attn_sample",
    )(qb, knb, vnb, cache_kt, cache_vt, cache_kt, cache_vt, tri2)


FF_CHUNK = 1024


def _out_kernel(n_first, xa_ref, xb_ref, oa_ref, ob_ref, hga_ref, hgb_ref, g1_ref, wga0_ref, wga1_ref, wgb0_ref,
                wgb1_ref, wa_ref, wb_ref, wo_ref, g2_ref, g3_ref, wup_ref, wdn_ref, g4_ref, ya_ref, yb_ref):
    first = pl.program_id(0) < n_first
    tm = xa_ref.shape[0]
    halves = [pl.ds(0, tm // 2), pl.ds(tm // 2, tm // 2)]
    n_ff = D_FF // FF_CHUNK
    st = [{}, {}]

    def pick(a_ref, b_ref, h):
        return jnp.where(first, a_ref[halves[h], :], b_ref[halves[h], :])

    def norm_in(h):
        st[h]["x"] = pick(xa_ref, xb_ref, h)
        st[h]["xn"] = _rms(st[h]["x"], g1_ref[...]).astype(BF16)

    def gate(h, w0_ref, w1_ref):
        return jnp.concatenate([jnp.dot(st[h]["xn"], w_ref[...], preferred_element_type=F32)
                                for w_ref in (w0_ref, w1_ref)], axis=1)

    def attn_gate(h):
        g_a = gate(h, wga0_ref, wga1_ref)
        y_a = jnp.dot(pick(oa_ref, ob_ref, h), wa_ref[...], preferred_element_type=F32)
        st[h]["m"] = jax.nn.sigmoid(g_a) * y_a

    def rnn_gate(h):
        g_b = gate(h, wgb0_ref, wgb1_ref)
        y_b = jnp.dot(pick(hga_ref, hgb_ref, h), wb_ref[...], preferred_element_type=F32)
        st[h]["m"] = st[h]["m"] + jax.nn.sigmoid(g_b) * y_b

    def mix(h):
        st[h]["mix"] = jnp.dot(st[h]["m"].astype(BF16), wo_ref[...], preferred_element_type=F32)

    def norm_mid(h):
        st[h]["x1"] = st[h]["x"] + _rms(st[h]["mix"], g2_ref[...])
        st[h]["f"] = _rms(st[h]["x1"], g3_ref[...]).astype(BF16)

    def ffn(h, c):
        cols = slice(c * FF_CHUNK, (c + 1) * FF_CHUNK)
        up = jnp.maximum(jnp.dot(st[h]["f"], wup_ref[:, cols], preferred_element_type=F32), 0.0)
        dn = jnp.dot((up * up).astype(BF16), wdn_ref[cols, :], preferred_element_type=F32)
        st[h]["acc"] = dn if c == 0 else st[h]["acc"] + dn

    def norm_out(h):
        yb_ref[halves[h], :] = st[h]["x1"] + _rms(st[h]["acc"], g4_ref[...])

    norm_in(0)
    norm_in(1)
    attn_gate(0)
    rnn_gate(0)
    mix(0)
    attn_gate(1)
    norm_mid(0)
    rnn_gate(1)
    mix(1)
    ffn(0, 0)
    norm_mid(1)
    for c in range(1, n_ff):
        ffn(0, c)
    ffn(1, 0)
    norm_out(0)
    for c in range(1, n_ff):
        ffn(1, c)
    norm_out(1)

    @pl.when(first)
    def _():
        ya_ref[...] = yb_ref[...]


def _out(xa, xb, oa, ob, hga, hgb, g1, w_in, wa, wb, wo, g2, g3, wup, wdn, g4):
    na, nb = xa.shape[0], xb.shape[0]
    tm = TOKEN_TILE
    assert na % tm == 0 and nb % tm == 0
    n_first = na // tm
    a_tok = lambda i: (jnp.minimum(i, n_first - 1), 0)
    b_tok = lambda i: (jnp.maximum(i - n_first, 0), 0)
    w_cols = (COL_GA, COL_GA + 1, COL_GB, COL_GB + 1)
    consts = (wa, wb, wo, g2, g3, wup, wdn, g4)
    return pl.pallas_call(
        functools.partial(_out_kernel, n_first),
        grid=((na + nb) // tm,),
        in_specs=[pl.BlockSpec((tm, D_MODEL), a_tok), pl.BlockSpec((tm, D_MODEL), b_tok),
                  pl.BlockSpec((tm, D_ATTN), a_tok), pl.BlockSpec((tm, D_ATTN), b_tok),
                  pl.BlockSpec((tm, D_RNN), a_tok), pl.BlockSpec((tm, D_RNN), b_tok), _const_spec(g1.shape)]
        + [_w_in_cols(j) for j in w_cols] + [_const_spec(c.shape) for c in consts],
        out_specs=[pl.BlockSpec((tm, D_MODEL), a_tok), pl.BlockSpec((tm, D_MODEL), b_tok)],
        out_shape=[jax.ShapeDtypeStruct((na, D_MODEL), F32), jax.ShapeDtypeStruct((nb, D_MODEL), F32)],
        compiler_params=pltpu.CompilerParams(dimension_semantics=("arbitrary",),
                                             vmem_limit_bytes=VMEM_LIMIT),
        name="out",
    )(xa, xb, oa, ob, hga, hgb, g1, *([w_in] * len(w_cols)), *consts)


def _block_diag(w):
    n, c, d = w.shape
    return jnp.einsum("ncd,nm->ncmd", w, jnp.eye(n, dtype=w.dtype)).reshape(n * c, n * d)


def _heads_last(kt):
    s, _, t = kt.shape
    return kt.reshape(s, N_HEADS, HEAD_DIM, t).transpose(0, 3, 1, 2)[None]


def _time_last(cache):
    s, t = cache.shape[:2]
    return cache.transpose(0, 2, 3, 1).reshape(s, D_ATTN, t)


def kernel(x_prompt, x_sample, cache_k, cache_v, state_conv, state_h, w_in, g_pre_mix, w_conv, b_conv, w_r, b_r, w_i, b_i, lam, w_a_out, w_b_out, w_o, g_post_mix, g_pre_ffn, w_up, w_down, g_post_ffn):
    assert w_in.shape[0] == 1
    row = lambda a: a[0].reshape(1, -1)
    assert w_in.shape[2] == (COL_GB + 2) * D_ATTN
    w_in_b = w_in[0].astype(BF16)
    wkv_t = w_in[0][:, COL_K * D_ATTN:(COL_V + 1) * D_ATTN].T.astype(BF16)
    wg = jnp.concatenate([_block_diag(w_r[0]), _block_diag(w_i[0])], axis=1).astype(BF16)
    bg = jnp.concatenate([row(b_r), row(b_i)], axis=1)
    lru_w = (w_conv[0], row(b_conv), wg, bg, row(lam))
    g1 = row(g_pre_mix)
    tri2 = _tri2()

    bp, tp, _ = x_prompt.shape
    bs, ts, _ = x_sample.shape
    n_p, n_s = bp * tp, bs * ts

    zc = jnp.zeros((bp, CONV_WIDTH - 1, D_RNN), F32)
    zh = jnp.zeros((bp, 1, D_RNN), F32)
    qb, kt, vt, ktb, vtb, hg_p, cp, hp = _proj_lru(x_prompt, g1, w_in_b, wkv_t, zc, zh, lru_w)
    o_p = _attn_prompt(qb.reshape(bp, tp, D_ATTN), ktb, vtb, tri2)

    s3 = lambda a: a.reshape(bs, ts, a.shape[-1])
    qs, ks, vs, ksb, vsb, hg_s, cs, hs = _proj(x_sample, g1, w_in_b, state_conv[0],
                                               state_h[0].reshape(bs, 1, D_RNN), lru_w)
    o_s = _attn_sample(s3(qs), s3(ksb), s3(vsb), _time_last(cache_k[0]), _time_last(cache_v[0]), tri2)

    ys, yp = _out(x_sample.reshape(n_s, D_MODEL), x_prompt.reshape(n_p, D_MODEL), o_s, o_p, hg_s, hg_p, g1, w_in_b,
                  w_a_out[0].astype(BF16), w_b_out[0].astype(BF16), w_o[0].astype(BF16), row(g_post_mix),
                  row(g_pre_ffn), w_up[0].astype(BF16), w_down[0].astype(BF16), row(g_post_ffn))
    return (yp.reshape(bp, tp, D_MODEL), ys.reshape(bs, ts, D_MODEL), _heads_last(kt), _heads_last(vt),
            cp[None], hp.reshape(1, bp, D_RNN), ks.reshape(1, bs, ts, N_HEADS, HEAD_DIM),
            vs.reshape(1, bs, ts, N_HEADS, HEAD_DIM), cs[None], hs.reshape(1, bs, D_RNN))
```

```python
import functools

import jax
import jax.numpy as jnp
import numpy as np
from jax import lax
from jax.experimental import pallas as pl
from jax.experimental.pallas import tpu as pltpu

F32 = jnp.float32
BF16 = jnp.bfloat16

D_MODEL = 1024
N_HEADS = 8
HEAD_DIM = 64
D_ATTN = N_HEADS * HEAD_DIM
D_RNN = 512
N_RNN_BLOCKS = 8
CONV_WIDTH = 4
LRU_C = 8.0
D_FF = 4 * D_MODEL
EPS = 1e-6

LANES = 128
SUBLANES = 8
KEY_BLOCK = 128
Q_ROWS = 128
FAST_BLOCKS = 3
Q_GROUP = 4
TOKEN_TILE = 512
SCAN_SLICES = 16
MXU_N = 256
LOG_KEEP_FLOOR = -104.0
VMEM_LIMIT = 56 * 1024 * 1024

NT_DIMS = (((1,), (1,)), ((), ()))
LOG2_E = 1.4426950408889634


def _rms(xf, g):
    return xf * lax.rsqrt(jnp.mean(xf * xf, axis=-1, keepdims=True) + EPS) * g


def _softplus(x):
    return jnp.maximum(x, 0.0) + jnp.log(1.0 + jnp.exp2(jnp.abs(x) * -LOG2_E))


def _sigmoid(x):
    return 0.5 * jnp.tanh(0.5 * x) + 0.5


def _const_spec(shape):
    nd = len(shape)
    return pl.BlockSpec(shape, lambda *_: (0,) * nd, pipeline_mode=pl.Buffered(1))


COL_Q, COL_K, COL_V, COL_U, COL_GR, COL_GA, COL_GB = 0, 1, 2, 3, 4, 5, 7
Q_SCALE = HEAD_DIM ** -0.5


def _w_in_cols(j):
    return pl.BlockSpec((D_MODEL, D_ATTN), lambda *_: (0, j), pipeline_mode=pl.Buffered(1))


def _rows(block_index, size):
    if isinstance(block_index, int):
        return pl.ds(block_index * size, size)
    return pl.ds(pl.multiple_of(block_index * size, size), size)


def _gelu_tanh(x):
    return 0.5 * x * (1.0 + jnp.tanh(np.sqrt(2.0 / np.pi).astype(np.float32) * (x + 0.044715 * (x * x * x))))


def _conv_seed(first, cs_ref, ext_ref):
    @pl.when(first)
    def _():
        ext_ref[0:SUBLANES, :] = jnp.zeros((SUBLANES, D_RNN), F32)
        ext_ref[SUBLANES - (CONV_WIDTH - 1):SUBLANES, :] = cs_ref[0]


def _scan_seed(first, h0_ref, hc_ref):
    @pl.when(first)
    def _():
        hc_ref[...] = jnp.broadcast_to(h0_ref[0], hc_ref.shape)


def _lru_conv(u, wc_ref, bc_ref, ext_ref):
    tt = u.shape[0]
    ext_ref[SUBLANES:SUBLANES + tt, :] = u
    uc = bc_ref[...] + wc_ref[CONV_WIDTH - 1:CONV_WIDTH, :] * u
    for d in range(1, CONV_WIDTH):
        uc = uc + wc_ref[CONV_WIDTH - 1 - d:CONV_WIDTH - d, :] * ext_ref[SUBLANES - d:SUBLANES - d + tt, :]
    ext_ref[0:SUBLANES, :] = ext_ref[tt:tt + SUBLANES, :]
    return uc


def _lru_gates(uc, wg_ref, bg_ref):
    return jnp.dot(uc.astype(BF16), wg_ref[...], preferred_element_type=F32) + bg_ref[...]


def _lru_scan(g, uc, g_rnn, lam_ref, h_prev):
    tt = uc.shape[0]
    r = _sigmoid(g[:, :D_RNN])
    ig = _sigmoid(g[:, D_RNN:])
    log_a = (-LRU_C * _softplus(-lam_ref[...])) * r
    a = jnp.exp(log_a)
    m2 = -jnp.tanh(log_a) * (a * a + 1.0)
    b = jnp.where(m2 > 0.0, m2 * lax.rsqrt(m2), 0.0) * (ig * uc)

    n_groups = tt // SUBLANES
    a3 = a.reshape(n_groups, SUBLANES, D_RNN)
    b3 = b.reshape(n_groups, SUBLANES, D_RNN)
    sub = lax.broadcasted_iota(jnp.int32, a3.shape, 1)
    k = 1
    while k < SUBLANES:
        has_prev = sub >= k
        b_prev = jnp.where(has_prev, pltpu.roll(b3, k, axis=1), 0.0)
        a_prev = jnp.where(has_prev, pltpu.roll(a3, k, axis=1), 1.0)
        b3 = b3 + a3 * b_prev
        a3 = a3 * a_prev
        k *= 2

    hs = []
    for gi in range(n_groups):
        h = b3[gi] + a3[gi] * h_prev
        hs.append(h)
        h_prev = jnp.broadcast_to(h[SUBLANES - 1:SUBLANES, :], h.shape)
    return (jnp.concatenate(hs, axis=0) * _gelu_tanh(g_rnn)).astype(BF16), h_prev


def _lru_state_specs(index_map):
    past = CONV_WIDTH - 1
    return (pl.BlockSpec((1, past, D_RNN), index_map), pl.BlockSpec((1, 1, D_RNN), index_map))


def _lru_scratch(tt):
    return [pltpu.VMEM((tt + SUBLANES, D_RNN), F32), pltpu.VMEM((SUBLANES, D_RNN), F32)]


def _proj_kernel(t, x_ref, g_ref, wq_ref, wk_ref, wv_ref, wu_ref, wgr_ref, cs_ref, h0_ref, wc_ref, bc_ref, wg_ref,
                 bg_ref, lam_ref, qb_ref, k_ref, v_ref, kb_ref, vb_ref, hg_ref, cn_ref, hl_ref, ext_ref):
    n_streams = x_ref.shape[0] // t
    past = CONV_WIDTH - 1
    xn = _rms(x_ref[...], g_ref[...]).astype(BF16)
    proj = lambda w_ref: jnp.dot(xn, w_ref[...], preferred_element_type=F32)
    u, g_rnn = proj(wu_ref), proj(wgr_ref)
    qb_ref[...] = (proj(wq_ref) * Q_SCALE).astype(BF16)
    k, v = proj(wk_ref), proj(wv_ref)
    k_ref[...] = k
    v_ref[...] = v
    kb_ref[...] = k.astype(BF16)
    vb_ref[...] = v.astype(BF16)

    ucs = []
    for s in range(n_streams):
        ext = ext_ref.at[s]
        ext[0:SUBLANES, :] = jnp.zeros((SUBLANES, D_RNN), F32)
        ext[SUBLANES - past:SUBLANES, :] = cs_ref[s]
        ucs.append(_lru_conv(u[s * t:(s + 1) * t], wc_ref, bc_ref, ext))
        cn_ref[s] = ext[SUBLANES - past:SUBLANES, :]
    uc = jnp.concatenate(ucs, axis=0)
    gates = _lru_gates(uc, wg_ref, bg_ref)
    for s in range(n_streams):
        rows = slice(s * t, (s + 1) * t)
        hg, h_last = _lru_scan(gates[rows], uc[rows], g_rnn[rows], lam_ref,
                               jnp.broadcast_to(h0_ref[s], (SUBLANES, D_RNN)))
        hg_ref[rows, :] = hg
        hl_ref[s] = h_last[0:1, :]


def _proj(x, g, w_in, conv_state, h0, lru_w):
    s, t, _ = x.shape
    n = s * t
    tm = TOKEN_TILE
    assert n % tm == 0 and tm % t == 0 and t % SUBLANES == 0
    per_tile = tm // t
    tok = lambda i: (i, 0)
    row_out = pl.BlockSpec((tm, D_ATTN), tok)
    f32o = jax.ShapeDtypeStruct((n, D_ATTN), F32)
    b16o = jax.ShapeDtypeStruct((n, D_ATTN), BF16)
    st3 = pl.BlockSpec((per_tile, CONV_WIDTH - 1, D_RNN), lambda i: (i, 0, 0))
    st1 = pl.BlockSpec((per_tile, 1, D_RNN), lambda i: (i, 0, 0))
    w_cols = (COL_Q, COL_K, COL_V, COL_U, COL_GR)
    return pl.pallas_call(
        functools.partial(_proj_kernel, t),
        grid=(n // tm,),
        in_specs=[pl.BlockSpec((tm, D_MODEL), tok), _const_spec(g.shape)] + [_w_in_cols(j) for j in w_cols]
        + [st3, st1] + [_const_spec(w.shape) for w in lru_w],
        out_specs=[row_out] * 6 + [st3, st1],
        out_shape=[b16o, f32o, f32o, b16o, b16o, b16o,
                   jax.ShapeDtypeStruct((s, CONV_WIDTH - 1, D_RNN), F32),
                   jax.ShapeDtypeStruct((s, 1, D_RNN), F32)],
        scratch_shapes=[pltpu.VMEM((per_tile, t + SUBLANES, D_RNN), F32)],
        compiler_params=pltpu.CompilerParams(dimension_semantics=("arbitrary",),
                                             vmem_limit_bytes=VMEM_LIMIT),
        name="proj",
    )(x.reshape(n, D_MODEL), g, *([w_in] * len(w_cols)), conv_state, h0, *lru_w)


def _proj_lru_kernel(per, n_tiles, x_ref, g_ref, wq_ref, wu_ref, wgr_ref, wkvt_ref, cs_ref, h0_ref, wc_ref, bc_ref,
                     wg_ref, bg_ref, lam_ref, qb_ref, kt_ref, vt_ref, ktb_ref, vtb_ref, hg_ref, cn_ref, hl_ref,
                     ext_ref, hc_ref, xn_ref, uc_ref, gr_ref, gt_ref):
    i = pl.program_id(0)
    dot_tile, scan_tile = i - 1, i - 2
    dot_live = jnp.logical_and(dot_tile >= 0, dot_tile < n_tiles)
    scan_live = jnp.logical_and(scan_tile >= 0, scan_tile < n_tiles)
    _conv_seed(jnp.logical_and(dot_live, dot_tile % per == 0), cs_ref, ext_ref)
    _scan_seed(jnp.logical_and(scan_live, scan_tile % per == 0), h0_ref, hc_ref)

    @pl.when(i == 0)
    def _():
        xn_ref[...] = jnp.zeros(xn_ref.shape, BF16)
        hc_ref[...] = jnp.zeros(hc_ref.shape, F32)
        ext_ref[...] = jnp.zeros(ext_ref.shape, F32)
        for ref in (uc_ref, gr_ref, gt_ref):
            ref[...] = jnp.zeros(ref.shape, F32)

    tm = x_ref.shape[0]
    sc, dt = i % 2, (i + 1) % 2
    xn = xn_ref[dt]
    state = [hc_ref[...]]
    rows_per = tm // SCAN_SLICES
    u_parts = {}
    uc_b = [None]

    def scan_slice(p):
        rows = pl.ds(p * rows_per, rows_per)
        hg, state[0] = _lru_scan(gt_ref[sc, rows, :], uc_ref[sc, rows, :], gr_ref[sc, rows, :], lam_ref, state[0])
        hg_ref[rows, :] = hg

    def rms_piece(r):
        rows = pl.ds(r * (tm // 4), tm // 4)
        xn_ref[sc, rows, :] = _rms(x_ref[rows, :], g_ref[...]).astype(BF16)

    def ug_chunk(c):
        is_u = c < D_RNN // MXU_N
        cols = slice(c * MXU_N % D_RNN, c * MXU_N % D_RNN + MXU_N)
        r = jnp.dot(xn, (wu_ref if is_u else wgr_ref)[:, cols], preferred_element_type=F32)
        if is_u:
            u_parts[c] = r
        else:
            gr_ref[dt, :, cols] = r

    def conv():
        uc = _lru_conv(jnp.concatenate([u_parts[c] for c in sorted(u_parts)], axis=1), wc_ref, bc_ref, ext_ref)
        uc_ref[dt] = uc
        uc_b[0] = uc.astype(BF16)

    def q_chunk(c):
        cols = slice(c * MXU_N, (c + 1) * MXU_N)
        qb_ref[:, cols] = (jnp.dot(xn, wq_ref[:, cols], preferred_element_type=F32) * Q_SCALE).astype(BF16)

    def kv_chunk(c):
        toks = slice(c * MXU_N, (c + 1) * MXU_N)
        kv = lax.dot_general(wkvt_ref[...], xn[toks], NT_DIMS, preferred_element_type=F32)
        k, v = kv[:D_ATTN], kv[D_ATTN:]
        kt_ref[0, :, toks] = k
        vt_ref[0, :, toks] = v
        for j in range(MXU_N // KEY_BLOCK):
            cols = slice(j * KEY_BLOCK, (j + 1) * KEY_BLOCK)
            ktb_ref[0, c * (MXU_N // KEY_BLOCK) + j] = k[:, cols].astype(BF16)
            vtb_ref[0, c * (MXU_N // KEY_BLOCK) + j] = v[:, cols].astype(BF16)

    def gates_chunk(c):
        cols = slice(c * MXU_N, (c + 1) * MXU_N)
        gt_ref[dt, :, cols] = jnp.dot(uc_b[0], wg_ref[:, cols], preferred_element_type=F32) + bg_ref[:, cols]

    S, P = scan_slice, functools.partial
    program = [
        P(ug_chunk, 0), P(S, 0), P(S, 1), P(ug_chunk, 1), P(S, 2), P(S, 3), P(ug_chunk, 2), conv,
        P(ug_chunk, 3), P(S, 4), P(S, 5), P(q_chunk, 0), P(S, 6), P(S, 7),
        P(q_chunk, 1), P(rms_piece, 0), P(rms_piece, 1), P(S, 8),
        P(kv_chunk, 0), P(S, 9), P(S, 10), P(S, 11), P(rms_piece, 2),
        P(kv_chunk, 1), P(S, 12), P(S, 13), P(S, 14), P(rms_piece, 3),
        P(gates_chunk, 0), P(S, 15), P(gates_chunk, 1), P(gates_chunk, 2), P(gates_chunk, 3),
    ]
    for piece in program:
        piece()
    h_last = state[0]
    hc_ref[...] = h_last

    @pl.when(jnp.logical_and(dot_live, dot_tile % per == per - 1))
    def _():
        cn_ref[0] = ext_ref[SUBLANES - (CONV_WIDTH - 1):SUBLANES, :]

    @pl.when(jnp.logical_and(scan_live, scan_tile % per == per - 1))
    def _():
        hl_ref[0] = h_last[0:1, :]


def _proj_lru(x, g, w_in, wkvt, conv_state, h0, lru_w):
    s, t, _ = x.shape
    n = s * t
    tm = TOKEN_TILE
    assert t % tm == 0 and tm % KEY_BLOCK == 0
    per = t // tm
    n_tiles = n // tm
    tile = lambda i, lag: jnp.clip(i - lag, 0, n_tiles - 1)
    tok_in = pl.BlockSpec((tm, D_MODEL), lambda i: (tile(i, 0), 0))
    q_out = pl.BlockSpec((tm, D_ATTN), lambda i: (tile(i, 1), 0))
    hg_out = pl.BlockSpec((tm, D_RNN), lambda i: (tile(i, 2), 0))
    b16o = jax.ShapeDtypeStruct((n, D_ATTN), BF16)
    kv_spec = pl.BlockSpec((1, D_ATTN, tm), lambda i: (tile(i, 1) // per, 0, tile(i, 1) % per))
    kvb_spec = pl.BlockSpec((1, tm // KEY_BLOCK, D_ATTN, KEY_BLOCK),
                            lambda i: (tile(i, 1) // per, tile(i, 1) % per, 0, 0))
    kv_shape = jax.ShapeDtypeStruct((s, D_ATTN, t), F32)
    kvb_shape = jax.ShapeDtypeStruct((s, t // KEY_BLOCK, D_ATTN, KEY_BLOCK), BF16)
    cs_spec, _ = _lru_state_specs(lambda i: (tile(i, 1) // per, 0, 0))
    _, h_spec = _lru_state_specs(lambda i: (tile(i, 2) // per, 0, 0))
    w_cols = (COL_Q, COL_U, COL_GR)
    return pl.pallas_call(
        functools.partial(_proj_lru_kernel, per, n_tiles),
        grid=(n_tiles + 2,),
        in_specs=[tok_in, _const_spec(g.shape)] + [_w_in_cols(j) for j in w_cols] + [_const_spec(wkvt.shape)]
        + [cs_spec, h_spec] + [_const_spec(w.shape) for w in lru_w],
        out_specs=[q_out, kv_spec, kv_spec, kvb_spec, kvb_spec, hg_out, cs_spec, h_spec],
        out_shape=[b16o, kv_shape, kv_shape, kvb_shape, kvb_shape, b16o,
                   jax.ShapeDtypeStruct((s, CONV_WIDTH - 1, D_RNN), F32),
                   jax.ShapeDtypeStruct((s, 1, D_RNN), F32)],
        scratch_shapes=_lru_scratch(tm) + [pltpu.VMEM((2, tm, D_MODEL), BF16), pltpu.VMEM((2, tm, D_RNN), F32),
                                           pltpu.VMEM((2, tm, D_RNN), F32), pltpu.VMEM((2, tm, 2 * D_RNN), F32)],
        compiler_params=pltpu.CompilerParams(dimension_semantics=("arbitrary",),
                                             vmem_limit_bytes=VMEM_LIMIT),
        name="proj_lru",
    )(x.reshape(n, D_MODEL), g, *([w_in] * len(w_cols)), wkvt, conv_state, h0, *lru_w)


def _sb_split(z, mask):
    n = z.shape[1] // KEY_BLOCK
    tq = z.shape[0] // 2
    sp = _softplus(z)
    cats = []
    for h in range(2):
        for d in range(n):
            blk = sp[h * tq:(h + 1) * tq, d * KEY_BLOCK:(d + 1) * KEY_BLOCK]
            if d == n - 1 and mask is not None:
                blk = jnp.where(mask, blk, 0.0)
            cats.append(_hi_lo(blk))
    return jnp.concatenate(cats, axis=0)


def _hi_lo(blk):
    hi = blk.astype(BF16)
    lo = (blk - hi.astype(F32)).astype(BF16)
    return jnp.concatenate([hi, lo], axis=1)


def _sb_cumsum(cat, tri2):
    return jnp.dot(cat, tri2, preferred_element_type=F32)


def _sb_weights(z, r, carries, mask):
    n = z.shape[1] // KEY_BLOCK
    tq = z.shape[0] // 2
    rows, new_carries = [], []
    for h in range(2):
        carry = None if carries is None else carries[h]
        es = [None] * n
        for d in reversed(range(n)):
            rd = r[(h * n + d) * tq:(h * n + d + 1) * tq]
            x = z[h * tq:(h + 1) * tq, d * KEY_BLOCK:(d + 1) * KEY_BLOCK] + rd[:, :KEY_BLOCK]
            if carry is not None:
                x = x + carry
            e = jnp.exp(x)
            if d == n - 1 and mask is not None:
                e = jnp.where(mask, e, 0.0)
            es[d] = e.astype(BF16)
            tot = rd[:, KEY_BLOCK:]
            carry = tot if carry is None else carry + tot
        rows.append(es[0] if n == 1 else jnp.concatenate(es, axis=1))
        new_carries.append(carry)
    return jnp.concatenate(rows, axis=0), new_carries


def _tri2():
    j = np.arange(KEY_BLOCK)[:, None]
    s = np.arange(KEY_BLOCK)[None, :]
    half = np.concatenate([-(j >= s).astype(np.float32), -np.ones((KEY_BLOCK, LANES), np.float32)], axis=1)
    return jnp.asarray(np.concatenate([half, half], axis=0), dtype=BF16)


def _stack_heads(q, lo_half):
    zero = jnp.zeros_like(q)
    return jnp.concatenate([jnp.where(lo_half, q, zero), jnp.where(lo_half, zero, q)], axis=0)


def _sb_window(z_fn, pv_fn, carries, mask, tri2):
    z = z_fn()
    e, cs = _sb_weights(z, _sb_cumsum(_sb_split(z, mask), tri2), carries, mask)
    pv = pv_fn(e)
    tq = pv.shape[0] // 2
    return (pv[:tq], cs[0]), (pv[tq:], cs[1])


def _finish(store, j0, outs, step_fn):
    (a0, c0), (a1, c1) = outs
    store(a0, a1)
    if isinstance(j0, int) and j0 < 0:
        return lambda: None

    def alive(x0, x1):
        return jnp.max(jnp.maximum(x0, x1)) >= LOG_KEEP_FLOOR

    go = jnp.logical_and(j0 >= 0, alive(c0, c1))

    def walk():
        @pl.when(go)
        def _():
            def body(s):
                j, b0, b1, d0, d1, _ = s
                (p0, d0), (p1, d1) = step_fn(j, [d0, d1])
                return j - 1, b0 + p0, b1 + p1, d0, d1, jnp.logical_and(j >= 1, alive(d0, d1))

            _, b0, b1, _, _, _ = lax.while_loop(lambda s: s[-1], body,
                                                (jnp.int32(j0), a0, a1, c0, c1, jnp.bool_(True)))
            store(b0, b1)

    return walk


def _sb_chains(chains, tri2):
    zs = [c[0]() for c in chains]
    sums = [_sb_cumsum(_sb_split(z, c[2]), tri2) for z, c in zip(zs, chains)]
    weights = [_sb_weights(z, r, None, c[2]) for z, r, c in zip(zs, sums, chains)]
    walks = []
    for c, (e, cs) in zip(chains, weights):
        pv = c[1](e)
        tq = pv.shape[0] // 2
        walks.append(_finish(c[3], c[5], ((pv[:tq], cs[0]), (pv[tq:], cs[1])), c[4]))
    for walk in walks:
        walk()


def _attn_prompt_kernel(q_ref, kt_ref, vt_ref, tri_ref, o_ref):
    t = q_ref.shape[1]
    nq = t // Q_ROWS
    row = lax.broadcasted_iota(jnp.int32, (Q_ROWS, LANES), 0)
    lane = lax.broadcasted_iota(jnp.int32, (Q_ROWS, LANES), 1)
    lo_half = lane < HEAD_DIM
    diag_mask = lane < row
    lane_minus_row = lane - row
    tri2 = tri_ref[...]

    def window(ref, first, n):
        return jnp.concatenate([ref[0, first + d] for d in range(n)], axis=1)

    def qblocks(blocks, n_fast):
        chains = []
        for i in blocks:
            q2 = _stack_heads(q_ref[0, _rows(i, Q_ROWS), :], lo_half)
            first = i + 1 - n_fast

            def z_fn(q2=q2, first=first):
                return jnp.dot(q2, window(kt_ref, first, n_fast), preferred_element_type=F32)

            def pv_fn(e, first=first):
                return lax.dot_general(e, window(vt_ref, first, n_fast), NT_DIMS, preferred_element_type=F32)

            def store(b0, b1, i=i):
                o_ref[_rows(i, Q_ROWS), :] = jnp.where(lo_half, b0, b1).astype(BF16)

            def step(j, cs, i=i, q2=q2):
                m = lane_minus_row < (i - j) * KEY_BLOCK
                return _sb_window(lambda: jnp.dot(q2, kt_ref[0, j], preferred_element_type=F32),
                                  lambda e: lax.dot_general(e, vt_ref[0, j], NT_DIMS, preferred_element_type=F32),
                                  cs, m, tri2)

            chains.append((z_fn, pv_fn, diag_mask, store, step, i - n_fast))
        _sb_chains(chains, tri2)

    half = Q_ROWS // 2
    bottom = row >= half

    def qblocks_tapered(blocks):
        chains = []
        for i in blocks:
            q2 = _stack_heads(q_ref[0, _rows(i, Q_ROWS), :], lo_half)
            q2_top = jnp.concatenate([q2[:half], q2[Q_ROWS:Q_ROWS + half]], axis=0)
            z_new = jnp.dot(q2, window(kt_ref, i - 1, 2), preferred_element_type=F32)
            z_old = jnp.dot(q2_top, kt_ref[0, i - 2], preferred_element_type=F32)
            chains.append([i, q2, z_new, z_old])
        for c in chains:
            _, _, z_new, z_old = c
            sp_new, sp_old = _softplus(z_new), _softplus(z_old)
            cats = []
            for h in range(2):
                rows = slice(h * Q_ROWS, (h + 1) * Q_ROWS)
                cats.append(_hi_lo(sp_new[rows, :KEY_BLOCK]))
                cats.append(_hi_lo(jnp.where(diag_mask, sp_new[rows, KEY_BLOCK:], 0.0)))
            for h in range(2):
                cats.append(_hi_lo(sp_old[h * half:(h + 1) * half]))
            c.append(_sb_cumsum(jnp.concatenate(cats, axis=0), tri2))
        walks = []
        for i, q2, z_new, z_old, r in chains:
            es, carries = [], []
            for h in range(2):
                rows = slice(h * Q_ROWS, (h + 1) * Q_ROWS)
                r_prev = r[2 * h * Q_ROWS:(2 * h + 1) * Q_ROWS]
                r_diag = r[(2 * h + 1) * Q_ROWS:(2 * h + 2) * Q_ROWS]
                r_old = r[4 * Q_ROWS + h * half:4 * Q_ROWS + (h + 1) * half]
                e_diag = jnp.where(diag_mask, jnp.exp(z_new[rows, KEY_BLOCK:] + r_diag[:, :KEY_BLOCK]), 0.0)
                carry = r_diag[:, KEY_BLOCK:]
                e_prev = jnp.exp(z_new[rows, :KEY_BLOCK] + r_prev[:, :KEY_BLOCK] + carry)
                carry = carry + r_prev[:, KEY_BLOCK:]
                e_old = jnp.exp(z_old[h * half:(h + 1) * half] + r_old[:, :KEY_BLOCK] + carry[:half])
                carries.append(jnp.concatenate([carry[:half] + r_old[:, KEY_BLOCK:], carry[half:]], axis=0))
                e_old = jnp.concatenate([e_old.astype(BF16), jnp.zeros((half, KEY_BLOCK), BF16)], axis=0)
                es.append(jnp.concatenate([e_old, e_prev.astype(BF16), e_diag.astype(BF16)], axis=1))
            pv = lax.dot_general(jnp.concatenate(es, axis=0), window(vt_ref, i - 2, FAST_BLOCKS), NT_DIMS,
                                 preferred_element_type=F32)

            def store(b0, b1, i=i):
                o_ref[_rows(i, Q_ROWS), :] = jnp.where(lo_half, b0, b1).astype(BF16)

            def step(j, cs, i=i, q2=q2):
                m = jnp.logical_and(lane_minus_row < (i - j) * KEY_BLOCK, jnp.logical_or(bottom, j < i - 2))
                return _sb_window(lambda: jnp.dot(q2, kt_ref[0, j], preferred_element_type=F32),
                                  lambda e: lax.dot_general(e, vt_ref[0, j], NT_DIMS, preferred_element_type=F32),
                                  cs, m, tri2)

            walks.append(_finish(store, i - 2, ((pv[:Q_ROWS], carries[0]), (pv[Q_ROWS:], carries[1])), step))
        for walk in walks:
            walk()

    n_head = FAST_BLOCKS - 1
    for i in range(n_head):
        qblocks([i], i + 1)

    start = n_head + (nq - n_head) % Q_GROUP
    if start > n_head:
        qblocks_tapered(list(range(n_head, start)))

    def group(g, carry):
        qblocks_tapered([start + g * Q_GROUP + u for u in range(Q_GROUP)])
        return carry

    lax.fori_loop(0, (nq - start) // Q_GROUP, group, 0)


def _attn_prompt(qb, ktb, vtb, tri2):
    b, t, _ = qb.shape
    assert t % Q_ROWS == 0 and Q_ROWS == KEY_BLOCK
    qspec = pl.BlockSpec((1, t, LANES), lambda i, h: (i, 0, h))
    kspec = pl.BlockSpec((1, t // KEY_BLOCK, LANES, KEY_BLOCK), lambda i, h: (i, 0, h, 0))
    return pl.pallas_call(
        _attn_prompt_kernel,
        grid=(b, D_ATTN // LANES),
        in_specs=[qspec, kspec, kspec, _const_spec(tri2.shape)],
        out_specs=pl.BlockSpec((t, LANES), lambda i, h: (i, h)),
        out_shape=jax.ShapeDtypeStruct((b * t, D_ATTN), BF16),
        compiler_params=pltpu.CompilerParams(dimension_semantics=("arbitrary", "arbitrary"),
                                             vmem_limit_bytes=VMEM_LIMIT),
        name="attn_prompt",
    )(qb, ktb, vtb, tri2)


def _attn_sample_kernel(n_tail, q_ref, kn_ref, vn_ref, ckt_ref, cvt_ref, ck_hbm, cv_hbm, tri_ref, o_ref,
                        kbuf, vbuf):
    s = pl.program_id(0)
    tq = q_ref.shape[1]
    n_past = ck_hbm.shape[2] // KEY_BLOCK
    row = lax.broadcasted_iota(jnp.int32, (tq, LANES), 0)
    lane = lax.broadcasted_iota(jnp.int32, (tq, LANES), 1)
    lo_half = lane < HEAD_DIM
    tri2 = tri_ref[...]

    pad = jnp.zeros((KEY_BLOCK - tq, LANES), BF16)
    chains = []
    for p in range(q_ref.shape[2] // LANES):
        cols = slice(p * LANES, (p + 1) * LANES)
        q2 = _stack_heads(q_ref[0, :, cols], lo_half)

        def z_fn(q2=q2, cols=cols):
            z_new = lax.dot_general(q2, jnp.concatenate([kn_ref[0, :, cols], pad], axis=0), NT_DIMS,
                                    preferred_element_type=F32)
            if n_tail == 0:
                return z_new
            z_past = jnp.dot(q2, ckt_ref[0, cols, :].astype(BF16), preferred_element_type=F32)
            return jnp.concatenate([z_past, z_new], axis=1)

        def pv_fn(e, cols=cols):
            pv = jnp.dot(e[:, n_tail * KEY_BLOCK:], jnp.concatenate([vn_ref[0, :, cols], pad], axis=0),
                         preferred_element_type=F32)
            if n_tail == 0:
                return pv
            return pv + lax.dot_general(e[:, :n_tail * KEY_BLOCK], cvt_ref[0, cols, :].astype(BF16), NT_DIMS,
                                        preferred_element_type=F32)

        def store(a0, a1, cols=cols):
            o_ref[:, cols] = jnp.where(lo_half, a0, a1).astype(BF16)

        def step(j, cs, cols=cols, q2=q2, p=p):
            keys = pl.ds(pl.multiple_of(j * KEY_BLOCK, KEY_BLOCK), KEY_BLOCK)
            pltpu.sync_copy(ck_hbm.at[s, pl.ds(p * LANES, LANES), keys], kbuf)
            pltpu.sync_copy(cv_hbm.at[s, pl.ds(p * LANES, LANES), keys], vbuf)
            return _sb_window(lambda: jnp.dot(q2, kbuf[...].astype(BF16), preferred_element_type=F32),
                              lambda e: lax.dot_general(e, vbuf[...].astype(BF16), NT_DIMS,
                                                        preferred_element_type=F32),
                              cs, None, tri2)

        chains.append((z_fn, pv_fn, lane < row, store, step, n_past - 1 - n_tail))
    _sb_chains(chains, tri2)


def _attn_sample(qb, knb, vnb, cache_kt, cache_vt, tri2):
    s, tq, _ = qb.shape
    past_len = cache_kt.shape[2]
    assert past_len % KEY_BLOCK == 0 and tq <= KEY_BLOCK and tq % 16 == 0
    n_tail = min(FAST_BLOCKS - 1, past_len // KEY_BLOCK)
    assert n_tail > 0 and past_len % (n_tail * KEY_BLOCK) == 0
    new = pl.BlockSpec((1, tq, D_ATTN), lambda i: (i, 0, 0))
    tail = pl.BlockSpec((1, D_ATTN, n_tail * KEY_BLOCK), lambda i: (i, 0, past_len // (n_tail * KEY_BLOCK) - 1))
    hbm = pl.BlockSpec(memory_space=pl.ANY)
    return pl.pallas_call(
        functools.partial(_attn_sample_kernel, n_tail),
        grid=(s,),
        in_specs=[new, new, new, tail, tail, hbm, hbm, _const_spec(tri2.shape)],
        out_specs=pl.BlockSpec((tq, D_ATTN), lambda i: (i, 0)),
        out_shape=jax.ShapeDtypeStruct((s * tq, D_ATTN), BF16),
        scratch_shapes=[pltpu.VMEM((LANES, KEY_BLOCK), F32), pltpu.VMEM((LANES, KEY_BLOCK), F32)],
        compiler_params=pltpu.CompilerParams(dimension_semantics=("arbitrary",),
                                             vmem_limit_bytes=VMEM_LIMIT),
        name="attn_sample",
    )(qb, knb, vnb, cache_kt, cache_vt, cache_kt, cache_vt, tri2)


FF_CHUNK = 1024


def _out_kernel(n_first, xa_ref, xb_ref, oa_ref, ob_ref, hga_ref, hgb_ref, g1_ref, wga0_ref, wga1_ref, wgb0_ref,
                wgb1_ref, wa_ref, wb_ref, wo_ref, g2_ref, g3_ref, wup_ref, wdn_ref, g4_ref, ya_ref, yb_ref):
    first = pl.program_id(0) < n_first
    tm = xa_ref.shape[0]
    halves = [pl.ds(0, tm // 2), pl.ds(tm // 2, tm // 2)]
    n_ff = D_FF // FF_CHUNK
    st = [{}, {}]

    def pick(a_ref, b_ref, h):
        return jnp.where(first, a_ref[halves[h], :], b_ref[halves[h], :])

    def norm_in(h):
        st[h]["x"] = pick(xa_ref, xb_ref, h)
        st[h]["xn"] = _rms(st[h]["x"], g1_ref[...]).astype(BF16)

    def gate(h, w0_ref, w1_ref):
        return jnp.concatenate([jnp.dot(st[h]["xn"], w_ref[...], preferred_element_type=F32)
                                for w_ref in (w0_ref, w1_ref)], axis=1)

    def attn_gate(h):
        g_a = gate(h, wga0_ref, wga1_ref)
        y_a = jnp.dot(pick(oa_ref, ob_ref, h), wa_ref[...], preferred_element_type=F32)
        st[h]["m"] = jax.nn.sigmoid(g_a) * y_a

    def rnn_gate(h):
        g_b = gate(h, wgb0_ref, wgb1_ref)
        y_b = jnp.dot(pick(hga_ref, hgb_ref, h), wb_ref[...], preferred_element_type=F32)
        st[h]["m"] = st[h]["m"] + jax.nn.sigmoid(g_b) * y_b

    def mix(h):
        st[h]["mix"] = jnp.dot(st[h]["m"].astype(BF16), wo_ref[...], preferred_element_type=F32)

    def norm_mid(h):
        st[h]["x1"] = st[h]["x"] + _rms(st[h]["mix"], g2_ref[...])
        st[h]["f"] = _rms(st[h]["x1"], g3_ref[...]).astype(BF16)

    def ffn(h, c):
        cols = slice(c * FF_CHUNK, (c + 1) * FF_CHUNK)
        up = jnp.maximum(jnp.dot(st[h]["f"], wup_ref[:, cols], preferred_element_type=F32), 0.0)
        dn = jnp.dot((up * up).astype(BF16), wdn_ref[cols, :], preferred_element_type=F32)
        st[h]["acc"] = dn if c == 0 else st[h]["acc"] + dn

    def norm_out(h):
        yb_ref[halves[h], :] = st[h]["x1"] + _rms(st[h]["acc"], g4_ref[...])

    norm_in(0)
    norm_in(1)
    attn_gate(0)
    rnn_gate(0)
    mix(0)
    attn_gate(1)
    norm_mid(0)
    rnn_gate(1)
    mix(1)
    ffn(0, 0)
    norm_mid(1)
    for c in range(1, n_ff):
        ffn(0, c)
    ffn(1, 0)
    norm_out(0)
    for c in range(1, n_ff):
        ffn(1, c)
    norm_out(1)

    @pl.when(first)
    def _():
        ya_ref[...] = yb_ref[...]


def _out(xa, xb, oa, ob, hga, hgb, g1, w_in, wa, wb, wo, g2, g3, wup, wdn, g4):
    na, nb = xa.shape[0], xb.shape[0]
    tm = TOKEN_TILE
    assert na % tm == 0 and nb % tm == 0
    n_first = na // tm
    a_tok = lambda i: (jnp.minimum(i, n_first - 1), 0)
    b_tok = lambda i: (jnp.maximum(i - n_first, 0), 0)
    w_cols = (COL_GA, COL_GA + 1, COL_GB, COL_GB + 1)
    consts = (wa, wb, wo, g2, g3, wup, wdn, g4)
    return pl.pallas_call(
        functools.partial(_out_kernel, n_first),
        grid=((na + nb) // tm,),
        in_specs=[pl.BlockSpec((tm, D_MODEL), a_tok), pl.BlockSpec((tm, D_MODEL), b_tok),
                  pl.BlockSpec((tm, D_ATTN), a_tok), pl.BlockSpec((tm, D_ATTN), b_tok),
                  pl.BlockSpec((tm, D_RNN), a_tok), pl.BlockSpec((tm, D_RNN), b_tok), _const_spec(g1.shape)]
        + [_w_in_cols(j) for j in w_cols] + [_const_spec(c.shape) for c in consts],
        out_specs=[pl.BlockSpec((tm, D_MODEL), a_tok), pl.BlockSpec((tm, D_MODEL), b_tok)],
        out_shape=[jax.ShapeDtypeStruct((na, D_MODEL), F32), jax.ShapeDtypeStruct((nb, D_MODEL), F32)],
        compiler_params=pltpu.CompilerParams(dimension_semantics=("arbitrary",),
                                             vmem_limit_bytes=VMEM_LIMIT),
        name="out",
    )(xa, xb, oa, ob, hga, hgb, g1, *([w_in] * len(w_cols)), *consts)


def _block_diag(w):
    n, c, d = w.shape
    return jnp.einsum("ncd,nm->ncmd", w, jnp.eye(n, dtype=w.dtype)).reshape(n * c, n * d)


def _heads_last(kt):
    s, _, t = kt.shape
    return kt.reshape(s, N_HEADS, HEAD_DIM, t).transpose(0, 3, 1, 2)[None]


def _time_last(cache):
    s, t = cache.shape[:2]
    return cache.transpose(0, 2, 3, 1).reshape(s, D_ATTN, t)


def kernel(x_prompt, x_sample, cache_k, cache_v, state_conv, state_h, w_in, g_pre_mix, w_conv, b_conv, w_r, b_r, w_i, b_i, lam, w_a_out, w_b_out, w_o, g_post_mix, g_pre_ffn, w_up, w_down, g_post_ffn):
    assert w_in.shape[0] == 1
    row = lambda a: a[0].reshape(1, -1)
    assert w_in.shape[2] == (COL_GB + 2) * D_ATTN
    w_in_b = w_in[0].astype(BF16)
    wkv_t = w_in[0][:, COL_K * D_ATTN:(COL_V + 1) * D_ATTN].T.astype(BF16)
    wg = jnp.concatenate([_block_diag(w_r[0]), _block_diag(w_i[0])], axis=1).astype(BF16)
    bg = jnp.concatenate([row(b_r), row(b_i)], axis=1)
    lru_w = (w_conv[0], row(b_conv), wg, bg, row(lam))
    g1 = row(g_pre_mix)
    tri2 = _tri2()

    bp, tp, _ = x_prompt.shape
    bs, ts, _ = x_sample.shape
    n_p, n_s = bp * tp, bs * ts

    zc = jnp.zeros((bp, CONV_WIDTH - 1, D_RNN), F32)
    zh = jnp.zeros((bp, 1, D_RNN), F32)
    qb, kt, vt, ktb, vtb, hg_p, cp, hp = _proj_lru(x_prompt, g1, w_in_b, wkv_t, zc, zh, lru_w)
    o_p = _attn_prompt(qb.reshape(bp, tp, D_ATTN), ktb, vtb, tri2)

    s3 = lambda a: a.reshape(bs, ts, a.shape[-1])
    qs, ks, vs, ksb, vsb, hg_s, cs, hs = _proj(x_sample, g1, w_in_b, state_conv[0],
                                               state_h[0].reshape(bs, 1, D_RNN), lru_w)
    o_s = _attn_sample(s3(qs), s3(ksb), s3(vsb), _time_last(cache_k[0]), _time_last(cache_v[0]), tri2)

    ys, yp = _out(x_sample.reshape(n_s, D_MODEL), x_prompt.reshape(n_p, D_MODEL), o_s, o_p, hg_s, hg_p, g1, w_in_b,
                  w_a_out[0].astype(BF16), w_b_out[0].astype(BF16), w_o[0].astype(BF16), row(g_post_mix),
                  row(g_pre_ffn), w_up[0].astype(BF16), w_down[0].astype(BF16), row(g_post_ffn))
    return (yp.reshape(bp, tp, D_MODEL), ys.reshape(bs, ts, D_MODEL), _heads_last(kt), _heads_last(vt),
            cp[None], hp.reshape(1, bp, D_RNN), ks.reshape(1, bs, ts, N_HEADS, HEAD_DIM),
            vs.reshape(1, bs, ts, N_HEADS, HEAD_DIM), cs[None], hs.reshape(1, bs, D_RNN))
```

```python
import functools

import jax
import jax.numpy as jnp
import numpy as np
from jax import lax
from jax.experimental import pallas as pl
from jax.experimental.pallas import tpu as pltpu

F32 = jnp.float32
BF16 = jnp.bfloat16

D_MODEL = 1024
N_HEADS = 8
HEAD_DIM = 64
D_ATTN = N_HEADS * HEAD_DIM
D_RNN = 512
N_RNN_BLOCKS = 8
CONV_WIDTH = 4
LRU_C = 8.0
D_FF = 4 * D_MODEL
EPS = 1e-6

LANES = 128
SUBLANES = 8
KEY_BLOCK = 128
Q_ROWS = 128
FAST_BLOCKS = 3
Q_GROUP = 6
SAMPLE_STREAMS = 2
TOKEN_TILE = 512
SCAN_SLICES = 16
MXU_N = 256
LOG_KEEP_FLOOR = -104.0
VMEM_LIMIT = 56 * 1024 * 1024

NT_DIMS = (((1,), (1,)), ((), ()))
LOG2_E = 1.4426950408889634


def _rms(xf, g):
    return xf * lax.rsqrt(jnp.mean(xf * xf, axis=-1, keepdims=True) + EPS) * g


def _softplus(x):
    return jnp.maximum(x, 0.0) + jnp.log(1.0 + jnp.exp2(jnp.abs(x) * -LOG2_E))


def _sigmoid(x):
    return 0.5 * jnp.tanh(0.5 * x) + 0.5


def _const_spec(shape):
    nd = len(shape)
    return pl.BlockSpec(shape, lambda *_: (0,) * nd, pipeline_mode=pl.Buffered(1))


COL_Q, COL_K, COL_V, COL_U, COL_GR, COL_GA, COL_GB = 0, 1, 2, 3, 4, 5, 7
Q_SCALE = HEAD_DIM ** -0.5


def _w_in_cols(j):
    return pl.BlockSpec((D_MODEL, D_ATTN), lambda *_: (0, j), pipeline_mode=pl.Buffered(1))


def _rows(block_index, size):
    if isinstance(block_index, int):
        return pl.ds(block_index * size, size)
    return pl.ds(pl.multiple_of(block_index * size, size), size)


def _gelu_tanh(x):
    return 0.5 * x * (1.0 + jnp.tanh(np.sqrt(2.0 / np.pi).astype(np.float32) * (x + 0.044715 * (x * x * x))))


def _conv_seed(first, cs_ref, ext_ref):
    @pl.when(first)
    def _():
        ext_ref[0:SUBLANES, :] = jnp.zeros((SUBLANES, D_RNN), F32)
        ext_ref[SUBLANES - (CONV_WIDTH - 1):SUBLANES, :] = cs_ref[0]


def _scan_seed(first, h0_ref, hc_ref):
    @pl.when(first)
    def _():
        hc_ref[...] = jnp.broadcast_to(h0_ref[0], hc_ref.shape)


def _lru_conv(u, wc_ref, bc_ref, ext_ref):
    tt = u.shape[0]
    ext_ref[SUBLANES:SUBLANES + tt, :] = u
    uc = bc_ref[...] + wc_ref[CONV_WIDTH - 1:CONV_WIDTH, :] * u
    for d in range(1, CONV_WIDTH):
        uc = uc + wc_ref[CONV_WIDTH - 1 - d:CONV_WIDTH - d, :] * ext_ref[SUBLANES - d:SUBLANES - d + tt, :]
    ext_ref[0:SUBLANES, :] = ext_ref[tt:tt + SUBLANES, :]
    return uc


def _lru_gates(uc, wg_ref, bg_ref):
    return jnp.dot(uc.astype(BF16), wg_ref[...], preferred_element_type=F32) + bg_ref[...]


def _lru_scan(g, uc, g_rnn, lam_ref, h_prev):
    tt = uc.shape[0]
    r = _sigmoid(g[:, :D_RNN])
    ig = _sigmoid(g[:, D_RNN:])
    log_a = (-LRU_C * _softplus(-lam_ref[...])) * r
    a = jnp.exp(log_a)
    m2 = -jnp.tanh(log_a) * (a * a + 1.0)
    b = jnp.where(m2 > 0.0, m2 * lax.rsqrt(m2), 0.0) * (ig * uc)

    n_groups = tt // SUBLANES
    a3 = a.reshape(n_groups, SUBLANES, D_RNN)
    b3 = b.reshape(n_groups, SUBLANES, D_RNN)
    sub = lax.broadcasted_iota(jnp.int32, a3.shape, 1)
    k = 1
    while k < SUBLANES:
        has_prev = sub >= k
        b_prev = jnp.where(has_prev, pltpu.roll(b3, k, axis=1), 0.0)
        a_prev = jnp.where(has_prev, pltpu.roll(a3, k, axis=1), 1.0)
        b3 = b3 + a3 * b_prev
        a3 = a3 * a_prev
        k *= 2

    hs = []
    for gi in range(n_groups):
        h = b3[gi] + a3[gi] * h_prev
        hs.append(h)
        h_prev = jnp.broadcast_to(h[SUBLANES - 1:SUBLANES, :], h.shape)
    return (jnp.concatenate(hs, axis=0) * _gelu_tanh(g_rnn)).astype(BF16), h_prev


def _lru_state_specs(index_map):
    past = CONV_WIDTH - 1
    return (pl.BlockSpec((1, past, D_RNN), index_map), pl.BlockSpec((1, 1, D_RNN), index_map))


def _lru_scratch(tt):
    return [pltpu.VMEM((tt + SUBLANES, D_RNN), F32), pltpu.VMEM((SUBLANES, D_RNN), F32)]


def _proj_kernel(t, x_ref, g_ref, wq_ref, wk_ref, wv_ref, wu_ref, wgr_ref, cs_ref, h0_ref, wc_ref, bc_ref, wg_ref,
                 bg_ref, lam_ref, qb_ref, k_ref, v_ref, kb_ref, vb_ref, hg_ref, cn_ref, hl_ref, ext_ref):
    n_streams = x_ref.shape[0] // t
    past = CONV_WIDTH - 1
    xn = _rms(x_ref[...], g_ref[...]).astype(BF16)
    proj = lambda w_ref: jnp.dot(xn, w_ref[...], preferred_element_type=F32)
    u, g_rnn = proj(wu_ref), proj(wgr_ref)
    qb_ref[...] = (proj(wq_ref) * Q_SCALE).astype(BF16)
    k, v = proj(wk_ref), proj(wv_ref)
    k_ref[...] = k
    v_ref[...] = v
    kb_ref[...] = k.astype(BF16)
    vb_ref[...] = v.astype(BF16)

    ucs = []
    for s in range(n_streams):
        ext = ext_ref.at[s]
        ext[0:SUBLANES, :] = jnp.zeros((SUBLANES, D_RNN), F32)
        ext[SUBLANES - past:SUBLANES, :] = cs_ref[s]
        ucs.append(_lru_conv(u[s * t:(s + 1) * t], wc_ref, bc_ref, ext))
        cn_ref[s] = ext[SUBLANES - past:SUBLANES, :]
    uc = jnp.concatenate(ucs, axis=0)
    gates = _lru_gates(uc, wg_ref, bg_ref)
    for s in range(n_streams):
        rows = slice(s * t, (s + 1) * t)
        hg, h_last = _lru_scan(gates[rows], uc[rows], g_rnn[rows], lam_ref,
                               jnp.broadcast_to(h0_ref[s], (SUBLANES, D_RNN)))
        hg_ref[rows, :] = hg
        hl_ref[s] = h_last[0:1, :]


def _proj(x, g, w_in, conv_state, h0, lru_w):
    s, t, _ = x.shape
    n = s * t
    tm = TOKEN_TILE
    assert n % tm == 0 and tm % t == 0 and t % SUBLANES == 0
    per_tile = tm // t
    tok = lambda i: (i, 0)
    row_out = pl.BlockSpec((tm, D_ATTN), tok)
    f32o = jax.ShapeDtypeStruct((n, D_ATTN), F32)
    b16o = jax.ShapeDtypeStruct((n, D_ATTN), BF16)
    st3 = pl.BlockSpec((per_tile, CONV_WIDTH - 1, D_RNN), lambda i: (i, 0, 0))
    st1 = pl.BlockSpec((per_tile, 1, D_RNN), lambda i: (i, 0, 0))
    w_cols = (COL_Q, COL_K, COL_V, COL_U, COL_GR)
    return pl.pallas_call(
        functools.partial(_proj_kernel, t),
        grid=(n // tm,),
        in_specs=[pl.BlockSpec((tm, D_MODEL), tok), _const_spec(g.shape)] + [_w_in_cols(j) for j in w_cols]
        + [st3, st1] + [_const_spec(w.shape) for w in lru_w],
        out_specs=[row_out] * 6 + [st3, st1],
        out_shape=[b16o, f32o, f32o, b16o, b16o, b16o,
                   jax.ShapeDtypeStruct((s, CONV_WIDTH - 1, D_RNN), F32),
                   jax.ShapeDtypeStruct((s, 1, D_RNN), F32)],
        scratch_shapes=[pltpu.VMEM((per_tile, t + SUBLANES, D_RNN), F32)],
        compiler_params=pltpu.CompilerParams(dimension_semantics=("arbitrary",),
                                             vmem_limit_bytes=VMEM_LIMIT),
        name="proj",
    )(x.reshape(n, D_MODEL), g, *([w_in] * len(w_cols)), conv_state, h0, *lru_w)


def _proj_lru_kernel(per, n_tiles, x_ref, g_ref, wq_ref, wu_ref, wgr_ref, wkvt_ref, cs_ref, h0_ref, wc_ref, bc_ref,
                     wg_ref, bg_ref, lam_ref, qb_ref, kt_ref, vt_ref, ktb_ref, vtb_ref, hg_ref, cn_ref, hl_ref,
                     ext_ref, hc_ref, xn_ref, uc_ref, gr_ref, gt_ref):
    i = pl.program_id(0)
    dot_tile, scan_tile = i - 1, i - 2
    dot_live = jnp.logical_and(dot_tile >= 0, dot_tile < n_tiles)
    scan_live = jnp.logical_and(scan_tile >= 0, scan_tile < n_tiles)
    _conv_seed(jnp.logical_and(dot_live, dot_tile % per == 0), cs_ref, ext_ref)
    _scan_seed(jnp.logical_and(scan_live, scan_tile % per == 0), h0_ref, hc_ref)

    @pl.when(i == 0)
    def _():
        xn_ref[...] = jnp.zeros(xn_ref.shape, BF16)
        hc_ref[...] = jnp.zeros(hc_ref.shape, F32)
        ext_ref[...] = jnp.zeros(ext_ref.shape, F32)
        for ref in (uc_ref, gr_ref, gt_ref):
            ref[...] = jnp.zeros(ref.shape, F32)

    tm = x_ref.shape[0]
    sc, dt = i % 2, (i + 1) % 2
    xn = xn_ref[dt]
    state = [hc_ref[...]]
    rows_per = tm // SCAN_SLICES
    u_parts = {}
    uc_b = [None]

    def scan_slice(p):
        rows = pl.ds(p * rows_per, rows_per)
        hg, state[0] = _lru_scan(gt_ref[sc, rows, :], uc_ref[sc, rows, :], gr_ref[sc, rows, :], lam_ref, state[0])
        hg_ref[rows, :] = hg

    def rms_piece(r):
        rows = pl.ds(r * (tm // 4), tm // 4)
        xn_ref[sc, rows, :] = _rms(x_ref[rows, :], g_ref[...]).astype(BF16)

    def ug_chunk(c):
        is_u = c < D_RNN // MXU_N
        cols = slice(c * MXU_N % D_RNN, c * MXU_N % D_RNN + MXU_N)
        r = jnp.dot(xn, (wu_ref if is_u else wgr_ref)[:, cols], preferred_element_type=F32)
        if is_u:
            u_parts[c] = r
        else:
            gr_ref[dt, :, cols] = r

    def conv():
        uc = _lru_conv(jnp.concatenate([u_parts[c] for c in sorted(u_parts)], axis=1), wc_ref, bc_ref, ext_ref)
        uc_ref[dt] = uc
        uc_b[0] = uc.astype(BF16)

    def q_chunk(c):
        cols = slice(c * MXU_N, (c + 1) * MXU_N)
        qb_ref[:, cols] = (jnp.dot(xn, wq_ref[:, cols], preferred_element_type=F32) * Q_SCALE).astype(BF16)

    def kv_chunk(c):
        toks = slice(c * MXU_N, (c + 1) * MXU_N)
        kv = lax.dot_general(wkvt_ref[...], xn[toks], NT_DIMS, preferred_element_type=F32)
        k, v = kv[:D_ATTN], kv[D_ATTN:]
        kt_ref[0, :, toks] = k
        vt_ref[0, :, toks] = v
        for j in range(MXU_N // KEY_BLOCK):
            cols = slice(j * KEY_BLOCK, (j + 1) * KEY_BLOCK)
            ktb_ref[0, c * (MXU_N // KEY_BLOCK) + j] = k[:, cols].astype(BF16)
            vtb_ref[0, c * (MXU_N // KEY_BLOCK) + j] = v[:, cols].astype(BF16)

    def gates_chunk(c):
        cols = slice(c * MXU_N, (c + 1) * MXU_N)
        gt_ref[dt, :, cols] = jnp.dot(uc_b[0], wg_ref[:, cols], preferred_element_type=F32) + bg_ref[:, cols]

    S, P = scan_slice, functools.partial
    program = [
        P(ug_chunk, 0), P(S, 0), P(S, 1), P(ug_chunk, 1), P(S, 2), P(S, 3), P(ug_chunk, 2), conv,
        P(ug_chunk, 3), P(S, 4), P(S, 5), P(q_chunk, 0), P(S, 6), P(S, 7),
        P(q_chunk, 1), P(rms_piece, 0), P(rms_piece, 1), P(S, 8),
        P(kv_chunk, 0), P(S, 9), P(S, 10), P(S, 11), P(rms_piece, 2),
        P(kv_chunk, 1), P(S, 12), P(S, 13), P(S, 14), P(rms_piece, 3),
        P(gates_chunk, 0), P(S, 15), P(gates_chunk, 1), P(gates_chunk, 2), P(gates_chunk, 3),
    ]
    for piece in program:
        piece()
    h_last = state[0]
    hc_ref[...] = h_last

    @pl.when(jnp.logical_and(dot_live, dot_tile % per == per - 1))
    def _():
        cn_ref[0] = ext_ref[SUBLANES - (CONV_WIDTH - 1):SUBLANES, :]

    @pl.when(jnp.logical_and(scan_live, scan_tile % per == per - 1))
    def _():
        hl_ref[0] = h_last[0:1, :]


def _proj_lru(x, g, w_in, wkvt, conv_state, h0, lru_w):
    s, t, _ = x.shape
    n = s * t
    tm = TOKEN_TILE
    assert t % tm == 0 and tm % KEY_BLOCK == 0
    per = t // tm
    n_tiles = n // tm
    tile = lambda i, lag: jnp.clip(i - lag, 0, n_tiles - 1)
    tok_in = pl.BlockSpec((tm, D_MODEL), lambda i: (tile(i, 0), 0))
    q_out = pl.BlockSpec((tm, D_ATTN), lambda i: (tile(i, 1), 0))
    hg_out = pl.BlockSpec((tm, D_RNN), lambda i: (tile(i, 2), 0))
    b16o = jax.ShapeDtypeStruct((n, D_ATTN), BF16)
    kv_spec = pl.BlockSpec((1, D_ATTN, tm), lambda i: (tile(i, 1) // per, 0, tile(i, 1) % per))
    kvb_spec = pl.BlockSpec((1, tm // KEY_BLOCK, D_ATTN, KEY_BLOCK),
                            lambda i: (tile(i, 1) // per, tile(i, 1) % per, 0, 0))
    kv_shape = jax.ShapeDtypeStruct((s, D_ATTN, t), F32)
    kvb_shape = jax.ShapeDtypeStruct((s, t // KEY_BLOCK, D_ATTN, KEY_BLOCK), BF16)
    cs_spec, _ = _lru_state_specs(lambda i: (tile(i, 1) // per, 0, 0))
    _, h_spec = _lru_state_specs(lambda i: (tile(i, 2) // per, 0, 0))
    w_cols = (COL_Q, COL_U, COL_GR)
    return pl.pallas_call(
        functools.partial(_proj_lru_kernel, per, n_tiles),
        grid=(n_tiles + 2,),
        in_specs=[tok_in, _const_spec(g.shape)] + [_w_in_cols(j) for j in w_cols] + [_const_spec(wkvt.shape)]
        + [cs_spec, h_spec] + [_const_spec(w.shape) for w in lru_w],
        out_specs=[q_out, kv_spec, kv_spec, kvb_spec, kvb_spec, hg_out, cs_spec, h_spec],
        out_shape=[b16o, kv_shape, kv_shape, kvb_shape, kvb_shape, b16o,
                   jax.ShapeDtypeStruct((s, CONV_WIDTH - 1, D_RNN), F32),
                   jax.ShapeDtypeStruct((s, 1, D_RNN), F32)],
        scratch_shapes=_lru_scratch(tm) + [pltpu.VMEM((2, tm, D_MODEL), BF16), pltpu.VMEM((2, tm, D_RNN), F32),
                                           pltpu.VMEM((2, tm, D_RNN), F32), pltpu.VMEM((2, tm, 2 * D_RNN), F32)],
        compiler_params=pltpu.CompilerParams(dimension_semantics=("arbitrary",),
                                             vmem_limit_bytes=VMEM_LIMIT),
        name="proj_lru",
    )(x.reshape(n, D_MODEL), g, *([w_in] * len(w_cols)), wkvt, conv_state, h0, *lru_w)


def _sb_split(z, mask):
    n = z.shape[1] // KEY_BLOCK
    tq = z.shape[0] // 2
    sp = _softplus(z)
    cats = []
    for h in range(2):
        for d in range(n):
            blk = sp[h * tq:(h + 1) * tq, d * KEY_BLOCK:(d + 1) * KEY_BLOCK]
            if d == n - 1 and mask is not None:
                blk = jnp.where(mask, blk, 0.0)
            cats.append(_hi_lo(blk))
    return jnp.concatenate(cats, axis=0)


def _hi_lo(blk):
    hi = blk.astype(BF16)
    lo = (blk - hi.astype(F32)).astype(BF16)
    return jnp.concatenate([hi, lo], axis=1)


def _sb_cumsum(cat, tri2):
    return jnp.dot(cat, tri2, preferred_element_type=F32)


def _sb_weights(z, r, carries, mask):
    n = z.shape[1] // KEY_BLOCK
    tq = z.shape[0] // 2
    rows, new_carries = [], []
    for h in range(2):
        carry = None if carries is None else carries[h]
        es = [None] * n
        for d in reversed(range(n)):
            rd = r[(h * n + d) * tq:(h * n + d + 1) * tq]
            x = z[h * tq:(h + 1) * tq, d * KEY_BLOCK:(d + 1) * KEY_BLOCK] + rd[:, :KEY_BLOCK]
            if carry is not None:
                x = x + carry
            e = jnp.exp(x)
            if d == n - 1 and mask is not None:
                e = jnp.where(mask, e, 0.0)
            es[d] = e.astype(BF16)
            tot = rd[:, KEY_BLOCK:]
            carry = tot if carry is None else carry + tot
        rows.append(es[0] if n == 1 else jnp.concatenate(es, axis=1))
        new_carries.append(carry)
    return jnp.concatenate(rows, axis=0), new_carries


def _tri2():
    j = np.arange(KEY_BLOCK)[:, None]
    s = np.arange(KEY_BLOCK)[None, :]
    half = np.concatenate([-(j >= s).astype(np.float32), -np.ones((KEY_BLOCK, LANES), np.float32)], axis=1)
    return jnp.asarray(np.concatenate([half, half], axis=0), dtype=BF16)


def _stack_heads(q, lo_half):
    zero = jnp.zeros_like(q)
    return jnp.concatenate([jnp.where(lo_half, q, zero), jnp.where(lo_half, zero, q)], axis=0)


def _sb_window(z_fn, pv_fn, carries, mask, tri2):
    z = z_fn()
    e, cs = _sb_weights(z, _sb_cumsum(_sb_split(z, mask), tri2), carries, mask)
    pv = pv_fn(e)
    tq = pv.shape[0] // 2
    return (pv[:tq], cs[0]), (pv[tq:], cs[1])


def _finish(store, j0, outs, step_fn):
    (a0, c0), (a1, c1) = outs
    store(a0, a1)
    if isinstance(j0, int) and j0 < 0:
        return lambda: None

    def alive(x0, x1):
        return jnp.max(jnp.maximum(x0, x1)) >= LOG_KEEP_FLOOR

    go = jnp.logical_and(j0 >= 0, alive(c0, c1))

    def walk():
        @pl.when(go)
        def _():
            def body(s):
                j, b0, b1, d0, d1, _ = s
                (p0, d0), (p1, d1) = step_fn(j, [d0, d1])
                return j - 1, b0 + p0, b1 + p1, d0, d1, jnp.logical_and(j >= 1, alive(d0, d1))

            _, b0, b1, _, _, _ = lax.while_loop(lambda s: s[-1], body,
                                                (jnp.int32(j0), a0, a1, c0, c1, jnp.bool_(True)))
            store(b0, b1)

    return walk


def _sb_chains(chains, tri2):
    zs = [c[0]() for c in chains]
    sums = [_sb_cumsum(_sb_split(z, c[2]), tri2) for z, c in zip(zs, chains)]
    weights = [_sb_weights(z, r, None, c[2]) for z, r, c in zip(zs, sums, chains)]
    walks = []
    for c, (e, cs) in zip(chains, weights):
        pv = c[1](e)
        tq = pv.shape[0] // 2
        walks.append(_finish(c[3], c[5], ((pv[:tq], cs[0]), (pv[tq:], cs[1])), c[4]))
    for walk in walks:
        walk()


def _attn_prompt_kernel(q_ref, kt_ref, vt_ref, tri_ref, o_ref):
    t = q_ref.shape[1]
    nq = t // Q_ROWS
    row = lax.broadcasted_iota(jnp.int32, (Q_ROWS, LANES), 0)
    lane = lax.broadcasted_iota(jnp.int32, (Q_ROWS, LANES), 1)
    lo_half = lane < HEAD_DIM
    diag_mask = lane < row
    lane_minus_row = lane - row
    tri2 = tri_ref[...]

    def window(ref, first, n):
        return jnp.concatenate([ref[0, first + d] for d in range(n)], axis=1)

    def qblocks(blocks, n_fast):
        chains = []
        for i in blocks:
            q2 = _stack_heads(q_ref[0, _rows(i, Q_ROWS), :], lo_half)
            first = i + 1 - n_fast

            def z_fn(q2=q2, first=first):
                return jnp.dot(q2, window(kt_ref, first, n_fast), preferred_element_type=F32)

            def pv_fn(e, first=first):
                return lax.dot_general(e, window(vt_ref, first, n_fast), NT_DIMS, preferred_element_type=F32)

            def store(b0, b1, i=i):
                o_ref[_rows(i, Q_ROWS), :] = jnp.where(lo_half, b0, b1).astype(BF16)

            def step(j, cs, i=i, q2=q2):
                m = lane_minus_row < (i - j) * KEY_BLOCK
                return _sb_window(lambda: jnp.dot(q2, kt_ref[0, j], preferred_element_type=F32),
                                  lambda e: lax.dot_general(e, vt_ref[0, j], NT_DIMS, preferred_element_type=F32),
                                  cs, m, tri2)

            chains.append((z_fn, pv_fn, diag_mask, store, step, i - n_fast))
        _sb_chains(chains, tri2)

    half = Q_ROWS // 2
    bottom = row >= half

    def qblocks_tapered(blocks):
        assert FAST_BLOCKS == 3
        chains = []
        for i in blocks:
            q2 = _stack_heads(q_ref[0, _rows(i, Q_ROWS), :], lo_half)
            q2_top = jnp.concatenate([q2[:half], q2[Q_ROWS:Q_ROWS + half]], axis=0)
            z_new = jnp.dot(q2, window(kt_ref, i - 1, 2), preferred_element_type=F32)
            z_old = jnp.dot(q2_top, kt_ref[0, i - 2], preferred_element_type=F32)
            chains.append([i, q2, z_new, z_old])
        for c in chains:
            _, _, z_new, z_old = c
            sp_new, sp_old = _softplus(z_new), _softplus(z_old)
            cats = []
            for h in range(2):
                rows = slice(h * Q_ROWS, (h + 1) * Q_ROWS)
                cats.append(_hi_lo(sp_new[rows, :KEY_BLOCK]))
                cats.append(_hi_lo(jnp.where(diag_mask, sp_new[rows, KEY_BLOCK:], 0.0)))
            for h in range(2):
                cats.append(_hi_lo(sp_old[h * half:(h + 1) * half]))
            c.append(_sb_cumsum(jnp.concatenate(cats, axis=0), tri2))
        walks = []
        for i, q2, z_new, z_old, r in chains:
            es, carries = [], []
            for h in range(2):
                rows = slice(h * Q_ROWS, (h + 1) * Q_ROWS)
                r_prev = r[2 * h * Q_ROWS:(2 * h + 1) * Q_ROWS]
                r_diag = r[(2 * h + 1) * Q_ROWS:(2 * h + 2) * Q_ROWS]
                r_old = r[4 * Q_ROWS + h * half:4 * Q_ROWS + (h + 1) * half]
                e_diag = jnp.where(diag_mask, jnp.exp(z_new[rows, KEY_BLOCK:] + r_diag[:, :KEY_BLOCK]), 0.0)
                carry = r_diag[:, KEY_BLOCK:]
                e_prev = jnp.exp(z_new[rows, :KEY_BLOCK] + r_prev[:, :KEY_BLOCK] + carry)
                carry = carry + r_prev[:, KEY_BLOCK:]
                e_old = jnp.exp(z_old[h * half:(h + 1) * half] + r_old[:, :KEY_BLOCK] + carry[:half])
                carries.append(jnp.concatenate([carry[:half] + r_old[:, KEY_BLOCK:], carry[half:]], axis=0))
                e_old = jnp.concatenate([e_old.astype(BF16), jnp.zeros((half, KEY_BLOCK), BF16)], axis=0)
                es.append(jnp.concatenate([e_old, e_prev.astype(BF16), e_diag.astype(BF16)], axis=1))
            pv = lax.dot_general(jnp.concatenate(es, axis=0), window(vt_ref, i - 2, FAST_BLOCKS), NT_DIMS,
                                 preferred_element_type=F32)

            def store(b0, b1, i=i):
                o_ref[_rows(i, Q_ROWS), :] = jnp.where(lo_half, b0, b1).astype(BF16)

            def step(j, cs, i=i, q2=q2):
                m = jnp.logical_and(lane_minus_row < (i - j) * KEY_BLOCK, jnp.logical_or(bottom, j < i - 2))
                return _sb_window(lambda: jnp.dot(q2, kt_ref[0, j], preferred_element_type=F32),
                                  lambda e: lax.dot_general(e, vt_ref[0, j], NT_DIMS, preferred_element_type=F32),
                                  cs, m, tri2)

            walks.append(_finish(store, i - 2, ((pv[:Q_ROWS], carries[0]), (pv[Q_ROWS:], carries[1])), step))
        for walk in walks:
            walk()

    n_head = FAST_BLOCKS - 1
    for i in range(n_head):
        qblocks([i], i + 1)

    start = n_head + (nq - n_head) % Q_GROUP
    if start > n_head:
        qblocks_tapered(list(range(n_head, start)))

    def group(g, carry):
        qblocks_tapered([start + g * Q_GROUP + u for u in range(Q_GROUP)])
        return carry

    lax.fori_loop(0, (nq - start) // Q_GROUP, group, 0)


def _attn_prompt(qb, ktb, vtb, tri2):
    b, t, _ = qb.shape
    assert t % Q_ROWS == 0 and Q_ROWS == KEY_BLOCK
    qspec = pl.BlockSpec((1, t, LANES), lambda i, h: (i, 0, h))
    kspec = pl.BlockSpec((1, t // KEY_BLOCK, LANES, KEY_BLOCK), lambda i, h: (i, 0, h, 0))
    return pl.pallas_call(
        _attn_prompt_kernel,
        grid=(b, D_ATTN // LANES),
        in_specs=[qspec, kspec, kspec, _const_spec(tri2.shape)],
        out_specs=pl.BlockSpec((t, LANES), lambda i, h: (i, h)),
        out_shape=jax.ShapeDtypeStruct((b * t, D_ATTN), BF16),
        compiler_params=pltpu.CompilerParams(dimension_semantics=("arbitrary", "arbitrary"),
                                             vmem_limit_bytes=VMEM_LIMIT),
        name="attn_prompt",
    )(qb, ktb, vtb, tri2)


def _attn_sample_kernel(n_tail, q_ref, kn_ref, vn_ref, ckt_ref, cvt_ref, ck_hbm, cv_hbm, tri_ref, o_ref,
                        kbuf, vbuf):
    n_streams = q_ref.shape[0]
    tq = q_ref.shape[1]
    n_past = ck_hbm.shape[2] // KEY_BLOCK
    row = lax.broadcasted_iota(jnp.int32, (tq, LANES), 0)
    lane = lax.broadcasted_iota(jnp.int32, (tq, LANES), 1)
    lo_half = lane < HEAD_DIM
    tri2 = tri_ref[...]

    pad = jnp.zeros((KEY_BLOCK - tq, LANES), BF16)
    chains = []
    for b in range(n_streams):
        s = pl.program_id(0) * n_streams + b
        for p in range(q_ref.shape[2] // LANES):
            cols = slice(p * LANES, (p + 1) * LANES)
            q2 = _stack_heads(q_ref[b, :, cols], lo_half)

            def z_fn(q2=q2, cols=cols, b=b):
                z_new = lax.dot_general(q2, jnp.concatenate([kn_ref[b, :, cols], pad], axis=0), NT_DIMS,
                                        preferred_element_type=F32)
                if n_tail == 0:
                    return z_new
                z_past = jnp.dot(q2, ckt_ref[b, cols, :].astype(BF16), preferred_element_type=F32)
                return jnp.concatenate([z_past, z_new], axis=1)

            def pv_fn(e, cols=cols, b=b):
                pv = jnp.dot(e[:, n_tail * KEY_BLOCK:], jnp.concatenate([vn_ref[b, :, cols], pad], axis=0),
                             preferred_element_type=F32)
                if n_tail == 0:
                    return pv
                return pv + lax.dot_general(e[:, :n_tail * KEY_BLOCK], cvt_ref[b, cols, :].astype(BF16), NT_DIMS,
                                            preferred_element_type=F32)

            def store(a0, a1, cols=cols, b=b):
                o_ref[b * tq:(b + 1) * tq, cols] = jnp.where(lo_half, a0, a1).astype(BF16)

            def step(j, cs, q2=q2, p=p, s=s):
                keys = pl.ds(pl.multiple_of(j * KEY_BLOCK, KEY_BLOCK), KEY_BLOCK)
                pltpu.sync_copy(ck_hbm.at[s, pl.ds(p * LANES, LANES), keys], kbuf)
                pltpu.sync_copy(cv_hbm.at[s, pl.ds(p * LANES, LANES), keys], vbuf)
                return _sb_window(lambda: jnp.dot(q2, kbuf[...].astype(BF16), preferred_element_type=F32),
                                  lambda e: lax.dot_general(e, vbuf[...].astype(BF16), NT_DIMS,
                                                            preferred_element_type=F32),
                                  cs, None, tri2)

            chains.append((z_fn, pv_fn, lane < row, store, step, n_past - 1 - n_tail))
    _sb_chains(chains, tri2)


def _attn_sample(qb, knb, vnb, cache_kt, cache_vt, tri2):
    s, tq, _ = qb.shape
    past_len = cache_kt.shape[2]
    assert past_len % KEY_BLOCK == 0 and tq <= KEY_BLOCK and tq % 16 == 0
    n_tail = min(FAST_BLOCKS - 1, past_len // KEY_BLOCK)
    assert n_tail > 0 and past_len % (n_tail * KEY_BLOCK) == 0
    per = SAMPLE_STREAMS if s % SAMPLE_STREAMS == 0 else 1
    new = pl.BlockSpec((per, tq, D_ATTN), lambda i: (i, 0, 0))
    tail = pl.BlockSpec((per, D_ATTN, n_tail * KEY_BLOCK), lambda i: (i, 0, past_len // (n_tail * KEY_BLOCK) - 1))
    hbm = pl.BlockSpec(memory_space=pl.ANY)
    return pl.pallas_call(
        functools.partial(_attn_sample_kernel, n_tail),
        grid=(s // per,),
        in_specs=[new, new, new, tail, tail, hbm, hbm, _const_spec(tri2.shape)],
        out_specs=pl.BlockSpec((per * tq, D_ATTN), lambda i: (i, 0)),
        out_shape=jax.ShapeDtypeStruct((s * tq, D_ATTN), BF16),
        scratch_shapes=[pltpu.VMEM((LANES, KEY_BLOCK), F32), pltpu.VMEM((LANES, KEY_BLOCK), F32)],
        compiler_params=pltpu.CompilerParams(dimension_semantics=("arbitrary",),
                                             vmem_limit_bytes=VMEM_LIMIT),
        name="attn_sample",
    )(qb, knb, vnb, cache_kt, cache_vt, cache_kt, cache_vt, tri2)


FF_CHUNK = 1024


def _out_kernel(n_first, xa_ref, xb_ref, oa_ref, ob_ref, hga_ref, hgb_ref, g1_ref, wga0_ref, wga1_ref, wgb0_ref,
                wgb1_ref, wa_ref, wb_ref, wo_ref, g2_ref, g3_ref, wup_ref, wdn_ref, g4_ref, ya_ref, yb_ref):
    first = pl.program_id(0) < n_first
    tm = xa_ref.shape[0]
    halves = [pl.ds(0, tm // 2), pl.ds(tm // 2, tm // 2)]
    n_ff = D_FF // FF_CHUNK
    st = [{}, {}]

    def pick(a_ref, b_ref, h):
        return jnp.where(first, a_ref[halves[h], :], b_ref[halves[h], :])

    def norm_in(h):
        st[h]["x"] = pick(xa_ref, xb_ref, h)
        st[h]["xn"] = _rms(st[h]["x"], g1_ref[...]).astype(BF16)

    def gate(h, w0_ref, w1_ref):
        return jnp.concatenate([jnp.dot(st[h]["xn"], w_ref[...], preferred_element_type=F32)
                                for w_ref in (w0_ref, w1_ref)], axis=1)

    def attn_gate(h):
        g_a = gate(h, wga0_ref, wga1_ref)
        y_a = jnp.dot(pick(oa_ref, ob_ref, h), wa_ref[...], preferred_element_type=F32)
        st[h]["m"] = jax.nn.sigmoid(g_a) * y_a

    def rnn_gate(h):
        g_b = gate(h, wgb0_ref, wgb1_ref)
        y_b = jnp.dot(pick(hga_ref, hgb_ref, h), wb_ref[...], preferred_element_type=F32)
        st[h]["m"] = st[h]["m"] + jax.nn.sigmoid(g_b) * y_b

    def mix(h):
        st[h]["mix"] = jnp.dot(st[h]["m"].astype(BF16), wo_ref[...], preferred_element_type=F32)

    def norm_mid(h):
        st[h]["x1"] = st[h]["x"] + _rms(st[h]["mix"], g2_ref[...])
        st[h]["f"] = _rms(st[h]["x1"], g3_ref[...]).astype(BF16)

    def ffn(h, c):
        cols = slice(c * FF_CHUNK, (c + 1) * FF_CHUNK)
        up = jnp.maximum(jnp.dot(st[h]["f"], wup_ref[:, cols], preferred_element_type=F32), 0.0)
        dn = jnp.dot((up * up).astype(BF16), wdn_ref[cols, :], preferred_element_type=F32)
        st[h]["acc"] = dn if c == 0 else st[h]["acc"] + dn

    def norm_out(h):
        yb_ref[halves[h], :] = st[h]["x1"] + _rms(st[h]["acc"], g4_ref[...])

    norm_in(0)
    norm_in(1)
    attn_gate(0)
    rnn_gate(0)
    mix(0)
    attn_gate(1)
    norm_mid(0)
    rnn_gate(1)
    mix(1)
    ffn(0, 0)
    norm_mid(1)
    for c in range(1, n_ff):
        ffn(0, c)
    ffn(1, 0)
    norm_out(0)
    for c in range(1, n_ff):
        ffn(1, c)
    norm_out(1)

    @pl.when(first)
    def _():
        ya_ref[...] = yb_ref[...]


def _out(xa, xb, oa, ob, hga, hgb, g1, w_in, wa, wb, wo, g2, g3, wup, wdn, g4):
    na, nb = xa.shape[0], xb.shape[0]
    tm = TOKEN_TILE
    assert na % tm == 0 and nb % tm == 0
    n_first = na // tm
    a_tok = lambda i: (jnp.minimum(i, n_first - 1), 0)
    b_tok = lambda i: (jnp.maximum(i - n_first, 0), 0)
    w_cols = (COL_GA, COL_GA + 1, COL_GB, COL_GB + 1)
    consts = (wa, wb, wo, g2, g3, wup, wdn, g4)
    return pl.pallas_call(
        functools.partial(_out_kernel, n_first),
        grid=((na + nb) // tm,),
        in_specs=[pl.BlockSpec((tm, D_MODEL), a_tok), pl.BlockSpec((tm, D_MODEL), b_tok),
                  pl.BlockSpec((tm, D_ATTN), a_tok), pl.BlockSpec((tm, D_ATTN), b_tok),
                  pl.BlockSpec((tm, D_RNN), a_tok), pl.BlockSpec((tm, D_RNN), b_tok), _const_spec(g1.shape)]
        + [_w_in_cols(j) for j in w_cols] + [_const_spec(c.shape) for c in consts],
        out_specs=[pl.BlockSpec((tm, D_MODEL), a_tok), pl.BlockSpec((tm, D_MODEL), b_tok)],
        out_shape=[jax.ShapeDtypeStruct((na, D_MODEL), F32), jax.ShapeDtypeStruct((nb, D_MODEL), F32)],
        compiler_params=pltpu.CompilerParams(dimension_semantics=("arbitrary",),
                                             vmem_limit_bytes=VMEM_LIMIT),
        name="out",
    )(xa, xb, oa, ob, hga, hgb, g1, *([w_in] * len(w_cols)), *consts)


def _block_diag(w):
    n, c, d = w.shape
    return jnp.einsum("ncd,nm->ncmd", w, jnp.eye(n, dtype=w.dtype)).reshape(n * c, n * d)


def _heads_last(kt):
    s, _, t = kt.shape
    return kt.reshape(s, N_HEADS, HEAD_DIM, t).transpose(0, 3, 1, 2)[None]


def _time_last(cache):
    s, t = cache.shape[:2]
    return cache.transpose(0, 2, 3, 1).reshape(s, D_ATTN, t)


def kernel(x_prompt, x_sample, cache_k, cache_v, state_conv, state_h, w_in, g_pre_mix, w_conv, b_conv, w_r, b_r, w_i, b_i, lam, w_a_out, w_b_out, w_o, g_post_mix, g_pre_ffn, w_up, w_down, g_post_ffn):
    assert w_in.shape[0] == 1
    row = lambda a: a[0].reshape(1, -1)
    assert w_in.shape[2] == (COL_GB + 2) * D_ATTN
    w_in_b = w_in[0].astype(BF16)
    wkv_t = w_in[0][:, COL_K * D_ATTN:(COL_V + 1) * D_ATTN].T.astype(BF16)
    wg = jnp.concatenate([_block_diag(w_r[0]), _block_diag(w_i[0])], axis=1).astype(BF16)
    bg = jnp.concatenate([row(b_r), row(b_i)], axis=1)
    lru_w = (w_conv[0], row(b_conv), wg, bg, row(lam))
    g1 = row(g_pre_mix)
    tri2 = _tri2()

    bp, tp, _ = x_prompt.shape
    bs, ts, _ = x_sample.shape
    n_p, n_s = bp * tp, bs * ts

    zc = jnp.zeros((bp, CONV_WIDTH - 1, D_RNN), F32)
    zh = jnp.zeros((bp, 1, D_RNN), F32)
    qb, kt, vt, ktb, vtb, hg_p, cp, hp = _proj_lru(x_prompt, g1, w_in_b, wkv_t, zc, zh, lru_w)
    o_p = _attn_prompt(qb.reshape(bp, tp, D_ATTN), ktb, vtb, tri2)

    s3 = lambda a: a.reshape(bs, ts, a.shape[-1])
    qs, ks, vs, ksb, vsb, hg_s, cs, hs = _proj(x_sample, g1, w_in_b, state_conv[0],
                                               state_h[0].reshape(bs, 1, D_RNN), lru_w)
    o_s = _attn_sample(s3(qs), s3(ksb), s3(vsb), _time_last(cache_k[0]), _time_last(cache_v[0]), tri2)

    ys, yp = _out(x_sample.reshape(n_s, D_MODEL), x_prompt.reshape(n_p, D_MODEL), o_s, o_p, hg_s, hg_p, g1, w_in_b,
                  w_a_out[0].astype(BF16), w_b_out[0].astype(BF16), w_o[0].astype(BF16), row(g_post_mix),
                  row(g_pre_ffn), w_up[0].astype(BF16), w_down[0].astype(BF16), row(g_post_ffn))
    return (yp.reshape(bp, tp, D_MODEL), ys.reshape(bs, ts, D_MODEL), _heads_last(kt), _heads_last(vt),
            cp[None], hp.reshape(1, bp, D_RNN), ks.reshape(1, bs, ts, N_HEADS, HEAD_DIM),
            vs.reshape(1, bs, ts, N_HEADS, HEAD_DIM), cs[None], hs.reshape(1, bs, D_RNN))
```

```python
import functools

import jax
import jax.numpy as jnp
import numpy as np
from jax import lax
from jax.experimental import pallas as pl
from jax.experimental.pallas import tpu as pltpu

F32 = jnp.float32
BF16 = jnp.bfloat16

D_MODEL = 1024
N_HEADS = 8
HEAD_DIM = 64
D_ATTN = N_HEADS * HEAD_DIM
D_RNN = 512
N_RNN_BLOCKS = 8
CONV_WIDTH = 4
LRU_C = 8.0
D_FF = 4 * D_MODEL
EPS = 1e-6

LANES = 128
SUBLANES = 8
KEY_BLOCK = 128
Q_ROWS = 128
FAST_BLOCKS = 3
Q_GROUP = 10
SAMPLE_STREAMS = 4
TOKEN_TILE = 512
SCAN_SLICES = 16
MXU_N = 256
LOG_KEEP_FLOOR = -104.0
VMEM_LIMIT = 56 * 1024 * 1024

NT_DIMS = (((1,), (1,)), ((), ()))
LOG2_E = 1.4426950408889634


def _rms(xf, g):
    return xf * lax.rsqrt(jnp.mean(xf * xf, axis=-1, keepdims=True) + EPS) * g


def _softplus(x):
    return jnp.maximum(x, 0.0) + jnp.log(1.0 + jnp.exp2(jnp.abs(x) * -LOG2_E))


def _sigmoid(x):
    return 0.5 * jnp.tanh(0.5 * x) + 0.5


def _const_spec(shape):
    nd = len(shape)
    return pl.BlockSpec(shape, lambda *_: (0,) * nd, pipeline_mode=pl.Buffered(1))


COL_Q, COL_K, COL_V, COL_U, COL_GR, COL_GA, COL_GB = 0, 1, 2, 3, 4, 5, 7
Q_SCALE = HEAD_DIM ** -0.5


def _w_in_cols(j):
    return pl.BlockSpec((D_MODEL, D_ATTN), lambda *_: (0, j), pipeline_mode=pl.Buffered(1))


def _rows(block_index, size):
    if isinstance(block_index, int):
        return pl.ds(block_index * size, size)
    return pl.ds(pl.multiple_of(block_index * size, size), size)


def _gelu_tanh(x):
    return 0.5 * x * (1.0 + jnp.tanh(np.sqrt(2.0 / np.pi).astype(np.float32) * (x + 0.044715 * (x * x * x))))


def _conv_seed(first, cs_ref, ext_ref):
    @pl.when(first)
    def _():
        ext_ref[0:SUBLANES, :] = jnp.zeros((SUBLANES, D_RNN), F32)
        ext_ref[SUBLANES - (CONV_WIDTH - 1):SUBLANES, :] = cs_ref[0]


def _scan_seed(first, h0_ref, hc_ref):
    @pl.when(first)
    def _():
        hc_ref[...] = jnp.broadcast_to(h0_ref[0], hc_ref.shape)


def _lru_conv(u, wc_ref, bc_ref, ext_ref):
    tt = u.shape[0]
    ext_ref[SUBLANES:SUBLANES + tt, :] = u
    uc = bc_ref[...] + wc_ref[CONV_WIDTH - 1:CONV_WIDTH, :] * u
    for d in range(1, CONV_WIDTH):
        uc = uc + wc_ref[CONV_WIDTH - 1 - d:CONV_WIDTH - d, :] * ext_ref[SUBLANES - d:SUBLANES - d + tt, :]
    ext_ref[0:SUBLANES, :] = ext_ref[tt:tt + SUBLANES, :]
    return uc


def _lru_gates(uc, wg_ref, bg_ref):
    return jnp.dot(uc.astype(BF16), wg_ref[...], preferred_element_type=F32) + bg_ref[...]


def _lru_scan(g, uc, g_rnn, lam_ref, h_prev):
    tt = uc.shape[0]
    r = _sigmoid(g[:, :D_RNN])
    ig = _sigmoid(g[:, D_RNN:])
    log_a = (-LRU_C * _softplus(-lam_ref[...])) * r
    a = jnp.exp(log_a)
    m2 = -jnp.tanh(log_a) * (a * a + 1.0)
    b = jnp.where(m2 > 0.0, m2 * lax.rsqrt(m2), 0.0) * (ig * uc)

    n_groups = tt // SUBLANES
    a3 = a.reshape(n_groups, SUBLANES, D_RNN)
    b3 = b.reshape(n_groups, SUBLANES, D_RNN)
    sub = lax.broadcasted_iota(jnp.int32, a3.shape, 1)
    k = 1
    while k < SUBLANES:
        has_prev = sub >= k
        b_prev = jnp.where(has_prev, pltpu.roll(b3, k, axis=1), 0.0)
        a_prev = jnp.where(has_prev, pltpu.roll(a3, k, axis=1), 1.0)
        b3 = b3 + a3 * b_prev
        a3 = a3 * a_prev
        k *= 2

    hs = []
    for gi in range(n_groups):
        h = b3[gi] + a3[gi] * h_prev
        hs.append(h)
        h_prev = jnp.broadcast_to(h[SUBLANES - 1:SUBLANES, :], h.shape)
    return (jnp.concatenate(hs, axis=0) * _gelu_tanh(g_rnn)).astype(BF16), h_prev


def _lru_state_specs(index_map):
    past = CONV_WIDTH - 1
    return (pl.BlockSpec((1, past, D_RNN), index_map), pl.BlockSpec((1, 1, D_RNN), index_map))


def _lru_scratch(tt):
    return [pltpu.VMEM((tt + SUBLANES, D_RNN), F32), pltpu.VMEM((SUBLANES, D_RNN), F32)]


def _proj_kernel(t, x_ref, g_ref, wq_ref, wk_ref, wv_ref, wu_ref, wgr_ref, cs_ref, h0_ref, wc_ref, bc_ref, wg_ref,
                 bg_ref, lam_ref, qb_ref, k_ref, v_ref, kb_ref, vb_ref, hg_ref, cn_ref, hl_ref, ext_ref):
    n_streams = x_ref.shape[0] // t
    past = CONV_WIDTH - 1
    xn = _rms(x_ref[...], g_ref[...]).astype(BF16)
    proj = lambda w_ref: jnp.dot(xn, w_ref[...], preferred_element_type=F32)
    u, g_rnn = proj(wu_ref), proj(wgr_ref)
    qb_ref[...] = (proj(wq_ref) * Q_SCALE).astype(BF16)
    k, v = proj(wk_ref), proj(wv_ref)
    k_ref[...] = k
    v_ref[...] = v
    kb_ref[...] = k.astype(BF16)
    vb_ref[...] = v.astype(BF16)

    ucs = []
    for s in range(n_streams):
        ext = ext_ref.at[s]
        ext[0:SUBLANES, :] = jnp.zeros((SUBLANES, D_RNN), F32)
        ext[SUBLANES - past:SUBLANES, :] = cs_ref[s]
        ucs.append(_lru_conv(u[s * t:(s + 1) * t], wc_ref, bc_ref, ext))
        cn_ref[s] = ext[SUBLANES - past:SUBLANES, :]
    uc = jnp.concatenate(ucs, axis=0)
    gates = _lru_gates(uc, wg_ref, bg_ref)
    for s in range(n_streams):
        rows = slice(s * t, (s + 1) * t)
        hg, h_last = _lru_scan(gates[rows], uc[rows], g_rnn[rows], lam_ref,
                               jnp.broadcast_to(h0_ref[s], (SUBLANES, D_RNN)))
        hg_ref[rows, :] = hg
        hl_ref[s] = h_last[0:1, :]


def _proj(x, g, w_in, conv_state, h0, lru_w):
    s, t, _ = x.shape
    n = s * t
    tm = TOKEN_TILE
    assert n % tm == 0 and tm % t == 0 and t % SUBLANES == 0
    per_tile = tm // t
    tok = lambda i: (i, 0)
    row_out = pl.BlockSpec((tm, D_ATTN), tok)
    f32o = jax.ShapeDtypeStruct((n, D_ATTN), F32)
    b16o = jax.ShapeDtypeStruct((n, D_ATTN), BF16)
    st3 = pl.BlockSpec((per_tile, CONV_WIDTH - 1, D_RNN), lambda i: (i, 0, 0))
    st1 = pl.BlockSpec((per_tile, 1, D_RNN), lambda i: (i, 0, 0))
    w_cols = (COL_Q, COL_K, COL_V, COL_U, COL_GR)
    return pl.pallas_call(
        functools.partial(_proj_kernel, t),
        grid=(n // tm,),
        in_specs=[pl.BlockSpec((tm, D_MODEL), tok), _const_spec(g.shape)] + [_w_in_cols(j) for j in w_cols]
        + [st3, st1] + [_const_spec(w.shape) for w in lru_w],
        out_specs=[row_out] * 6 + [st3, st1],
        out_shape=[b16o, f32o, f32o, b16o, b16o, b16o,
                   jax.ShapeDtypeStruct((s, CONV_WIDTH - 1, D_RNN), F32),
                   jax.ShapeDtypeStruct((s, 1, D_RNN), F32)],
        scratch_shapes=[pltpu.VMEM((per_tile, t + SUBLANES, D_RNN), F32)],
        compiler_params=pltpu.CompilerParams(dimension_semantics=("arbitrary",),
                                             vmem_limit_bytes=VMEM_LIMIT),
        name="proj",
    )(x.reshape(n, D_MODEL), g, *([w_in] * len(w_cols)), conv_state, h0, *lru_w)


def _proj_lru_kernel(per, n_tiles, x_ref, g_ref, wq_ref, wu_ref, wgr_ref, wkvt_ref, cs_ref, h0_ref, wc_ref, bc_ref,
                     wg_ref, bg_ref, lam_ref, qb_ref, kt_ref, vt_ref, ktb_ref, vtb_ref, hg_ref, cn_ref, hl_ref,
                     ext_ref, hc_ref, xn_ref, uc_ref, gr_ref, gt_ref):
    i = pl.program_id(0)
    dot_tile, scan_tile = i - 1, i - 2
    dot_live = jnp.logical_and(dot_tile >= 0, dot_tile < n_tiles)
    scan_live = jnp.logical_and(scan_tile >= 0, scan_tile < n_tiles)
    _conv_seed(jnp.logical_and(dot_live, dot_tile % per == 0), cs_ref, ext_ref)
    _scan_seed(jnp.logical_and(scan_live, scan_tile % per == 0), h0_ref, hc_ref)

    @pl.when(i == 0)
    def _():
        xn_ref[...] = jnp.zeros(xn_ref.shape, BF16)
        hc_ref[...] = jnp.zeros(hc_ref.shape, F32)
        ext_ref[...] = jnp.zeros(ext_ref.shape, F32)
        for ref in (uc_ref, gr_ref, gt_ref):
            ref[...] = jnp.zeros(ref.shape, F32)

    tm = x_ref.shape[0]
    sc, dt = i % 2, (i + 1) % 2
    xn = xn_ref[dt]
    state = [hc_ref[...]]
    rows_per = tm // SCAN_SLICES
    u_parts = {}
    uc_b = [None]

    def scan_slice(p):
        rows = pl.ds(p * rows_per, rows_per)
        hg, state[0] = _lru_scan(gt_ref[sc, rows, :], uc_ref[sc, rows, :], gr_ref[sc, rows, :], lam_ref, state[0])
        hg_ref[rows, :] = hg

    def rms_piece(r):
        rows = pl.ds(r * (tm // 4), tm // 4)
        xn_ref[sc, rows, :] = _rms(x_ref[rows, :], g_ref[...]).astype(BF16)

    def ug_chunk(c):
        is_u = c < D_RNN // MXU_N
        cols = slice(c * MXU_N % D_RNN, c * MXU_N % D_RNN + MXU_N)
        r = jnp.dot(xn, (wu_ref if is_u else wgr_ref)[:, cols], preferred_element_type=F32)
        if is_u:
            u_parts[c] = r
        else:
            gr_ref[dt, :, cols] = r

    def conv():
        uc = _lru_conv(jnp.concatenate([u_parts[c] for c in sorted(u_parts)], axis=1), wc_ref, bc_ref, ext_ref)
        uc_ref[dt] = uc
        uc_b[0] = uc.astype(BF16)

    def q_chunk(c):
        cols = slice(c * MXU_N, (c + 1) * MXU_N)
        qb_ref[:, cols] = (jnp.dot(xn, wq_ref[:, cols], preferred_element_type=F32) * Q_SCALE).astype(BF16)

    def kv_chunk(c):
        toks = slice(c * MXU_N, (c + 1) * MXU_N)
        kv = lax.dot_general(wkvt_ref[...], xn[toks], NT_DIMS, preferred_element_type=F32)
        k, v = kv[:D_ATTN], kv[D_ATTN:]
        kt_ref[0, :, toks] = k
        vt_ref[0, :, toks] = v
        for j in range(MXU_N // KEY_BLOCK):
            cols = slice(j * KEY_BLOCK, (j + 1) * KEY_BLOCK)
            ktb_ref[0, c * (MXU_N // KEY_BLOCK) + j] = k[:, cols].astype(BF16)
            vtb_ref[0, c * (MXU_N // KEY_BLOCK) + j] = v[:, cols].astype(BF16)

    def gates_chunk(c):
        cols = slice(c * MXU_N, (c + 1) * MXU_N)
        gt_ref[dt, :, cols] = jnp.dot(uc_b[0], wg_ref[:, cols], preferred_element_type=F32) + bg_ref[:, cols]

    S, P = scan_slice, functools.partial
    program = [
        P(ug_chunk, 0), P(S, 0), P(S, 1), P(ug_chunk, 1), P(S, 2), P(S, 3), P(ug_chunk, 2), conv,
        P(ug_chunk, 3), P(S, 4), P(S, 5), P(q_chunk, 0), P(S, 6), P(S, 7),
        P(q_chunk, 1), P(rms_piece, 0), P(rms_piece, 1), P(S, 8),
        P(kv_chunk, 0), P(S, 9), P(S, 10), P(S, 11), P(rms_piece, 2),
        P(kv_chunk, 1), P(S, 12), P(S, 13), P(S, 14), P(rms_piece, 3),
        P(gates_chunk, 0), P(S, 15), P(gates_chunk, 1), P(gates_chunk, 2), P(gates_chunk, 3),
    ]
    for piece in program:
        piece()
    h_last = state[0]
    hc_ref[...] = h_last

    @pl.when(jnp.logical_and(dot_live, dot_tile % per == per - 1))
    def _():
        cn_ref[0] = ext_ref[SUBLANES - (CONV_WIDTH - 1):SUBLANES, :]

    @pl.when(jnp.logical_and(scan_live, scan_tile % per == per - 1))
    def _():
        hl_ref[0] = h_last[0:1, :]


def _proj_lru(x, g, w_in, wkvt, conv_state, h0, lru_w):
    s, t, _ = x.shape
    n = s * t
    tm = TOKEN_TILE
    assert t % tm == 0 and tm % KEY_BLOCK == 0
    per = t // tm
    n_tiles = n // tm
    tile = lambda i, lag: jnp.clip(i - lag, 0, n_tiles - 1)
    tok_in = pl.BlockSpec((tm, D_MODEL), lambda i: (tile(i, 0), 0))
    q_out = pl.BlockSpec((tm, D_ATTN), lambda i: (tile(i, 1), 0))
    hg_out = pl.BlockSpec((tm, D_RNN), lambda i: (tile(i, 2), 0))
    b16o = jax.ShapeDtypeStruct((n, D_ATTN), BF16)
    kv_spec = pl.BlockSpec((1, D_ATTN, tm), lambda i: (tile(i, 1) // per, 0, tile(i, 1) % per))
    kvb_spec = pl.BlockSpec((1, tm // KEY_BLOCK, D_ATTN, KEY_BLOCK),
                            lambda i: (tile(i, 1) // per, tile(i, 1) % per, 0, 0))
    kv_shape = jax.ShapeDtypeStruct((s, D_ATTN, t), F32)
    kvb_shape = jax.ShapeDtypeStruct((s, t // KEY_BLOCK, D_ATTN, KEY_BLOCK), BF16)
    cs_spec, _ = _lru_state_specs(lambda i: (tile(i, 1) // per, 0, 0))
    _, h_spec = _lru_state_specs(lambda i: (tile(i, 2) // per, 0, 0))
    w_cols = (COL_Q, COL_U, COL_GR)
    return pl.pallas_call(
        functools.partial(_proj_lru_kernel, per, n_tiles),
        grid=(n_tiles + 2,),
        in_specs=[tok_in, _const_spec(g.shape)] + [_w_in_cols(j) for j in w_cols] + [_const_spec(wkvt.shape)]
        + [cs_spec, h_spec] + [_const_spec(w.shape) for w in lru_w],
        out_specs=[q_out, kv_spec, kv_spec, kvb_spec, kvb_spec, hg_out, cs_spec, h_spec],
        out_shape=[b16o, kv_shape, kv_shape, kvb_shape, kvb_shape, b16o,
                   jax.ShapeDtypeStruct((s, CONV_WIDTH - 1, D_RNN), F32),
                   jax.ShapeDtypeStruct((s, 1, D_RNN), F32)],
        scratch_shapes=_lru_scratch(tm) + [pltpu.VMEM((2, tm, D_MODEL), BF16), pltpu.VMEM((2, tm, D_RNN), F32),
                                           pltpu.VMEM((2, tm, D_RNN), F32), pltpu.VMEM((2, tm, 2 * D_RNN), F32)],
        compiler_params=pltpu.CompilerParams(dimension_semantics=("arbitrary",),
                                             vmem_limit_bytes=VMEM_LIMIT),
        name="proj_lru",
    )(x.reshape(n, D_MODEL), g, *([w_in] * len(w_cols)), wkvt, conv_state, h0, *lru_w)


def _sb_split(z, mask):
    n = z.shape[1] // KEY_BLOCK
    tq = z.shape[0] // 2
    sp = _softplus(z)
    cats = []
    for h in range(2):
        for d in range(n):
            blk = sp[h * tq:(h + 1) * tq, d * KEY_BLOCK:(d + 1) * KEY_BLOCK]
            if d == n - 1 and mask is not None:
                blk = jnp.where(mask, blk, 0.0)
            cats.append(_hi_lo(blk))
    return jnp.concatenate(cats, axis=0)


def _hi_lo(blk):
    hi = blk.astype(BF16)
    lo = (blk - hi.astype(F32)).astype(BF16)
    return jnp.concatenate([hi, lo], axis=1)


def _sb_cumsum(cat, tri2):
    return jnp.dot(cat, tri2, preferred_element_type=F32)


def _sb_weights(z, r, carries, mask):
    n = z.shape[1] // KEY_BLOCK
    tq = z.shape[0] // 2
    rows, new_carries = [], []
    for h in range(2):
        carry = None if carries is None else carries[h]
        es = [None] * n
        for d in reversed(range(n)):
            rd = r[(h * n + d) * tq:(h * n + d + 1) * tq]
            x = z[h * tq:(h + 1) * tq, d * KEY_BLOCK:(d + 1) * KEY_BLOCK] + rd[:, :KEY_BLOCK]
            if carry is not None:
                x = x + carry
            e = jnp.exp(x)
            if d == n - 1 and mask is not None:
                e = jnp.where(mask, e, 0.0)
            es[d] = e.astype(BF16)
            tot = rd[:, KEY_BLOCK:]
            carry = tot if carry is None else carry + tot
        rows.append(es[0] if n == 1 else jnp.concatenate(es, axis=1))
        new_carries.append(carry)
    return jnp.concatenate(rows, axis=0), new_carries


def _tri2():
    j = np.arange(KEY_BLOCK)[:, None]
    s = np.arange(KEY_BLOCK)[None, :]
    half = np.concatenate([-(j >= s).astype(np.float32), -np.ones((KEY_BLOCK, LANES), np.float32)], axis=1)
    return jnp.asarray(np.concatenate([half, half], axis=0), dtype=BF16)


def _stack_heads(q, lo_half):
    zero = jnp.zeros_like(q)
    return jnp.concatenate([jnp.where(lo_half, q, zero), jnp.where(lo_half, zero, q)], axis=0)


def _sb_window(z_fn, pv_fn, carries, mask, tri2):
    z = z_fn()
    e, cs = _sb_weights(z, _sb_cumsum(_sb_split(z, mask), tri2), carries, mask)
    pv = pv_fn(e)
    tq = pv.shape[0] // 2
    return (pv[:tq], cs[0]), (pv[tq:], cs[1])


def _finish(store, j0, outs, step_fn):
    (a0, c0), (a1, c1) = outs
    store(a0, a1)
    if isinstance(j0, int) and j0 < 0:
        return lambda: None

    def alive(x0, x1):
        return jnp.max(jnp.maximum(x0, x1)) >= LOG_KEEP_FLOOR

    go = jnp.logical_and(j0 >= 0, alive(c0, c1))

    def walk():
        @pl.when(go)
        def _():
            def body(s):
                j, b0, b1, d0, d1, _ = s
                (p0, d0), (p1, d1) = step_fn(j, [d0, d1])
                return j - 1, b0 + p0, b1 + p1, d0, d1, jnp.logical_and(j >= 1, alive(d0, d1))

            _, b0, b1, _, _, _ = lax.while_loop(lambda s: s[-1], body,
                                                (jnp.int32(j0), a0, a1, c0, c1, jnp.bool_(True)))
            store(b0, b1)

    return walk


def _sb_chains(chains, tri2):
    zs = [c[0]() for c in chains]
    sums = [_sb_cumsum(_sb_split(z, c[2]), tri2) for z, c in zip(zs, chains)]
    weights = [_sb_weights(z, r, None, c[2]) for z, r, c in zip(zs, sums, chains)]
    walks = []
    for c, (e, cs) in zip(chains, weights):
        pv = c[1](e)
        tq = pv.shape[0] // 2
        walks.append(_finish(c[3], c[5], ((pv[:tq], cs[0]), (pv[tq:], cs[1])), c[4]))
    for walk in walks:
        walk()


def _attn_prompt_kernel(q_ref, kt_ref, vt_ref, tri_ref, o_ref):
    t = q_ref.shape[1]
    nq = t // Q_ROWS
    row = lax.broadcasted_iota(jnp.int32, (Q_ROWS, LANES), 0)
    lane = lax.broadcasted_iota(jnp.int32, (Q_ROWS, LANES), 1)
    lo_half = lane < HEAD_DIM
    diag_mask = lane < row
    lane_minus_row = lane - row
    tri2 = tri_ref[...]

    def window(ref, first, n):
        return jnp.concatenate([ref[0, first + d] for d in range(n)], axis=1)

    def qblocks(blocks, n_fast):
        chains = []
        for i in blocks:
            q2 = _stack_heads(q_ref[0, _rows(i, Q_ROWS), :], lo_half)
            first = i + 1 - n_fast

            def z_fn(q2=q2, first=first):
                return jnp.dot(q2, window(kt_ref, first, n_fast), preferred_element_type=F32)

            def pv_fn(e, first=first):
                return lax.dot_general(e, window(vt_ref, first, n_fast), NT_DIMS, preferred_element_type=F32)

            def store(b0, b1, i=i):
                o_ref[_rows(i, Q_ROWS), :] = jnp.where(lo_half, b0, b1).astype(BF16)

            def step(j, cs, i=i, q2=q2):
                m = lane_minus_row < (i - j) * KEY_BLOCK
                return _sb_window(lambda: jnp.dot(q2, kt_ref[0, j], preferred_element_type=F32),
                                  lambda e: lax.dot_general(e, vt_ref[0, j], NT_DIMS, preferred_element_type=F32),
                                  cs, m, tri2)

            chains.append((z_fn, pv_fn, diag_mask, store, step, i - n_fast))
        _sb_chains(chains, tri2)

    half = Q_ROWS // 2
    bottom = row >= half

    def qblocks_tapered(blocks):
        assert FAST_BLOCKS == 3
        chains = []
        for i in blocks:
            q2 = _stack_heads(q_ref[0, _rows(i, Q_ROWS), :], lo_half)
            q2_top = jnp.concatenate([q2[:half], q2[Q_ROWS:Q_ROWS + half]], axis=0)
            z_new = jnp.dot(q2, window(kt_ref, i - 1, 2), preferred_element_type=F32)
            z_old = jnp.dot(q2_top, kt_ref[0, i - 2], preferred_element_type=F32)
            chains.append([i, q2, z_new, z_old])
        for c in chains:
            _, _, z_new, z_old = c
            sp_new, sp_old = _softplus(z_new), _softplus(z_old)
            cats = []
            for h in range(2):
                rows = slice(h * Q_ROWS, (h + 1) * Q_ROWS)
                cats.append(_hi_lo(sp_new[rows, :KEY_BLOCK]))
                cats.append(_hi_lo(jnp.where(diag_mask, sp_new[rows, KEY_BLOCK:], 0.0)))
            for h in range(2):
                cats.append(_hi_lo(sp_old[h * half:(h + 1) * half]))
            c.append(_sb_cumsum(jnp.concatenate(cats, axis=0), tri2))
        walks = []
        for i, q2, z_new, z_old, r in chains:
            es, carries = [], []
            for h in range(2):
                rows = slice(h * Q_ROWS, (h + 1) * Q_ROWS)
                r_prev = r[2 * h * Q_ROWS:(2 * h + 1) * Q_ROWS]
                r_diag = r[(2 * h + 1) * Q_ROWS:(2 * h + 2) * Q_ROWS]
                r_old = r[4 * Q_ROWS + h * half:4 * Q_ROWS + (h + 1) * half]
                e_diag = jnp.where(diag_mask, jnp.exp(z_new[rows, KEY_BLOCK:] + r_diag[:, :KEY_BLOCK]), 0.0)
                carry = r_diag[:, KEY_BLOCK:]
                e_prev = jnp.exp(z_new[rows, :KEY_BLOCK] + r_prev[:, :KEY_BLOCK] + carry)
                carry = carry + r_prev[:, KEY_BLOCK:]
                e_old = jnp.exp(z_old[h * half:(h + 1) * half] + r_old[:, :KEY_BLOCK] + carry[:half])
                carries.append(jnp.concatenate([carry[:half] + r_old[:, KEY_BLOCK:], carry[half:]], axis=0))
                e_old = jnp.concatenate([e_old.astype(BF16), jnp.zeros((half, KEY_BLOCK), BF16)], axis=0)
                es.append(jnp.concatenate([e_old, e_prev.astype(BF16), e_diag.astype(BF16)], axis=1))
            pv = lax.dot_general(jnp.concatenate(es, axis=0), window(vt_ref, i - 2, FAST_BLOCKS), NT_DIMS,
                                 preferred_element_type=F32)

            def store(b0, b1, i=i):
                o_ref[_rows(i, Q_ROWS), :] = jnp.where(lo_half, b0, b1).astype(BF16)

            def step(j, cs, i=i, q2=q2):
                m = jnp.logical_and(lane_minus_row < (i - j) * KEY_BLOCK, jnp.logical_or(bottom, j < i - 2))
                return _sb_window(lambda: jnp.dot(q2, kt_ref[0, j], preferred_element_type=F32),
                                  lambda e: lax.dot_general(e, vt_ref[0, j], NT_DIMS, preferred_element_type=F32),
                                  cs, m, tri2)

            walks.append(_finish(store, i - 2, ((pv[:Q_ROWS], carries[0]), (pv[Q_ROWS:], carries[1])), step))
        for walk in walks:
            walk()

    n_head = FAST_BLOCKS - 1
    for i in range(n_head):
        qblocks([i], i + 1)

    start = n_head + (nq - n_head) % Q_GROUP
    if start > n_head:
        qblocks_tapered(list(range(n_head, start)))

    def group(g, carry):
        qblocks_tapered([start + g * Q_GROUP + u for u in range(Q_GROUP)])
        return carry

    lax.fori_loop(0, (nq - start) // Q_GROUP, group, 0)


def _attn_prompt(qb, ktb, vtb, tri2):
    b, t, _ = qb.shape
    assert t % Q_ROWS == 0 and Q_ROWS == KEY_BLOCK
    qspec = pl.BlockSpec((1, t, LANES), lambda i, h: (i, 0, h))
    kspec = pl.BlockSpec((1, t // KEY_BLOCK, LANES, KEY_BLOCK), lambda i, h: (i, 0, h, 0))
    return pl.pallas_call(
        _attn_prompt_kernel,
        grid=(b, D_ATTN // LANES),
        in_specs=[qspec, kspec, kspec, _const_spec(tri2.shape)],
        out_specs=pl.BlockSpec((t, LANES), lambda i, h: (i, h)),
        out_shape=jax.ShapeDtypeStruct((b * t, D_ATTN), BF16),
        compiler_params=pltpu.CompilerParams(dimension_semantics=("arbitrary", "arbitrary"),
                                             vmem_limit_bytes=VMEM_LIMIT),
        name="attn_prompt",
    )(qb, ktb, vtb, tri2)


def _attn_sample_kernel(n_tail, q_ref, kn_ref, vn_ref, ckt_ref, cvt_ref, ck_hbm, cv_hbm, tri_ref, o_ref,
                        kbuf, vbuf):
    n_streams = q_ref.shape[0]
    tq = q_ref.shape[1]
    n_past = ck_hbm.shape[2] // KEY_BLOCK
    row = lax.broadcasted_iota(jnp.int32, (tq, LANES), 0)
    lane = lax.broadcasted_iota(jnp.int32, (tq, LANES), 1)
    lo_half = lane < HEAD_DIM
    tri2 = tri_ref[...]

    pad = jnp.zeros((KEY_BLOCK - tq, LANES), BF16)
    chains = []
    for b in range(n_streams):
        s = pl.program_id(0) * n_streams + b
        for p in range(q_ref.shape[2] // LANES):
            cols = slice(p * LANES, (p + 1) * LANES)
            q2 = _stack_heads(q_ref[b, :, cols], lo_half)

            def z_fn(q2=q2, cols=cols, b=b):
                z_new = lax.dot_general(q2, jnp.concatenate([kn_ref[b, :, cols], pad], axis=0), NT_DIMS,
                                        preferred_element_type=F32)
                if n_tail == 0:
                    return z_new
                z_past = jnp.dot(q2, ckt_ref[b, cols, :].astype(BF16), preferred_element_type=F32)
                return jnp.concatenate([z_past, z_new], axis=1)

            def pv_fn(e, cols=cols, b=b):
                pv = jnp.dot(e[:, n_tail * KEY_BLOCK:], jnp.concatenate([vn_ref[b, :, cols], pad], axis=0),
                             preferred_element_type=F32)
                if n_tail == 0:
                    return pv
                return pv + lax.dot_general(e[:, :n_tail * KEY_BLOCK], cvt_ref[b, cols, :].astype(BF16), NT_DIMS,
                                            preferred_element_type=F32)

            def store(a0, a1, cols=cols, b=b):
                o_ref[b * tq:(b + 1) * tq, cols] = jnp.where(lo_half, a0, a1).astype(BF16)

            def step(j, cs, q2=q2, p=p, s=s):
                keys = pl.ds(pl.multiple_of(j * KEY_BLOCK, KEY_BLOCK), KEY_BLOCK)
                pltpu.sync_copy(ck_hbm.at[s, pl.ds(p * LANES, LANES), keys], kbuf)
                pltpu.sync_copy(cv_hbm.at[s, pl.ds(p * LANES, LANES), keys], vbuf)
                return _sb_window(lambda: jnp.dot(q2, kbuf[...].astype(BF16), preferred_element_type=F32),
                                  lambda e: lax.dot_general(e, vbuf[...].astype(BF16), NT_DIMS,
                                                            preferred_element_type=F32),
                                  cs, None, tri2)

            chains.append((z_fn, pv_fn, lane < row, store, step, n_past - 1 - n_tail))
    _sb_chains(chains, tri2)


def _attn_sample(qb, knb, vnb, cache_kt, cache_vt, tri2):
    s, tq, _ = qb.shape
    past_len = cache_kt.shape[2]
    assert past_len % KEY_BLOCK == 0 and tq <= KEY_BLOCK and tq % 16 == 0
    n_tail = min(FAST_BLOCKS - 1, past_len // KEY_BLOCK)
    assert n_tail > 0 and past_len % (n_tail * KEY_BLOCK) == 0
    per = SAMPLE_STREAMS if s % SAMPLE_STREAMS == 0 else 1
    new = pl.BlockSpec((per, tq, D_ATTN), lambda i: (i, 0, 0))
    tail = pl.BlockSpec((per, D_ATTN, n_tail * KEY_BLOCK), lambda i: (i, 0, past_len // (n_tail * KEY_BLOCK) - 1))
    hbm = pl.BlockSpec(memory_space=pl.ANY)
    return pl.pallas_call(
        functools.partial(_attn_sample_kernel, n_tail),
        grid=(s // per,),
        in_specs=[new, new, new, tail, tail, hbm, hbm, _const_spec(tri2.shape)],
        out_specs=pl.BlockSpec((per * tq, D_ATTN), lambda i: (i, 0)),
        out_shape=jax.ShapeDtypeStruct((s * tq, D_ATTN), BF16),
        scratch_shapes=[pltpu.VMEM((LANES, KEY_BLOCK), F32), pltpu.VMEM((LANES, KEY_BLOCK), F32)],
        compiler_params=pltpu.CompilerParams(dimension_semantics=("arbitrary",),
                                             vmem_limit_bytes=VMEM_LIMIT),
        name="attn_sample",
    )(qb, knb, vnb, cache_kt, cache_vt, cache_kt, cache_vt, tri2)


FF_CHUNK = 1024


def _out_kernel(n_first, xa_ref, xb_ref, oa_ref, ob_ref, hga_ref, hgb_ref, g1_ref, wga0_ref, wga1_ref, wgb0_ref,
                wgb1_ref, wa_ref, wb_ref, wo_ref, g2_ref, g3_ref, wup_ref, wdn_ref, g4_ref, ya_ref, yb_ref):
    first = pl.program_id(0) < n_first
    tm = xa_ref.shape[0]
    halves = [pl.ds(0, tm // 2), pl.ds(tm // 2, tm // 2)]
    n_ff = D_FF // FF_CHUNK
    st = [{}, {}]

    def pick(a_ref, b_ref, h):
        return jnp.where(first, a_ref[halves[h], :], b_ref[halves[h], :])

    def norm_in(h):
        st[h]["x"] = pick(xa_ref, xb_ref, h)
        st[h]["xn"] = _rms(st[h]["x"], g1_ref[...]).astype(BF16)

    def gate(h, w0_ref, w1_ref):
        return jnp.concatenate([jnp.dot(st[h]["xn"], w_ref[...], preferred_element_type=F32)
                                for w_ref in (w0_ref, w1_ref)], axis=1)

    def attn_gate(h):
        g_a = gate(h, wga0_ref, wga1_ref)
        y_a = jnp.dot(pick(oa_ref, ob_ref, h), wa_ref[...], preferred_element_type=F32)
        st[h]["m"] = jax.nn.sigmoid(g_a) * y_a

    def rnn_gate(h):
        g_b = gate(h, wgb0_ref, wgb1_ref)
        y_b = jnp.dot(pick(hga_ref, hgb_ref, h), wb_ref[...], preferred_element_type=F32)
        st[h]["m"] = st[h]["m"] + jax.nn.sigmoid(g_b) * y_b

    def mix(h):
        st[h]["mix"] = jnp.dot(st[h]["m"].astype(BF16), wo_ref[...], preferred_element_type=F32)

    def norm_mid(h):
        st[h]["x1"] = st[h]["x"] + _rms(st[h]["mix"], g2_ref[...])
        st[h]["f"] = _rms(st[h]["x1"], g3_ref[...]).astype(BF16)

    def ffn(h, c):
        cols = slice(c * FF_CHUNK, (c + 1) * FF_CHUNK)
        up = jnp.maximum(jnp.dot(st[h]["f"], wup_ref[:, cols], preferred_element_type=F32), 0.0)
        dn = jnp.dot((up * up).astype(BF16), wdn_ref[cols, :], preferred_element_type=F32)
        st[h]["acc"] = dn if c == 0 else st[h]["acc"] + dn

    def norm_out(h):
        yb_ref[halves[h], :] = st[h]["x1"] + _rms(st[h]["acc"], g4_ref[...])

    norm_in(0)
    norm_in(1)
    attn_gate(0)
    rnn_gate(0)
    mix(0)
    attn_gate(1)
    norm_mid(0)
    rnn_gate(1)
    mix(1)
    ffn(0, 0)
    norm_mid(1)
    for c in range(1, n_ff):
        ffn(0, c)
    ffn(1, 0)
    norm_out(0)
    for c in range(1, n_ff):
        ffn(1, c)
    norm_out(1)

    @pl.when(first)
    def _():
        ya_ref[...] = yb_ref[...]


def _out(xa, xb, oa, ob, hga, hgb, g1, w_in, wa, wb, wo, g2, g3, wup, wdn, g4):
    na, nb = xa.shape[0], xb.shape[0]
    tm = TOKEN_TILE
    assert na % tm == 0 and nb % tm == 0
    n_first = na // tm
    a_tok = lambda i: (jnp.minimum(i, n_first - 1), 0)
    b_tok = lambda i: (jnp.maximum(i - n_first, 0), 0)
    w_cols = (COL_GA, COL_GA + 1, COL_GB, COL_GB + 1)
    consts = (wa, wb, wo, g2, g3, wup, wdn, g4)
    return pl.pallas_call(
        functools.partial(_out_kernel, n_first),
        grid=((na + nb) // tm,),
        in_specs=[pl.BlockSpec((tm, D_MODEL), a_tok), pl.BlockSpec((tm, D_MODEL), b_tok),
                  pl.BlockSpec((tm, D_ATTN), a_tok), pl.BlockSpec((tm, D_ATTN), b_tok),
                  pl.BlockSpec((tm, D_RNN), a_tok), pl.BlockSpec((tm, D_RNN), b_tok), _const_spec(g1.shape)]
        + [_w_in_cols(j) for j in w_cols] + [_const_spec(c.shape) for c in consts],
        out_specs=[pl.BlockSpec((tm, D_MODEL), a_tok), pl.BlockSpec((tm, D_MODEL), b_tok)],
        out_shape=[jax.ShapeDtypeStruct((na, D_MODEL), F32), jax.ShapeDtypeStruct((nb, D_MODEL), F32)],
        compiler_params=pltpu.CompilerParams(dimension_semantics=("arbitrary",),
                                             vmem_limit_bytes=VMEM_LIMIT),
        name="out",
    )(xa, xb, oa, ob, hga, hgb, g1, *([w_in] * len(w_cols)), *consts)


def _block_diag(w):
    n, c, d = w.shape
    return jnp.einsum("ncd,nm->ncmd", w, jnp.eye(n, dtype=w.dtype)).reshape(n * c, n * d)


def _heads_last(kt):
    s, _, t = kt.shape
    return kt.reshape(s, N_HEADS, HEAD_DIM, t).transpose(0, 3, 1, 2)[None]


def _time_last(cache):
    s, t = cache.shape[:2]
    return cache.transpose(0, 2, 3, 1).reshape(s, D_ATTN, t)


def kernel(x_prompt, x_sample, cache_k, cache_v, state_conv, state_h, w_in, g_pre_mix, w_conv, b_conv, w_r, b_r, w_i, b_i, lam, w_a_out, w_b_out, w_o, g_post_mix, g_pre_ffn, w_up, w_down, g_post_ffn):
    assert w_in.shape[0] == 1
    row = lambda a: a[0].reshape(1, -1)
    assert w_in.shape[2] == (COL_GB + 2) * D_ATTN
    w_in_b = w_in[0].astype(BF16)
    wkv_t = w_in[0][:, COL_K * D_ATTN:(COL_V + 1) * D_ATTN].T.astype(BF16)
    wg = jnp.concatenate([_block_diag(w_r[0]), _block_diag(w_i[0])], axis=1).astype(BF16)
    bg = jnp.concatenate([row(b_r), row(b_i)], axis=1)
    lru_w = (w_conv[0], row(b_conv), wg, bg, row(lam))
    g1 = row(g_pre_mix)
    tri2 = _tri2()

    bp, tp, _ = x_prompt.shape
    bs, ts, _ = x_sample.shape
    n_p, n_s = bp * tp, bs * ts

    zc = jnp.zeros((bp, CONV_WIDTH - 1, D_RNN), F32)
    zh = jnp.zeros((bp, 1, D_RNN), F32)
    qb, kt, vt, ktb, vtb, hg_p, cp, hp = _proj_lru(x_prompt, g1, w_in_b, wkv_t, zc, zh, lru_w)
    o_p = _attn_prompt(qb.reshape(bp, tp, D_ATTN), ktb, vtb, tri2)

    s3 = lambda a: a.reshape(bs, ts, a.shape[-1])
    qs, ks, vs, ksb, vsb, hg_s, cs, hs = _proj(x_sample, g1, w_in_b, state_conv[0],
                                               state_h[0].reshape(bs, 1, D_RNN), lru_w)
    o_s = _attn_sample(s3(qs), s3(ksb), s3(vsb), _time_last(cache_k[0]), _time_last(cache_v[0]), tri2)

    ys, yp = _out(x_sample.reshape(n_s, D_MODEL), x_prompt.reshape(n_p, D_MODEL), o_s, o_p, hg_s, hg_p, g1, w_in_b,
                  w_a_out[0].astype(BF16), w_b_out[0].astype(BF16), w_o[0].astype(BF16), row(g_post_mix),
                  row(g_pre_ffn), w_up[0].astype(BF16), w_down[0].astype(BF16), row(g_post_ffn))
    return (yp.reshape(bp, tp, D_MODEL), ys.reshape(bs, ts, D_MODEL), _heads_last(kt), _heads_last(vt),
            cp[None], hp.reshape(1, bp, D_RNN), ks.reshape(1, bs, ts, N_HEADS, HEAD_DIM),
            vs.reshape(1, bs, ts, N_HEADS, HEAD_DIM), cs[None], hs.reshape(1, bs, D_RNN))
```

```python
import functools

import jax
import jax.numpy as jnp
import numpy as np
from jax import lax
from jax.experimental import pallas as pl
from jax.experimental.pallas import tpu as pltpu

F32 = jnp.float32
BF16 = jnp.bfloat16

D_MODEL = 1024
N_HEADS = 8
HEAD_DIM = 64
D_ATTN = N_HEADS * HEAD_DIM
D_RNN = 512
N_RNN_BLOCKS = 8
CONV_WIDTH = 4
LRU_C = 8.0
D_FF = 4 * D_MODEL
EPS = 1e-6

LANES = 128
SUBLANES = 8
KEY_BLOCK = 128
Q_ROWS = 128
FAST_BLOCKS = 3
Q_GROUP = 10
SAMPLE_STREAMS = 4
TOKEN_TILE = 512
SCAN_SLICES = 16
NORM_PIECES = 4
MXU_N = 256
LOG_KEEP_FLOOR = -104.0
VMEM_LIMIT = 56 * 1024 * 1024

NT_DIMS = (((1,), (1,)), ((), ()))
LOG2_E = 1.4426950408889634


def _rms(xf, g):
    return xf * lax.rsqrt(jnp.mean(xf * xf, axis=-1, keepdims=True) + EPS) * g


def _softplus(x):
    return jnp.maximum(x, 0.0) + jnp.log(1.0 + jnp.exp2(jnp.abs(x) * -LOG2_E))


def _sigmoid(x):
    return 0.5 * jnp.tanh(0.5 * x) + 0.5


def _const_spec(shape):
    nd = len(shape)
    return pl.BlockSpec(shape, lambda *_: (0,) * nd, pipeline_mode=pl.Buffered(1))


COL_Q, COL_K, COL_V, COL_U, COL_GR, COL_GA, COL_GB = 0, 1, 2, 3, 4, 5, 7
Q_SCALE = HEAD_DIM ** -0.5


def _w_in_cols(j):
    return pl.BlockSpec((D_MODEL, D_ATTN), lambda *_: (0, j), pipeline_mode=pl.Buffered(1))


def _rows(block_index, size):
    if isinstance(block_index, int):
        return pl.ds(block_index * size, size)
    return pl.ds(pl.multiple_of(block_index * size, size), size)


def _gelu_tanh(x):
    return 0.5 * x * (1.0 + jnp.tanh(np.sqrt(2.0 / np.pi).astype(np.float32) * (x + 0.044715 * (x * x * x))))


def _conv_seed(first, cs_ref, ext_ref):
    @pl.when(first)
    def _():
        ext_ref[0:SUBLANES, :] = jnp.zeros((SUBLANES, D_RNN), F32)
        ext_ref[SUBLANES - (CONV_WIDTH - 1):SUBLANES, :] = cs_ref[0]


def _scan_seed(first, h0_ref, hc_ref):
    @pl.when(first)
    def _():
        hc_ref[...] = jnp.broadcast_to(h0_ref[0], hc_ref.shape)


def _lru_conv(u, wc_ref, bc_ref, ext_ref):
    tt = u.shape[0]
    ext_ref[SUBLANES:SUBLANES + tt, :] = u
    uc = bc_ref[...] + wc_ref[CONV_WIDTH - 1:CONV_WIDTH, :] * u
    for d in range(1, CONV_WIDTH):
        uc = uc + wc_ref[CONV_WIDTH - 1 - d:CONV_WIDTH - d, :] * ext_ref[SUBLANES - d:SUBLANES - d + tt, :]
    ext_ref[0:SUBLANES, :] = ext_ref[tt:tt + SUBLANES, :]
    return uc


def _lru_gates(uc, wg_ref, bg_ref):
    return jnp.dot(uc.astype(BF16), wg_ref[...], preferred_element_type=F32) + bg_ref[...]


def _lru_scan(g, uc, g_rnn, lam_ref, h_prev):
    tt = uc.shape[0]
    r = _sigmoid(g[:, :D_RNN])
    ig = _sigmoid(g[:, D_RNN:])
    log_a = (-LRU_C * _softplus(-lam_ref[...])) * r
    a = jnp.exp(log_a)
    m2 = -jnp.tanh(log_a) * (a * a + 1.0)
    b = jnp.where(m2 > 0.0, m2 * lax.rsqrt(m2), 0.0) * (ig * uc)

    n_groups = tt // SUBLANES
    a3 = a.reshape(n_groups, SUBLANES, D_RNN)
    b3 = b.reshape(n_groups, SUBLANES, D_RNN)
    sub = lax.broadcasted_iota(jnp.int32, a3.shape, 1)
    k = 1
    while k < SUBLANES:
        has_prev = sub >= k
        b_prev = jnp.where(has_prev, pltpu.roll(b3, k, axis=1), 0.0)
        a_prev = jnp.where(has_prev, pltpu.roll(a3, k, axis=1), 1.0)
        b3 = b3 + a3 * b_prev
        a3 = a3 * a_prev
        k *= 2

    hs = []
    for gi in range(n_groups):
        h = b3[gi] + a3[gi] * h_prev
        hs.append(h)
        h_prev = jnp.broadcast_to(h[SUBLANES - 1:SUBLANES, :], h.shape)
    return (jnp.concatenate(hs, axis=0) * _gelu_tanh(g_rnn)).astype(BF16), h_prev


def _lru_state_specs(index_map):
    past = CONV_WIDTH - 1
    return (pl.BlockSpec((1, past, D_RNN), index_map), pl.BlockSpec((1, 1, D_RNN), index_map))


def _lru_scratch(tt):
    return [pltpu.VMEM((tt + SUBLANES, D_RNN), F32), pltpu.VMEM((SUBLANES, D_RNN), F32)]


def _proj_kernel(t, x_ref, g_ref, wq_ref, wk_ref, wv_ref, wu_ref, wgr_ref, cs_ref, h0_ref, wc_ref, bc_ref, wg_ref,
                 bg_ref, lam_ref, qb_ref, k_ref, v_ref, kb_ref, vb_ref, hg_ref, cn_ref, hl_ref, ext_ref):
    n_streams = x_ref.shape[0] // t
    past = CONV_WIDTH - 1
    xn = _rms(x_ref[...], g_ref[...]).astype(BF16)
    proj = lambda w_ref: jnp.dot(xn, w_ref[...], preferred_element_type=F32)
    u, g_rnn = proj(wu_ref), proj(wgr_ref)
    qb_ref[...] = (proj(wq_ref) * Q_SCALE).astype(BF16)
    k, v = proj(wk_ref), proj(wv_ref)
    k_ref[...] = k
    v_ref[...] = v
    kb_ref[...] = k.astype(BF16)
    vb_ref[...] = v.astype(BF16)

    ucs = []
    for s in range(n_streams):
        ext = ext_ref.at[s]
        ext[0:SUBLANES, :] = jnp.zeros((SUBLANES, D_RNN), F32)
        ext[SUBLANES - past:SUBLANES, :] = cs_ref[s]
        ucs.append(_lru_conv(u[s * t:(s + 1) * t], wc_ref, bc_ref, ext))
        cn_ref[s] = ext[SUBLANES - past:SUBLANES, :]
    uc = jnp.concatenate(ucs, axis=0)
    gates = _lru_gates(uc, wg_ref, bg_ref)
    for s in range(n_streams):
        rows = slice(s * t, (s + 1) * t)
        hg, h_last = _lru_scan(gates[rows], uc[rows], g_rnn[rows], lam_ref,
                               jnp.broadcast_to(h0_ref[s], (SUBLANES, D_RNN)))
        hg_ref[rows, :] = hg
        hl_ref[s] = h_last[0:1, :]


def _proj(x, g, w_in, conv_state, h0, lru_w):
    s, t, _ = x.shape
    n = s * t
    tm = TOKEN_TILE
    assert n % tm == 0 and tm % t == 0 and t % SUBLANES == 0
    per_tile = tm // t
    tok = lambda i: (i, 0)
    row_out = pl.BlockSpec((tm, D_ATTN), tok)
    f32o = jax.ShapeDtypeStruct((n, D_ATTN), F32)
    b16o = jax.ShapeDtypeStruct((n, D_ATTN), BF16)
    st3 = pl.BlockSpec((per_tile, CONV_WIDTH - 1, D_RNN), lambda i: (i, 0, 0))
    st1 = pl.BlockSpec((per_tile, 1, D_RNN), lambda i: (i, 0, 0))
    w_cols = (COL_Q, COL_K, COL_V, COL_U, COL_GR)
    return pl.pallas_call(
        functools.partial(_proj_kernel, t),
        grid=(n // tm,),
        in_specs=[pl.BlockSpec((tm, D_MODEL), tok), _const_spec(g.shape)] + [_w_in_cols(j) for j in w_cols]
        + [st3, st1] + [_const_spec(w.shape) for w in lru_w],
        out_specs=[row_out] * 6 + [st3, st1],
        out_shape=[b16o, f32o, f32o, b16o, b16o, b16o,
                   jax.ShapeDtypeStruct((s, CONV_WIDTH - 1, D_RNN), F32),
                   jax.ShapeDtypeStruct((s, 1, D_RNN), F32)],
        scratch_shapes=[pltpu.VMEM((per_tile, t + SUBLANES, D_RNN), F32)],
        compiler_params=pltpu.CompilerParams(dimension_semantics=("arbitrary",),
                                             vmem_limit_bytes=VMEM_LIMIT),
        name="proj",
    )(x.reshape(n, D_MODEL), g, *([w_in] * len(w_cols)), conv_state, h0, *lru_w)


def _proj_lru_kernel(per, n_tiles, x_ref, g_ref, wq_ref, wu_ref, wgr_ref, wkvt_ref, cs_ref, h0_ref, wc_ref, bc_ref,
                     wg_ref, bg_ref, lam_ref, qb_ref, kt_ref, vt_ref, ktb_ref, vtb_ref, hg_ref, cn_ref, hl_ref,
                     ext_ref, hc_ref, xn_ref, uc_ref, gr_ref, gt_ref):
    i = pl.program_id(0)
    dot_tile, scan_tile = i - 1, i - 2
    dot_live = jnp.logical_and(dot_tile >= 0, dot_tile < n_tiles)
    scan_live = jnp.logical_and(scan_tile >= 0, scan_tile < n_tiles)
    _conv_seed(jnp.logical_and(dot_live, dot_tile % per == 0), cs_ref, ext_ref)
    _scan_seed(jnp.logical_and(scan_live, scan_tile % per == 0), h0_ref, hc_ref)

    @pl.when(i == 0)
    def _():
        xn_ref[...] = jnp.zeros(xn_ref.shape, BF16)
        hc_ref[...] = jnp.zeros(hc_ref.shape, F32)
        ext_ref[...] = jnp.zeros(ext_ref.shape, F32)
        for ref in (uc_ref, gr_ref, gt_ref):
            ref[...] = jnp.zeros(ref.shape, F32)

    tm = x_ref.shape[0]
    sc, dt = i % 2, (i + 1) % 2
    xn = xn_ref[dt]
    rows_per = tm // SCAN_SLICES
    u_parts = {}
    uc_b = [None]

    def scan_slice(state, p):
        rows = pl.ds(p * rows_per, rows_per)
        hg, state[0] = _lru_scan(gt_ref[sc, rows, :], uc_ref[sc, rows, :], gr_ref[sc, rows, :], lam_ref, state[0])
        hg_ref[rows, :] = hg

    def scan_done(state):
        hc_ref[...] = state[0]

        @pl.when(jnp.logical_and(scan_live, scan_tile % per == per - 1))
        def _():
            hl_ref[0] = state[0][0:1, :]

    def rms_piece(r):
        rows = pl.ds(r * (tm // NORM_PIECES), tm // NORM_PIECES)
        xn_ref[sc, rows, :] = _rms(x_ref[rows, :], g_ref[...]).astype(BF16)

    def ug_chunk(c):
        is_u = c < D_RNN // MXU_N
        cols = slice(c * MXU_N % D_RNN, c * MXU_N % D_RNN + MXU_N)
        r = jnp.dot(xn, (wu_ref if is_u else wgr_ref)[:, cols], preferred_element_type=F32)
        if is_u:
            u_parts[c] = r
        else:
            gr_ref[dt, :, cols] = r

    def conv():
        uc = _lru_conv(jnp.concatenate([u_parts[c] for c in sorted(u_parts)], axis=1), wc_ref, bc_ref, ext_ref)
        uc_ref[dt] = uc
        uc_b[0] = uc.astype(BF16)

    def q_chunk(c):
        cols = slice(c * MXU_N, (c + 1) * MXU_N)
        qb_ref[:, cols] = (jnp.dot(xn, wq_ref[:, cols], preferred_element_type=F32) * Q_SCALE).astype(BF16)

    def kv_chunk(c):
        toks = slice(c * MXU_N, (c + 1) * MXU_N)
        kv = lax.dot_general(wkvt_ref[...], xn[toks], NT_DIMS, preferred_element_type=F32)
        k, v = kv[:D_ATTN], kv[D_ATTN:]
        kt_ref[0, :, toks] = k
        vt_ref[0, :, toks] = v
        for j in range(MXU_N // KEY_BLOCK):
            cols = slice(j * KEY_BLOCK, (j + 1) * KEY_BLOCK)
            ktb_ref[0, c * (MXU_N // KEY_BLOCK) + j] = k[:, cols].astype(BF16)
            vtb_ref[0, c * (MXU_N // KEY_BLOCK) + j] = v[:, cols].astype(BF16)

    def gates_chunk(c):
        cols = slice(c * MXU_N, (c + 1) * MXU_N)
        gt_ref[dt, :, cols] = jnp.dot(uc_b[0], wg_ref[:, cols], preferred_element_type=F32) + bg_ref[:, cols]

    @pl.when(i == 0)
    def _():
        for r in range(NORM_PIECES):
            rms_piece(r)

    @pl.when(i == n_tiles + 1)
    def _():
        state = [hc_ref[...]]
        for p in range(SCAN_SLICES):
            scan_slice(state, p)
        scan_done(state)

    @pl.when(jnp.logical_and(i > 0, i <= n_tiles))
    def _():
        state = [hc_ref[...]]
        S, P = functools.partial(scan_slice, state), functools.partial
        program = [
            P(ug_chunk, 0), P(S, 0), P(S, 1), P(ug_chunk, 1), P(S, 2), P(S, 3), P(ug_chunk, 2), conv,
            P(ug_chunk, 3), P(S, 4), P(S, 5), P(q_chunk, 0), P(S, 6), P(S, 7),
            P(q_chunk, 1), P(rms_piece, 0), P(rms_piece, 1), P(S, 8),
            P(kv_chunk, 0), P(S, 9), P(S, 10), P(S, 11), P(rms_piece, 2),
            P(kv_chunk, 1), P(S, 12), P(S, 13), P(S, 14), P(rms_piece, 3),
            P(gates_chunk, 0), P(S, 15), P(gates_chunk, 1), P(gates_chunk, 2), P(gates_chunk, 3),
        ]
        assert SCAN_SLICES == 16 and NORM_PIECES == 4
        for piece in program:
            piece()
        scan_done(state)

        @pl.when(jnp.logical_and(dot_live, dot_tile % per == per - 1))
        def _():
            cn_ref[0] = ext_ref[SUBLANES - (CONV_WIDTH - 1):SUBLANES, :]


def _proj_lru(x, g, w_in, wkvt, conv_state, h0, lru_w):
    s, t, _ = x.shape
    n = s * t
    tm = TOKEN_TILE
    assert t % tm == 0 and tm % KEY_BLOCK == 0
    per = t // tm
    n_tiles = n // tm
    tile = lambda i, lag: jnp.clip(i - lag, 0, n_tiles - 1)
    tok_in = pl.BlockSpec((tm, D_MODEL), lambda i: (tile(i, 0), 0))
    q_out = pl.BlockSpec((tm, D_ATTN), lambda i: (tile(i, 1), 0))
    hg_out = pl.BlockSpec((tm, D_RNN), lambda i: (tile(i, 2), 0))
    b16o = jax.ShapeDtypeStruct((n, D_ATTN), BF16)
    kv_spec = pl.BlockSpec((1, D_ATTN, tm), lambda i: (tile(i, 1) // per, 0, tile(i, 1) % per))
    kvb_spec = pl.BlockSpec((1, tm // KEY_BLOCK, D_ATTN, KEY_BLOCK),
                            lambda i: (tile(i, 1) // per, tile(i, 1) % per, 0, 0))
    kv_shape = jax.ShapeDtypeStruct((s, D_ATTN, t), F32)
    kvb_shape = jax.ShapeDtypeStruct((s, t // KEY_BLOCK, D_ATTN, KEY_BLOCK), BF16)
    cs_spec, _ = _lru_state_specs(lambda i: (tile(i, 1) // per, 0, 0))
    _, h_spec = _lru_state_specs(lambda i: (tile(i, 2) // per, 0, 0))
    w_cols = (COL_Q, COL_U, COL_GR)
    return pl.pallas_call(
        functools.partial(_proj_lru_kernel, per, n_tiles),
        grid=(n_tiles + 2,),
        in_specs=[tok_in, _const_spec(g.shape)] + [_w_in_cols(j) for j in w_cols] + [_const_spec(wkvt.shape)]
        + [cs_spec, h_spec] + [_const_spec(w.shape) for w in lru_w],
        out_specs=[q_out, kv_spec, kv_spec, kvb_spec, kvb_spec, hg_out, cs_spec, h_spec],
        out_shape=[b16o, kv_shape, kv_shape, kvb_shape, kvb_shape, b16o,
                   jax.ShapeDtypeStruct((s, CONV_WIDTH - 1, D_RNN), F32),
                   jax.ShapeDtypeStruct((s, 1, D_RNN), F32)],
        scratch_shapes=_lru_scratch(tm) + [pltpu.VMEM((2, tm, D_MODEL), BF16), pltpu.VMEM((2, tm, D_RNN), F32),
                                           pltpu.VMEM((2, tm, D_RNN), F32), pltpu.VMEM((2, tm, 2 * D_RNN), F32)],
        compiler_params=pltpu.CompilerParams(dimension_semantics=("arbitrary",),
                                             vmem_limit_bytes=VMEM_LIMIT),
        name="proj_lru",
    )(x.reshape(n, D_MODEL), g, *([w_in] * len(w_cols)), wkvt, conv_state, h0, *lru_w)


def _sb_split(z, mask):
    n = z.shape[1] // KEY_BLOCK
    tq = z.shape[0] // 2
    sp = _softplus(z)
    cats = []
    for h in range(2):
        for d in range(n):
            blk = sp[h * tq:(h + 1) * tq, d * KEY_BLOCK:(d + 1) * KEY_BLOCK]
            if d == n - 1 and mask is not None:
                blk = jnp.where(mask, blk, 0.0)
            cats.append(_hi_lo(blk))
    return jnp.concatenate(cats, axis=0)


def _hi_lo(blk):
    hi = blk.astype(BF16)
    lo = (blk - hi.astype(F32)).astype(BF16)
    return jnp.concatenate([hi, lo], axis=1)


def _sb_cumsum(cat, tri2):
    return jnp.dot(cat, tri2, preferred_element_type=F32)


def _sb_weights(z, r, carries, mask):
    n = z.shape[1] // KEY_BLOCK
    tq = z.shape[0] // 2
    rows, new_carries = [], []
    for h in range(2):
        carry = None if carries is None else carries[h]
        es = [None] * n
        for d in reversed(range(n)):
            rd = r[(h * n + d) * tq:(h * n + d + 1) * tq]
            x = z[h * tq:(h + 1) * tq, d * KEY_BLOCK:(d + 1) * KEY_BLOCK] + rd[:, :KEY_BLOCK]
            if carry is not None:
                x = x + carry
            e = jnp.exp(x)
            if d == n - 1 and mask is not None:
                e = jnp.where(mask, e, 0.0)
            es[d] = e.astype(BF16)
            tot = rd[:, KEY_BLOCK:]
            carry = tot if carry is None else carry + tot
        rows.append(es[0] if n == 1 else jnp.concatenate(es, axis=1))
        new_carries.append(carry)
    return jnp.concatenate(rows, axis=0), new_carries


def _tri2():
    j = np.arange(KEY_BLOCK)[:, None]
    s = np.arange(KEY_BLOCK)[None, :]
    half = np.concatenate([-(j >= s).astype(np.float32), -np.ones((KEY_BLOCK, LANES), np.float32)], axis=1)
    return jnp.asarray(np.concatenate([half, half], axis=0), dtype=BF16)


def _stack_heads(q, lo_half):
    zero = jnp.zeros_like(q)
    return jnp.concatenate([jnp.where(lo_half, q, zero), jnp.where(lo_half, zero, q)], axis=0)


def _sb_window(z_fn, pv_fn, carries, mask, tri2):
    z = z_fn()
    e, cs = _sb_weights(z, _sb_cumsum(_sb_split(z, mask), tri2), carries, mask)
    pv = pv_fn(e)
    tq = pv.shape[0] // 2
    return (pv[:tq], cs[0]), (pv[tq:], cs[1])


def _finish(store, j0, outs, step_fn):
    (a0, c0), (a1, c1) = outs
    store(a0, a1)
    if isinstance(j0, int) and j0 < 0:
        return lambda: None

    def alive(x0, x1):
        return jnp.max(jnp.maximum(x0, x1)) >= LOG_KEEP_FLOOR

    go = jnp.logical_and(j0 >= 0, alive(c0, c1))

    def walk():
        @pl.when(go)
        def _():
            def body(s):
                j, b0, b1, d0, d1, _ = s
                (p0, d0), (p1, d1) = step_fn(j, [d0, d1])
                return j - 1, b0 + p0, b1 + p1, d0, d1, jnp.logical_and(j >= 1, alive(d0, d1))

            _, b0, b1, _, _, _ = lax.while_loop(lambda s: s[-1], body,
                                                (jnp.int32(j0), a0, a1, c0, c1, jnp.bool_(True)))
            store(b0, b1)

    return walk


def _sb_chains(chains, tri2):
    zs = [c[0]() for c in chains]
    sums = [_sb_cumsum(_sb_split(z, c[2]), tri2) for z, c in zip(zs, chains)]
    weights = [_sb_weights(z, r, None, c[2]) for z, r, c in zip(zs, sums, chains)]
    walks = []
    for c, (e, cs) in zip(chains, weights):
        pv = c[1](e)
        tq = pv.shape[0] // 2
        walks.append(_finish(c[3], c[5], ((pv[:tq], cs[0]), (pv[tq:], cs[1])), c[4]))
    for walk in walks:
        walk()


def _attn_prompt_kernel(q_ref, kt_ref, vt_ref, tri_ref, o_ref):
    t = q_ref.shape[1]
    nq = t // Q_ROWS
    row = lax.broadcasted_iota(jnp.int32, (Q_ROWS, LANES), 0)
    lane = lax.broadcasted_iota(jnp.int32, (Q_ROWS, LANES), 1)
    lo_half = lane < HEAD_DIM
    diag_mask = lane < row
    lane_minus_row = lane - row
    tri2 = tri_ref[...]

    def window(ref, first, n):
        return jnp.concatenate([ref[0, first + d] for d in range(n)], axis=1)

    def qblocks(blocks, n_fast):
        chains = []
        for i in blocks:
            q2 = _stack_heads(q_ref[0, _rows(i, Q_ROWS), :], lo_half)
            first = i + 1 - n_fast

            def z_fn(q2=q2, first=first):
                return jnp.dot(q2, window(kt_ref, first, n_fast), preferred_element_type=F32)

            def pv_fn(e, first=first):
                return lax.dot_general(e, window(vt_ref, first, n_fast), NT_DIMS, preferred_element_type=F32)

            def store(b0, b1, i=i):
                o_ref[_rows(i, Q_ROWS), :] = jnp.where(lo_half, b0, b1).astype(BF16)

            def step(j, cs, i=i, q2=q2):
                m = lane_minus_row < (i - j) * KEY_BLOCK
                return _sb_window(lambda: jnp.dot(q2, kt_ref[0, j], preferred_element_type=F32),
                                  lambda e: lax.dot_general(e, vt_ref[0, j], NT_DIMS, preferred_element_type=F32),
                                  cs, m, tri2)

            chains.append((z_fn, pv_fn, diag_mask, store, step, i - n_fast))
        _sb_chains(chains, tri2)

    half = Q_ROWS // 2
    bottom = row >= half

    def qblocks_tapered(blocks):
        assert FAST_BLOCKS == 3
        chains = []
        for i in blocks:
            q2 = _stack_heads(q_ref[0, _rows(i, Q_ROWS), :], lo_half)
            q2_top = jnp.concatenate([q2[:half], q2[Q_ROWS:Q_ROWS + half]], axis=0)
            z_new = jnp.dot(q2, window(kt_ref, i - 1, 2), preferred_element_type=F32)
            z_old = jnp.dot(q2_top, kt_ref[0, i - 2], preferred_element_type=F32)
            chains.append([i, q2, z_new, z_old])
        for c in chains:
            _, _, z_new, z_old = c
            sp_new, sp_old = _softplus(z_new), _softplus(z_old)
            cats = []
            for h in range(2):
                rows = slice(h * Q_ROWS, (h + 1) * Q_ROWS)
                cats.append(_hi_lo(sp_new[rows, :KEY_BLOCK]))
                cats.append(_hi_lo(jnp.where(diag_mask, sp_new[rows, KEY_BLOCK:], 0.0)))
            for h in range(2):
                cats.append(_hi_lo(sp_old[h * half:(h + 1) * half]))
            c.append(_sb_cumsum(jnp.concatenate(cats, axis=0), tri2))
        walks = []
        for i, q2, z_new, z_old, r in chains:
            es, carries = [], []
            for h in range(2):
                rows = slice(h * Q_ROWS, (h + 1) * Q_ROWS)
                r_prev = r[2 * h * Q_ROWS:(2 * h + 1) * Q_ROWS]
                r_diag = r[(2 * h + 1) * Q_ROWS:(2 * h + 2) * Q_ROWS]
                r_old = r[4 * Q_ROWS + h * half:4 * Q_ROWS + (h + 1) * half]
                e_diag = jnp.where(diag_mask, jnp.exp(z_new[rows, KEY_BLOCK:] + r_diag[:, :KEY_BLOCK]), 0.0)
                carry = r_diag[:, KEY_BLOCK:]
                e_prev = jnp.exp(z_new[rows, :KEY_BLOCK] + r_prev[:, :KEY_BLOCK] + carry)
                carry = carry + r_prev[:, KEY_BLOCK:]
                e_old = jnp.exp(z_old[h * half:(h + 1) * half] + r_old[:, :KEY_BLOCK] + carry[:half])
                carries.append(jnp.concatenate([carry[:half] + r_old[:, KEY_BLOCK:], carry[half:]], axis=0))
                e_old = jnp.concatenate([e_old.astype(BF16), jnp.zeros((half, KEY_BLOCK), BF16)], axis=0)
                es.append(jnp.concatenate([e_old, e_prev.astype(BF16), e_diag.astype(BF16)], axis=1))
            pv = lax.dot_general(jnp.concatenate(es, axis=0), window(vt_ref, i - 2, FAST_BLOCKS), NT_DIMS,
                                 preferred_element_type=F32)

            def store(b0, b1, i=i):
                o_ref[_rows(i, Q_ROWS), :] = jnp.where(lo_half, b0, b1).astype(BF16)

            def step(j, cs, i=i, q2=q2):
                m = jnp.logical_and(lane_minus_row < (i - j) * KEY_BLOCK, jnp.logical_or(bottom, j < i - 2))
                return _sb_window(lambda: jnp.dot(q2, kt_ref[0, j], preferred_element_type=F32),
                                  lambda e: lax.dot_general(e, vt_ref[0, j], NT_DIMS, preferred_element_type=F32),
                                  cs, m, tri2)

            walks.append(_finish(store, i - 2, ((pv[:Q_ROWS], carries[0]), (pv[Q_ROWS:], carries[1])), step))
        for walk in walks:
            walk()

    n_head = FAST_BLOCKS - 1
    for i in range(n_head):
        qblocks([i], i + 1)

    start = n_head + (nq - n_head) % Q_GROUP
    if start > n_head:
        qblocks_tapered(list(range(n_head, start)))

    def group(g, carry):
        qblocks_tapered([start + g * Q_GROUP + u for u in range(Q_GROUP)])
        return carry

    lax.fori_loop(0, (nq - start) // Q_GROUP, group, 0)


def _attn_prompt(qb, ktb, vtb, tri2):
    b, t, _ = qb.shape
    assert t % Q_ROWS == 0 and Q_ROWS == KEY_BLOCK
    qspec = pl.BlockSpec((1, t, LANES), lambda i, h: (i, 0, h))
    kspec = pl.BlockSpec((1, t // KEY_BLOCK, LANES, KEY_BLOCK), lambda i, h: (i, 0, h, 0))
    return pl.pallas_call(
        _attn_prompt_kernel,
        grid=(b, D_ATTN // LANES),
        in_specs=[qspec, kspec, kspec, _const_spec(tri2.shape)],
        out_specs=pl.BlockSpec((t, LANES), lambda i, h: (i, h)),
        out_shape=jax.ShapeDtypeStruct((b * t, D_ATTN), BF16),
        compiler_params=pltpu.CompilerParams(dimension_semantics=("arbitrary", "arbitrary"),
                                             vmem_limit_bytes=VMEM_LIMIT),
        name="attn_prompt",
    )(qb, ktb, vtb, tri2)


def _attn_sample_kernel(n_tail, q_ref, kn_ref, vn_ref, ckt_ref, cvt_ref, ck_hbm, cv_hbm, tri_ref, o_ref,
                        kbuf, vbuf):
    n_streams = q_ref.shape[0]
    tq = q_ref.shape[1]
    n_past = ck_hbm.shape[2] // KEY_BLOCK
    row = lax.broadcasted_iota(jnp.int32, (tq, LANES), 0)
    lane = lax.broadcasted_iota(jnp.int32, (tq, LANES), 1)
    lo_half = lane < HEAD_DIM
    tri2 = tri_ref[...]

    pad = jnp.zeros((KEY_BLOCK - tq, LANES), BF16)
    chains = []
    for b in range(n_streams):
        s = pl.program_id(0) * n_streams + b
        for p in range(q_ref.shape[2] // LANES):
            cols = slice(p * LANES, (p + 1) * LANES)
            q2 = _stack_heads(q_ref[b, :, cols], lo_half)

            def z_fn(q2=q2, cols=cols, b=b):
                z_new = lax.dot_general(q2, jnp.concatenate([kn_ref[b, :, cols], pad], axis=0), NT_DIMS,
                                        preferred_element_type=F32)
                if n_tail == 0:
                    return z_new
                z_past = jnp.dot(q2, ckt_ref[b, cols, :].astype(BF16), preferred_element_type=F32)
                return jnp.concatenate([z_past, z_new], axis=1)

            def pv_fn(e, cols=cols, b=b):
                pv = jnp.dot(e[:, n_tail * KEY_BLOCK:], jnp.concatenate([vn_ref[b, :, cols], pad], axis=0),
                             preferred_element_type=F32)
                if n_tail == 0:
                    return pv
                return pv + lax.dot_general(e[:, :n_tail * KEY_BLOCK], cvt_ref[b, cols, :].astype(BF16), NT_DIMS,
                                            preferred_element_type=F32)

            def store(a0, a1, cols=cols, b=b):
                o_ref[b * tq:(b + 1) * tq, cols] = jnp.where(lo_half, a0, a1).astype(BF16)

            def step(j, cs, q2=q2, p=p, s=s):
                keys = pl.ds(pl.multiple_of(j * KEY_BLOCK, KEY_BLOCK), KEY_BLOCK)
                pltpu.sync_copy(ck_hbm.at[s, pl.ds(p * LANES, LANES), keys], kbuf)
                pltpu.sync_copy(cv_hbm.at[s, pl.ds(p * LANES, LANES), keys], vbuf)
                return _sb_window(lambda: jnp.dot(q2, kbuf[...].astype(BF16), preferred_element_type=F32),
                                  lambda e: lax.dot_general(e, vbuf[...].astype(BF16), NT_DIMS,
                                                            preferred_element_type=F32),
                                  cs, None, tri2)

            chains.append((z_fn, pv_fn, lane < row, store, step, n_past - 1 - n_tail))
    _sb_chains(chains, tri2)


def _attn_sample(qb, knb, vnb, cache_kt, cache_vt, tri2):
    s, tq, _ = qb.shape
    past_len = cache_kt.shape[2]
    assert past_len % KEY_BLOCK == 0 and tq <= KEY_BLOCK and tq % 16 == 0
    n_tail = min(FAST_BLOCKS - 1, past_len // KEY_BLOCK)
    assert n_tail > 0 and past_len % (n_tail * KEY_BLOCK) == 0
    per = SAMPLE_STREAMS if s % SAMPLE_STREAMS == 0 else 1
    new = pl.BlockSpec((per, tq, D_ATTN), lambda i: (i, 0, 0))
    tail = pl.BlockSpec((per, D_ATTN, n_tail * KEY_BLOCK), lambda i: (i, 0, past_len // (n_tail * KEY_BLOCK) - 1))
    hbm = pl.BlockSpec(memory_space=pl.ANY)
    return pl.pallas_call(
        functools.partial(_attn_sample_kernel, n_tail),
        grid=(s // per,),
        in_specs=[new, new, new, tail, tail, hbm, hbm, _const_spec(tri2.shape)],
        out_specs=pl.BlockSpec((per * tq, D_ATTN), lambda i: (i, 0)),
        out_shape=jax.ShapeDtypeStruct((s * tq, D_ATTN), BF16),
        scratch_shapes=[pltpu.VMEM((LANES, KEY_BLOCK), F32), pltpu.VMEM((LANES, KEY_BLOCK), F32)],
        compiler_params=pltpu.CompilerParams(dimension_semantics=("arbitrary",),
                                             vmem_limit_bytes=VMEM_LIMIT),
        name="attn_sample",
    )(qb, knb, vnb, cache_kt, cache_vt, cache_kt, cache_vt, tri2)


FF_CHUNK = 1024


def _out_kernel(n_first, xa_ref, xb_ref, oa_ref, ob_ref, hga_ref, hgb_ref, g1_ref, wga0_ref, wga1_ref, wgb0_ref,
                wgb1_ref, wa_ref, wb_ref, wo_ref, g2_ref, g3_ref, wup_ref, wdn_ref, g4_ref, ya_ref, yb_ref):
    first = pl.program_id(0) < n_first
    tm = xa_ref.shape[0]
    halves = [pl.ds(0, tm // 2), pl.ds(tm // 2, tm // 2)]
    n_ff = D_FF // FF_CHUNK
    st = [{}, {}]

    def pick(a_ref, b_ref, h):
        return jnp.where(first, a_ref[halves[h], :], b_ref[halves[h], :])

    def norm_in(h):
        st[h]["x"] = pick(xa_ref, xb_ref, h)
        st[h]["xn"] = _rms(st[h]["x"], g1_ref[...]).astype(BF16)

    def gate(h, w0_ref, w1_ref):
        return jnp.concatenate([jnp.dot(st[h]["xn"], w_ref[...], preferred_element_type=F32)
                                for w_ref in (w0_ref, w1_ref)], axis=1)

    def attn_gate(h):
        g_a = gate(h, wga0_ref, wga1_ref)
        y_a = jnp.dot(pick(oa_ref, ob_ref, h), wa_ref[...], preferred_element_type=F32)
        st[h]["m"] = jax.nn.sigmoid(g_a) * y_a

    def rnn_gate(h):
        g_b = gate(h, wgb0_ref, wgb1_ref)
        y_b = jnp.dot(pick(hga_ref, hgb_ref, h), wb_ref[...], preferred_element_type=F32)
        st[h]["m"] = st[h]["m"] + jax.nn.sigmoid(g_b) * y_b

    def mix(h):
        st[h]["mix"] = jnp.dot(st[h]["m"].astype(BF16), wo_ref[...], preferred_element_type=F32)

    def norm_mid(h):
        st[h]["x1"] = st[h]["x"] + _rms(st[h]["mix"], g2_ref[...])
        st[h]["f"] = _rms(st[h]["x1"], g3_ref[...]).astype(BF16)

    def ffn(h, c):
        cols = slice(c * FF_CHUNK, (c + 1) * FF_CHUNK)
        up = jnp.maximum(jnp.dot(st[h]["f"], wup_ref[:, cols], preferred_element_type=F32), 0.0)
        dn = jnp.dot((up * up).astype(BF16), wdn_ref[cols, :], preferred_element_type=F32)
        st[h]["acc"] = dn if c == 0 else st[h]["acc"] + dn

    def norm_out(h):
        yb_ref[halves[h], :] = st[h]["x1"] + _rms(st[h]["acc"], g4_ref[...])

    norm_in(0)
    norm_in(1)
    attn_gate(0)
    rnn_gate(0)
    mix(0)
    attn_gate(1)
    norm_mid(0)
    rnn_gate(1)
    mix(1)
    ffn(0, 0)
    norm_mid(1)
    for c in range(1, n_ff):
        ffn(0, c)
    ffn(1, 0)
    norm_out(0)
    for c in range(1, n_ff):
        ffn(1, c)
    norm_out(1)

    @pl.when(first)
    def _():
        ya_ref[...] = yb_ref[...]


def _out(xa, xb, oa, ob, hga, hgb, g1, w_in, wa, wb, wo, g2, g3, wup, wdn, g4):
    na, nb = xa.shape[0], xb.shape[0]
    tm = TOKEN_TILE
    assert na % tm == 0 and nb % tm == 0
    n_first = na // tm
    a_tok = lambda i: (jnp.minimum(i, n_first - 1), 0)
    b_tok = lambda i: (jnp.maximum(i - n_first, 0), 0)
    w_cols = (COL_GA, COL_GA + 1, COL_GB, COL_GB + 1)
    consts = (wa, wb, wo, g2, g3, wup, wdn, g4)
    return pl.pallas_call(
        functools.partial(_out_kernel, n_first),
        grid=((na + nb) // tm,),
        in_specs=[pl.BlockSpec((tm, D_MODEL), a_tok), pl.BlockSpec((tm, D_MODEL), b_tok),
                  pl.BlockSpec((tm, D_ATTN), a_tok), pl.BlockSpec((tm, D_ATTN), b_tok),
                  pl.BlockSpec((tm, D_RNN), a_tok), pl.BlockSpec((tm, D_RNN), b_tok), _const_spec(g1.shape)]
        + [_w_in_cols(j) for j in w_cols] + [_const_spec(c.shape) for c in consts],
        out_specs=[pl.BlockSpec((tm, D_MODEL), a_tok), pl.BlockSpec((tm, D_MODEL), b_tok)],
        out_shape=[jax.ShapeDtypeStruct((na, D_MODEL), F32), jax.ShapeDtypeStruct((nb, D_MODEL), F32)],
        compiler_params=pltpu.CompilerParams(dimension_semantics=("arbitrary",),
                                             vmem_limit_bytes=VMEM_LIMIT),
        name="out",
    )(xa, xb, oa, ob, hga, hgb, g1, *([w_in] * len(w_cols)), *consts)


def _block_diag(w):
    n, c, d = w.shape
    return jnp.einsum("ncd,nm->ncmd", w, jnp.eye(n, dtype=w.dtype)).reshape(n * c, n * d)


def _heads_last(kt):
    s, _, t = kt.shape
    return kt.reshape(s, N_HEADS, HEAD_DIM, t).transpose(0, 3, 1, 2)[None]


def _time_last(cache):
    s, t = cache.shape[:2]
    return cache.transpose(0, 2, 3, 1).reshape(s, D_ATTN, t)


def kernel(x_prompt, x_sample, cache_k, cache_v, state_conv, state_h, w_in, g_pre_mix, w_conv, b_conv, w_r, b_r, w_i, b_i, lam, w_a_out, w_b_out, w_o, g_post_mix, g_pre_ffn, w_up, w_down, g_post_ffn):
    assert w_in.shape[0] == 1
    row = lambda a: a[0].reshape(1, -1)
    assert w_in.shape[2] == (COL_GB + 2) * D_ATTN
    w_in_b = w_in[0].astype(BF16)
    wkv_t = w_in[0][:, COL_K * D_ATTN:(COL_V + 1) * D_ATTN].T.astype(BF16)
    wg = jnp.concatenate([_block_diag(w_r[0]), _block_diag(w_i[0])], axis=1).astype(BF16)
    bg = jnp.concatenate([row(b_r), row(b_i)], axis=1)
    lru_w = (w_conv[0], row(b_conv), wg, bg, row(lam))
    g1 = row(g_pre_mix)
    tri2 = _tri2()

    bp, tp, _ = x_prompt.shape
    bs, ts, _ = x_sample.shape
    n_p, n_s = bp * tp, bs * ts

    zc = jnp.zeros((bp, CONV_WIDTH - 1, D_RNN), F32)
    zh = jnp.zeros((bp, 1, D_RNN), F32)
    qb, kt, vt, ktb, vtb, hg_p, cp, hp = _proj_lru(x_prompt, g1, w_in_b, wkv_t, zc, zh, lru_w)
    o_p = _attn_prompt(qb.reshape(bp, tp, D_ATTN), ktb, vtb, tri2)

    s3 = lambda a: a.reshape(bs, ts, a.shape[-1])
    qs, ks, vs, ksb, vsb, hg_s, cs, hs = _proj(x_sample, g1, w_in_b, state_conv[0],
                                               state_h[0].reshape(bs, 1, D_RNN), lru_w)
    o_s = _attn_sample(s3(qs), s3(ksb), s3(vsb), _time_last(cache_k[0]), _time_last(cache_v[0]), tri2)

    ys, yp = _out(x_sample.reshape(n_s, D_MODEL), x_prompt.reshape(n_p, D_MODEL), o_s, o_p, hg_s, hg_p, g1, w_in_b,
                  w_a_out[0].astype(BF16), w_b_out[0].astype(BF16), w_o[0].astype(BF16), row(g_post_mix),
                  row(g_pre_ffn), w_up[0].astype(BF16), w_down[0].astype(BF16), row(g_post_ffn))
    return (yp.reshape(bp, tp, D_MODEL), ys.reshape(bs, ts, D_MODEL), _heads_last(kt), _heads_last(vt),
            cp[None], hp.reshape(1, bp, D_RNN), ks.reshape(1, bs, ts, N_HEADS, HEAD_DIM),
            vs.reshape(1, bs, ts, N_HEADS, HEAD_DIM), cs[None], hs.reshape(1, bs, D_RNN))
```

```python
import functools

import jax
import jax.numpy as jnp
import numpy as np
from jax import lax
from jax.experimental import pallas as pl
from jax.experimental.pallas import tpu as pltpu

F32 = jnp.float32
BF16 = jnp.bfloat16

D_MODEL = 1024
N_HEADS = 8
HEAD_DIM = 64
D_ATTN = N_HEADS * HEAD_DIM
D_RNN = 512
N_RNN_BLOCKS = 8
CONV_WIDTH = 4
LRU_C = 8.0
D_FF = 4 * D_MODEL
EPS = 1e-6

LANES = 128
SUBLANES = 8
KEY_BLOCK = 128
Q_ROWS = 128
FAST_BLOCKS = 3
Q_GROUP = 10
SAMPLE_STREAMS = 4
TOKEN_TILE = 512
SCAN_SLICES = 16
NORM_PIECES = 4
MXU_N = 256
LOG_KEEP_FLOOR = -104.0
VMEM_LIMIT = 56 * 1024 * 1024

NT_DIMS = (((1,), (1,)), ((), ()))
LOG2_E = 1.4426950408889634


def _rms(xf, g):
    return xf * lax.rsqrt(jnp.mean(xf * xf, axis=-1, keepdims=True) + EPS) * g


def _softplus(x):
    return jnp.maximum(x, 0.0) + jnp.log(1.0 + jnp.exp2(jnp.abs(x) * -LOG2_E))


def _sigmoid(x):
    return 0.5 * jnp.tanh(0.5 * x) + 0.5


def _const_spec(shape):
    nd = len(shape)
    return pl.BlockSpec(shape, lambda *_: (0,) * nd, pipeline_mode=pl.Buffered(1))


COL_Q, COL_K, COL_V, COL_U, COL_GR, COL_GA, COL_GB = 0, 1, 2, 3, 4, 5, 7
Q_SCALE = HEAD_DIM ** -0.5


def _w_in_cols(j):
    return pl.BlockSpec((D_MODEL, D_ATTN), lambda *_: (0, j), pipeline_mode=pl.Buffered(1))


def _rows(block_index, size):
    if isinstance(block_index, int):
        return pl.ds(block_index * size, size)
    return pl.ds(pl.multiple_of(block_index * size, size), size)


def _gelu_tanh(x):
    k = float(np.sqrt(2.0 / np.pi))
    half_x = 0.5 * x
    return half_x + half_x * jnp.tanh(x * (k + (k * 0.044715) * (x * x)))


def _conv_seed(first, cs_ref, ext_ref):
    @pl.when(first)
    def _():
        ext_ref[0:SUBLANES, :] = jnp.zeros((SUBLANES, D_RNN), F32)
        ext_ref[SUBLANES - (CONV_WIDTH - 1):SUBLANES, :] = cs_ref[0]


def _scan_seed(first, h0_ref, hc_ref):
    @pl.when(first)
    def _():
        hc_ref[...] = jnp.broadcast_to(h0_ref[0], hc_ref.shape)


def _lru_conv(u, wc_ref, bc_ref, ext_ref):
    tt = u.shape[0]
    ext_ref[SUBLANES:SUBLANES + tt, :] = u
    uc = bc_ref[...] + wc_ref[CONV_WIDTH - 1:CONV_WIDTH, :] * u
    for d in range(1, CONV_WIDTH):
        uc = uc + wc_ref[CONV_WIDTH - 1 - d:CONV_WIDTH - d, :] * ext_ref[SUBLANES - d:SUBLANES - d + tt, :]
    ext_ref[0:SUBLANES, :] = ext_ref[tt:tt + SUBLANES, :]
    return uc


def _lru_gates(uc, wg_ref, bg_ref):
    return jnp.dot(uc.astype(BF16), wg_ref[...], preferred_element_type=F32) + bg_ref[...]


def _lru_scan(g, uc, g_rnn, lam_ref, h_prev):
    tt = uc.shape[0]
    r = _sigmoid(g[:, :D_RNN])
    ig = _sigmoid(g[:, D_RNN:])
    decay = (LRU_C * _softplus(-lam_ref[...])) * r
    a = jnp.exp2(decay * -LOG2_E)
    m2 = jnp.tanh(decay) * (a * a + 1.0)
    b = jnp.where(m2 > 0.0, m2 * lax.rsqrt(m2), 0.0) * (ig * uc)

    n_groups = tt // SUBLANES
    a3 = a.reshape(n_groups, SUBLANES, D_RNN)
    b3 = b.reshape(n_groups, SUBLANES, D_RNN)
    sub = lax.broadcasted_iota(jnp.int32, a3.shape, 1)
    k = 1
    while k < SUBLANES:
        has_prev = sub >= k
        b_prev = jnp.where(has_prev, pltpu.roll(b3, k, axis=1), 0.0)
        a_prev = jnp.where(has_prev, pltpu.roll(a3, k, axis=1), 1.0)
        b3 = b3 + a3 * b_prev
        a3 = a3 * a_prev
        k *= 2

    hs = []
    for gi in range(n_groups):
        h = b3[gi] + a3[gi] * h_prev
        hs.append(h)
        h_prev = jnp.broadcast_to(h[SUBLANES - 1:SUBLANES, :], h.shape)
    return (jnp.concatenate(hs, axis=0) * _gelu_tanh(g_rnn)).astype(BF16), h_prev


def _lru_state_specs(index_map):
    past = CONV_WIDTH - 1
    return (pl.BlockSpec((1, past, D_RNN), index_map), pl.BlockSpec((1, 1, D_RNN), index_map))


def _lru_scratch(tt):
    return [pltpu.VMEM((tt + SUBLANES, D_RNN), F32), pltpu.VMEM((SUBLANES, D_RNN), F32)]


def _proj_kernel(t, x_ref, g_ref, wq_ref, wk_ref, wv_ref, wu_ref, wgr_ref, cs_ref, h0_ref, wc_ref, bc_ref, wg_ref,
                 bg_ref, lam_ref, qb_ref, k_ref, v_ref, kb_ref, vb_ref, hg_ref, cn_ref, hl_ref, ext_ref):
    n_streams = x_ref.shape[0] // t
    past = CONV_WIDTH - 1
    xn = _rms(x_ref[...], g_ref[...]).astype(BF16)
    proj = lambda w_ref: jnp.dot(xn, w_ref[...], preferred_element_type=F32)
    u, g_rnn = proj(wu_ref), proj(wgr_ref)
    qb_ref[...] = (proj(wq_ref) * Q_SCALE).astype(BF16)
    k, v = proj(wk_ref), proj(wv_ref)
    k_ref[...] = k
    v_ref[...] = v
    kb_ref[...] = k.astype(BF16)
    vb_ref[...] = v.astype(BF16)

    ucs = []
    for s in range(n_streams):
        ext = ext_ref.at[s]
        ext[0:SUBLANES, :] = jnp.zeros((SUBLANES, D_RNN), F32)
        ext[SUBLANES - past:SUBLANES, :] = cs_ref[s]
        ucs.append(_lru_conv(u[s * t:(s + 1) * t], wc_ref, bc_ref, ext))
        cn_ref[s] = ext[SUBLANES - past:SUBLANES, :]
    uc = jnp.concatenate(ucs, axis=0)
    gates = _lru_gates(uc, wg_ref, bg_ref)
    for s in range(n_streams):
        rows = slice(s * t, (s + 1) * t)
        hg, h_last = _lru_scan(gates[rows], uc[rows], g_rnn[rows], lam_ref,
                               jnp.broadcast_to(h0_ref[s], (SUBLANES, D_RNN)))
        hg_ref[rows, :] = hg
        hl_ref[s] = h_last[0:1, :]


def _proj(x, g, w_in, conv_state, h0, lru_w):
    s, t, _ = x.shape
    n = s * t
    tm = TOKEN_TILE
    assert n % tm == 0 and tm % t == 0 and t % SUBLANES == 0
    per_tile = tm // t
    tok = lambda i: (i, 0)
    row_out = pl.BlockSpec((tm, D_ATTN), tok)
    f32o = jax.ShapeDtypeStruct((n, D_ATTN), F32)
    b16o = jax.ShapeDtypeStruct((n, D_ATTN), BF16)
    st3 = pl.BlockSpec((per_tile, CONV_WIDTH - 1, D_RNN), lambda i: (i, 0, 0))
    st1 = pl.BlockSpec((per_tile, 1, D_RNN), lambda i: (i, 0, 0))
    w_cols = (COL_Q, COL_K, COL_V, COL_U, COL_GR)
    return pl.pallas_call(
        functools.partial(_proj_kernel, t),
        grid=(n // tm,),
        in_specs=[pl.BlockSpec((tm, D_MODEL), tok), _const_spec(g.shape)] + [_w_in_cols(j) for j in w_cols]
        + [st3, st1] + [_const_spec(w.shape) for w in lru_w],
        out_specs=[row_out] * 6 + [st3, st1],
        out_shape=[b16o, f32o, f32o, b16o, b16o, b16o,
                   jax.ShapeDtypeStruct((s, CONV_WIDTH - 1, D_RNN), F32),
                   jax.ShapeDtypeStruct((s, 1, D_RNN), F32)],
        scratch_shapes=[pltpu.VMEM((per_tile, t + SUBLANES, D_RNN), F32)],
        compiler_params=pltpu.CompilerParams(dimension_semantics=("arbitrary",),
                                             vmem_limit_bytes=VMEM_LIMIT),
        name="proj",
    )(x.reshape(n, D_MODEL), g, *([w_in] * len(w_cols)), conv_state, h0, *lru_w)


def _proj_lru_kernel(per, n_tiles, x_ref, g_ref, wq_ref, wu_ref, wgr_ref, wkvt_ref, cs_ref, h0_ref, wc_ref, bc_ref,
                     wg_ref, bg_ref, lam_ref, qb_ref, kt_ref, vt_ref, ktb_ref, vtb_ref, hg_ref, cn_ref, hl_ref,
                     ext_ref, hc_ref, xn_ref, uc_ref, gr_ref, gt_ref):
    i = pl.program_id(0)
    dot_tile, scan_tile = i - 1, i - 2
    dot_live = jnp.logical_and(dot_tile >= 0, dot_tile < n_tiles)
    scan_live = jnp.logical_and(scan_tile >= 0, scan_tile < n_tiles)
    _conv_seed(jnp.logical_and(dot_live, dot_tile % per == 0), cs_ref, ext_ref)
    _scan_seed(jnp.logical_and(scan_live, scan_tile % per == 0), h0_ref, hc_ref)

    @pl.when(i == 0)
    def _():
        xn_ref[...] = jnp.zeros(xn_ref.shape, BF16)
        hc_ref[...] = jnp.zeros(hc_ref.shape, F32)
        ext_ref[...] = jnp.zeros(ext_ref.shape, F32)
        for ref in (uc_ref, gr_ref, gt_ref):
            ref[...] = jnp.zeros(ref.shape, F32)

    tm = x_ref.shape[0]
    sc, dt = i % 2, (i + 1) % 2
    xn = xn_ref[dt]
    rows_per = tm // SCAN_SLICES
    u_parts = {}
    uc_b = [None]

    def scan_slice(state, p):
        rows = pl.ds(p * rows_per, rows_per)
        hg, state[0] = _lru_scan(gt_ref[sc, rows, :], uc_ref[sc, rows, :], gr_ref[sc, rows, :], lam_ref, state[0])
        hg_ref[rows, :] = hg

    def scan_done(state):
        hc_ref[...] = state[0]

        @pl.when(jnp.logical_and(scan_live, scan_tile % per == per - 1))
        def _():
            hl_ref[0] = state[0][0:1, :]

    def rms_piece(r):
        rows = pl.ds(r * (tm // NORM_PIECES), tm // NORM_PIECES)
        xn_ref[sc, rows, :] = _rms(x_ref[rows, :], g_ref[...]).astype(BF16)

    def ug_chunk(c):
        is_u = c < D_RNN // MXU_N
        cols = slice(c * MXU_N % D_RNN, c * MXU_N % D_RNN + MXU_N)
        r = jnp.dot(xn, (wu_ref if is_u else wgr_ref)[:, cols], preferred_element_type=F32)
        if is_u:
            u_parts[c] = r
        else:
            gr_ref[dt, :, cols] = r

    def conv():
        uc = _lru_conv(jnp.concatenate([u_parts[c] for c in sorted(u_parts)], axis=1), wc_ref, bc_ref, ext_ref)
        uc_ref[dt] = uc
        uc_b[0] = uc.astype(BF16)

    def q_chunk(c):
        cols = slice(c * MXU_N, (c + 1) * MXU_N)
        qb_ref[:, cols] = (jnp.dot(xn, wq_ref[:, cols], preferred_element_type=F32) * Q_SCALE).astype(BF16)

    def kv_chunk(c):
        toks = slice(c * MXU_N, (c + 1) * MXU_N)
        kv = lax.dot_general(wkvt_ref[...], xn[toks], NT_DIMS, preferred_element_type=F32)
        k, v = kv[:D_ATTN], kv[D_ATTN:]
        kt_ref[0, :, toks] = k
        vt_ref[0, :, toks] = v
        for j in range(MXU_N // KEY_BLOCK):
            cols = slice(j * KEY_BLOCK, (j + 1) * KEY_BLOCK)
            ktb_ref[0, c * (MXU_N // KEY_BLOCK) + j] = k[:, cols].astype(BF16)
            vtb_ref[0, c * (MXU_N // KEY_BLOCK) + j] = v[:, cols].astype(BF16)

    def gates_chunk(c):
        cols = slice(c * MXU_N, (c + 1) * MXU_N)
        gt_ref[dt, :, cols] = jnp.dot(uc_b[0], wg_ref[:, cols], preferred_element_type=F32) + bg_ref[:, cols]

    @pl.when(i == 0)
    def _():
        for r in range(NORM_PIECES):
            rms_piece(r)

    @pl.when(i == n_tiles + 1)
    def _():
        state = [hc_ref[...]]
        for p in range(SCAN_SLICES):
            scan_slice(state, p)
        scan_done(state)

    @pl.when(jnp.logical_and(i > 0, i <= n_tiles))
    def _():
        state = [hc_ref[...]]
        S, P = functools.partial(scan_slice, state), functools.partial
        program = [
            P(ug_chunk, 0), P(S, 0), P(S, 1), P(ug_chunk, 1), P(S, 2), P(S, 3), P(ug_chunk, 2), conv,
            P(ug_chunk, 3), P(S, 4), P(S, 5), P(q_chunk, 0), P(S, 6), P(S, 7),
            P(q_chunk, 1), P(rms_piece, 0), P(rms_piece, 1), P(S, 8),
            P(kv_chunk, 0), P(S, 9), P(S, 10), P(S, 11), P(rms_piece, 2),
            P(kv_chunk, 1), P(S, 12), P(S, 13), P(S, 14), P(rms_piece, 3),
            P(gates_chunk, 0), P(S, 15), P(gates_chunk, 1), P(gates_chunk, 2), P(gates_chunk, 3),
        ]
        assert SCAN_SLICES == 16 and NORM_PIECES == 4
        for piece in program:
            piece()
        scan_done(state)

        @pl.when(jnp.logical_and(dot_live, dot_tile % per == per - 1))
        def _():
            cn_ref[0] = ext_ref[SUBLANES - (CONV_WIDTH - 1):SUBLANES, :]


def _proj_lru(x, g, w_in, wkvt, conv_state, h0, lru_w):
    s, t, _ = x.shape
    n = s * t
    tm = TOKEN_TILE
    assert t % tm == 0 and tm % KEY_BLOCK == 0
    per = t // tm
    n_tiles = n // tm
    tile = lambda i, lag: jnp.clip(i - lag, 0, n_tiles - 1)
    tok_in = pl.BlockSpec((tm, D_MODEL), lambda i: (tile(i, 0), 0))
    q_out = pl.BlockSpec((tm, D_ATTN), lambda i: (tile(i, 1), 0))
    hg_out = pl.BlockSpec((tm, D_RNN), lambda i: (tile(i, 2), 0))
    b16o = jax.ShapeDtypeStruct((n, D_ATTN), BF16)
    kv_spec = pl.BlockSpec((1, D_ATTN, tm), lambda i: (tile(i, 1) // per, 0, tile(i, 1) % per))
    kvb_spec = pl.BlockSpec((1, tm // KEY_BLOCK, D_ATTN, KEY_BLOCK),
                            lambda i: (tile(i, 1) // per, tile(i, 1) % per, 0, 0))
    kv_shape = jax.ShapeDtypeStruct((s, D_ATTN, t), F32)
    kvb_shape = jax.ShapeDtypeStruct((s, t // KEY_BLOCK, D_ATTN, KEY_BLOCK), BF16)
    cs_spec, _ = _lru_state_specs(lambda i: (tile(i, 1) // per, 0, 0))
    _, h_spec = _lru_state_specs(lambda i: (tile(i, 2) // per, 0, 0))
    w_cols = (COL_Q, COL_U, COL_GR)
    return pl.pallas_call(
        functools.partial(_proj_lru_kernel, per, n_tiles),
        grid=(n_tiles + 2,),
        in_specs=[tok_in, _const_spec(g.shape)] + [_w_in_cols(j) for j in w_cols] + [_const_spec(wkvt.shape)]
        + [cs_spec, h_spec] + [_const_spec(w.shape) for w in lru_w],
        out_specs=[q_out, kv_spec, kv_spec, kvb_spec, kvb_spec, hg_out, cs_spec, h_spec],
        out_shape=[b16o, kv_shape, kv_shape, kvb_shape, kvb_shape, b16o,
                   jax.ShapeDtypeStruct((s, CONV_WIDTH - 1, D_RNN), F32),
                   jax.ShapeDtypeStruct((s, 1, D_RNN), F32)],
        scratch_shapes=_lru_scratch(tm) + [pltpu.VMEM((2, tm, D_MODEL), BF16), pltpu.VMEM((2, tm, D_RNN), F32),
                                           pltpu.VMEM((2, tm, D_RNN), F32), pltpu.VMEM((2, tm, 2 * D_RNN), F32)],
        compiler_params=pltpu.CompilerParams(dimension_semantics=("arbitrary",),
                                             vmem_limit_bytes=VMEM_LIMIT),
        name="proj_lru",
    )(x.reshape(n, D_MODEL), g, *([w_in] * len(w_cols)), wkvt, conv_state, h0, *lru_w)


def _sb_split(z, mask):
    n = z.shape[1] // KEY_BLOCK
    tq = z.shape[0] // 2
    sp = _softplus(z)
    cats = []
    for h in range(2):
        for d in range(n):
            blk = sp[h * tq:(h + 1) * tq, d * KEY_BLOCK:(d + 1) * KEY_BLOCK]
            if d == n - 1 and mask is not None:
                blk = jnp.where(mask, blk, 0.0)
            cats.append(_hi_lo(blk))
    return jnp.concatenate(cats, axis=0)


def _hi_lo(blk):
    hi = blk.astype(BF16)
    lo = (blk - hi.astype(F32)).astype(BF16)
    return jnp.concatenate([hi, lo], axis=1)


def _sb_cumsum(cat, tri2):
    return jnp.dot(cat, tri2, preferred_element_type=F32)


def _sb_weights(z, r, carries, mask):
    n = z.shape[1] // KEY_BLOCK
    tq = z.shape[0] // 2
    rows, new_carries = [], []
    for h in range(2):
        carry = None if carries is None else carries[h]
        es = [None] * n
        for d in reversed(range(n)):
            rd = r[(h * n + d) * tq:(h * n + d + 1) * tq]
            x = z[h * tq:(h + 1) * tq, d * KEY_BLOCK:(d + 1) * KEY_BLOCK] + rd[:, :KEY_BLOCK]
            if carry is not None:
                x = x + carry
            e = jnp.exp(x)
            if d == n - 1 and mask is not None:
                e = jnp.where(mask, e, 0.0)
            es[d] = e.astype(BF16)
            tot = rd[:, KEY_BLOCK:]
            carry = tot if carry is None else carry + tot
        rows.append(es[0] if n == 1 else jnp.concatenate(es, axis=1))
        new_carries.append(carry)
    return jnp.concatenate(rows, axis=0), new_carries


def _tri2():
    j = np.arange(KEY_BLOCK)[:, None]
    s = np.arange(KEY_BLOCK)[None, :]
    half = np.concatenate([-(j >= s).astype(np.float32), -np.ones((KEY_BLOCK, LANES), np.float32)], axis=1)
    return jnp.asarray(np.concatenate([half, half], axis=0), dtype=BF16)


def _stack_heads(q, lo_half):
    zero = jnp.zeros_like(q)
    return jnp.concatenate([jnp.where(lo_half, q, zero), jnp.where(lo_half, zero, q)], axis=0)


def _sb_window(z_fn, pv_fn, carries, mask, tri2):
    z = z_fn()
    e, cs = _sb_weights(z, _sb_cumsum(_sb_split(z, mask), tri2), carries, mask)
    pv = pv_fn(e)
    tq = pv.shape[0] // 2
    return (pv[:tq], cs[0]), (pv[tq:], cs[1])


def _finish(store, j0, outs, step_fn):
    (a0, c0), (a1, c1) = outs
    store(a0, a1)
    if isinstance(j0, int) and j0 < 0:
        return lambda: None

    def alive(x0, x1):
        return jnp.max(jnp.maximum(x0, x1)) >= LOG_KEEP_FLOOR

    go = jnp.logical_and(j0 >= 0, alive(c0, c1))

    def walk():
        @pl.when(go)
        def _():
            def body(s):
                j, b0, b1, d0, d1, _ = s
                (p0, d0), (p1, d1) = step_fn(j, [d0, d1])
                return j - 1, b0 + p0, b1 + p1, d0, d1, jnp.logical_and(j >= 1, alive(d0, d1))

            _, b0, b1, _, _, _ = lax.while_loop(lambda s: s[-1], body,
                                                (jnp.int32(j0), a0, a1, c0, c1, jnp.bool_(True)))
            store(b0, b1)

    return walk


def _sb_chains(chains, tri2):
    zs = [c[0]() for c in chains]
    sums = [_sb_cumsum(_sb_split(z, c[2]), tri2) for z, c in zip(zs, chains)]
    weights = [_sb_weights(z, r, None, c[2]) for z, r, c in zip(zs, sums, chains)]
    walks = []
    for c, (e, cs) in zip(chains, weights):
        pv = c[1](e)
        tq = pv.shape[0] // 2
        walks.append(_finish(c[3], c[5], ((pv[:tq], cs[0]), (pv[tq:], cs[1])), c[4]))
    for walk in walks:
        walk()


def _attn_prompt_kernel(q_ref, kt_ref, vt_ref, tri_ref, o_ref):
    t = q_ref.shape[1]
    nq = t // Q_ROWS
    row = lax.broadcasted_iota(jnp.int32, (Q_ROWS, LANES), 0)
    lane = lax.broadcasted_iota(jnp.int32, (Q_ROWS, LANES), 1)
    lo_half = lane < HEAD_DIM
    diag_mask = lane < row
    lane_minus_row = lane - row
    tri2 = tri_ref[...]

    def window(ref, first, n):
        return jnp.concatenate([ref[0, first + d] for d in range(n)], axis=1)

    def qblocks(blocks, n_fast):
        chains = []
        for i in blocks:
            q2 = _stack_heads(q_ref[0, _rows(i, Q_ROWS), :], lo_half)
            first = i + 1 - n_fast

            def z_fn(q2=q2, first=first):
                return jnp.dot(q2, window(kt_ref, first, n_fast), preferred_element_type=F32)

            def pv_fn(e, first=first):
                return lax.dot_general(e, window(vt_ref, first, n_fast), NT_DIMS, preferred_element_type=F32)

            def store(b0, b1, i=i):
                o_ref[_rows(i, Q_ROWS), :] = jnp.where(lo_half, b0, b1).astype(BF16)

            def step(j, cs, i=i, q2=q2):
                m = lane_minus_row < (i - j) * KEY_BLOCK
                return _sb_window(lambda: jnp.dot(q2, kt_ref[0, j], preferred_element_type=F32),
                                  lambda e: lax.dot_general(e, vt_ref[0, j], NT_DIMS, preferred_element_type=F32),
                                  cs, m, tri2)

            chains.append((z_fn, pv_fn, diag_mask, store, step, i - n_fast))
        _sb_chains(chains, tri2)

    half = Q_ROWS // 2
    bottom = row >= half

    def qblocks_tapered(blocks):
        assert FAST_BLOCKS == 3
        chains = []
        for i in blocks:
            q2 = _stack_heads(q_ref[0, _rows(i, Q_ROWS), :], lo_half)
            q2_top = jnp.concatenate([q2[:half], q2[Q_ROWS:Q_ROWS + half]], axis=0)
            z_new = jnp.dot(q2, window(kt_ref, i - 1, 2), preferred_element_type=F32)
            z_old = jnp.dot(q2_top, kt_ref[0, i - 2], preferred_element_type=F32)
            chains.append([i, q2, z_new, z_old])
        for c in chains:
            _, _, z_new, z_old = c
            sp_new, sp_old = _softplus(z_new), _softplus(z_old)
            cats = []
            for h in range(2):
                rows = slice(h * Q_ROWS, (h + 1) * Q_ROWS)
                cats.append(_hi_lo(sp_new[rows, :KEY_BLOCK]))
                cats.append(_hi_lo(jnp.where(diag_mask, sp_new[rows, KEY_BLOCK:], 0.0)))
            for h in range(2):
                cats.append(_hi_lo(sp_old[h * half:(h + 1) * half]))
            c.append(_sb_cumsum(jnp.concatenate(cats, axis=0), tri2))
        walks = []
        for i, q2, z_new, z_old, r in chains:
            es, carries = [], []
            for h in range(2):
                rows = slice(h * Q_ROWS, (h + 1) * Q_ROWS)
                r_prev = r[2 * h * Q_ROWS:(2 * h + 1) * Q_ROWS]
                r_diag = r[(2 * h + 1) * Q_ROWS:(2 * h + 2) * Q_ROWS]
                r_old = r[4 * Q_ROWS + h * half:4 * Q_ROWS + (h + 1) * half]
                e_diag = jnp.where(diag_mask, jnp.exp(z_new[rows, KEY_BLOCK:] + r_diag[:, :KEY_BLOCK]), 0.0)
                carry = r_diag[:, KEY_BLOCK:]
                e_prev = jnp.exp(z_new[rows, :KEY_BLOCK] + r_prev[:, :KEY_BLOCK] + carry)
                carry = carry + r_prev[:, KEY_BLOCK:]
                e_old = jnp.exp(z_old[h * half:(h + 1) * half] + r_old[:, :KEY_BLOCK] + carry[:half])
                carries.append(jnp.concatenate([carry[:half] + r_old[:, KEY_BLOCK:], carry[half:]], axis=0))
                e_old = jnp.concatenate([e_old.astype(BF16), jnp.zeros((half, KEY_BLOCK), BF16)], axis=0)
                es.append(jnp.concatenate([e_old, e_prev.astype(BF16), e_diag.astype(BF16)], axis=1))
            pv = lax.dot_general(jnp.concatenate(es, axis=0), window(vt_ref, i - 2, FAST_BLOCKS), NT_DIMS,
                                 preferred_element_type=F32)

            def store(b0, b1, i=i):
                o_ref[_rows(i, Q_ROWS), :] = jnp.where(lo_half, b0, b1).astype(BF16)

            def step(j, cs, i=i, q2=q2):
                m = jnp.logical_and(lane_minus_row < (i - j) * KEY_BLOCK, jnp.logical_or(bottom, j < i - 2))
                return _sb_window(lambda: jnp.dot(q2, kt_ref[0, j], preferred_element_type=F32),
                                  lambda e: lax.dot_general(e, vt_ref[0, j], NT_DIMS, preferred_element_type=F32),
                                  cs, m, tri2)

            walks.append(_finish(store, i - 2, ((pv[:Q_ROWS], carries[0]), (pv[Q_ROWS:], carries[1])), step))
        for walk in walks:
            walk()

    n_head = FAST_BLOCKS - 1
    for i in range(n_head):
        qblocks([i], i + 1)

    start = n_head + (nq - n_head) % Q_GROUP
    if start > n_head:
        qblocks_tapered(list(range(n_head, start)))

    def group(g, carry):
        qblocks_tapered([start + g * Q_GROUP + u for u in range(Q_GROUP)])
        return carry

    lax.fori_loop(0, (nq - start) // Q_GROUP, group, 0)


def _attn_prompt(qb, ktb, vtb, tri2):
    b, t, _ = qb.shape
    assert t % Q_ROWS == 0 and Q_ROWS == KEY_BLOCK
    qspec = pl.BlockSpec((1, t, LANES), lambda i, h: (i, 0, h))
    kspec = pl.BlockSpec((1, t // KEY_BLOCK, LANES, KEY_BLOCK), lambda i, h: (i, 0, h, 0))
    return pl.pallas_call(
        _attn_prompt_kernel,
        grid=(b, D_ATTN // LANES),
        in_specs=[qspec, kspec, kspec, _const_spec(tri2.shape)],
        out_specs=pl.BlockSpec((t, LANES), lambda i, h: (i, h)),
        out_shape=jax.ShapeDtypeStruct((b * t, D_ATTN), BF16),
        compiler_params=pltpu.CompilerParams(dimension_semantics=("arbitrary", "arbitrary"),
                                             vmem_limit_bytes=VMEM_LIMIT),
        name="attn_prompt",
    )(qb, ktb, vtb, tri2)


def _attn_sample_kernel(n_tail, q_ref, kn_ref, vn_ref, ckt_ref, cvt_ref, ck_hbm, cv_hbm, tri_ref, o_ref,
                        kbuf, vbuf):
    n_streams = q_ref.shape[0]
    tq = q_ref.shape[1]
    n_past = ck_hbm.shape[2] // KEY_BLOCK
    row = lax.broadcasted_iota(jnp.int32, (tq, LANES), 0)
    lane = lax.broadcasted_iota(jnp.int32, (tq, LANES), 1)
    lo_half = lane < HEAD_DIM
    tri2 = tri_ref[...]

    pad = jnp.zeros((KEY_BLOCK - tq, LANES), BF16)
    chains = []
    for b in range(n_streams):
        s = pl.program_id(0) * n_streams + b
        for p in range(q_ref.shape[2] // LANES):
            cols = slice(p * LANES, (p + 1) * LANES)
            q2 = _stack_heads(q_ref[b, :, cols], lo_half)

            def z_fn(q2=q2, cols=cols, b=b):
                z_new = lax.dot_general(q2, jnp.concatenate([kn_ref[b, :, cols], pad], axis=0), NT_DIMS,
                                        preferred_element_type=F32)
                if n_tail == 0:
                    return z_new
                z_past = jnp.dot(q2, ckt_ref[b, cols, :].astype(BF16), preferred_element_type=F32)
                return jnp.concatenate([z_past, z_new], axis=1)

            def pv_fn(e, cols=cols, b=b):
                pv = jnp.dot(e[:, n_tail * KEY_BLOCK:], jnp.concatenate([vn_ref[b, :, cols], pad], axis=0),
                             preferred_element_type=F32)
                if n_tail == 0:
                    return pv
                return pv + lax.dot_general(e[:, :n_tail * KEY_BLOCK], cvt_ref[b, cols, :].astype(BF16), NT_DIMS,
                                            preferred_element_type=F32)

            def store(a0, a1, cols=cols, b=b):
                o_ref[b * tq:(b + 1) * tq, cols] = jnp.where(lo_half, a0, a1).astype(BF16)

            def step(j, cs, q2=q2, p=p, s=s):
                keys = pl.ds(pl.multiple_of(j * KEY_BLOCK, KEY_BLOCK), KEY_BLOCK)
                pltpu.sync_copy(ck_hbm.at[s, pl.ds(p * LANES, LANES), keys], kbuf)
                pltpu.sync_copy(cv_hbm.at[s, pl.ds(p * LANES, LANES), keys], vbuf)
                return _sb_window(lambda: jnp.dot(q2, kbuf[...].astype(BF16), preferred_element_type=F32),
                                  lambda e: lax.dot_general(e, vbuf[...].astype(BF16), NT_DIMS,
                                                            preferred_element_type=F32),
                                  cs, None, tri2)

            chains.append((z_fn, pv_fn, lane < row, store, step, n_past - 1 - n_tail))
    _sb_chains(chains, tri2)


def _attn_sample(qb, knb, vnb, cache_kt, cache_vt, tri2):
    s, tq, _ = qb.shape
    past_len = cache_kt.shape[2]
    assert past_len % KEY_BLOCK == 0 and tq <= KEY_BLOCK and tq % 16 == 0
    n_tail = min(FAST_BLOCKS - 1, past_len // KEY_BLOCK)
    assert n_tail > 0 and past_len % (n_tail * KEY_BLOCK) == 0
    per = SAMPLE_STREAMS if s % SAMPLE_STREAMS == 0 else 1
    new = pl.BlockSpec((per, tq, D_ATTN), lambda i: (i, 0, 0))
    tail = pl.BlockSpec((per, D_ATTN, n_tail * KEY_BLOCK), lambda i: (i, 0, past_len // (n_tail * KEY_BLOCK) - 1))
    hbm = pl.BlockSpec(memory_space=pl.ANY)
    return pl.pallas_call(
        functools.partial(_attn_sample_kernel, n_tail),
        grid=(s // per,),
        in_specs=[new, new, new, tail, tail, hbm, hbm, _const_spec(tri2.shape)],
        out_specs=pl.BlockSpec((per * tq, D_ATTN), lambda i: (i, 0)),
        out_shape=jax.ShapeDtypeStruct((s * tq, D_ATTN), BF16),
        scratch_shapes=[pltpu.VMEM((LANES, KEY_BLOCK), F32), pltpu.VMEM((LANES, KEY_BLOCK), F32)],
        compiler_params=pltpu.CompilerParams(dimension_semantics=("arbitrary",),
                                             vmem_limit_bytes=VMEM_LIMIT),
        name="attn_sample",
    )(qb, knb, vnb, cache_kt, cache_vt, cache_kt, cache_vt, tri2)


FF_CHUNK = 1024


def _out_kernel(n_first, xa_ref, xb_ref, oa_ref, ob_ref, hga_ref, hgb_ref, g1_ref, wga0_ref, wga1_ref, wgb0_ref,
                wgb1_ref, wa_ref, wb_ref, wo_ref, g2_ref, g3_ref, wup_ref, wdn_ref, g4_ref, ya_ref, yb_ref):
    first = pl.program_id(0) < n_first
    tm = xa_ref.shape[0]
    halves = [pl.ds(0, tm // 2), pl.ds(tm // 2, tm // 2)]
    n_ff = D_FF // FF_CHUNK
    st = [{}, {}]

    def pick(a_ref, b_ref, h):
        return jnp.where(first, a_ref[halves[h], :], b_ref[halves[h], :])

    def norm_in(h):
        st[h]["x"] = pick(xa_ref, xb_ref, h)
        st[h]["xn"] = _rms(st[h]["x"], g1_ref[...]).astype(BF16)

    def gate(h, w0_ref, w1_ref):
        return jnp.concatenate([jnp.dot(st[h]["xn"], w_ref[...], preferred_element_type=F32)
                                for w_ref in (w0_ref, w1_ref)], axis=1)

    def attn_gate(h):
        g_a = gate(h, wga0_ref, wga1_ref)
        y_a = jnp.dot(pick(oa_ref, ob_ref, h), wa_ref[...], preferred_element_type=F32)
        st[h]["m"] = jax.nn.sigmoid(g_a) * y_a

    def rnn_gate(h):
        g_b = gate(h, wgb0_ref, wgb1_ref)
        y_b = jnp.dot(pick(hga_ref, hgb_ref, h), wb_ref[...], preferred_element_type=F32)
        st[h]["m"] = st[h]["m"] + jax.nn.sigmoid(g_b) * y_b

    def mix(h):
        st[h]["mix"] = jnp.dot(st[h]["m"].astype(BF16), wo_ref[...], preferred_element_type=F32)

    def norm_mid(h):
        st[h]["x1"] = st[h]["x"] + _rms(st[h]["mix"], g2_ref[...])
        st[h]["f"] = _rms(st[h]["x1"], g3_ref[...]).astype(BF16)

    def ffn(h, c):
        cols = slice(c * FF_CHUNK, (c + 1) * FF_CHUNK)
        up = jnp.maximum(jnp.dot(st[h]["f"], wup_ref[:, cols], preferred_element_type=F32), 0.0)
        dn = jnp.dot((up * up).astype(BF16), wdn_ref[cols, :], preferred_element_type=F32)
        st[h]["acc"] = dn if c == 0 else st[h]["acc"] + dn

    def norm_out(h):
        yb_ref[halves[h], :] = st[h]["x1"] + _rms(st[h]["acc"], g4_ref[...])

    norm_in(0)
    norm_in(1)
    attn_gate(0)
    rnn_gate(0)
    mix(0)
    attn_gate(1)
    norm_mid(0)
    rnn_gate(1)
    mix(1)
    ffn(0, 0)
    norm_mid(1)
    for c in range(1, n_ff):
        ffn(0, c)
    ffn(1, 0)
    norm_out(0)
    for c in range(1, n_ff):
        ffn(1, c)
    norm_out(1)

    @pl.when(first)
    def _():
        ya_ref[...] = yb_ref[...]


def _out(xa, xb, oa, ob, hga, hgb, g1, w_in, wa, wb, wo, g2, g3, wup, wdn, g4):
    na, nb = xa.shape[0], xb.shape[0]
    tm = TOKEN_TILE
    assert na % tm == 0 and nb % tm == 0
    n_first = na // tm
    a_tok = lambda i: (jnp.minimum(i, n_first - 1), 0)
    b_tok = lambda i: (jnp.maximum(i - n_first, 0), 0)
    w_cols = (COL_GA, COL_GA + 1, COL_GB, COL_GB + 1)
    consts = (wa, wb, wo, g2, g3, wup, wdn, g4)
    return pl.pallas_call(
        functools.partial(_out_kernel, n_first),
        grid=((na + nb) // tm,),
        in_specs=[pl.BlockSpec((tm, D_MODEL), a_tok), pl.BlockSpec((tm, D_MODEL), b_tok),
                  pl.BlockSpec((tm, D_ATTN), a_tok), pl.BlockSpec((tm, D_ATTN), b_tok),
                  pl.BlockSpec((tm, D_RNN), a_tok), pl.BlockSpec((tm, D_RNN), b_tok), _const_spec(g1.shape)]
        + [_w_in_cols(j) for j in w_cols] + [_const_spec(c.shape) for c in consts],
        out_specs=[pl.BlockSpec((tm, D_MODEL), a_tok), pl.BlockSpec((tm, D_MODEL), b_tok)],
        out_shape=[jax.ShapeDtypeStruct((na, D_MODEL), F32), jax.ShapeDtypeStruct((nb, D_MODEL), F32)],
        compiler_params=pltpu.CompilerParams(dimension_semantics=("arbitrary",),
                                             vmem_limit_bytes=VMEM_LIMIT),
        name="out",
    )(xa, xb, oa, ob, hga, hgb, g1, *([w_in] * len(w_cols)), *consts)


def _block_diag(w):
    n, c, d = w.shape
    return jnp.einsum("ncd,nm->ncmd", w, jnp.eye(n, dtype=w.dtype)).reshape(n * c, n * d)


def _heads_last(kt):
    s, _, t = kt.shape
    return kt.reshape(s, N_HEADS, HEAD_DIM, t).transpose(0, 3, 1, 2)[None]


def _time_last(cache):
    s, t = cache.shape[:2]
    return cache.transpose(0, 2, 3, 1).reshape(s, D_ATTN, t)


def kernel(x_prompt, x_sample, cache_k, cache_v, state_conv, state_h, w_in, g_pre_mix, w_conv, b_conv, w_r, b_r, w_i, b_i, lam, w_a_out, w_b_out, w_o, g_post_mix, g_pre_ffn, w_up, w_down, g_post_ffn):
    assert w_in.shape[0] == 1
    row = lambda a: a[0].reshape(1, -1)
    assert w_in.shape[2] == (COL_GB + 2) * D_ATTN
    w_in_b = w_in[0].astype(BF16)
    wkv_t = w_in[0][:, COL_K * D_ATTN:(COL_V + 1) * D_ATTN].T.astype(BF16)
    wg = jnp.concatenate([_block_diag(w_r[0]), _block_diag(w_i[0])], axis=1).astype(BF16)
    bg = jnp.concatenate([row(b_r), row(b_i)], axis=1)
    lru_w = (w_conv[0], row(b_conv), wg, bg, row(lam))
    g1 = row(g_pre_mix)
    tri2 = _tri2()

    bp, tp, _ = x_prompt.shape
    bs, ts, _ = x_sample.shape
    n_p, n_s = bp * tp, bs * ts

    zc = jnp.zeros((bp, CONV_WIDTH - 1, D_RNN), F32)
    zh = jnp.zeros((bp, 1, D_RNN), F32)
    qb, kt, vt, ktb, vtb, hg_p, cp, hp = _proj_lru(x_prompt, g1, w_in_b, wkv_t, zc, zh, lru_w)
    o_p = _attn_prompt(qb.reshape(bp, tp, D_ATTN), ktb, vtb, tri2)

    s3 = lambda a: a.reshape(bs, ts, a.shape[-1])
    qs, ks, vs, ksb, vsb, hg_s, cs, hs = _proj(x_sample, g1, w_in_b, state_conv[0],
                                               state_h[0].reshape(bs, 1, D_RNN), lru_w)
    o_s = _attn_sample(s3(qs), s3(ksb), s3(vsb), _time_last(cache_k[0]), _time_last(cache_v[0]), tri2)

    ys, yp = _out(x_sample.reshape(n_s, D_MODEL), x_prompt.reshape(n_p, D_MODEL), o_s, o_p, hg_s, hg_p, g1, w_in_b,
                  w_a_out[0].astype(BF16), w_b_out[0].astype(BF16), w_o[0].astype(BF16), row(g_post_mix),
                  row(g_pre_ffn), w_up[0].astype(BF16), w_down[0].astype(BF16), row(g_post_ffn))
    return (yp.reshape(bp, tp, D_MODEL), ys.reshape(bs, ts, D_MODEL), _heads_last(kt), _heads_last(vt),
            cp[None], hp.reshape(1, bp, D_RNN), ks.reshape(1, bs, ts, N_HEADS, HEAD_DIM),
            vs.reshape(1, bs, ts, N_HEADS, HEAD_DIM), cs[None], hs.reshape(1, bs, D_RNN))
```

```python
import functools

import jax
import jax.numpy as jnp
import numpy as np
from jax import lax
from jax.experimental import pallas as pl
from jax.experimental.pallas import tpu as pltpu

F32 = jnp.float32
BF16 = jnp.bfloat16

D_MODEL = 1024
N_HEADS = 8
HEAD_DIM = 64
D_ATTN = N_HEADS * HEAD_DIM
D_RNN = 512
N_RNN_BLOCKS = 8
CONV_WIDTH = 4
LRU_C = 8.0
D_FF = 4 * D_MODEL
EPS = 1e-6

LANES = 128
SUBLANES = 8
KEY_BLOCK = 128
Q_ROWS = 128
FAST_BLOCKS = 3
Q_GROUP = 10
SAMPLE_STREAMS = 4
TOKEN_TILE = 512
SCAN_SLICES = 16
NORM_PIECES = 4
MXU_N = 256
LOG_KEEP_FLOOR = -104.0
VMEM_LIMIT = 56 * 1024 * 1024

NT_DIMS = (((1,), (1,)), ((), ()))
LOG2_E = 1.4426950408889634


def _rms(xf, g):
    return xf * lax.rsqrt(jnp.mean(xf * xf, axis=-1, keepdims=True) + EPS) * g


def _softplus(x):
    return jnp.maximum(x, 0.0) + jnp.log(1.0 + jnp.exp2(jnp.abs(x) * -LOG2_E))


def _sigmoid(x):
    return 0.5 * jnp.tanh(0.5 * x) + 0.5


def _const_spec(shape):
    nd = len(shape)
    return pl.BlockSpec(shape, lambda *_: (0,) * nd, pipeline_mode=pl.Buffered(1))


COL_Q, COL_K, COL_V, COL_U, COL_GR, COL_GA, COL_GB = 0, 1, 2, 3, 4, 5, 7
Q_SCALE = HEAD_DIM ** -0.5


def _w_in_cols(j):
    return pl.BlockSpec((D_MODEL, D_ATTN), lambda *_: (0, j), pipeline_mode=pl.Buffered(1))


def _rows(block_index, size):
    if isinstance(block_index, int):
        return pl.ds(block_index * size, size)
    return pl.ds(pl.multiple_of(block_index * size, size), size)


def _gelu_tanh(x):
    k = float(np.sqrt(2.0 / np.pi))
    half_x = 0.5 * x
    return half_x + half_x * jnp.tanh(x * (k + (k * 0.044715) * (x * x)))


def _conv_seed(first, cs_ref, ext_ref):
    @pl.when(first)
    def _():
        ext_ref[0:SUBLANES, :] = jnp.zeros((SUBLANES, D_RNN), F32)
        ext_ref[SUBLANES - (CONV_WIDTH - 1):SUBLANES, :] = cs_ref[0]


def _scan_seed(first, h0_ref, hc_ref):
    @pl.when(first)
    def _():
        hc_ref[...] = jnp.broadcast_to(h0_ref[0], hc_ref.shape)


def _lru_conv(u, wc_ref, bc_ref, ext_ref):
    tt = u.shape[0]
    ext_ref[SUBLANES:SUBLANES + tt, :] = u
    uc = bc_ref[...] + wc_ref[CONV_WIDTH - 1:CONV_WIDTH, :] * u
    for d in range(1, CONV_WIDTH):
        uc = uc + wc_ref[CONV_WIDTH - 1 - d:CONV_WIDTH - d, :] * ext_ref[SUBLANES - d:SUBLANES - d + tt, :]
    ext_ref[0:SUBLANES, :] = ext_ref[tt:tt + SUBLANES, :]
    return uc


def _lru_gates(uc, wg_ref, bg_ref):
    return jnp.dot(uc.astype(BF16), wg_ref[...], preferred_element_type=F32) + bg_ref[...]


def _lru_scan(g, uc, g_rnn, lam_ref, h_prev):
    tt = uc.shape[0]
    r = _sigmoid(g[:, :D_RNN])
    ig = _sigmoid(g[:, D_RNN:])
    decay = (LRU_C * _softplus(-lam_ref[...])) * r
    a = jnp.exp2(decay * -LOG2_E)
    m2 = jnp.tanh(decay) * (a * a + 1.0)
    b = jnp.where(m2 > 0.0, m2 * lax.rsqrt(m2), 0.0) * (ig * uc)

    n_groups = tt // SUBLANES
    a3 = a.reshape(n_groups, SUBLANES, D_RNN)
    b3 = b.reshape(n_groups, SUBLANES, D_RNN)
    sub = lax.broadcasted_iota(jnp.int32, a3.shape, 1)
    k = 1
    while k < SUBLANES:
        has_prev = sub >= k
        b_prev = jnp.where(has_prev, pltpu.roll(b3, k, axis=1), 0.0)
        a_prev = jnp.where(has_prev, pltpu.roll(a3, k, axis=1), 1.0)
        b3 = b3 + a3 * b_prev
        a3 = a3 * a_prev
        k *= 2

    hs = []
    for gi in range(n_groups):
        h = b3[gi] + a3[gi] * h_prev
        hs.append(h)
        h_prev = jnp.broadcast_to(h[SUBLANES - 1:SUBLANES, :], h.shape)
    return (jnp.concatenate(hs, axis=0) * _gelu_tanh(g_rnn)).astype(BF16), h_prev


def _lru_state_specs(index_map):
    past = CONV_WIDTH - 1
    return (pl.BlockSpec((1, past, D_RNN), index_map), pl.BlockSpec((1, 1, D_RNN), index_map))


def _lru_scratch(tt):
    return [pltpu.VMEM((tt + SUBLANES, D_RNN), F32), pltpu.VMEM((SUBLANES, D_RNN), F32)]


def _proj_kernel(t, x_ref, g_ref, wq_ref, wk_ref, wv_ref, wu_ref, wgr_ref, cs_ref, h0_ref, wc_ref, bc_ref, wg_ref,
                 bg_ref, lam_ref, qb_ref, k_ref, v_ref, kb_ref, vb_ref, hg_ref, cn_ref, hl_ref, ext_ref):
    n_streams = x_ref.shape[0] // t
    past = CONV_WIDTH - 1
    xn = _rms(x_ref[...], g_ref[...]).astype(BF16)
    proj = lambda w_ref: jnp.dot(xn, w_ref[...], preferred_element_type=F32)
    u, g_rnn = proj(wu_ref), proj(wgr_ref)
    qb_ref[...] = (proj(wq_ref) * Q_SCALE).astype(BF16)
    k, v = proj(wk_ref), proj(wv_ref)
    k_ref[...] = k
    v_ref[...] = v
    kb_ref[...] = k.astype(BF16)
    vb_ref[...] = v.astype(BF16)

    ucs = []
    for s in range(n_streams):
        ext = ext_ref.at[s]
        ext[0:SUBLANES, :] = jnp.zeros((SUBLANES, D_RNN), F32)
        ext[SUBLANES - past:SUBLANES, :] = cs_ref[s]
        ucs.append(_lru_conv(u[s * t:(s + 1) * t], wc_ref, bc_ref, ext))
        cn_ref[s] = ext[SUBLANES - past:SUBLANES, :]
    uc = jnp.concatenate(ucs, axis=0)
    gates = _lru_gates(uc, wg_ref, bg_ref)
    for s in range(n_streams):
        rows = slice(s * t, (s + 1) * t)
        hg, h_last = _lru_scan(gates[rows], uc[rows], g_rnn[rows], lam_ref,
                               jnp.broadcast_to(h0_ref[s], (SUBLANES, D_RNN)))
        hg_ref[rows, :] = hg
        hl_ref[s] = h_last[0:1, :]


def _proj(x, g, w_in, conv_state, h0, lru_w):
    s, t, _ = x.shape
    n = s * t
    tm = TOKEN_TILE
    assert n % tm == 0 and tm % t == 0 and t % SUBLANES == 0
    per_tile = tm // t
    tok = lambda i: (i, 0)
    row_out = pl.BlockSpec((tm, D_ATTN), tok)
    f32o = jax.ShapeDtypeStruct((n, D_ATTN), F32)
    b16o = jax.ShapeDtypeStruct((n, D_ATTN), BF16)
    st3 = pl.BlockSpec((per_tile, CONV_WIDTH - 1, D_RNN), lambda i: (i, 0, 0))
    st1 = pl.BlockSpec((per_tile, 1, D_RNN), lambda i: (i, 0, 0))
    w_cols = (COL_Q, COL_K, COL_V, COL_U, COL_GR)
    return pl.pallas_call(
        functools.partial(_proj_kernel, t),
        grid=(n // tm,),
        in_specs=[pl.BlockSpec((tm, D_MODEL), tok), _const_spec(g.shape)] + [_w_in_cols(j) for j in w_cols]
        + [st3, st1] + [_const_spec(w.shape) for w in lru_w],
        out_specs=[row_out] * 6 + [st3, st1],
        out_shape=[b16o, f32o, f32o, b16o, b16o, b16o,
                   jax.ShapeDtypeStruct((s, CONV_WIDTH - 1, D_RNN), F32),
                   jax.ShapeDtypeStruct((s, 1, D_RNN), F32)],
        scratch_shapes=[pltpu.VMEM((per_tile, t + SUBLANES, D_RNN), F32)],
        compiler_params=pltpu.CompilerParams(dimension_semantics=("arbitrary",),
                                             vmem_limit_bytes=VMEM_LIMIT),
        name="proj",
    )(x.reshape(n, D_MODEL), g, *([w_in] * len(w_cols)), conv_state, h0, *lru_w)


def _proj_lru_kernel(per, n_tiles, x_ref, g_ref, wq_ref, wu_ref, wgr_ref, wkvt_ref, cs_ref, h0_ref, wc_ref, bc_ref,
                     wg_ref, bg_ref, lam_ref, qb_ref, kt_ref, vt_ref, ktb_ref, vtb_ref, hg_ref, cn_ref, hl_ref,
                     ext_ref, hc_ref, xn_ref, uc_ref, gr_ref, gt_ref):
    i = pl.program_id(0)
    dot_tile, scan_tile = i - 1, i - 2
    dot_live = jnp.logical_and(dot_tile >= 0, dot_tile < n_tiles)
    scan_live = jnp.logical_and(scan_tile >= 0, scan_tile < n_tiles)
    _conv_seed(jnp.logical_and(dot_live, dot_tile % per == 0), cs_ref, ext_ref)
    _scan_seed(jnp.logical_and(scan_live, scan_tile % per == 0), h0_ref, hc_ref)

    @pl.when(i == 0)
    def _():
        xn_ref[...] = jnp.zeros(xn_ref.shape, BF16)
        hc_ref[...] = jnp.zeros(hc_ref.shape, F32)
        ext_ref[...] = jnp.zeros(ext_ref.shape, F32)
        for ref in (uc_ref, gr_ref, gt_ref):
            ref[...] = jnp.zeros(ref.shape, F32)

    tm = x_ref.shape[0]
    sc, dt = i % 2, (i + 1) % 2
    xn = xn_ref[dt]
    rows_per = tm // SCAN_SLICES
    u_parts = {}
    uc_b = [None]

    def scan_slice(state, p):
        rows = pl.ds(p * rows_per, rows_per)
        hg, state[0] = _lru_scan(gt_ref[sc, rows, :], uc_ref[sc, rows, :], gr_ref[sc, rows, :], lam_ref, state[0])
        hg_ref[rows, :] = hg

    def scan_done(state):
        hc_ref[...] = state[0]

        @pl.when(jnp.logical_and(scan_live, scan_tile % per == per - 1))
        def _():
            hl_ref[0] = state[0][0:1, :]

    def rms_piece(r):
        rows = pl.ds(r * (tm // NORM_PIECES), tm // NORM_PIECES)
        xn_ref[sc, rows, :] = _rms(x_ref[rows, :], g_ref[...]).astype(BF16)

    def ug_chunk(c):
        is_u = c < D_RNN // MXU_N
        cols = slice(c * MXU_N % D_RNN, c * MXU_N % D_RNN + MXU_N)
        r = jnp.dot(xn, (wu_ref if is_u else wgr_ref)[:, cols], preferred_element_type=F32)
        if is_u:
            u_parts[c] = r
        else:
            gr_ref[dt, :, cols] = r

    def conv():
        uc = _lru_conv(jnp.concatenate([u_parts[c] for c in sorted(u_parts)], axis=1), wc_ref, bc_ref, ext_ref)
        uc_ref[dt] = uc
        uc_b[0] = uc.astype(BF16)

    def q_chunk(c):
        cols = slice(c * MXU_N, (c + 1) * MXU_N)
        qb_ref[:, cols] = (jnp.dot(xn, wq_ref[:, cols], preferred_element_type=F32) * Q_SCALE).astype(BF16)

    def kv_chunk(c):
        toks = slice(c * MXU_N, (c + 1) * MXU_N)
        kv = lax.dot_general(wkvt_ref[...], xn[toks], NT_DIMS, preferred_element_type=F32)
        k, v = kv[:D_ATTN], kv[D_ATTN:]
        kt_ref[0, :, toks] = k
        vt_ref[0, :, toks] = v
        for j in range(MXU_N // KEY_BLOCK):
            cols = slice(j * KEY_BLOCK, (j + 1) * KEY_BLOCK)
            ktb_ref[0, c * (MXU_N // KEY_BLOCK) + j] = k[:, cols].astype(BF16)
            vtb_ref[0, c * (MXU_N // KEY_BLOCK) + j] = v[:, cols].astype(BF16)

    def gates_chunk(c):
        cols = slice(c * MXU_N, (c + 1) * MXU_N)
        gt_ref[dt, :, cols] = jnp.dot(uc_b[0], wg_ref[:, cols], preferred_element_type=F32) + bg_ref[:, cols]

    @pl.when(i == 0)
    def _():
        for r in range(NORM_PIECES):
            rms_piece(r)

    @pl.when(i == n_tiles + 1)
    def _():
        state = [hc_ref[...]]
        for p in range(SCAN_SLICES):
            scan_slice(state, p)
        scan_done(state)

    @pl.when(jnp.logical_and(i > 0, i <= n_tiles))
    def _():
        state = [hc_ref[...]]
        S, P = functools.partial(scan_slice, state), functools.partial
        program = [
            P(ug_chunk, 0), P(S, 0), P(S, 1), P(ug_chunk, 1), P(S, 2), P(S, 3), P(ug_chunk, 2), conv,
            P(ug_chunk, 3), P(S, 4), P(S, 5), P(q_chunk, 0), P(S, 6), P(S, 7),
            P(q_chunk, 1), P(rms_piece, 0), P(rms_piece, 1), P(S, 8),
            P(kv_chunk, 0), P(S, 9), P(S, 10), P(S, 11), P(rms_piece, 2),
            P(kv_chunk, 1), P(S, 12), P(S, 13), P(S, 14), P(rms_piece, 3),
            P(gates_chunk, 0), P(S, 15), P(gates_chunk, 1), P(gates_chunk, 2), P(gates_chunk, 3),
        ]
        assert SCAN_SLICES == 16 and NORM_PIECES == 4
        for piece in program:
            piece()
        scan_done(state)

        @pl.when(jnp.logical_and(dot_live, dot_tile % per == per - 1))
        def _():
            cn_ref[0] = ext_ref[SUBLANES - (CONV_WIDTH - 1):SUBLANES, :]


def _proj_lru(x, g, w_in, wkvt, conv_state, h0, lru_w):
    s, t, _ = x.shape
    n = s * t
    tm = TOKEN_TILE
    assert t % tm == 0 and tm % KEY_BLOCK == 0
    per = t // tm
    n_tiles = n // tm
    tile = lambda i, lag: jnp.clip(i - lag, 0, n_tiles - 1)
    tok_in = pl.BlockSpec((tm, D_MODEL), lambda i: (tile(i, 0), 0))
    q_out = pl.BlockSpec((tm, D_ATTN), lambda i: (tile(i, 1), 0))
    hg_out = pl.BlockSpec((tm, D_RNN), lambda i: (tile(i, 2), 0))
    b16o = jax.ShapeDtypeStruct((n, D_ATTN), BF16)
    kv_spec = pl.BlockSpec((1, D_ATTN, tm), lambda i: (tile(i, 1) // per, 0, tile(i, 1) % per))
    kvb_spec = pl.BlockSpec((1, tm // KEY_BLOCK, D_ATTN, KEY_BLOCK),
                            lambda i: (tile(i, 1) // per, tile(i, 1) % per, 0, 0))
    kv_shape = jax.ShapeDtypeStruct((s, D_ATTN, t), F32)
    kvb_shape = jax.ShapeDtypeStruct((s, t // KEY_BLOCK, D_ATTN, KEY_BLOCK), BF16)
    cs_spec, _ = _lru_state_specs(lambda i: (tile(i, 1) // per, 0, 0))
    _, h_spec = _lru_state_specs(lambda i: (tile(i, 2) // per, 0, 0))
    w_cols = (COL_Q, COL_U, COL_GR)
    return pl.pallas_call(
        functools.partial(_proj_lru_kernel, per, n_tiles),
        grid=(n_tiles + 2,),
        in_specs=[tok_in, _const_spec(g.shape)] + [_w_in_cols(j) for j in w_cols] + [_const_spec(wkvt.shape)]
        + [cs_spec, h_spec] + [_const_spec(w.shape) for w in lru_w],
        out_specs=[q_out, kv_spec, kv_spec, kvb_spec, kvb_spec, hg_out, cs_spec, h_spec],
        out_shape=[b16o, kv_shape, kv_shape, kvb_shape, kvb_shape, b16o,
                   jax.ShapeDtypeStruct((s, CONV_WIDTH - 1, D_RNN), F32),
                   jax.ShapeDtypeStruct((s, 1, D_RNN), F32)],
        scratch_shapes=_lru_scratch(tm) + [pltpu.VMEM((2, tm, D_MODEL), BF16), pltpu.VMEM((2, tm, D_RNN), F32),
                                           pltpu.VMEM((2, tm, D_RNN), F32), pltpu.VMEM((2, tm, 2 * D_RNN), F32)],
        compiler_params=pltpu.CompilerParams(dimension_semantics=("arbitrary",),
                                             vmem_limit_bytes=VMEM_LIMIT),
        name="proj_lru",
    )(x.reshape(n, D_MODEL), g, *([w_in] * len(w_cols)), wkvt, conv_state, h0, *lru_w)


def _sb_split(z, mask):
    n = z.shape[1] // KEY_BLOCK
    tq = z.shape[0] // 2
    sp = _softplus(z)
    cats = []
    for h in range(2):
        for d in range(n):
            blk = sp[h * tq:(h + 1) * tq, d * KEY_BLOCK:(d + 1) * KEY_BLOCK]
            if d == n - 1 and mask is not None:
                blk = jnp.where(mask, blk, 0.0)
            cats.append(_hi_lo(blk))
    return jnp.concatenate(cats, axis=0)


def _hi_lo(blk):
    hi = blk.astype(BF16)
    lo = (blk - hi.astype(F32)).astype(BF16)
    return jnp.concatenate([hi, lo], axis=1)


def _sb_cumsum(cat, tri2):
    return jnp.dot(cat, tri2, preferred_element_type=F32)


def _sb_weights(z, r, carries, mask):
    n = z.shape[1] // KEY_BLOCK
    tq = z.shape[0] // 2
    rows, new_carries = [], []
    for h in range(2):
        carry = None if carries is None else carries[h]
        es = [None] * n
        for d in reversed(range(n)):
            rd = r[(h * n + d) * tq:(h * n + d + 1) * tq]
            x = z[h * tq:(h + 1) * tq, d * KEY_BLOCK:(d + 1) * KEY_BLOCK] + rd[:, :KEY_BLOCK]
            if carry is not None:
                x = x + carry
            e = jnp.exp(x)
            if d == n - 1 and mask is not None:
                e = jnp.where(mask, e, 0.0)
            es[d] = e.astype(BF16)
            tot = rd[:, KEY_BLOCK:]
            carry = tot if carry is None else carry + tot
        rows.append(es[0] if n == 1 else jnp.concatenate(es, axis=1))
        new_carries.append(carry)
    return jnp.concatenate(rows, axis=0), new_carries


def _tri2():
    j = np.arange(KEY_BLOCK)[:, None]
    s = np.arange(KEY_BLOCK)[None, :]
    half = np.concatenate([-(j >= s).astype(np.float32), -np.ones((KEY_BLOCK, LANES), np.float32)], axis=1)
    return jnp.asarray(np.concatenate([half, half], axis=0), dtype=BF16)


def _stack_heads(q, lo_half):
    zero = jnp.zeros_like(q)
    return jnp.concatenate([jnp.where(lo_half, q, zero), jnp.where(lo_half, zero, q)], axis=0)


def _sb_window(z_fn, pv_fn, carries, mask, tri2):
    z = z_fn()
    e, cs = _sb_weights(z, _sb_cumsum(_sb_split(z, mask), tri2), carries, mask)
    pv = pv_fn(e)
    tq = pv.shape[0] // 2
    return (pv[:tq], cs[0]), (pv[tq:], cs[1])


def _finish(store, j0, outs, step_fn):
    (a0, c0), (a1, c1) = outs
    store(a0, a1)
    if isinstance(j0, int) and j0 < 0:
        return lambda: None

    def alive(x0, x1):
        return jnp.max(jnp.maximum(x0, x1)) >= LOG_KEEP_FLOOR

    go = jnp.logical_and(j0 >= 0, alive(c0, c1))

    def walk():
        @pl.when(go)
        def _():
            def body(s):
                j, b0, b1, d0, d1, _ = s
                (p0, d0), (p1, d1) = step_fn(j, [d0, d1])
                return j - 1, b0 + p0, b1 + p1, d0, d1, jnp.logical_and(j >= 1, alive(d0, d1))

            _, b0, b1, _, _, _ = lax.while_loop(lambda s: s[-1], body,
                                                (jnp.int32(j0), a0, a1, c0, c1, jnp.bool_(True)))
            store(b0, b1)

    return walk


def _sb_chains(chains, tri2):
    zs = [c[0]() for c in chains]
    sums = [_sb_cumsum(_sb_split(z, c[2]), tri2) for z, c in zip(zs, chains)]
    weights = [_sb_weights(z, r, None, c[2]) for z, r, c in zip(zs, sums, chains)]
    walks = []
    for c, (e, cs) in zip(chains, weights):
        pv = c[1](e)
        tq = pv.shape[0] // 2
        walks.append(_finish(c[3], c[5], ((pv[:tq], cs[0]), (pv[tq:], cs[1])), c[4]))
    for walk in walks:
        walk()


def _attn_prompt_kernel(q_ref, kt_ref, vt_ref, tri_ref, o_ref):
    t = q_ref.shape[1]
    nq = t // Q_ROWS
    row = lax.broadcasted_iota(jnp.int32, (Q_ROWS, LANES), 0)
    lane = lax.broadcasted_iota(jnp.int32, (Q_ROWS, LANES), 1)
    lo_half = lane < HEAD_DIM
    diag_mask = lane < row
    lane_minus_row = lane - row
    tri2 = tri_ref[...]

    def window(ref, first, n):
        return jnp.concatenate([ref[0, first + d] for d in range(n)], axis=1)

    def qblocks(blocks, n_fast):
        chains = []
        for i in blocks:
            q2 = _stack_heads(q_ref[0, _rows(i, Q_ROWS), :], lo_half)
            first = i + 1 - n_fast

            def z_fn(q2=q2, first=first):
                return jnp.dot(q2, window(kt_ref, first, n_fast), preferred_element_type=F32)

            def pv_fn(e, first=first):
                return lax.dot_general(e, window(vt_ref, first, n_fast), NT_DIMS, preferred_element_type=F32)

            def store(b0, b1, i=i):
                o_ref[_rows(i, Q_ROWS), :] = jnp.where(lo_half, b0, b1).astype(BF16)

            def step(j, cs, i=i, q2=q2):
                m = lane_minus_row < (i - j) * KEY_BLOCK
                return _sb_window(lambda: jnp.dot(q2, kt_ref[0, j], preferred_element_type=F32),
                                  lambda e: lax.dot_general(e, vt_ref[0, j], NT_DIMS, preferred_element_type=F32),
                                  cs, m, tri2)

            chains.append((z_fn, pv_fn, diag_mask, store, step, i - n_fast))
        _sb_chains(chains, tri2)

    half = Q_ROWS // 2
    bottom = row >= half

    def qblocks_tapered(blocks):
        assert FAST_BLOCKS == 3
        chains = []
        for i in blocks:
            q2 = _stack_heads(q_ref[0, _rows(i, Q_ROWS), :], lo_half)
            q2_top = jnp.concatenate([q2[:half], q2[Q_ROWS:Q_ROWS + half]], axis=0)
            z_new = jnp.dot(q2, window(kt_ref, i - 1, 2), preferred_element_type=F32)
            z_old = jnp.dot(q2_top, kt_ref[0, i - 2], preferred_element_type=F32)
            chains.append([i, q2, z_new, z_old])
        for c in chains:
            _, _, z_new, z_old = c
            sp_new, sp_old = _softplus(z_new), _softplus(z_old)
            cats = []
            for h in range(2):
                rows = slice(h * Q_ROWS, (h + 1) * Q_ROWS)
                cats.append(_hi_lo(sp_new[rows, :KEY_BLOCK]))
                cats.append(_hi_lo(jnp.where(diag_mask, sp_new[rows, KEY_BLOCK:], 0.0)))
            for h in range(2):
                cats.append(_hi_lo(sp_old[h * half:(h + 1) * half]))
            c.append(_sb_cumsum(jnp.concatenate(cats, axis=0), tri2))
        walks = []
        for i, q2, z_new, z_old, r in chains:
            es, carries = [], []
            for h in range(2):
                rows = slice(h * Q_ROWS, (h + 1) * Q_ROWS)
                r_prev = r[2 * h * Q_ROWS:(2 * h + 1) * Q_ROWS]
                r_diag = r[(2 * h + 1) * Q_ROWS:(2 * h + 2) * Q_ROWS]
                r_old = r[4 * Q_ROWS + h * half:4 * Q_ROWS + (h + 1) * half]
                e_diag = jnp.where(diag_mask, jnp.exp(z_new[rows, KEY_BLOCK:] + r_diag[:, :KEY_BLOCK]), 0.0)
                carry = r_diag[:, KEY_BLOCK:]
                e_prev = jnp.exp(z_new[rows, :KEY_BLOCK] + r_prev[:, :KEY_BLOCK] + carry)
                carry = carry + r_prev[:, KEY_BLOCK:]
                e_old = jnp.exp(z_old[h * half:(h + 1) * half] + r_old[:, :KEY_BLOCK] + carry[:half])
                carries.append(jnp.concatenate([carry[:half] + r_old[:, KEY_BLOCK:], carry[half:]], axis=0))
                e_old = jnp.concatenate([e_old.astype(BF16), jnp.zeros((half, KEY_BLOCK), BF16)], axis=0)
                es.append(jnp.concatenate([e_old, e_prev.astype(BF16), e_diag.astype(BF16)], axis=1))
            pv = lax.dot_general(jnp.concatenate(es, axis=0), window(vt_ref, i - 2, FAST_BLOCKS), NT_DIMS,
                                 preferred_element_type=F32)

            def store(b0, b1, i=i):
                o_ref[_rows(i, Q_ROWS), :] = jnp.where(lo_half, b0, b1).astype(BF16)

            def step(j, cs, i=i, q2=q2):
                m = jnp.logical_and(lane_minus_row < (i - j) * KEY_BLOCK, jnp.logical_or(bottom, j < i - 2))
                return _sb_window(lambda: jnp.dot(q2, kt_ref[0, j], preferred_element_type=F32),
                                  lambda e: lax.dot_general(e, vt_ref[0, j], NT_DIMS, preferred_element_type=F32),
                                  cs, m, tri2)

            walks.append(_finish(store, i - 2, ((pv[:Q_ROWS], carries[0]), (pv[Q_ROWS:], carries[1])), step))
        for walk in walks:
            walk()

    n_head = FAST_BLOCKS - 1
    for i in range(n_head):
        qblocks([i], i + 1)

    start = n_head + (nq - n_head) % Q_GROUP
    if start > n_head:
        qblocks_tapered(list(range(n_head, start)))

    def group(g, carry):
        qblocks_tapered([start + g * Q_GROUP + u for u in range(Q_GROUP)])
        return carry

    lax.fori_loop(0, (nq - start) // Q_GROUP, group, 0)


def _attn_prompt(qb, ktb, vtb, tri2):
    b, t, _ = qb.shape
    assert t % Q_ROWS == 0 and Q_ROWS == KEY_BLOCK
    qspec = pl.BlockSpec((1, t, LANES), lambda i, h: (i, 0, h))
    kspec = pl.BlockSpec((1, t // KEY_BLOCK, LANES, KEY_BLOCK), lambda i, h: (i, 0, h, 0))
    return pl.pallas_call(
        _attn_prompt_kernel,
        grid=(b, D_ATTN // LANES),
        in_specs=[qspec, kspec, kspec, _const_spec(tri2.shape)],
        out_specs=pl.BlockSpec((t, LANES), lambda i, h: (i, h)),
        out_shape=jax.ShapeDtypeStruct((b * t, D_ATTN), BF16),
        compiler_params=pltpu.CompilerParams(dimension_semantics=("arbitrary", "arbitrary"),
                                             vmem_limit_bytes=VMEM_LIMIT),
        name="attn_prompt",
    )(qb, ktb, vtb, tri2)


def _attn_sample_kernel(n_tail, q_ref, kn_ref, vn_ref, ckt_ref, cvt_ref, ck_hbm, cv_hbm, tri_ref, o_ref,
                        kbuf, vbuf):
    n_streams = q_ref.shape[0]
    tq = q_ref.shape[1]
    n_past = ck_hbm.shape[2] // KEY_BLOCK
    row = lax.broadcasted_iota(jnp.int32, (tq, LANES), 0)
    lane = lax.broadcasted_iota(jnp.int32, (tq, LANES), 1)
    lo_half = lane < HEAD_DIM
    tri2 = tri_ref[...]

    pad = jnp.zeros((KEY_BLOCK - tq, LANES), BF16)
    chains = []
    for b in range(n_streams):
        s = pl.program_id(0) * n_streams + b
        for p in range(q_ref.shape[2] // LANES):
            cols = slice(p * LANES, (p + 1) * LANES)
            q2 = _stack_heads(q_ref[b, :, cols], lo_half)

            def z_fn(q2=q2, cols=cols, b=b):
                z_new = lax.dot_general(q2, jnp.concatenate([kn_ref[b, :, cols], pad], axis=0), NT_DIMS,
                                        preferred_element_type=F32)
                if n_tail == 0:
                    return z_new
                z_past = jnp.dot(q2, ckt_ref[b, cols, :].astype(BF16), preferred_element_type=F32)
                return jnp.concatenate([z_past, z_new], axis=1)

            def pv_fn(e, cols=cols, b=b):
                pv = jnp.dot(e[:, n_tail * KEY_BLOCK:], jnp.concatenate([vn_ref[b, :, cols], pad], axis=0),
                             preferred_element_type=F32)
                if n_tail == 0:
                    return pv
                return pv + lax.dot_general(e[:, :n_tail * KEY_BLOCK], cvt_ref[b, cols, :].astype(BF16), NT_DIMS,
                                            preferred_element_type=F32)

            def store(a0, a1, cols=cols, b=b):
                o_ref[b * tq:(b + 1) * tq, cols] = jnp.where(lo_half, a0, a1).astype(BF16)

            def step(j, cs, q2=q2, p=p, s=s):
                keys = pl.ds(pl.multiple_of(j * KEY_BLOCK, KEY_BLOCK), KEY_BLOCK)
                pltpu.sync_copy(ck_hbm.at[s, pl.ds(p * LANES, LANES), keys], kbuf)
                pltpu.sync_copy(cv_hbm.at[s, pl.ds(p * LANES, LANES), keys], vbuf)
                return _sb_window(lambda: jnp.dot(q2, kbuf[...].astype(BF16), preferred_element_type=F32),
                                  lambda e: lax.dot_general(e, vbuf[...].astype(BF16), NT_DIMS,
                                                            preferred_element_type=F32),
                                  cs, None, tri2)

            chains.append((z_fn, pv_fn, lane < row, store, step, n_past - 1 - n_tail))
    _sb_chains(chains, tri2)


def _attn_sample(qb, knb, vnb, cache_kt, cache_vt, tri2):
    s, tq, _ = qb.shape
    past_len = cache_kt.shape[2]
    assert past_len % KEY_BLOCK == 0 and tq <= KEY_BLOCK and tq % 16 == 0
    n_tail = min(FAST_BLOCKS - 1, past_len // KEY_BLOCK)
    assert n_tail > 0 and past_len % (n_tail * KEY_BLOCK) == 0
    per = SAMPLE_STREAMS if s % SAMPLE_STREAMS == 0 else 1
    new = pl.BlockSpec((per, tq, D_ATTN), lambda i: (i, 0, 0))
    tail = pl.BlockSpec((per, D_ATTN, n_tail * KEY_BLOCK), lambda i: (i, 0, past_len // (n_tail * KEY_BLOCK) - 1))
    hbm = pl.BlockSpec(memory_space=pl.ANY)
    return pl.pallas_call(
        functools.partial(_attn_sample_kernel, n_tail),
        grid=(s // per,),
        in_specs=[new, new, new, tail, tail, hbm, hbm, _const_spec(tri2.shape)],
        out_specs=pl.BlockSpec((per * tq, D_ATTN), lambda i: (i, 0)),
        out_shape=jax.ShapeDtypeStruct((s * tq, D_ATTN), BF16),
        scratch_shapes=[pltpu.VMEM((LANES, KEY_BLOCK), F32), pltpu.VMEM((LANES, KEY_BLOCK), F32)],
        compiler_params=pltpu.CompilerParams(dimension_semantics=("arbitrary",),
                                             vmem_limit_bytes=VMEM_LIMIT),
        name="attn_sample",
    )(qb, knb, vnb, cache_kt, cache_vt, cache_kt, cache_vt, tri2)


FF_CHUNK = 1024
STAGE_BYTES = 1024 * 1024


def _stream_cast(src_hbm, dst_ref, stage_ref, sem):
    rows = stage_ref.shape[1]
    n = src_hbm.shape[0] // rows
    assert src_hbm.shape[0] % rows == 0

    def copy(c, slot):
        return pltpu.make_async_copy(src_hbm.at[pl.ds(c * rows, rows)], stage_ref.at[slot], sem.at[slot])

    copy(0, 0).start()

    def body(c, carry):
        slot = c % 2

        @pl.when(c + 1 < n)
        def _():
            copy(c + 1, 1 - slot).start()

        copy(c, slot).wait()
        dst_ref[pl.ds(pl.multiple_of(c * rows, rows), rows), :] = stage_ref[slot].astype(BF16)
        return carry

    lax.fori_loop(0, n, body, 0)


def _out_kernel(n_first, xa_ref, xb_ref, oa_ref, ob_ref, hga_ref, hgb_ref, g1_ref, wga0_ref, wga1_ref, wgb0_ref,
                wgb1_ref, wa_hbm, wb_hbm, wo_hbm, g2_ref, g3_ref, wup_hbm, wdn_hbm, g4_ref, ya_ref, yb_ref,
                wa_ref, wb_ref, wo_ref, wup_ref, wdn_ref, wide_stage, stage, wide_sem, sem):
    @pl.when(pl.program_id(0) == 0)
    def _():
        _stream_cast(wa_hbm, wa_ref, stage, sem)
        _stream_cast(wb_hbm, wb_ref, stage, sem)
        _stream_cast(wo_hbm, wo_ref, stage, sem)
        _stream_cast(wup_hbm, wup_ref, wide_stage, wide_sem)
        _stream_cast(wdn_hbm, wdn_ref, stage, sem)

    first = pl.program_id(0) < n_first
    tm = xa_ref.shape[0]
    halves = [pl.ds(0, tm // 2), pl.ds(tm // 2, tm // 2)]
    n_ff = D_FF // FF_CHUNK
    st = [{}, {}]

    def pick(a_ref, b_ref, h):
        return jnp.where(first, a_ref[halves[h], :], b_ref[halves[h], :])

    def norm_in(h):
        st[h]["x"] = pick(xa_ref, xb_ref, h)
        st[h]["xn"] = _rms(st[h]["x"], g1_ref[...]).astype(BF16)

    def gate(h, w0_ref, w1_ref):
        return jnp.concatenate([jnp.dot(st[h]["xn"], w_ref[...], preferred_element_type=F32)
                                for w_ref in (w0_ref, w1_ref)], axis=1)

    def attn_gate(h):
        g_a = gate(h, wga0_ref, wga1_ref)
        y_a = jnp.dot(pick(oa_ref, ob_ref, h), wa_ref[...], preferred_element_type=F32)
        st[h]["m"] = jax.nn.sigmoid(g_a) * y_a

    def rnn_gate(h):
        g_b = gate(h, wgb0_ref, wgb1_ref)
        y_b = jnp.dot(pick(hga_ref, hgb_ref, h), wb_ref[...], preferred_element_type=F32)
        st[h]["m"] = st[h]["m"] + jax.nn.sigmoid(g_b) * y_b

    def mix(h):
        st[h]["mix"] = jnp.dot(st[h]["m"].astype(BF16), wo_ref[...], preferred_element_type=F32)

    def norm_mid(h):
        st[h]["x1"] = st[h]["x"] + _rms(st[h]["mix"], g2_ref[...])
        st[h]["f"] = _rms(st[h]["x1"], g3_ref[...]).astype(BF16)

    def ffn(h, c):
        cols = slice(c * FF_CHUNK, (c + 1) * FF_CHUNK)
        up = jnp.maximum(jnp.dot(st[h]["f"], wup_ref[:, cols], preferred_element_type=F32), 0.0)
        dn = jnp.dot((up * up).astype(BF16), wdn_ref[cols, :], preferred_element_type=F32)
        st[h]["acc"] = dn if c == 0 else st[h]["acc"] + dn

    def norm_out(h):
        yb_ref[halves[h], :] = st[h]["x1"] + _rms(st[h]["acc"], g4_ref[...])

    norm_in(0)
    norm_in(1)
    attn_gate(0)
    rnn_gate(0)
    mix(0)
    attn_gate(1)
    norm_mid(0)
    rnn_gate(1)
    mix(1)
    ffn(0, 0)
    norm_mid(1)
    for c in range(1, n_ff):
        ffn(0, c)
    ffn(1, 0)
    norm_out(0)
    for c in range(1, n_ff):
        ffn(1, c)
    norm_out(1)

    @pl.when(first)
    def _():
        ya_ref[...] = yb_ref[...]


def _out(xa, xb, oa, ob, hga, hgb, g1, w_in, wa, wb, wo, g2, g3, wup, wdn, g4):
    na, nb = xa.shape[0], xb.shape[0]
    tm = TOKEN_TILE
    assert na % tm == 0 and nb % tm == 0
    n_first = na // tm
    a_tok = lambda i: (jnp.minimum(i, n_first - 1), 0)
    b_tok = lambda i: (jnp.maximum(i - n_first, 0), 0)
    w_cols = (COL_GA, COL_GA + 1, COL_GB, COL_GB + 1)
    consts = (wa, wb, wo, g2, g3, wup, wdn, g4)
    big = (wa, wb, wo, wup, wdn)
    assert all(w.dtype == F32 for w in big)
    hbm = pl.BlockSpec(memory_space=pl.ANY)
    stage_rows = STAGE_BYTES // (4 * D_MODEL)
    wide_rows = STAGE_BYTES // (4 * D_FF)
    return pl.pallas_call(
        functools.partial(_out_kernel, n_first),
        grid=((na + nb) // tm,),
        in_specs=[pl.BlockSpec((tm, D_MODEL), a_tok), pl.BlockSpec((tm, D_MODEL), b_tok),
                  pl.BlockSpec((tm, D_ATTN), a_tok), pl.BlockSpec((tm, D_ATTN), b_tok),
                  pl.BlockSpec((tm, D_RNN), a_tok), pl.BlockSpec((tm, D_RNN), b_tok), _const_spec(g1.shape)]
        + [_w_in_cols(j) for j in w_cols]
        + [hbm if any(c is w for w in big) else _const_spec(c.shape) for c in consts],
        out_specs=[pl.BlockSpec((tm, D_MODEL), a_tok), pl.BlockSpec((tm, D_MODEL), b_tok)],
        out_shape=[jax.ShapeDtypeStruct((na, D_MODEL), F32), jax.ShapeDtypeStruct((nb, D_MODEL), F32)],
        scratch_shapes=[pltpu.VMEM(w.shape, BF16) for w in big]
        + [pltpu.VMEM((2, wide_rows, D_FF), F32), pltpu.VMEM((2, stage_rows, D_MODEL), F32),
           pltpu.SemaphoreType.DMA((2,)), pltpu.SemaphoreType.DMA((2,))],
        compiler_params=pltpu.CompilerParams(dimension_semantics=("arbitrary",),
                                             vmem_limit_bytes=VMEM_LIMIT),
        name="out",
    )(xa, xb, oa, ob, hga, hgb, g1, *([w_in] * len(w_cols)), *consts)


def _block_diag(w):
    n, c, d = w.shape
    return jnp.einsum("ncd,nm->ncmd", w, jnp.eye(n, dtype=w.dtype)).reshape(n * c, n * d)


def _heads_last(kt):
    s, _, t = kt.shape
    return kt.reshape(s, N_HEADS, HEAD_DIM, t).transpose(0, 3, 1, 2)[None]


def _time_last(cache):
    s, t = cache.shape[:2]
    return cache.transpose(0, 2, 3, 1).reshape(s, D_ATTN, t)


def kernel(x_prompt, x_sample, cache_k, cache_v, state_conv, state_h, w_in, g_pre_mix, w_conv, b_conv, w_r, b_r, w_i, b_i, lam, w_a_out, w_b_out, w_o, g_post_mix, g_pre_ffn, w_up, w_down, g_post_ffn):
    assert w_in.shape[0] == 1
    row = lambda a: a[0].reshape(1, -1)
    assert w_in.shape[2] == (COL_GB + 2) * D_ATTN
    w_in_b = w_in[0].astype(BF16)
    wkv_t = w_in[0][:, COL_K * D_ATTN:(COL_V + 1) * D_ATTN].T.astype(BF16)
    wg = jnp.concatenate([_block_diag(w_r[0]), _block_diag(w_i[0])], axis=1).astype(BF16)
    bg = jnp.concatenate([row(b_r), row(b_i)], axis=1)
    lru_w = (w_conv[0], row(b_conv), wg, bg, row(lam))
    g1 = row(g_pre_mix)
    tri2 = _tri2()

    bp, tp, _ = x_prompt.shape
    bs, ts, _ = x_sample.shape
    n_p, n_s = bp * tp, bs * ts

    zc = jnp.zeros((bp, CONV_WIDTH - 1, D_RNN), F32)
    zh = jnp.zeros((bp, 1, D_RNN), F32)
    qb, kt, vt, ktb, vtb, hg_p, cp, hp = _proj_lru(x_prompt, g1, w_in_b, wkv_t, zc, zh, lru_w)
    o_p = _attn_prompt(qb.reshape(bp, tp, D_ATTN), ktb, vtb, tri2)

    s3 = lambda a: a.reshape(bs, ts, a.shape[-1])
    qs, ks, vs, ksb, vsb, hg_s, cs, hs = _proj(x_sample, g1, w_in_b, state_conv[0],
                                               state_h[0].reshape(bs, 1, D_RNN), lru_w)
    o_s = _attn_sample(s3(qs), s3(ksb), s3(vsb), _time_last(cache_k[0]), _time_last(cache_v[0]), tri2)

    ys, yp = _out(x_sample.reshape(n_s, D_MODEL), x_prompt.reshape(n_p, D_MODEL), o_s, o_p, hg_s, hg_p, g1, w_in_b,
                  w_a_out[0], w_b_out[0], w_o[0], row(g_post_mix), row(g_pre_ffn), w_up[0], w_down[0],
                  row(g_post_ffn))
    return (yp.reshape(bp, tp, D_MODEL), ys.reshape(bs, ts, D_MODEL), _heads_last(kt), _heads_last(vt),
            cp[None], hp.reshape(1, bp, D_RNN), ks.reshape(1, bs, ts, N_HEADS, HEAD_DIM),
            vs.reshape(1, bs, ts, N_HEADS, HEAD_DIM), cs[None], hs.reshape(1, bs, D_RNN))
```

```python
import functools

import jax
import jax.numpy as jnp
import numpy as np
from jax import lax
from jax.experimental import pallas as pl
from jax.experimental.pallas import tpu as pltpu

F32 = jnp.float32
BF16 = jnp.bfloat16

D_MODEL = 1024
N_HEADS = 8
HEAD_DIM = 64
D_ATTN = N_HEADS * HEAD_DIM
D_RNN = 512
N_RNN_BLOCKS = 8
CONV_WIDTH = 4
LRU_C = 8.0
D_FF = 4 * D_MODEL
EPS = 1e-6

LANES = 128
SUBLANES = 8
KEY_BLOCK = 128
Q_ROWS = 128
FAST_BLOCKS = 3
Q_GROUP = 10
SAMPLE_STREAMS = 4
TOKEN_TILE = 512
SCAN_SLICES = 16
NORM_PIECES = 4
MXU_N = 256
LOG_KEEP_FLOOR = -104.0
VMEM_LIMIT = 58 * 1024 * 1024

NT_DIMS = (((1,), (1,)), ((), ()))
LOG2_E = 1.4426950408889634


def _rms(xf, g):
    return xf * lax.rsqrt(jnp.mean(xf * xf, axis=-1, keepdims=True) + EPS) * g


def _softplus(x):
    return jnp.maximum(x, 0.0) + jnp.log(1.0 + jnp.exp2(jnp.abs(x) * -LOG2_E))


def _sigmoid(x):
    return 0.5 * jnp.tanh(0.5 * x) + 0.5


def _const_spec(shape):
    nd = len(shape)
    return pl.BlockSpec(shape, lambda *_: (0,) * nd, pipeline_mode=pl.Buffered(1))


COL_Q, COL_K, COL_V, COL_U, COL_GR, COL_GA, COL_GB = 0, 1, 2, 3, 4, 5, 7
Q_SCALE = HEAD_DIM ** -0.5


def _w_in_cols(j):
    return pl.BlockSpec((D_MODEL, D_ATTN), lambda *_: (0, j), pipeline_mode=pl.Buffered(1))


def _rows(block_index, size):
    if isinstance(block_index, int):
        return pl.ds(block_index * size, size)
    return pl.ds(pl.multiple_of(block_index * size, size), size)


def _gelu_tanh(x):
    k = float(np.sqrt(2.0 / np.pi))
    half_x = 0.5 * x
    return half_x + half_x * jnp.tanh(x * (k + (k * 0.044715) * (x * x)))


def _conv_seed(first, cs_ref, ext_ref):
    @pl.when(first)
    def _():
        ext_ref[0:SUBLANES, :] = jnp.zeros((SUBLANES, D_RNN), F32)
        ext_ref[SUBLANES - (CONV_WIDTH - 1):SUBLANES, :] = cs_ref[0]


def _scan_seed(first, h0_ref, hc_ref):
    @pl.when(first)
    def _():
        hc_ref[...] = jnp.broadcast_to(h0_ref[0], hc_ref.shape)


def _lru_conv(u, wc_ref, bc_ref, ext_ref):
    tt = u.shape[0]
    ext_ref[SUBLANES:SUBLANES + tt, :] = u
    uc = bc_ref[...] + wc_ref[CONV_WIDTH - 1:CONV_WIDTH, :] * u
    for d in range(1, CONV_WIDTH):
        uc = uc + wc_ref[CONV_WIDTH - 1 - d:CONV_WIDTH - d, :] * ext_ref[SUBLANES - d:SUBLANES - d + tt, :]
    ext_ref[0:SUBLANES, :] = ext_ref[tt:tt + SUBLANES, :]
    return uc


def _lru_gates(uc, wg_ref, bg_ref):
    return jnp.dot(uc.astype(BF16), wg_ref[...], preferred_element_type=F32) + bg_ref[...]


def _lru_scan(g, uc, g_rnn, lam_ref, h_prev):
    tt = uc.shape[0]
    r = _sigmoid(g[:, :D_RNN])
    ig = _sigmoid(g[:, D_RNN:])
    decay = (LRU_C * _softplus(-lam_ref[...])) * r
    a = jnp.exp2(decay * -LOG2_E)
    m2 = jnp.tanh(decay) * (a * a + 1.0)
    b = jnp.where(m2 > 0.0, m2 * lax.rsqrt(m2), 0.0) * (ig * uc)

    n_groups = tt // SUBLANES
    a3 = a.reshape(n_groups, SUBLANES, D_RNN)
    b3 = b.reshape(n_groups, SUBLANES, D_RNN)
    sub = lax.broadcasted_iota(jnp.int32, a3.shape, 1)
    k = 1
    while k < SUBLANES:
        has_prev = sub >= k
        b_prev = jnp.where(has_prev, pltpu.roll(b3, k, axis=1), 0.0)
        a_prev = jnp.where(has_prev, pltpu.roll(a3, k, axis=1), 1.0)
        b3 = b3 + a3 * b_prev
        a3 = a3 * a_prev
        k *= 2

    hs = []
    for gi in range(n_groups):
        h = b3[gi] + a3[gi] * h_prev
        hs.append(h)
        h_prev = jnp.broadcast_to(h[SUBLANES - 1:SUBLANES, :], h.shape)
    return (jnp.concatenate(hs, axis=0) * _gelu_tanh(g_rnn)).astype(BF16), h_prev


def _lru_state_specs(index_map):
    past = CONV_WIDTH - 1
    return (pl.BlockSpec((1, past, D_RNN), index_map), pl.BlockSpec((1, 1, D_RNN), index_map))


def _lru_scratch(tt):
    return [pltpu.VMEM((tt + SUBLANES, D_RNN), F32), pltpu.VMEM((SUBLANES, D_RNN), F32)]


def _proj_kernel(t, x_ref, g_ref, wq_ref, wk_ref, wv_ref, wu_ref, wgr_ref, cs_ref, h0_ref, wc_ref, bc_ref, wg_ref,
                 bg_ref, lam_ref, qb_ref, k_ref, v_ref, kb_ref, vb_ref, hg_ref, cn_ref, hl_ref, ext_ref):
    n_streams = x_ref.shape[0] // t
    past = CONV_WIDTH - 1
    xn = _rms(x_ref[...], g_ref[...]).astype(BF16)
    proj = lambda w_ref: jnp.dot(xn, w_ref[...], preferred_element_type=F32)
    u, g_rnn = proj(wu_ref), proj(wgr_ref)
    qb_ref[...] = (proj(wq_ref) * Q_SCALE).astype(BF16)
    k, v = proj(wk_ref), proj(wv_ref)
    k_ref[...] = k
    v_ref[...] = v
    kb_ref[...] = k.astype(BF16)
    vb_ref[...] = v.astype(BF16)

    ucs = []
    for s in range(n_streams):
        ext = ext_ref.at[s]
        ext[0:SUBLANES, :] = jnp.zeros((SUBLANES, D_RNN), F32)
        ext[SUBLANES - past:SUBLANES, :] = cs_ref[s]
        ucs.append(_lru_conv(u[s * t:(s + 1) * t], wc_ref, bc_ref, ext))
        cn_ref[s] = ext[SUBLANES - past:SUBLANES, :]
    uc = jnp.concatenate(ucs, axis=0)
    gates = _lru_gates(uc, wg_ref, bg_ref)
    for s in range(n_streams):
        rows = slice(s * t, (s + 1) * t)
        hg, h_last = _lru_scan(gates[rows], uc[rows], g_rnn[rows], lam_ref,
                               jnp.broadcast_to(h0_ref[s], (SUBLANES, D_RNN)))
        hg_ref[rows, :] = hg
        hl_ref[s] = h_last[0:1, :]


def _proj(x, g, w_in, conv_state, h0, lru_w):
    s, t, _ = x.shape
    n = s * t
    tm = TOKEN_TILE
    assert n % tm == 0 and tm % t == 0 and t % SUBLANES == 0
    per_tile = tm // t
    tok = lambda i: (i, 0)
    row_out = pl.BlockSpec((tm, D_ATTN), tok)
    f32o = jax.ShapeDtypeStruct((n, D_ATTN), F32)
    b16o = jax.ShapeDtypeStruct((n, D_ATTN), BF16)
    st3 = pl.BlockSpec((per_tile, CONV_WIDTH - 1, D_RNN), lambda i: (i, 0, 0))
    st1 = pl.BlockSpec((per_tile, 1, D_RNN), lambda i: (i, 0, 0))
    w_cols = (COL_Q, COL_K, COL_V, COL_U, COL_GR)
    return pl.pallas_call(
        functools.partial(_proj_kernel, t),
        grid=(n // tm,),
        in_specs=[pl.BlockSpec((tm, D_MODEL), tok), _const_spec(g.shape)] + [_w_in_cols(j) for j in w_cols]
        + [st3, st1] + [_const_spec(w.shape) for w in lru_w],
        out_specs=[row_out] * 6 + [st3, st1],
        out_shape=[b16o, f32o, f32o, b16o, b16o, b16o,
                   jax.ShapeDtypeStruct((s, CONV_WIDTH - 1, D_RNN), F32),
                   jax.ShapeDtypeStruct((s, 1, D_RNN), F32)],
        scratch_shapes=[pltpu.VMEM((per_tile, t + SUBLANES, D_RNN), F32)],
        compiler_params=pltpu.CompilerParams(dimension_semantics=("arbitrary",),
                                             vmem_limit_bytes=VMEM_LIMIT),
        name="proj",
    )(x.reshape(n, D_MODEL), g, *([w_in] * len(w_cols)), conv_state, h0, *lru_w)


def _proj_lru_kernel(per, n_tiles, x_ref, g_ref, wq_ref, wu_ref, wgr_ref, wkvt_ref, cs_ref, h0_ref, wc_ref, bc_ref,
                     wg_ref, bg_ref, lam_ref, qb_ref, kt_ref, vt_ref, ktb_ref, vtb_ref, hg_ref, cn_ref, hl_ref,
                     ext_ref, hc_ref, xn_ref, uc_ref, gr_ref, gt_ref):
    i = pl.program_id(0)
    dot_tile, scan_tile = i - 1, i - 2
    dot_live = jnp.logical_and(dot_tile >= 0, dot_tile < n_tiles)
    scan_live = jnp.logical_and(scan_tile >= 0, scan_tile < n_tiles)
    _conv_seed(jnp.logical_and(dot_live, dot_tile % per == 0), cs_ref, ext_ref)
    _scan_seed(jnp.logical_and(scan_live, scan_tile % per == 0), h0_ref, hc_ref)

    @pl.when(i == 0)
    def _():
        xn_ref[...] = jnp.zeros(xn_ref.shape, BF16)
        hc_ref[...] = jnp.zeros(hc_ref.shape, F32)
        ext_ref[...] = jnp.zeros(ext_ref.shape, F32)
        for ref in (uc_ref, gr_ref, gt_ref):
            ref[...] = jnp.zeros(ref.shape, F32)

    tm = x_ref.shape[0]
    sc, dt = i % 2, (i + 1) % 2
    xn = xn_ref[dt]
    rows_per = tm // SCAN_SLICES
    u_parts = {}
    uc_b = [None]

    def scan_slice(state, p):
        rows = pl.ds(p * rows_per, rows_per)
        hg, state[0] = _lru_scan(gt_ref[sc, rows, :], uc_ref[sc, rows, :], gr_ref[sc, rows, :], lam_ref, state[0])
        hg_ref[rows, :] = hg

    def scan_done(state):
        hc_ref[...] = state[0]

        @pl.when(jnp.logical_and(scan_live, scan_tile % per == per - 1))
        def _():
            hl_ref[0] = state[0][0:1, :]

    def rms_piece(r):
        rows = pl.ds(r * (tm // NORM_PIECES), tm // NORM_PIECES)
        xn_ref[sc, rows, :] = _rms(x_ref[rows, :], g_ref[...]).astype(BF16)

    def ug_chunk(c):
        is_u = c < D_RNN // MXU_N
        cols = slice(c * MXU_N % D_RNN, c * MXU_N % D_RNN + MXU_N)
        r = jnp.dot(xn, (wu_ref if is_u else wgr_ref)[:, cols], preferred_element_type=F32)
        if is_u:
            u_parts[c] = r
        else:
            gr_ref[dt, :, cols] = r

    def conv():
        uc = _lru_conv(jnp.concatenate([u_parts[c] for c in sorted(u_parts)], axis=1), wc_ref, bc_ref, ext_ref)
        uc_ref[dt] = uc
        uc_b[0] = uc.astype(BF16)

    def q_chunk(c):
        cols = slice(c * MXU_N, (c + 1) * MXU_N)
        qb_ref[:, cols] = (jnp.dot(xn, wq_ref[:, cols], preferred_element_type=F32) * Q_SCALE).astype(BF16)

    def kv_chunk(c):
        toks = slice(c * MXU_N, (c + 1) * MXU_N)
        kv = lax.dot_general(wkvt_ref[...], xn[toks], NT_DIMS, preferred_element_type=F32)
        k, v = kv[:D_ATTN], kv[D_ATTN:]
        kt_ref[0, :, toks] = k
        vt_ref[0, :, toks] = v
        for j in range(MXU_N // KEY_BLOCK):
            cols = slice(j * KEY_BLOCK, (j + 1) * KEY_BLOCK)
            ktb_ref[0, c * (MXU_N // KEY_BLOCK) + j] = k[:, cols].astype(BF16)
            vtb_ref[0, c * (MXU_N // KEY_BLOCK) + j] = v[:, cols].astype(BF16)

    def gates_chunk(c):
        cols = slice(c * MXU_N, (c + 1) * MXU_N)
        gt_ref[dt, :, cols] = jnp.dot(uc_b[0], wg_ref[:, cols], preferred_element_type=F32) + bg_ref[:, cols]

    @pl.when(i == 0)
    def _():
        for r in range(NORM_PIECES):
            rms_piece(r)

    @pl.when(i == n_tiles + 1)
    def _():
        state = [hc_ref[...]]
        for p in range(SCAN_SLICES):
            scan_slice(state, p)
        scan_done(state)

    @pl.when(jnp.logical_and(i > 0, i <= n_tiles))
    def _():
        state = [hc_ref[...]]
        S, P = functools.partial(scan_slice, state), functools.partial
        program = [
            P(ug_chunk, 0), P(S, 0), P(S, 1), P(ug_chunk, 1), P(S, 2), P(S, 3), P(ug_chunk, 2), conv,
            P(ug_chunk, 3), P(S, 4), P(S, 5), P(q_chunk, 0), P(S, 6), P(S, 7),
            P(q_chunk, 1), P(rms_piece, 0), P(rms_piece, 1), P(S, 8),
            P(kv_chunk, 0), P(S, 9), P(S, 10), P(S, 11), P(rms_piece, 2),
            P(kv_chunk, 1), P(S, 12), P(S, 13), P(S, 14), P(rms_piece, 3),
            P(gates_chunk, 0), P(S, 15), P(gates_chunk, 1), P(gates_chunk, 2), P(gates_chunk, 3),
        ]
        assert SCAN_SLICES == 16 and NORM_PIECES == 4
        for piece in program:
            piece()
        scan_done(state)

        @pl.when(jnp.logical_and(dot_live, dot_tile % per == per - 1))
        def _():
            cn_ref[0] = ext_ref[SUBLANES - (CONV_WIDTH - 1):SUBLANES, :]


def _proj_lru(x, g, w_in, wkvt, conv_state, h0, lru_w):
    s, t, _ = x.shape
    n = s * t
    tm = TOKEN_TILE
    assert t % tm == 0 and tm % KEY_BLOCK == 0
    per = t // tm
    n_tiles = n // tm
    tile = lambda i, lag: jnp.clip(i - lag, 0, n_tiles - 1)
    tok_in = pl.BlockSpec((tm, D_MODEL), lambda i: (tile(i, 0), 0))
    q_out = pl.BlockSpec((tm, D_ATTN), lambda i: (tile(i, 1), 0))
    hg_out = pl.BlockSpec((tm, D_RNN), lambda i: (tile(i, 2), 0))
    b16o = jax.ShapeDtypeStruct((n, D_ATTN), BF16)
    kv_spec = pl.BlockSpec((1, D_ATTN, tm), lambda i: (tile(i, 1) // per, 0, tile(i, 1) % per))
    kvb_spec = pl.BlockSpec((1, tm // KEY_BLOCK, D_ATTN, KEY_BLOCK),
                            lambda i: (tile(i, 1) // per, tile(i, 1) % per, 0, 0))
    kv_shape = jax.ShapeDtypeStruct((s, D_ATTN, t), F32)
    kvb_shape = jax.ShapeDtypeStruct((s, t // KEY_BLOCK, D_ATTN, KEY_BLOCK), BF16)
    cs_spec, _ = _lru_state_specs(lambda i: (tile(i, 1) // per, 0, 0))
    _, h_spec = _lru_state_specs(lambda i: (tile(i, 2) // per, 0, 0))
    w_cols = (COL_Q, COL_U, COL_GR)
    return pl.pallas_call(
        functools.partial(_proj_lru_kernel, per, n_tiles),
        grid=(n_tiles + 2,),
        in_specs=[tok_in, _const_spec(g.shape)] + [_w_in_cols(j) for j in w_cols] + [_const_spec(wkvt.shape)]
        + [cs_spec, h_spec] + [_const_spec(w.shape) for w in lru_w],
        out_specs=[q_out, kv_spec, kv_spec, kvb_spec, kvb_spec, hg_out, cs_spec, h_spec],
        out_shape=[b16o, kv_shape, kv_shape, kvb_shape, kvb_shape, b16o,
                   jax.ShapeDtypeStruct((s, CONV_WIDTH - 1, D_RNN), F32),
                   jax.ShapeDtypeStruct((s, 1, D_RNN), F32)],
        scratch_shapes=_lru_scratch(tm) + [pltpu.VMEM((2, tm, D_MODEL), BF16), pltpu.VMEM((2, tm, D_RNN), F32),
                                           pltpu.VMEM((2, tm, D_RNN), F32), pltpu.VMEM((2, tm, 2 * D_RNN), F32)],
        compiler_params=pltpu.CompilerParams(dimension_semantics=("arbitrary",),
                                             vmem_limit_bytes=VMEM_LIMIT),
        name="proj_lru",
    )(x.reshape(n, D_MODEL), g, *([w_in] * len(w_cols)), wkvt, conv_state, h0, *lru_w)


def _sb_split(z, mask):
    n = z.shape[1] // KEY_BLOCK
    tq = z.shape[0] // 2
    sp = _softplus(z)
    cats = []
    for h in range(2):
        for d in range(n):
            blk = sp[h * tq:(h + 1) * tq, d * KEY_BLOCK:(d + 1) * KEY_BLOCK]
            if d == n - 1 and mask is not None:
                blk = jnp.where(mask, blk, 0.0)
            cats.append(_hi_lo(blk))
    return jnp.concatenate(cats, axis=0)


def _hi_lo(blk):
    hi = blk.astype(BF16)
    lo = (blk - hi.astype(F32)).astype(BF16)
    return jnp.concatenate([hi, lo], axis=1)


def _sb_cumsum(cat, tri2):
    return jnp.dot(cat, tri2, preferred_element_type=F32)


def _sb_weights(z, r, carries, mask):
    n = z.shape[1] // KEY_BLOCK
    tq = z.shape[0] // 2
    rows, new_carries = [], []
    for h in range(2):
        carry = None if carries is None else carries[h]
        es = [None] * n
        for d in reversed(range(n)):
            rd = r[(h * n + d) * tq:(h * n + d + 1) * tq]
            x = z[h * tq:(h + 1) * tq, d * KEY_BLOCK:(d + 1) * KEY_BLOCK] + rd[:, :KEY_BLOCK]
            if carry is not None:
                x = x + carry
            e = jnp.exp(x)
            if d == n - 1 and mask is not None:
                e = jnp.where(mask, e, 0.0)
            es[d] = e.astype(BF16)
            tot = rd[:, KEY_BLOCK:]
            carry = tot if carry is None else carry + tot
        rows.append(es[0] if n == 1 else jnp.concatenate(es, axis=1))
        new_carries.append(carry)
    return jnp.concatenate(rows, axis=0), new_carries


def _tri2():
    j = np.arange(KEY_BLOCK)[:, None]
    s = np.arange(KEY_BLOCK)[None, :]
    half = np.concatenate([-(j >= s).astype(np.float32), -np.ones((KEY_BLOCK, LANES), np.float32)], axis=1)
    return jnp.asarray(np.concatenate([half, half], axis=0), dtype=BF16)


def _stack_heads(q, lo_half):
    zero = jnp.zeros_like(q)
    return jnp.concatenate([jnp.where(lo_half, q, zero), jnp.where(lo_half, zero, q)], axis=0)


def _sb_window(z_fn, pv_fn, carries, mask, tri2):
    z = z_fn()
    e, cs = _sb_weights(z, _sb_cumsum(_sb_split(z, mask), tri2), carries, mask)
    pv = pv_fn(e)
    tq = pv.shape[0] // 2
    return (pv[:tq], cs[0]), (pv[tq:], cs[1])


def _finish(store, j0, outs, step_fn):
    (a0, c0), (a1, c1) = outs
    store(a0, a1)
    if isinstance(j0, int) and j0 < 0:
        return lambda: None

    def alive(x0, x1):
        return jnp.max(jnp.maximum(x0, x1)) >= LOG_KEEP_FLOOR

    go = jnp.logical_and(j0 >= 0, alive(c0, c1))

    def walk():
        @pl.when(go)
        def _():
            def body(s):
                j, b0, b1, d0, d1, _ = s
                (p0, d0), (p1, d1) = step_fn(j, [d0, d1])
                return j - 1, b0 + p0, b1 + p1, d0, d1, jnp.logical_and(j >= 1, alive(d0, d1))

            _, b0, b1, _, _, _ = lax.while_loop(lambda s: s[-1], body,
                                                (jnp.int32(j0), a0, a1, c0, c1, jnp.bool_(True)))
            store(b0, b1)

    return walk


def _sb_chains(chains, tri2):
    zs = [c[0]() for c in chains]
    sums = [_sb_cumsum(_sb_split(z, c[2]), tri2) for z, c in zip(zs, chains)]
    weights = [_sb_weights(z, r, None, c[2]) for z, r, c in zip(zs, sums, chains)]
    walks = []
    for c, (e, cs) in zip(chains, weights):
        pv = c[1](e)
        tq = pv.shape[0] // 2
        walks.append(_finish(c[3], c[5], ((pv[:tq], cs[0]), (pv[tq:], cs[1])), c[4]))
    for walk in walks:
        walk()


def _attn_prompt_kernel(q_ref, kt_ref, vt_ref, tri_ref, o_ref):
    t = q_ref.shape[1]
    nq = t // Q_ROWS
    row = lax.broadcasted_iota(jnp.int32, (Q_ROWS, LANES), 0)
    lane = lax.broadcasted_iota(jnp.int32, (Q_ROWS, LANES), 1)
    lo_half = lane < HEAD_DIM
    diag_mask = lane < row
    lane_minus_row = lane - row
    tri2 = tri_ref[...]

    def window(ref, first, n):
        return jnp.concatenate([ref[0, first + d] for d in range(n)], axis=1)

    def qblocks(blocks, n_fast):
        chains = []
        for i in blocks:
            q2 = _stack_heads(q_ref[0, _rows(i, Q_ROWS), :], lo_half)
            first = i + 1 - n_fast

            def z_fn(q2=q2, first=first):
                return jnp.dot(q2, window(kt_ref, first, n_fast), preferred_element_type=F32)

            def pv_fn(e, first=first):
                return lax.dot_general(e, window(vt_ref, first, n_fast), NT_DIMS, preferred_element_type=F32)

            def store(b0, b1, i=i):
                o_ref[_rows(i, Q_ROWS), :] = jnp.where(lo_half, b0, b1).astype(BF16)

            def step(j, cs, i=i, q2=q2):
                m = lane_minus_row < (i - j) * KEY_BLOCK
                return _sb_window(lambda: jnp.dot(q2, kt_ref[0, j], preferred_element_type=F32),
                                  lambda e: lax.dot_general(e, vt_ref[0, j], NT_DIMS, preferred_element_type=F32),
                                  cs, m, tri2)

            chains.append((z_fn, pv_fn, diag_mask, store, step, i - n_fast))
        _sb_chains(chains, tri2)

    half = Q_ROWS // 2
    bottom = row >= half

    def qblocks_tapered(blocks):
        assert FAST_BLOCKS == 3
        chains = []
        for i in blocks:
            q2 = _stack_heads(q_ref[0, _rows(i, Q_ROWS), :], lo_half)
            q2_top = jnp.concatenate([q2[:half], q2[Q_ROWS:Q_ROWS + half]], axis=0)
            z_new = jnp.dot(q2, window(kt_ref, i - 1, 2), preferred_element_type=F32)
            z_old = jnp.dot(q2_top, kt_ref[0, i - 2], preferred_element_type=F32)
            chains.append([i, q2, z_new, z_old])
        for c in chains:
            _, _, z_new, z_old = c
            sp_new, sp_old = _softplus(z_new), _softplus(z_old)
            cats = []
            for h in range(2):
                rows = slice(h * Q_ROWS, (h + 1) * Q_ROWS)
                cats.append(_hi_lo(sp_new[rows, :KEY_BLOCK]))
                cats.append(_hi_lo(jnp.where(diag_mask, sp_new[rows, KEY_BLOCK:], 0.0)))
            for h in range(2):
                cats.append(_hi_lo(sp_old[h * half:(h + 1) * half]))
            c.append(_sb_cumsum(jnp.concatenate(cats, axis=0), tri2))
        walks = []
        for i, q2, z_new, z_old, r in chains:
            es, carries = [], []
            for h in range(2):
                rows = slice(h * Q_ROWS, (h + 1) * Q_ROWS)
                r_prev = r[2 * h * Q_ROWS:(2 * h + 1) * Q_ROWS]
                r_diag = r[(2 * h + 1) * Q_ROWS:(2 * h + 2) * Q_ROWS]
                r_old = r[4 * Q_ROWS + h * half:4 * Q_ROWS + (h + 1) * half]
                e_diag = jnp.where(diag_mask, jnp.exp(z_new[rows, KEY_BLOCK:] + r_diag[:, :KEY_BLOCK]), 0.0)
                carry = r_diag[:, KEY_BLOCK:]
                e_prev = jnp.exp(z_new[rows, :KEY_BLOCK] + r_prev[:, :KEY_BLOCK] + carry)
                carry = carry + r_prev[:, KEY_BLOCK:]
                e_old = jnp.exp(z_old[h * half:(h + 1) * half] + r_old[:, :KEY_BLOCK] + carry[:half])
                carries.append(jnp.concatenate([carry[:half] + r_old[:, KEY_BLOCK:], carry[half:]], axis=0))
                e_old = jnp.concatenate([e_old.astype(BF16), jnp.zeros((half, KEY_BLOCK), BF16)], axis=0)
                es.append(jnp.concatenate([e_old, e_prev.astype(BF16), e_diag.astype(BF16)], axis=1))
            pv = lax.dot_general(jnp.concatenate(es, axis=0), window(vt_ref, i - 2, FAST_BLOCKS), NT_DIMS,
                                 preferred_element_type=F32)

            def store(b0, b1, i=i):
                o_ref[_rows(i, Q_ROWS), :] = jnp.where(lo_half, b0, b1).astype(BF16)

            def step(j, cs, i=i, q2=q2):
                m = jnp.logical_and(lane_minus_row < (i - j) * KEY_BLOCK, jnp.logical_or(bottom, j < i - 2))
                return _sb_window(lambda: jnp.dot(q2, kt_ref[0, j], preferred_element_type=F32),
                                  lambda e: lax.dot_general(e, vt_ref[0, j], NT_DIMS, preferred_element_type=F32),
                                  cs, m, tri2)

            walks.append(_finish(store, i - 2, ((pv[:Q_ROWS], carries[0]), (pv[Q_ROWS:], carries[1])), step))
        for walk in walks:
            walk()

    n_head = FAST_BLOCKS - 1
    for i in range(n_head):
        qblocks([i], i + 1)

    start = n_head + (nq - n_head) % Q_GROUP
    if start > n_head:
        qblocks_tapered(list(range(n_head, start)))

    def group(g, carry):
        qblocks_tapered([start + g * Q_GROUP + u for u in range(Q_GROUP)])
        return carry

    lax.fori_loop(0, (nq - start) // Q_GROUP, group, 0)


def _attn_prompt(qb, ktb, vtb, tri2):
    b, t, _ = qb.shape
    assert t % Q_ROWS == 0 and Q_ROWS == KEY_BLOCK
    qspec = pl.BlockSpec((1, t, LANES), lambda i, h: (i, 0, h))
    kspec = pl.BlockSpec((1, t // KEY_BLOCK, LANES, KEY_BLOCK), lambda i, h: (i, 0, h, 0))
    return pl.pallas_call(
        _attn_prompt_kernel,
        grid=(b, D_ATTN // LANES),
        in_specs=[qspec, kspec, kspec, _const_spec(tri2.shape)],
        out_specs=pl.BlockSpec((t, LANES), lambda i, h: (i, h)),
        out_shape=jax.ShapeDtypeStruct((b * t, D_ATTN), BF16),
        compiler_params=pltpu.CompilerParams(dimension_semantics=("arbitrary", "arbitrary"),
                                             vmem_limit_bytes=VMEM_LIMIT),
        name="attn_prompt",
    )(qb, ktb, vtb, tri2)


def _attn_sample_kernel(n_tail, q_ref, kn_ref, vn_ref, ckt_ref, cvt_ref, ck_hbm, cv_hbm, tri_ref, o_ref,
                        kbuf, vbuf):
    n_streams = q_ref.shape[0]
    tq = q_ref.shape[1]
    n_past = ck_hbm.shape[2] // KEY_BLOCK
    row = lax.broadcasted_iota(jnp.int32, (tq, LANES), 0)
    lane = lax.broadcasted_iota(jnp.int32, (tq, LANES), 1)
    lo_half = lane < HEAD_DIM
    tri2 = tri_ref[...]

    pad = jnp.zeros((KEY_BLOCK - tq, LANES), BF16)
    chains = []
    for b in range(n_streams):
        s = pl.program_id(0) * n_streams + b
        for p in range(q_ref.shape[2] // LANES):
            cols = slice(p * LANES, (p + 1) * LANES)
            q2 = _stack_heads(q_ref[b, :, cols], lo_half)

            def z_fn(q2=q2, cols=cols, b=b):
                z_new = lax.dot_general(q2, jnp.concatenate([kn_ref[b, :, cols], pad], axis=0), NT_DIMS,
                                        preferred_element_type=F32)
                if n_tail == 0:
                    return z_new
                z_past = jnp.dot(q2, ckt_ref[b, cols, :].astype(BF16), preferred_element_type=F32)
                return jnp.concatenate([z_past, z_new], axis=1)

            def pv_fn(e, cols=cols, b=b):
                pv = jnp.dot(e[:, n_tail * KEY_BLOCK:], jnp.concatenate([vn_ref[b, :, cols], pad], axis=0),
                             preferred_element_type=F32)
                if n_tail == 0:
                    return pv
                return pv + lax.dot_general(e[:, :n_tail * KEY_BLOCK], cvt_ref[b, cols, :].astype(BF16), NT_DIMS,
                                            preferred_element_type=F32)

            def store(a0, a1, cols=cols, b=b):
                o_ref[b * tq:(b + 1) * tq, cols] = jnp.where(lo_half, a0, a1).astype(BF16)

            def step(j, cs, q2=q2, p=p, s=s):
                keys = pl.ds(pl.multiple_of(j * KEY_BLOCK, KEY_BLOCK), KEY_BLOCK)
                pltpu.sync_copy(ck_hbm.at[s, pl.ds(p * LANES, LANES), keys], kbuf)
                pltpu.sync_copy(cv_hbm.at[s, pl.ds(p * LANES, LANES), keys], vbuf)
                return _sb_window(lambda: jnp.dot(q2, kbuf[...].astype(BF16), preferred_element_type=F32),
                                  lambda e: lax.dot_general(e, vbuf[...].astype(BF16), NT_DIMS,
                                                            preferred_element_type=F32),
                                  cs, None, tri2)

            chains.append((z_fn, pv_fn, lane < row, store, step, n_past - 1 - n_tail))
    _sb_chains(chains, tri2)


def _attn_sample(qb, knb, vnb, cache_kt, cache_vt, tri2):
    s, tq, _ = qb.shape
    past_len = cache_kt.shape[2]
    assert past_len % KEY_BLOCK == 0 and tq <= KEY_BLOCK and tq % 16 == 0
    n_tail = min(FAST_BLOCKS - 1, past_len // KEY_BLOCK)
    assert n_tail > 0 and past_len % (n_tail * KEY_BLOCK) == 0
    per = SAMPLE_STREAMS if s % SAMPLE_STREAMS == 0 else 1
    new = pl.BlockSpec((per, tq, D_ATTN), lambda i: (i, 0, 0))
    tail = pl.BlockSpec((per, D_ATTN, n_tail * KEY_BLOCK), lambda i: (i, 0, past_len // (n_tail * KEY_BLOCK) - 1))
    hbm = pl.BlockSpec(memory_space=pl.ANY)
    return pl.pallas_call(
        functools.partial(_attn_sample_kernel, n_tail),
        grid=(s // per,),
        in_specs=[new, new, new, tail, tail, hbm, hbm, _const_spec(tri2.shape)],
        out_specs=pl.BlockSpec((per * tq, D_ATTN), lambda i: (i, 0)),
        out_shape=jax.ShapeDtypeStruct((s * tq, D_ATTN), BF16),
        scratch_shapes=[pltpu.VMEM((LANES, KEY_BLOCK), F32), pltpu.VMEM((LANES, KEY_BLOCK), F32)],
        compiler_params=pltpu.CompilerParams(dimension_semantics=("arbitrary",),
                                             vmem_limit_bytes=VMEM_LIMIT),
        name="attn_sample",
    )(qb, knb, vnb, cache_kt, cache_vt, cache_kt, cache_vt, tri2)


FF_CHUNK = 1024
STAGE_BYTES = 1024 * 1024
STAGE_SLOTS = 3


def _stream_cast(src_hbm, dst_ref, stage_ref, sem):
    slots, rows = stage_ref.shape[0], stage_ref.shape[1]
    n = src_hbm.shape[0] // rows
    assert src_hbm.shape[0] % rows == 0

    def copy(c, slot):
        return pltpu.make_async_copy(src_hbm.at[pl.ds(c * rows, rows)], stage_ref.at[slot], sem.at[slot])

    for c in range(min(slots - 1, n)):
        copy(c, c).start()

    def body(c, carry):
        @pl.when(c + slots - 1 < n)
        def _():
            copy(c + slots - 1, (c + slots - 1) % slots).start()

        copy(c, c % slots).wait()
        dst_ref[pl.ds(pl.multiple_of(c * rows, rows), rows), :] = stage_ref[c % slots].astype(BF16)
        return carry

    lax.fori_loop(0, n, body, 0)


def _out_kernel(n_first, xa_ref, xb_ref, oa_ref, ob_ref, hga_ref, hgb_ref, g1_ref, wga0_ref, wga1_ref, wgb0_ref,
                wgb1_ref, wa_hbm, wb_hbm, wo_hbm, g2_ref, g3_ref, wup_hbm, wdn_hbm, g4_ref, ya_ref, yb_ref,
                wa_ref, wb_ref, wo_ref, wup_ref, wdn_ref, wide_stage, stage, wide_sem, sem):
    @pl.when(pl.program_id(0) == 0)
    def _():
        _stream_cast(wa_hbm, wa_ref, stage, sem)
        _stream_cast(wb_hbm, wb_ref, stage, sem)
        _stream_cast(wo_hbm, wo_ref, stage, sem)
        _stream_cast(wup_hbm, wup_ref, wide_stage, wide_sem)
        _stream_cast(wdn_hbm, wdn_ref, stage, sem)

    first = pl.program_id(0) < n_first
    tm = xa_ref.shape[0]
    halves = [pl.ds(0, tm // 2), pl.ds(tm // 2, tm // 2)]
    n_ff = D_FF // FF_CHUNK
    st = [{}, {}]

    def pick(a_ref, b_ref, h):
        return jnp.where(first, a_ref[halves[h], :], b_ref[halves[h], :])

    def norm_in(h):
        st[h]["x"] = pick(xa_ref, xb_ref, h)
        st[h]["xn"] = _rms(st[h]["x"], g1_ref[...]).astype(BF16)

    def gate(h, w0_ref, w1_ref):
        return jnp.concatenate([jnp.dot(st[h]["xn"], w_ref[...], preferred_element_type=F32)
                                for w_ref in (w0_ref, w1_ref)], axis=1)

    def attn_gate(h):
        g_a = gate(h, wga0_ref, wga1_ref)
        y_a = jnp.dot(pick(oa_ref, ob_ref, h), wa_ref[...], preferred_element_type=F32)
        st[h]["m"] = jax.nn.sigmoid(g_a) * y_a

    def rnn_gate(h):
        g_b = gate(h, wgb0_ref, wgb1_ref)
        y_b = jnp.dot(pick(hga_ref, hgb_ref, h), wb_ref[...], preferred_element_type=F32)
        st[h]["m"] = st[h]["m"] + jax.nn.sigmoid(g_b) * y_b

    def mix(h):
        st[h]["mix"] = jnp.dot(st[h]["m"].astype(BF16), wo_ref[...], preferred_element_type=F32)

    def norm_mid(h):
        st[h]["x1"] = st[h]["x"] + _rms(st[h]["mix"], g2_ref[...])
        st[h]["f"] = _rms(st[h]["x1"], g3_ref[...]).astype(BF16)

    def ffn(h, c):
        cols = slice(c * FF_CHUNK, (c + 1) * FF_CHUNK)
        up = jnp.maximum(jnp.dot(st[h]["f"], wup_ref[:, cols], preferred_element_type=F32), 0.0)
        dn = jnp.dot((up * up).astype(BF16), wdn_ref[cols, :], preferred_element_type=F32)
        st[h]["acc"] = dn if c == 0 else st[h]["acc"] + dn

    def norm_out(h):
        yb_ref[halves[h], :] = st[h]["x1"] + _rms(st[h]["acc"], g4_ref[...])

    norm_in(0)
    norm_in(1)
    attn_gate(0)
    rnn_gate(0)
    mix(0)
    attn_gate(1)
    norm_mid(0)
    rnn_gate(1)
    mix(1)
    ffn(0, 0)
    norm_mid(1)
    for c in range(1, n_ff):
        ffn(0, c)
    ffn(1, 0)
    norm_out(0)
    for c in range(1, n_ff):
        ffn(1, c)
    norm_out(1)

    @pl.when(first)
    def _():
        ya_ref[...] = yb_ref[...]


def _out(xa, xb, oa, ob, hga, hgb, g1, w_in, wa, wb, wo, g2, g3, wup, wdn, g4):
    na, nb = xa.shape[0], xb.shape[0]
    tm = TOKEN_TILE
    assert na % tm == 0 and nb % tm == 0
    n_first = na // tm
    a_tok = lambda i: (jnp.minimum(i, n_first - 1), 0)
    b_tok = lambda i: (jnp.maximum(i - n_first, 0), 0)
    w_cols = (COL_GA, COL_GA + 1, COL_GB, COL_GB + 1)
    consts = (wa, wb, wo, g2, g3, wup, wdn, g4)
    big = (wa, wb, wo, wup, wdn)
    assert all(w.dtype == F32 for w in big)
    hbm = pl.BlockSpec(memory_space=pl.ANY)
    stage_rows = STAGE_BYTES // (4 * D_MODEL)
    wide_rows = STAGE_BYTES // (4 * D_FF)
    return pl.pallas_call(
        functools.partial(_out_kernel, n_first),
        grid=((na + nb) // tm,),
        in_specs=[pl.BlockSpec((tm, D_MODEL), a_tok), pl.BlockSpec((tm, D_MODEL), b_tok),
                  pl.BlockSpec((tm, D_ATTN), a_tok), pl.BlockSpec((tm, D_ATTN), b_tok),
                  pl.BlockSpec((tm, D_RNN), a_tok), pl.BlockSpec((tm, D_RNN), b_tok), _const_spec(g1.shape)]
        + [_w_in_cols(j) for j in w_cols]
        + [hbm if any(c is w for w in big) else _const_spec(c.shape) for c in consts],
        out_specs=[pl.BlockSpec((tm, D_MODEL), a_tok), pl.BlockSpec((tm, D_MODEL), b_tok)],
        out_shape=[jax.ShapeDtypeStruct((na, D_MODEL), F32), jax.ShapeDtypeStruct((nb, D_MODEL), F32)],
        scratch_shapes=[pltpu.VMEM(w.shape, BF16) for w in big]
        + [pltpu.VMEM((STAGE_SLOTS, wide_rows, D_FF), F32), pltpu.VMEM((STAGE_SLOTS, stage_rows, D_MODEL), F32),
           pltpu.SemaphoreType.DMA((STAGE_SLOTS,)), pltpu.SemaphoreType.DMA((STAGE_SLOTS,))],
        compiler_params=pltpu.CompilerParams(dimension_semantics=("arbitrary",),
                                             vmem_limit_bytes=VMEM_LIMIT),
        name="out",
    )(xa, xb, oa, ob, hga, hgb, g1, *([w_in] * len(w_cols)), *consts)


def _block_diag(w):
    n, c, d = w.shape
    return jnp.einsum("ncd,nm->ncmd", w, jnp.eye(n, dtype=w.dtype)).reshape(n * c, n * d)


def _heads_last(kt):
    s, _, t = kt.shape
    return kt.reshape(s, N_HEADS, HEAD_DIM, t).transpose(0, 3, 1, 2)[None]


def _time_last(cache):
    s, t = cache.shape[:2]
    return cache.transpose(0, 2, 3, 1).reshape(s, D_ATTN, t)


def kernel(x_prompt, x_sample, cache_k, cache_v, state_conv, state_h, w_in, g_pre_mix, w_conv, b_conv, w_r, b_r, w_i, b_i, lam, w_a_out, w_b_out, w_o, g_post_mix, g_pre_ffn, w_up, w_down, g_post_ffn):
    assert w_in.shape[0] == 1
    row = lambda a: a[0].reshape(1, -1)
    assert w_in.shape[2] == (COL_GB + 2) * D_ATTN
    w_in_b = w_in[0].astype(BF16)
    wkv_t = w_in[0][:, COL_K * D_ATTN:(COL_V + 1) * D_ATTN].T.astype(BF16)
    wg = jnp.concatenate([_block_diag(w_r[0]), _block_diag(w_i[0])], axis=1).astype(BF16)
    bg = jnp.concatenate([row(b_r), row(b_i)], axis=1)
    lru_w = (w_conv[0], row(b_conv), wg, bg, row(lam))
    g1 = row(g_pre_mix)
    tri2 = _tri2()

    bp, tp, _ = x_prompt.shape
    bs, ts, _ = x_sample.shape
    n_p, n_s = bp * tp, bs * ts

    zc = jnp.zeros((bp, CONV_WIDTH - 1, D_RNN), F32)
    zh = jnp.zeros((bp, 1, D_RNN), F32)
    qb, kt, vt, ktb, vtb, hg_p, cp, hp = _proj_lru(x_prompt, g1, w_in_b, wkv_t, zc, zh, lru_w)
    o_p = _attn_prompt(qb.reshape(bp, tp, D_ATTN), ktb, vtb, tri2)

    s3 = lambda a: a.reshape(bs, ts, a.shape[-1])
    qs, ks, vs, ksb, vsb, hg_s, cs, hs = _proj(x_sample, g1, w_in_b, state_conv[0],
                                               state_h[0].reshape(bs, 1, D_RNN), lru_w)
    o_s = _attn_sample(s3(qs), s3(ksb), s3(vsb), _time_last(cache_k[0]), _time_last(cache_v[0]), tri2)

    ys, yp = _out(x_sample.reshape(n_s, D_MODEL), x_prompt.reshape(n_p, D_MODEL), o_s, o_p, hg_s, hg_p, g1, w_in_b,
                  w_a_out[0], w_b_out[0], w_o[0], row(g_post_mix), row(g_pre_ffn), w_up[0], w_down[0],
                  row(g_post_ffn))
    return (yp.reshape(bp, tp, D_MODEL), ys.reshape(bs, ts, D_MODEL), _heads_last(kt), _heads_last(vt),
            cp[None], hp.reshape(1, bp, D_RNN), ks.reshape(1, bs, ts, N_HEADS, HEAD_DIM),
            vs.reshape(1, bs, ts, N_HEADS, HEAD_DIM), cs[None], hs.reshape(1, bs, D_RNN))
```

```python
import functools

import jax
import jax.numpy as jnp
import numpy as np
from jax import lax
from jax.experimental import pallas as pl
from jax.experimental.pallas import tpu as pltpu

F32 = jnp.float32
BF16 = jnp.bfloat16

D_MODEL = 1024
N_HEADS = 8
HEAD_DIM = 64
D_ATTN = N_HEADS * HEAD_DIM
D_RNN = 512
N_RNN_BLOCKS = 8
CONV_WIDTH = 4
LRU_C = 8.0
D_FF = 4 * D_MODEL
EPS = 1e-6

LANES = 128
SUBLANES = 8
KEY_BLOCK = 128
Q_ROWS = 128
FAST_BLOCKS = 3
Q_GROUP = 15
SAMPLE_STREAMS = 4
TOKEN_TILE = 512
SCAN_SLICES = 16
NORM_PIECES = 4
MXU_N = 256
LOG_KEEP_FLOOR = -104.0
VMEM_LIMIT = 58 * 1024 * 1024

NT_DIMS = (((1,), (1,)), ((), ()))
LOG2_E = 1.4426950408889634


def _rms(xf, g):
    return xf * lax.rsqrt(jnp.mean(xf * xf, axis=-1, keepdims=True) + EPS) * g


def _softplus(x):
    return jnp.maximum(x, 0.0) + jnp.log(1.0 + jnp.exp2(jnp.abs(x) * -LOG2_E))


def _sigmoid(x):
    return 0.5 * jnp.tanh(0.5 * x) + 0.5


def _const_spec(shape):
    nd = len(shape)
    return pl.BlockSpec(shape, lambda *_: (0,) * nd, pipeline_mode=pl.Buffered(1))


COL_Q, COL_K, COL_V, COL_U, COL_GR, COL_GA, COL_GB = 0, 1, 2, 3, 4, 5, 7
Q_SCALE = HEAD_DIM ** -0.5


def _w_in_cols(j):
    return pl.BlockSpec((D_MODEL, D_ATTN), lambda *_: (0, j), pipeline_mode=pl.Buffered(1))


def _rows(block_index, size):
    if isinstance(block_index, int):
        return pl.ds(block_index * size, size)
    return pl.ds(pl.multiple_of(block_index * size, size), size)


def _gelu_tanh(x):
    k = float(np.sqrt(2.0 / np.pi))
    half_x = 0.5 * x
    return half_x + half_x * jnp.tanh(x * (k + (k * 0.044715) * (x * x)))


def _conv_seed(first, cs_ref, ext_ref):
    @pl.when(first)
    def _():
        ext_ref[0:SUBLANES, :] = jnp.zeros((SUBLANES, D_RNN), F32)
        ext_ref[SUBLANES - (CONV_WIDTH - 1):SUBLANES, :] = cs_ref[0]


def _scan_seed(first, h0_ref, hc_ref):
    @pl.when(first)
    def _():
        hc_ref[...] = jnp.broadcast_to(h0_ref[0], hc_ref.shape)


def _lru_conv(u, wc_ref, bc_ref, ext_ref):
    tt = u.shape[0]
    ext_ref[SUBLANES:SUBLANES + tt, :] = u
    uc = bc_ref[...] + wc_ref[CONV_WIDTH - 1:CONV_WIDTH, :] * u
    for d in range(1, CONV_WIDTH):
        uc = uc + wc_ref[CONV_WIDTH - 1 - d:CONV_WIDTH - d, :] * ext_ref[SUBLANES - d:SUBLANES - d + tt, :]
    ext_ref[0:SUBLANES, :] = ext_ref[tt:tt + SUBLANES, :]
    return uc


def _lru_gates(uc, wg_ref, bg_ref):
    return jnp.dot(uc.astype(BF16), wg_ref[...], preferred_element_type=F32) + bg_ref[...]


def _lru_scan(g, uc, g_rnn, lam_ref, h_prev):
    tt = uc.shape[0]
    r = _sigmoid(g[:, :D_RNN])
    ig = _sigmoid(g[:, D_RNN:])
    decay = (LRU_C * _softplus(-lam_ref[...])) * r
    a = jnp.exp2(decay * -LOG2_E)
    m2 = jnp.tanh(decay) * (a * a + 1.0)
    b = jnp.where(m2 > 0.0, m2 * lax.rsqrt(m2), 0.0) * (ig * uc)

    n_groups = tt // SUBLANES
    a3 = a.reshape(n_groups, SUBLANES, D_RNN)
    b3 = b.reshape(n_groups, SUBLANES, D_RNN)
    sub = lax.broadcasted_iota(jnp.int32, a3.shape, 1)
    k = 1
    while k < SUBLANES:
        has_prev = sub >= k
        b_prev = jnp.where(has_prev, pltpu.roll(b3, k, axis=1), 0.0)
        a_prev = jnp.where(has_prev, pltpu.roll(a3, k, axis=1), 1.0)
        b3 = b3 + a3 * b_prev
        a3 = a3 * a_prev
        k *= 2

    hs = []
    for gi in range(n_groups):
        h = b3[gi] + a3[gi] * h_prev
        hs.append(h)
        h_prev = jnp.broadcast_to(h[SUBLANES - 1:SUBLANES, :], h.shape)
    return (jnp.concatenate(hs, axis=0) * _gelu_tanh(g_rnn)).astype(BF16), h_prev


def _lru_state_specs(index_map):
    past = CONV_WIDTH - 1
    return (pl.BlockSpec((1, past, D_RNN), index_map), pl.BlockSpec((1, 1, D_RNN), index_map))


def _lru_scratch(tt):
    return [pltpu.VMEM((tt + SUBLANES, D_RNN), F32), pltpu.VMEM((SUBLANES, D_RNN), F32)]


def _proj_kernel(t, x_ref, g_ref, wq_ref, wk_ref, wv_ref, wu_ref, wgr_ref, cs_ref, h0_ref, wc_ref, bc_ref, wg_ref,
                 bg_ref, lam_ref, qb_ref, k_ref, v_ref, kb_ref, vb_ref, hg_ref, cn_ref, hl_ref, ext_ref):
    n_streams = x_ref.shape[0] // t
    past = CONV_WIDTH - 1
    xn = _rms(x_ref[...], g_ref[...]).astype(BF16)
    proj = lambda w_ref: jnp.dot(xn, w_ref[...], preferred_element_type=F32)
    u, g_rnn = proj(wu_ref), proj(wgr_ref)
    qb_ref[...] = (proj(wq_ref) * Q_SCALE).astype(BF16)
    k, v = proj(wk_ref), proj(wv_ref)
    k_ref[...] = k
    v_ref[...] = v
    kb_ref[...] = k.astype(BF16)
    vb_ref[...] = v.astype(BF16)

    ucs = []
    for s in range(n_streams):
        ext = ext_ref.at[s]
        ext[0:SUBLANES, :] = jnp.zeros((SUBLANES, D_RNN), F32)
        ext[SUBLANES - past:SUBLANES, :] = cs_ref[s]
        ucs.append(_lru_conv(u[s * t:(s + 1) * t], wc_ref, bc_ref, ext))
        cn_ref[s] = ext[SUBLANES - past:SUBLANES, :]
    uc = jnp.concatenate(ucs, axis=0)
    gates = _lru_gates(uc, wg_ref, bg_ref)
    for s in range(n_streams):
        rows = slice(s * t, (s + 1) * t)
        hg, h_last = _lru_scan(gates[rows], uc[rows], g_rnn[rows], lam_ref,
                               jnp.broadcast_to(h0_ref[s], (SUBLANES, D_RNN)))
        hg_ref[rows, :] = hg
        hl_ref[s] = h_last[0:1, :]


def _proj(x, g, w_in, conv_state, h0, lru_w):
    s, t, _ = x.shape
    n = s * t
    tm = TOKEN_TILE
    assert n % tm == 0 and tm % t == 0 and t % SUBLANES == 0
    per_tile = tm // t
    tok = lambda i: (i, 0)
    row_out = pl.BlockSpec((tm, D_ATTN), tok)
    f32o = jax.ShapeDtypeStruct((n, D_ATTN), F32)
    b16o = jax.ShapeDtypeStruct((n, D_ATTN), BF16)
    st3 = pl.BlockSpec((per_tile, CONV_WIDTH - 1, D_RNN), lambda i: (i, 0, 0))
    st1 = pl.BlockSpec((per_tile, 1, D_RNN), lambda i: (i, 0, 0))
    w_cols = (COL_Q, COL_K, COL_V, COL_U, COL_GR)
    return pl.pallas_call(
        functools.partial(_proj_kernel, t),
        grid=(n // tm,),
        in_specs=[pl.BlockSpec((tm, D_MODEL), tok), _const_spec(g.shape)] + [_w_in_cols(j) for j in w_cols]
        + [st3, st1] + [_const_spec(w.shape) for w in lru_w],
        out_specs=[row_out] * 6 + [st3, st1],
        out_shape=[b16o, f32o, f32o, b16o, b16o, b16o,
                   jax.ShapeDtypeStruct((s, CONV_WIDTH - 1, D_RNN), F32),
                   jax.ShapeDtypeStruct((s, 1, D_RNN), F32)],
        scratch_shapes=[pltpu.VMEM((per_tile, t + SUBLANES, D_RNN), F32)],
        compiler_params=pltpu.CompilerParams(dimension_semantics=("arbitrary",),
                                             vmem_limit_bytes=VMEM_LIMIT),
        name="proj",
    )(x.reshape(n, D_MODEL), g, *([w_in] * len(w_cols)), conv_state, h0, *lru_w)


def _proj_lru_kernel(per, n_tiles, x_ref, g_ref, wq_ref, wu_ref, wgr_ref, wkvt_ref, cs_ref, h0_ref, wc_ref, bc_ref,
                     wg_ref, bg_ref, lam_ref, qb_ref, kt_ref, vt_ref, ktb_ref, vtb_ref, hg_ref, cn_ref, hl_ref,
                     ext_ref, hc_ref, xn_ref, uc_ref, gr_ref, gt_ref):
    i = pl.program_id(0)
    dot_tile, scan_tile = i - 1, i - 2
    dot_live = jnp.logical_and(dot_tile >= 0, dot_tile < n_tiles)
    scan_live = jnp.logical_and(scan_tile >= 0, scan_tile < n_tiles)
    _conv_seed(jnp.logical_and(dot_live, dot_tile % per == 0), cs_ref, ext_ref)
    _scan_seed(jnp.logical_and(scan_live, scan_tile % per == 0), h0_ref, hc_ref)

    @pl.when(i == 0)
    def _():
        xn_ref[...] = jnp.zeros(xn_ref.shape, BF16)
        hc_ref[...] = jnp.zeros(hc_ref.shape, F32)
        ext_ref[...] = jnp.zeros(ext_ref.shape, F32)
        for ref in (uc_ref, gr_ref, gt_ref):
            ref[...] = jnp.zeros(ref.shape, F32)

    tm = x_ref.shape[0]
    sc, dt = i % 2, (i + 1) % 2
    xn = xn_ref[dt]
    rows_per = tm // SCAN_SLICES
    u_parts = {}
    uc_b = [None]

    def scan_slice(state, p):
        rows = pl.ds(p * rows_per, rows_per)
        hg, state[0] = _lru_scan(gt_ref[sc, rows, :], uc_ref[sc, rows, :], gr_ref[sc, rows, :], lam_ref, state[0])
        hg_ref[rows, :] = hg

    def scan_done(state):
        hc_ref[...] = state[0]

        @pl.when(jnp.logical_and(scan_live, scan_tile % per == per - 1))
        def _():
            hl_ref[0] = state[0][0:1, :]

    def rms_piece(r):
        rows = pl.ds(r * (tm // NORM_PIECES), tm // NORM_PIECES)
        xn_ref[sc, rows, :] = _rms(x_ref[rows, :], g_ref[...]).astype(BF16)

    def ug_chunk(c):
        is_u = c < D_RNN // MXU_N
        cols = slice(c * MXU_N % D_RNN, c * MXU_N % D_RNN + MXU_N)
        r = jnp.dot(xn, (wu_ref if is_u else wgr_ref)[:, cols], preferred_element_type=F32)
        if is_u:
            u_parts[c] = r
        else:
            gr_ref[dt, :, cols] = r

    def conv():
        uc = _lru_conv(jnp.concatenate([u_parts[c] for c in sorted(u_parts)], axis=1), wc_ref, bc_ref, ext_ref)
        uc_ref[dt] = uc
        uc_b[0] = uc.astype(BF16)

    def q_chunk(c):
        cols = slice(c * MXU_N, (c + 1) * MXU_N)
        qb_ref[:, cols] = (jnp.dot(xn, wq_ref[:, cols], preferred_element_type=F32) * Q_SCALE).astype(BF16)

    def kv_chunk(c):
        toks = slice(c * MXU_N, (c + 1) * MXU_N)
        kv = lax.dot_general(wkvt_ref[...], xn[toks], NT_DIMS, preferred_element_type=F32)
        k, v = kv[:D_ATTN], kv[D_ATTN:]
        kt_ref[0, :, toks] = k
        vt_ref[0, :, toks] = v
        for j in range(MXU_N // KEY_BLOCK):
            cols = slice(j * KEY_BLOCK, (j + 1) * KEY_BLOCK)
            ktb_ref[0, c * (MXU_N // KEY_BLOCK) + j] = k[:, cols].astype(BF16)
            vtb_ref[0, c * (MXU_N // KEY_BLOCK) + j] = v[:, cols].astype(BF16)

    def gates_chunk(c):
        cols = slice(c * MXU_N, (c + 1) * MXU_N)
        gt_ref[dt, :, cols] = jnp.dot(uc_b[0], wg_ref[:, cols], preferred_element_type=F32) + bg_ref[:, cols]

    @pl.when(i == 0)
    def _():
        for r in range(NORM_PIECES):
            rms_piece(r)

    @pl.when(i == n_tiles + 1)
    def _():
        state = [hc_ref[...]]
        for p in range(SCAN_SLICES):
            scan_slice(state, p)
        scan_done(state)

    @pl.when(jnp.logical_and(i > 0, i <= n_tiles))
    def _():
        state = [hc_ref[...]]
        S, P = functools.partial(scan_slice, state), functools.partial
        program = [
            P(ug_chunk, 0), P(S, 0), P(S, 1), P(ug_chunk, 1), P(S, 2), P(S, 3), P(ug_chunk, 2), conv,
            P(ug_chunk, 3), P(S, 4), P(S, 5), P(q_chunk, 0), P(S, 6), P(S, 7),
            P(q_chunk, 1), P(rms_piece, 0), P(rms_piece, 1), P(S, 8),
            P(kv_chunk, 0), P(S, 9), P(S, 10), P(S, 11), P(rms_piece, 2),
            P(kv_chunk, 1), P(S, 12), P(S, 13), P(S, 14), P(rms_piece, 3),
            P(gates_chunk, 0), P(S, 15), P(gates_chunk, 1), P(gates_chunk, 2), P(gates_chunk, 3),
        ]
        assert SCAN_SLICES == 16 and NORM_PIECES == 4
        for piece in program:
            piece()
        scan_done(state)

        @pl.when(jnp.logical_and(dot_live, dot_tile % per == per - 1))
        def _():
            cn_ref[0] = ext_ref[SUBLANES - (CONV_WIDTH - 1):SUBLANES, :]


def _proj_lru(x, g, w_in, wkvt, conv_state, h0, lru_w):
    s, t, _ = x.shape
    n = s * t
    tm = TOKEN_TILE
    assert t % tm == 0 and tm % KEY_BLOCK == 0
    per = t // tm
    n_tiles = n // tm
    tile = lambda i, lag: jnp.clip(i - lag, 0, n_tiles - 1)
    tok_in = pl.BlockSpec((tm, D_MODEL), lambda i: (tile(i, 0), 0))
    q_out = pl.BlockSpec((tm, D_ATTN), lambda i: (tile(i, 1), 0))
    hg_out = pl.BlockSpec((tm, D_RNN), lambda i: (tile(i, 2), 0))
    b16o = jax.ShapeDtypeStruct((n, D_ATTN), BF16)
    kv_spec = pl.BlockSpec((1, D_ATTN, tm), lambda i: (tile(i, 1) // per, 0, tile(i, 1) % per))
    kvb_spec = pl.BlockSpec((1, tm // KEY_BLOCK, D_ATTN, KEY_BLOCK),
                            lambda i: (tile(i, 1) // per, tile(i, 1) % per, 0, 0))
    kv_shape = jax.ShapeDtypeStruct((s, D_ATTN, t), F32)
    kvb_shape = jax.ShapeDtypeStruct((s, t // KEY_BLOCK, D_ATTN, KEY_BLOCK), BF16)
    cs_spec, _ = _lru_state_specs(lambda i: (tile(i, 1) // per, 0, 0))
    _, h_spec = _lru_state_specs(lambda i: (tile(i, 2) // per, 0, 0))
    w_cols = (COL_Q, COL_U, COL_GR)
    return pl.pallas_call(
        functools.partial(_proj_lru_kernel, per, n_tiles),
        grid=(n_tiles + 2,),
        in_specs=[tok_in, _const_spec(g.shape)] + [_w_in_cols(j) for j in w_cols] + [_const_spec(wkvt.shape)]
        + [cs_spec, h_spec] + [_const_spec(w.shape) for w in lru_w],
        out_specs=[q_out, kv_spec, kv_spec, kvb_spec, kvb_spec, hg_out, cs_spec, h_spec],
        out_shape=[b16o, kv_shape, kv_shape, kvb_shape, kvb_shape, b16o,
                   jax.ShapeDtypeStruct((s, CONV_WIDTH - 1, D_RNN), F32),
                   jax.ShapeDtypeStruct((s, 1, D_RNN), F32)],
        scratch_shapes=_lru_scratch(tm) + [pltpu.VMEM((2, tm, D_MODEL), BF16), pltpu.VMEM((2, tm, D_RNN), F32),
                                           pltpu.VMEM((2, tm, D_RNN), F32), pltpu.VMEM((2, tm, 2 * D_RNN), F32)],
        compiler_params=pltpu.CompilerParams(dimension_semantics=("arbitrary",),
                                             vmem_limit_bytes=VMEM_LIMIT),
        name="proj_lru",
    )(x.reshape(n, D_MODEL), g, *([w_in] * len(w_cols)), wkvt, conv_state, h0, *lru_w)


def _sb_split(z, mask):
    n = z.shape[1] // KEY_BLOCK
    tq = z.shape[0] // 2
    sp = _softplus(z)
    cats = []
    for h in range(2):
        for d in range(n):
            blk = sp[h * tq:(h + 1) * tq, d * KEY_BLOCK:(d + 1) * KEY_BLOCK]
            if d == n - 1 and mask is not None:
                blk = jnp.where(mask, blk, 0.0)
            cats.append(_hi_lo(blk))
    return jnp.concatenate(cats, axis=0)


def _hi_lo(blk):
    hi = blk.astype(BF16)
    lo = (blk - hi.astype(F32)).astype(BF16)
    return jnp.concatenate([hi, lo], axis=1)


def _sb_cumsum(cat, tri2):
    return jnp.dot(cat, tri2, preferred_element_type=F32)


def _sb_weights(z, r, carries, mask):
    n = z.shape[1] // KEY_BLOCK
    tq = z.shape[0] // 2
    rows, new_carries = [], []
    for h in range(2):
        carry = None if carries is None else carries[h]
        es = [None] * n
        for d in reversed(range(n)):
            rd = r[(h * n + d) * tq:(h * n + d + 1) * tq]
            x = z[h * tq:(h + 1) * tq, d * KEY_BLOCK:(d + 1) * KEY_BLOCK] + rd[:, :KEY_BLOCK]
            if carry is not None:
                x = x + carry
            e = jnp.exp(x)
            if d == n - 1 and mask is not None:
                e = jnp.where(mask, e, 0.0)
            es[d] = e.astype(BF16)
            tot = rd[:, KEY_BLOCK:]
            carry = tot if carry is None else carry + tot
        rows.append(es[0] if n == 1 else jnp.concatenate(es, axis=1))
        new_carries.append(carry)
    return jnp.concatenate(rows, axis=0), new_carries


def _tri2():
    j = np.arange(KEY_BLOCK)[:, None]
    s = np.arange(KEY_BLOCK)[None, :]
    half = np.concatenate([-(j >= s).astype(np.float32), -np.ones((KEY_BLOCK, LANES), np.float32)], axis=1)
    return jnp.asarray(np.concatenate([half, half], axis=0), dtype=BF16)


def _stack_heads(q, lo_half):
    zero = jnp.zeros_like(q)
    return jnp.concatenate([jnp.where(lo_half, q, zero), jnp.where(lo_half, zero, q)], axis=0)


def _sb_window(z_fn, pv_fn, carries, mask, tri2):
    z = z_fn()
    e, cs = _sb_weights(z, _sb_cumsum(_sb_split(z, mask), tri2), carries, mask)
    pv = pv_fn(e)
    tq = pv.shape[0] // 2
    return (pv[:tq], cs[0]), (pv[tq:], cs[1])


def _finish(store, j0, outs, step_fn):
    (a0, c0), (a1, c1) = outs
    store(a0, a1)
    if isinstance(j0, int) and j0 < 0:
        return lambda: None

    def alive(x0, x1):
        return jnp.max(jnp.maximum(x0, x1)) >= LOG_KEEP_FLOOR

    go = jnp.logical_and(j0 >= 0, alive(c0, c1))

    def walk():
        @pl.when(go)
        def _():
            def body(s):
                j, b0, b1, d0, d1, _ = s
                (p0, d0), (p1, d1) = step_fn(j, [d0, d1])
                return j - 1, b0 + p0, b1 + p1, d0, d1, jnp.logical_and(j >= 1, alive(d0, d1))

            _, b0, b1, _, _, _ = lax.while_loop(lambda s: s[-1], body,
                                                (jnp.int32(j0), a0, a1, c0, c1, jnp.bool_(True)))
            store(b0, b1)

    return walk


def _sb_chains(chains, tri2):
    zs = [c[0]() for c in chains]
    sums = [_sb_cumsum(_sb_split(z, c[2]), tri2) for z, c in zip(zs, chains)]
    weights = [_sb_weights(z, r, None, c[2]) for z, r, c in zip(zs, sums, chains)]
    walks = []
    for c, (e, cs) in zip(chains, weights):
        pv = c[1](e)
        tq = pv.shape[0] // 2
        walks.append(_finish(c[3], c[5], ((pv[:tq], cs[0]), (pv[tq:], cs[1])), c[4]))
    for walk in walks:
        walk()


def _attn_prompt_kernel(q_ref, kt_ref, vt_ref, tri_ref, o_ref):
    t = q_ref.shape[1]
    nq = t // Q_ROWS
    row = lax.broadcasted_iota(jnp.int32, (Q_ROWS, LANES), 0)
    lane = lax.broadcasted_iota(jnp.int32, (Q_ROWS, LANES), 1)
    lo_half = lane < HEAD_DIM
    diag_mask = lane < row
    lane_minus_row = lane - row
    tri2 = tri_ref[...]

    def window(ref, first, n):
        return jnp.concatenate([ref[0, first + d] for d in range(n)], axis=1)

    def qblocks(blocks, n_fast):
        chains = []
        for i in blocks:
            q2 = _stack_heads(q_ref[0, _rows(i, Q_ROWS), :], lo_half)
            first = i + 1 - n_fast

            def z_fn(q2=q2, first=first):
                return jnp.dot(q2, window(kt_ref, first, n_fast), preferred_element_type=F32)

            def pv_fn(e, first=first):
                return lax.dot_general(e, window(vt_ref, first, n_fast), NT_DIMS, preferred_element_type=F32)

            def store(b0, b1, i=i):
                o_ref[_rows(i, Q_ROWS), :] = jnp.where(lo_half, b0, b1).astype(BF16)

            def step(j, cs, i=i, q2=q2):
                m = lane_minus_row < (i - j) * KEY_BLOCK
                return _sb_window(lambda: jnp.dot(q2, kt_ref[0, j], preferred_element_type=F32),
                                  lambda e: lax.dot_general(e, vt_ref[0, j], NT_DIMS, preferred_element_type=F32),
                                  cs, m, tri2)

            chains.append((z_fn, pv_fn, diag_mask, store, step, i - n_fast))
        _sb_chains(chains, tri2)

    half = Q_ROWS // 2
    bottom = row >= half

    def qblocks_tapered(blocks):
        assert FAST_BLOCKS == 3
        chains = []
        for i in blocks:
            q2 = _stack_heads(q_ref[0, _rows(i, Q_ROWS), :], lo_half)
            q2_top = jnp.concatenate([q2[:half], q2[Q_ROWS:Q_ROWS + half]], axis=0)
            z_new = jnp.dot(q2, window(kt_ref, i - 1, 2), preferred_element_type=F32)
            z_old = jnp.dot(q2_top, kt_ref[0, i - 2], preferred_element_type=F32)
            chains.append([i, q2, z_new, z_old])
        for c in chains:
            _, _, z_new, z_old = c
            sp_new, sp_old = _softplus(z_new), _softplus(z_old)
            cats = []
            for h in range(2):
                rows = slice(h * Q_ROWS, (h + 1) * Q_ROWS)
                cats.append(_hi_lo(sp_new[rows, :KEY_BLOCK]))
                cats.append(_hi_lo(jnp.where(diag_mask, sp_new[rows, KEY_BLOCK:], 0.0)))
            for h in range(2):
                cats.append(_hi_lo(sp_old[h * half:(h + 1) * half]))
            c.append(_sb_cumsum(jnp.concatenate(cats, axis=0), tri2))
        walks = []
        for i, q2, z_new, z_old, r in chains:
            es, carries = [], []
            for h in range(2):
                rows = slice(h * Q_ROWS, (h + 1) * Q_ROWS)
                r_prev = r[2 * h * Q_ROWS:(2 * h + 1) * Q_ROWS]
                r_diag = r[(2 * h + 1) * Q_ROWS:(2 * h + 2) * Q_ROWS]
                r_old = r[4 * Q_ROWS + h * half:4 * Q_ROWS + (h + 1) * half]
                e_diag = jnp.where(diag_mask, jnp.exp(z_new[rows, KEY_BLOCK:] + r_diag[:, :KEY_BLOCK]), 0.0)
                carry = r_diag[:, KEY_BLOCK:]
                e_prev = jnp.exp(z_new[rows, :KEY_BLOCK] + r_prev[:, :KEY_BLOCK] + carry)
                carry = carry + r_prev[:, KEY_BLOCK:]
                e_old = jnp.exp(z_old[h * half:(h + 1) * half] + r_old[:, :KEY_BLOCK] + carry[:half])
                carries.append(jnp.concatenate([carry[:half] + r_old[:, KEY_BLOCK:], carry[half:]], axis=0))
                e_old = jnp.concatenate([e_old.astype(BF16), jnp.zeros((half, KEY_BLOCK), BF16)], axis=0)
                es.append(jnp.concatenate([e_old, e_prev.astype(BF16), e_diag.astype(BF16)], axis=1))
            pv = lax.dot_general(jnp.concatenate(es, axis=0), window(vt_ref, i - 2, FAST_BLOCKS), NT_DIMS,
                                 preferred_element_type=F32)

            def store(b0, b1, i=i):
                o_ref[_rows(i, Q_ROWS), :] = jnp.where(lo_half, b0, b1).astype(BF16)

            def step(j, cs, i=i, q2=q2):
                m = jnp.logical_and(lane_minus_row < (i - j) * KEY_BLOCK, jnp.logical_or(bottom, j < i - 2))
                return _sb_window(lambda: jnp.dot(q2, kt_ref[0, j], preferred_element_type=F32),
                                  lambda e: lax.dot_general(e, vt_ref[0, j], NT_DIMS, preferred_element_type=F32),
                                  cs, m, tri2)

            walks.append(_finish(store, i - 2, ((pv[:Q_ROWS], carries[0]), (pv[Q_ROWS:], carries[1])), step))
        for walk in walks:
            walk()

    n_head = FAST_BLOCKS - 1
    for i in range(n_head):
        qblocks([i], i + 1)

    start = n_head + (nq - n_head) % Q_GROUP
    if start > n_head:
        qblocks_tapered(list(range(n_head, start)))

    def group(g, carry):
        qblocks_tapered([start + g * Q_GROUP + u for u in range(Q_GROUP)])
        return carry

    lax.fori_loop(0, (nq - start) // Q_GROUP, group, 0)


def _attn_prompt(qb, ktb, vtb, tri2):
    b, t, _ = qb.shape
    assert t % Q_ROWS == 0 and Q_ROWS == KEY_BLOCK
    qspec = pl.BlockSpec((1, t, LANES), lambda i, h: (i, 0, h))
    kspec = pl.BlockSpec((1, t // KEY_BLOCK, LANES, KEY_BLOCK), lambda i, h: (i, 0, h, 0))
    return pl.pallas_call(
        _attn_prompt_kernel,
        grid=(b, D_ATTN // LANES),
        in_specs=[qspec, kspec, kspec, _const_spec(tri2.shape)],
        out_specs=pl.BlockSpec((t, LANES), lambda i, h: (i, h)),
        out_shape=jax.ShapeDtypeStruct((b * t, D_ATTN), BF16),
        compiler_params=pltpu.CompilerParams(dimension_semantics=("arbitrary", "arbitrary"),
                                             vmem_limit_bytes=VMEM_LIMIT),
        name="attn_prompt",
    )(qb, ktb, vtb, tri2)


def _attn_sample_kernel(n_tail, q_ref, kn_ref, vn_ref, ckt_ref, cvt_ref, ck_hbm, cv_hbm, tri_ref, o_ref,
                        kbuf, vbuf):
    n_streams = q_ref.shape[0]
    tq = q_ref.shape[1]
    n_past = ck_hbm.shape[2] // KEY_BLOCK
    row = lax.broadcasted_iota(jnp.int32, (tq, LANES), 0)
    lane = lax.broadcasted_iota(jnp.int32, (tq, LANES), 1)
    lo_half = lane < HEAD_DIM
    tri2 = tri_ref[...]

    pad = jnp.zeros((KEY_BLOCK - tq, LANES), BF16)
    chains = []
    for b in range(n_streams):
        s = pl.program_id(0) * n_streams + b
        for p in range(q_ref.shape[2] // LANES):
            cols = slice(p * LANES, (p + 1) * LANES)
            q2 = _stack_heads(q_ref[b, :, cols], lo_half)

            def z_fn(q2=q2, cols=cols, b=b):
                z_new = lax.dot_general(q2, jnp.concatenate([kn_ref[b, :, cols], pad], axis=0), NT_DIMS,
                                        preferred_element_type=F32)
                if n_tail == 0:
                    return z_new
                z_past = jnp.dot(q2, ckt_ref[b, cols, :].astype(BF16), preferred_element_type=F32)
                return jnp.concatenate([z_past, z_new], axis=1)

            def pv_fn(e, cols=cols, b=b):
                pv = jnp.dot(e[:, n_tail * KEY_BLOCK:], jnp.concatenate([vn_ref[b, :, cols], pad], axis=0),
                             preferred_element_type=F32)
                if n_tail == 0:
                    return pv
                return pv + lax.dot_general(e[:, :n_tail * KEY_BLOCK], cvt_ref[b, cols, :].astype(BF16), NT_DIMS,
                                            preferred_element_type=F32)

            def store(a0, a1, cols=cols, b=b):
                o_ref[b * tq:(b + 1) * tq, cols] = jnp.where(lo_half, a0, a1).astype(BF16)

            def step(j, cs, q2=q2, p=p, s=s):
                keys = pl.ds(pl.multiple_of(j * KEY_BLOCK, KEY_BLOCK), KEY_BLOCK)
                pltpu.sync_copy(ck_hbm.at[s, pl.ds(p * LANES, LANES), keys], kbuf)
                pltpu.sync_copy(cv_hbm.at[s, pl.ds(p * LANES, LANES), keys], vbuf)
                return _sb_window(lambda: jnp.dot(q2, kbuf[...].astype(BF16), preferred_element_type=F32),
                                  lambda e: lax.dot_general(e, vbuf[...].astype(BF16), NT_DIMS,
                                                            preferred_element_type=F32),
                                  cs, None, tri2)

            chains.append((z_fn, pv_fn, lane < row, store, step, n_past - 1 - n_tail))
    _sb_chains(chains, tri2)


def _attn_sample(qb, knb, vnb, cache_kt, cache_vt, tri2):
    s, tq, _ = qb.shape
    past_len = cache_kt.shape[2]
    assert past_len % KEY_BLOCK == 0 and tq <= KEY_BLOCK and tq % 16 == 0
    n_tail = min(FAST_BLOCKS - 1, past_len // KEY_BLOCK)
    assert n_tail > 0 and past_len % (n_tail * KEY_BLOCK) == 0
    per = SAMPLE_STREAMS if s % SAMPLE_STREAMS == 0 else 1
    new = pl.BlockSpec((per, tq, D_ATTN), lambda i: (i, 0, 0))
    tail = pl.BlockSpec((per, D_ATTN, n_tail * KEY_BLOCK), lambda i: (i, 0, past_len // (n_tail * KEY_BLOCK) - 1))
    hbm = pl.BlockSpec(memory_space=pl.ANY)
    return pl.pallas_call(
        functools.partial(_attn_sample_kernel, n_tail),
        grid=(s // per,),
        in_specs=[new, new, new, tail, tail, hbm, hbm, _const_spec(tri2.shape)],
        out_specs=pl.BlockSpec((per * tq, D_ATTN), lambda i: (i, 0)),
        out_shape=jax.ShapeDtypeStruct((s * tq, D_ATTN), BF16),
        scratch_shapes=[pltpu.VMEM((LANES, KEY_BLOCK), F32), pltpu.VMEM((LANES, KEY_BLOCK), F32)],
        compiler_params=pltpu.CompilerParams(dimension_semantics=("arbitrary",),
                                             vmem_limit_bytes=VMEM_LIMIT),
        name="attn_sample",
    )(qb, knb, vnb, cache_kt, cache_vt, cache_kt, cache_vt, tri2)


FF_CHUNK = 1024
STAGE_BYTES = 1024 * 1024
STAGE_SLOTS = 3


def _stream_cast(src_hbm, dst_ref, stage_ref, sem):
    slots, rows = stage_ref.shape[0], stage_ref.shape[1]
    n = src_hbm.shape[0] // rows
    assert src_hbm.shape[0] % rows == 0

    def copy(c, slot):
        return pltpu.make_async_copy(src_hbm.at[pl.ds(c * rows, rows)], stage_ref.at[slot], sem.at[slot])

    for c in range(min(slots - 1, n)):
        copy(c, c).start()

    def body(c, carry):
        @pl.when(c + slots - 1 < n)
        def _():
            copy(c + slots - 1, (c + slots - 1) % slots).start()

        copy(c, c % slots).wait()
        dst_ref[pl.ds(pl.multiple_of(c * rows, rows), rows), :] = stage_ref[c % slots].astype(BF16)
        return carry

    lax.fori_loop(0, n, body, 0)


def _out_kernel(n_first, xa_ref, xb_ref, oa_ref, ob_ref, hga_ref, hgb_ref, g1_ref, wga0_ref, wga1_ref, wgb0_ref,
                wgb1_ref, wa_hbm, wb_hbm, wo_hbm, g2_ref, g3_ref, wup_hbm, wdn_hbm, g4_ref, ya_ref, yb_ref,
                wa_ref, wb_ref, wo_ref, wup_ref, wdn_ref, wide_stage, stage, wide_sem, sem):
    @pl.when(pl.program_id(0) == 0)
    def _():
        _stream_cast(wa_hbm, wa_ref, stage, sem)
        _stream_cast(wb_hbm, wb_ref, stage, sem)
        _stream_cast(wo_hbm, wo_ref, stage, sem)
        _stream_cast(wup_hbm, wup_ref, wide_stage, wide_sem)
        _stream_cast(wdn_hbm, wdn_ref, stage, sem)

    first = pl.program_id(0) < n_first
    tm = xa_ref.shape[0]
    halves = [pl.ds(0, tm // 2), pl.ds(tm // 2, tm // 2)]
    n_ff = D_FF // FF_CHUNK
    st = [{}, {}]

    def pick(a_ref, b_ref, h):
        return jnp.where(first, a_ref[halves[h], :], b_ref[halves[h], :])

    def norm_in(h):
        st[h]["x"] = pick(xa_ref, xb_ref, h)
        st[h]["xn"] = _rms(st[h]["x"], g1_ref[...]).astype(BF16)

    def gate(h, w0_ref, w1_ref):
        return jnp.concatenate([jnp.dot(st[h]["xn"], w_ref[...], preferred_element_type=F32)
                                for w_ref in (w0_ref, w1_ref)], axis=1)

    def attn_gate(h):
        g_a = gate(h, wga0_ref, wga1_ref)
        y_a = jnp.dot(pick(oa_ref, ob_ref, h), wa_ref[...], preferred_element_type=F32)
        st[h]["m"] = jax.nn.sigmoid(g_a) * y_a

    def rnn_gate(h):
        g_b = gate(h, wgb0_ref, wgb1_ref)
        y_b = jnp.dot(pick(hga_ref, hgb_ref, h), wb_ref[...], preferred_element_type=F32)
        st[h]["m"] = st[h]["m"] + jax.nn.sigmoid(g_b) * y_b

    def mix(h):
        st[h]["mix"] = jnp.dot(st[h]["m"].astype(BF16), wo_ref[...], preferred_element_type=F32)

    def norm_mid(h):
        st[h]["x1"] = st[h]["x"] + _rms(st[h]["mix"], g2_ref[...])
        st[h]["f"] = _rms(st[h]["x1"], g3_ref[...]).astype(BF16)

    def ffn(h, c):
        cols = slice(c * FF_CHUNK, (c + 1) * FF_CHUNK)
        up = jnp.maximum(jnp.dot(st[h]["f"], wup_ref[:, cols], preferred_element_type=F32), 0.0)
        dn = jnp.dot((up * up).astype(BF16), wdn_ref[cols, :], preferred_element_type=F32)
        st[h]["acc"] = dn if c == 0 else st[h]["acc"] + dn

    def norm_out(h):
        yb_ref[halves[h], :] = st[h]["x1"] + _rms(st[h]["acc"], g4_ref[...])

    norm_in(0)
    norm_in(1)
    attn_gate(0)
    rnn_gate(0)
    mix(0)
    attn_gate(1)
    norm_mid(0)
    rnn_gate(1)
    mix(1)
    ffn(0, 0)
    norm_mid(1)
    for c in range(1, n_ff):
        ffn(0, c)
    ffn(1, 0)
    norm_out(0)
    for c in range(1, n_ff):
        ffn(1, c)
    norm_out(1)

    @pl.when(first)
    def _():
        ya_ref[...] = yb_ref[...]


def _out(xa, xb, oa, ob, hga, hgb, g1, w_in, wa, wb, wo, g2, g3, wup, wdn, g4):
    na, nb = xa.shape[0], xb.shape[0]
    tm = TOKEN_TILE
    assert na % tm == 0 and nb % tm == 0
    n_first = na // tm
    a_tok = lambda i: (jnp.minimum(i, n_first - 1), 0)
    b_tok = lambda i: (jnp.maximum(i - n_first, 0), 0)
    w_cols = (COL_GA, COL_GA + 1, COL_GB, COL_GB + 1)
    consts = (wa, wb, wo, g2, g3, wup, wdn, g4)
    big = (wa, wb, wo, wup, wdn)
    assert all(w.dtype == F32 for w in big)
    hbm = pl.BlockSpec(memory_space=pl.ANY)
    stage_rows = STAGE_BYTES // (4 * D_MODEL)
    wide_rows = STAGE_BYTES // (4 * D_FF)
    return pl.pallas_call(
        functools.partial(_out_kernel, n_first),
        grid=((na + nb) // tm,),
        in_specs=[pl.BlockSpec((tm, D_MODEL), a_tok), pl.BlockSpec((tm, D_MODEL), b_tok),
                  pl.BlockSpec((tm, D_ATTN), a_tok), pl.BlockSpec((tm, D_ATTN), b_tok),
                  pl.BlockSpec((tm, D_RNN), a_tok), pl.BlockSpec((tm, D_RNN), b_tok), _const_spec(g1.shape)]
        + [_w_in_cols(j) for j in w_cols]
        + [hbm if any(c is w for w in big) else _const_spec(c.shape) for c in consts],
        out_specs=[pl.BlockSpec((tm, D_MODEL), a_tok), pl.BlockSpec((tm, D_MODEL), b_tok)],
        out_shape=[jax.ShapeDtypeStruct((na, D_MODEL), F32), jax.ShapeDtypeStruct((nb, D_MODEL), F32)],
        scratch_shapes=[pltpu.VMEM(w.shape, BF16) for w in big]
        + [pltpu.VMEM((STAGE_SLOTS, wide_rows, D_FF), F32), pltpu.VMEM((STAGE_SLOTS, stage_rows, D_MODEL), F32),
           pltpu.SemaphoreType.DMA((STAGE_SLOTS,)), pltpu.SemaphoreType.DMA((STAGE_SLOTS,))],
        compiler_params=pltpu.CompilerParams(dimension_semantics=("arbitrary",),
                                             vmem_limit_bytes=VMEM_LIMIT),
        name="out",
    )(xa, xb, oa, ob, hga, hgb, g1, *([w_in] * len(w_cols)), *consts)


def _block_diag(w):
    n, c, d = w.shape
    return jnp.einsum("ncd,nm->ncmd", w, jnp.eye(n, dtype=w.dtype)).reshape(n * c, n * d)


def _heads_last(kt):
    s, _, t = kt.shape
    return kt.reshape(s, N_HEADS, HEAD_DIM, t).transpose(0, 3, 1, 2)[None]


def _time_last(cache):
    s, t = cache.shape[:2]
    return cache.transpose(0, 2, 3, 1).reshape(s, D_ATTN, t)


def kernel(x_prompt, x_sample, cache_k, cache_v, state_conv, state_h, w_in, g_pre_mix, w_conv, b_conv, w_r, b_r, w_i, b_i, lam, w_a_out, w_b_out, w_o, g_post_mix, g_pre_ffn, w_up, w_down, g_post_ffn):
    assert w_in.shape[0] == 1
    row = lambda a: a[0].reshape(1, -1)
    assert w_in.shape[2] == (COL_GB + 2) * D_ATTN
    w_in_b = w_in[0].astype(BF16)
    wkv_t = w_in[0][:, COL_K * D_ATTN:(COL_V + 1) * D_ATTN].T.astype(BF16)
    wg = jnp.concatenate([_block_diag(w_r[0]), _block_diag(w_i[0])], axis=1).astype(BF16)
    bg = jnp.concatenate([row(b_r), row(b_i)], axis=1)
    lru_w = (w_conv[0], row(b_conv), wg, bg, row(lam))
    g1 = row(g_pre_mix)
    tri2 = _tri2()

    bp, tp, _ = x_prompt.shape
    bs, ts, _ = x_sample.shape
    n_p, n_s = bp * tp, bs * ts

    zc = jnp.zeros((bp, CONV_WIDTH - 1, D_RNN), F32)
    zh = jnp.zeros((bp, 1, D_RNN), F32)
    qb, kt, vt, ktb, vtb, hg_p, cp, hp = _proj_lru(x_prompt, g1, w_in_b, wkv_t, zc, zh, lru_w)
    o_p = _attn_prompt(qb.reshape(bp, tp, D_ATTN), ktb, vtb, tri2)

    s3 = lambda a: a.reshape(bs, ts, a.shape[-1])
    qs, ks, vs, ksb, vsb, hg_s, cs, hs = _proj(x_sample, g1, w_in_b, state_conv[0],
                                               state_h[0].reshape(bs, 1, D_RNN), lru_w)
    o_s = _attn_sample(s3(qs), s3(ksb), s3(vsb), _time_last(cache_k[0]), _time_last(cache_v[0]), tri2)

    ys, yp = _out(x_sample.reshape(n_s, D_MODEL), x_prompt.reshape(n_p, D_MODEL), o_s, o_p, hg_s, hg_p, g1, w_in_b,
                  w_a_out[0], w_b_out[0], w_o[0], row(g_post_mix), row(g_pre_ffn), w_up[0], w_down[0],
                  row(g_post_ffn))
    return (yp.reshape(bp, tp, D_MODEL), ys.reshape(bs, ts, D_MODEL), _heads_last(kt), _heads_last(vt),
            cp[None], hp.reshape(1, bp, D_RNN), ks.reshape(1, bs, ts, N_HEADS, HEAD_DIM),
            vs.reshape(1, bs, ts, N_HEADS, HEAD_DIM), cs[None], hs.reshape(1, bs, D_RNN))
```

```python
import functools

import jax
import jax.numpy as jnp
import numpy as np
from jax import lax
from jax.experimental import pallas as pl
from jax.experimental.pallas import tpu as pltpu

F32 = jnp.float32
BF16 = jnp.bfloat16

D_MODEL = 1024
N_HEADS = 8
HEAD_DIM = 64
D_ATTN = N_HEADS * HEAD_DIM
D_RNN = 512
N_RNN_BLOCKS = 8
CONV_WIDTH = 4
LRU_C = 8.0
D_FF = 4 * D_MODEL
EPS = 1e-6

LANES = 128
SUBLANES = 8
KEY_BLOCK = 128
Q_ROWS = 128
FAST_BLOCKS = 3
TAPER_ROWS = 48
Q_GROUP = 15
SAMPLE_STREAMS = 4
TOKEN_TILE = 512
SCAN_SLICES = 16
NORM_PIECES = 4
MXU_N = 256
LOG_KEEP_FLOOR = -104.0
VMEM_LIMIT = 58 * 1024 * 1024

NT_DIMS = (((1,), (1,)), ((), ()))
LOG2_E = 1.4426950408889634


def _rms(xf, g):
    return xf * lax.rsqrt(jnp.mean(xf * xf, axis=-1, keepdims=True) + EPS) * g


def _softplus(x):
    return jnp.maximum(x, 0.0) + jnp.log(1.0 + jnp.exp2(jnp.abs(x) * -LOG2_E))


def _sigmoid(x):
    return 0.5 * jnp.tanh(0.5 * x) + 0.5


def _const_spec(shape):
    nd = len(shape)
    return pl.BlockSpec(shape, lambda *_: (0,) * nd, pipeline_mode=pl.Buffered(1))


COL_Q, COL_K, COL_V, COL_U, COL_GR, COL_GA, COL_GB = 0, 1, 2, 3, 4, 5, 7
Q_SCALE = HEAD_DIM ** -0.5


def _w_in_cols(j):
    return pl.BlockSpec((D_MODEL, D_ATTN), lambda *_: (0, j), pipeline_mode=pl.Buffered(1))


def _rows(block_index, size):
    if isinstance(block_index, int):
        return pl.ds(block_index * size, size)
    return pl.ds(pl.multiple_of(block_index * size, size), size)


def _gelu_tanh(x):
    k = float(np.sqrt(2.0 / np.pi))
    half_x = 0.5 * x
    return half_x + half_x * jnp.tanh(x * (k + (k * 0.044715) * (x * x)))


def _conv_seed(first, cs_ref, ext_ref):
    @pl.when(first)
    def _():
        ext_ref[0:SUBLANES, :] = jnp.zeros((SUBLANES, D_RNN), F32)
        ext_ref[SUBLANES - (CONV_WIDTH - 1):SUBLANES, :] = cs_ref[0]


def _scan_seed(first, h0_ref, hc_ref):
    @pl.when(first)
    def _():
        hc_ref[...] = jnp.broadcast_to(h0_ref[0], hc_ref.shape)


def _lru_conv(u, wc_ref, bc_ref, ext_ref):
    tt = u.shape[0]
    ext_ref[SUBLANES:SUBLANES + tt, :] = u
    uc = bc_ref[...] + wc_ref[CONV_WIDTH - 1:CONV_WIDTH, :] * u
    for d in range(1, CONV_WIDTH):
        uc = uc + wc_ref[CONV_WIDTH - 1 - d:CONV_WIDTH - d, :] * ext_ref[SUBLANES - d:SUBLANES - d + tt, :]
    ext_ref[0:SUBLANES, :] = ext_ref[tt:tt + SUBLANES, :]
    return uc


def _lru_gates(uc, wg_ref, bg_ref):
    return jnp.dot(uc.astype(BF16), wg_ref[...], preferred_element_type=F32) + bg_ref[...]


def _lru_scan(g, uc, g_rnn, lam_ref, h_prev):
    tt = uc.shape[0]
    r = _sigmoid(g[:, :D_RNN])
    ig = _sigmoid(g[:, D_RNN:])
    decay = (LRU_C * _softplus(-lam_ref[...])) * r
    a = jnp.exp2(decay * -LOG2_E)
    m2 = jnp.tanh(decay) * (a * a + 1.0)
    b = jnp.where(m2 > 0.0, m2 * lax.rsqrt(m2), 0.0) * (ig * uc)

    n_groups = tt // SUBLANES
    a3 = a.reshape(n_groups, SUBLANES, D_RNN)
    b3 = b.reshape(n_groups, SUBLANES, D_RNN)
    sub = lax.broadcasted_iota(jnp.int32, a3.shape, 1)
    k = 1
    while k < SUBLANES:
        has_prev = sub >= k
        b_prev = jnp.where(has_prev, pltpu.roll(b3, k, axis=1), 0.0)
        a_prev = jnp.where(has_prev, pltpu.roll(a3, k, axis=1), 1.0)
        b3 = b3 + a3 * b_prev
        a3 = a3 * a_prev
        k *= 2

    hs = []
    for gi in range(n_groups):
        h = b3[gi] + a3[gi] * h_prev
        hs.append(h)
        h_prev = jnp.broadcast_to(h[SUBLANES - 1:SUBLANES, :], h.shape)
    return (jnp.concatenate(hs, axis=0) * _gelu_tanh(g_rnn)).astype(BF16), h_prev


def _lru_state_specs(index_map):
    past = CONV_WIDTH - 1
    return (pl.BlockSpec((1, past, D_RNN), index_map), pl.BlockSpec((1, 1, D_RNN), index_map))


def _lru_scratch(tt):
    return [pltpu.VMEM((tt + SUBLANES, D_RNN), F32), pltpu.VMEM((SUBLANES, D_RNN), F32)]


def _proj_kernel(t, x_ref, g_ref, wq_ref, wk_ref, wv_ref, wu_ref, wgr_ref, cs_ref, h0_ref, wc_ref, bc_ref, wg_ref,
                 bg_ref, lam_ref, qb_ref, k_ref, v_ref, kb_ref, vb_ref, hg_ref, cn_ref, hl_ref, ext_ref):
    n_streams = x_ref.shape[0] // t
    past = CONV_WIDTH - 1
    xn = _rms(x_ref[...], g_ref[...]).astype(BF16)
    proj = lambda w_ref: jnp.dot(xn, w_ref[...], preferred_element_type=F32)
    u, g_rnn = proj(wu_ref), proj(wgr_ref)
    qb_ref[...] = (proj(wq_ref) * Q_SCALE).astype(BF16)
    k, v = proj(wk_ref), proj(wv_ref)
    k_ref[...] = k
    v_ref[...] = v
    kb_ref[...] = k.astype(BF16)
    vb_ref[...] = v.astype(BF16)

    ucs = []
    for s in range(n_streams):
        ext = ext_ref.at[s]
        ext[0:SUBLANES, :] = jnp.zeros((SUBLANES, D_RNN), F32)
        ext[SUBLANES - past:SUBLANES, :] = cs_ref[s]
        ucs.append(_lru_conv(u[s * t:(s + 1) * t], wc_ref, bc_ref, ext))
        cn_ref[s] = ext[SUBLANES - past:SUBLANES, :]
    uc = jnp.concatenate(ucs, axis=0)
    gates = _lru_gates(uc, wg_ref, bg_ref)
    for s in range(n_streams):
        rows = slice(s * t, (s + 1) * t)
        hg, h_last = _lru_scan(gates[rows], uc[rows], g_rnn[rows], lam_ref,
                               jnp.broadcast_to(h0_ref[s], (SUBLANES, D_RNN)))
        hg_ref[rows, :] = hg
        hl_ref[s] = h_last[0:1, :]


def _proj(x, g, w_in, conv_state, h0, lru_w):
    s, t, _ = x.shape
    n = s * t
    tm = TOKEN_TILE
    assert n % tm == 0 and tm % t == 0 and t % SUBLANES == 0
    per_tile = tm // t
    tok = lambda i: (i, 0)
    row_out = pl.BlockSpec((tm, D_ATTN), tok)
    f32o = jax.ShapeDtypeStruct((n, D_ATTN), F32)
    b16o = jax.ShapeDtypeStruct((n, D_ATTN), BF16)
    st3 = pl.BlockSpec((per_tile, CONV_WIDTH - 1, D_RNN), lambda i: (i, 0, 0))
    st1 = pl.BlockSpec((per_tile, 1, D_RNN), lambda i: (i, 0, 0))
    w_cols = (COL_Q, COL_K, COL_V, COL_U, COL_GR)
    return pl.pallas_call(
        functools.partial(_proj_kernel, t),
        grid=(n // tm,),
        in_specs=[pl.BlockSpec((tm, D_MODEL), tok), _const_spec(g.shape)] + [_w_in_cols(j) for j in w_cols]
        + [st3, st1] + [_const_spec(w.shape) for w in lru_w],
        out_specs=[row_out] * 6 + [st3, st1],
        out_shape=[b16o, f32o, f32o, b16o, b16o, b16o,
                   jax.ShapeDtypeStruct((s, CONV_WIDTH - 1, D_RNN), F32),
                   jax.ShapeDtypeStruct((s, 1, D_RNN), F32)],
        scratch_shapes=[pltpu.VMEM((per_tile, t + SUBLANES, D_RNN), F32)],
        compiler_params=pltpu.CompilerParams(dimension_semantics=("arbitrary",),
                                             vmem_limit_bytes=VMEM_LIMIT),
        name="proj",
    )(x.reshape(n, D_MODEL), g, *([w_in] * len(w_cols)), conv_state, h0, *lru_w)


def _proj_lru_kernel(per, n_tiles, x_ref, g_ref, wq_ref, wu_ref, wgr_ref, wkvt_ref, cs_ref, h0_ref, wc_ref, bc_ref,
                     wg_ref, bg_ref, lam_ref, qb_ref, kt_ref, vt_ref, ktb_ref, vtb_ref, hg_ref, cn_ref, hl_ref,
                     ext_ref, hc_ref, xn_ref, uc_ref, gr_ref, gt_ref):
    i = pl.program_id(0)
    dot_tile, scan_tile = i - 1, i - 2
    dot_live = jnp.logical_and(dot_tile >= 0, dot_tile < n_tiles)
    scan_live = jnp.logical_and(scan_tile >= 0, scan_tile < n_tiles)
    _conv_seed(jnp.logical_and(dot_live, dot_tile % per == 0), cs_ref, ext_ref)
    _scan_seed(jnp.logical_and(scan_live, scan_tile % per == 0), h0_ref, hc_ref)

    @pl.when(i == 0)
    def _():
        xn_ref[...] = jnp.zeros(xn_ref.shape, BF16)
        hc_ref[...] = jnp.zeros(hc_ref.shape, F32)
        ext_ref[...] = jnp.zeros(ext_ref.shape, F32)
        for ref in (uc_ref, gr_ref, gt_ref):
            ref[...] = jnp.zeros(ref.shape, F32)

    tm = x_ref.shape[0]
    sc, dt = i % 2, (i + 1) % 2
    xn = xn_ref[dt]
    rows_per = tm // SCAN_SLICES
    u_parts = {}
    uc_b = [None]

    def scan_slice(state, p):
        rows = pl.ds(p * rows_per, rows_per)
        hg, state[0] = _lru_scan(gt_ref[sc, rows, :], uc_ref[sc, rows, :], gr_ref[sc, rows, :], lam_ref, state[0])
        hg_ref[rows, :] = hg

    def scan_done(state):
        hc_ref[...] = state[0]

        @pl.when(jnp.logical_and(scan_live, scan_tile % per == per - 1))
        def _():
            hl_ref[0] = state[0][0:1, :]

    def rms_piece(r):
        rows = pl.ds(r * (tm // NORM_PIECES), tm // NORM_PIECES)
        xn_ref[sc, rows, :] = _rms(x_ref[rows, :], g_ref[...]).astype(BF16)

    def ug_chunk(c):
        is_u = c < D_RNN // MXU_N
        cols = slice(c * MXU_N % D_RNN, c * MXU_N % D_RNN + MXU_N)
        r = jnp.dot(xn, (wu_ref if is_u else wgr_ref)[:, cols], preferred_element_type=F32)
        if is_u:
            u_parts[c] = r
        else:
            gr_ref[dt, :, cols] = r

    def conv():
        uc = _lru_conv(jnp.concatenate([u_parts[c] for c in sorted(u_parts)], axis=1), wc_ref, bc_ref, ext_ref)
        uc_ref[dt] = uc
        uc_b[0] = uc.astype(BF16)

    def q_chunk(c):
        cols = slice(c * MXU_N, (c + 1) * MXU_N)
        qb_ref[:, cols] = (jnp.dot(xn, wq_ref[:, cols], preferred_element_type=F32) * Q_SCALE).astype(BF16)

    def kv_chunk(c):
        toks = slice(c * MXU_N, (c + 1) * MXU_N)
        kv = lax.dot_general(wkvt_ref[...], xn[toks], NT_DIMS, preferred_element_type=F32)
        k, v = kv[:D_ATTN], kv[D_ATTN:]
        kt_ref[0, :, toks] = k
        vt_ref[0, :, toks] = v
        for j in range(MXU_N // KEY_BLOCK):
            cols = slice(j * KEY_BLOCK, (j + 1) * KEY_BLOCK)
            ktb_ref[0, c * (MXU_N // KEY_BLOCK) + j] = k[:, cols].astype(BF16)
            vtb_ref[0, c * (MXU_N // KEY_BLOCK) + j] = v[:, cols].astype(BF16)

    def gates_chunk(c):
        cols = slice(c * MXU_N, (c + 1) * MXU_N)
        gt_ref[dt, :, cols] = jnp.dot(uc_b[0], wg_ref[:, cols], preferred_element_type=F32) + bg_ref[:, cols]

    @pl.when(i == 0)
    def _():
        for r in range(NORM_PIECES):
            rms_piece(r)

    @pl.when(i == n_tiles + 1)
    def _():
        state = [hc_ref[...]]
        for p in range(SCAN_SLICES):
            scan_slice(state, p)
        scan_done(state)

    @pl.when(jnp.logical_and(i > 0, i <= n_tiles))
    def _():
        state = [hc_ref[...]]
        S, P = functools.partial(scan_slice, state), functools.partial
        program = [
            P(ug_chunk, 0), P(S, 0), P(S, 1), P(ug_chunk, 1), P(S, 2), P(S, 3), P(ug_chunk, 2), conv,
            P(ug_chunk, 3), P(S, 4), P(S, 5), P(q_chunk, 0), P(S, 6), P(S, 7),
            P(q_chunk, 1), P(rms_piece, 0), P(rms_piece, 1), P(S, 8),
            P(kv_chunk, 0), P(S, 9), P(S, 10), P(S, 11), P(rms_piece, 2),
            P(kv_chunk, 1), P(S, 12), P(S, 13), P(S, 14), P(rms_piece, 3),
            P(gates_chunk, 0), P(S, 15), P(gates_chunk, 1), P(gates_chunk, 2), P(gates_chunk, 3),
        ]
        assert SCAN_SLICES == 16 and NORM_PIECES == 4
        for piece in program:
            piece()
        scan_done(state)

        @pl.when(jnp.logical_and(dot_live, dot_tile % per == per - 1))
        def _():
            cn_ref[0] = ext_ref[SUBLANES - (CONV_WIDTH - 1):SUBLANES, :]


def _proj_lru(x, g, w_in, wkvt, conv_state, h0, lru_w):
    s, t, _ = x.shape
    n = s * t
    tm = TOKEN_TILE
    assert t % tm == 0 and tm % KEY_BLOCK == 0
    per = t // tm
    n_tiles = n // tm
    tile = lambda i, lag: jnp.clip(i - lag, 0, n_tiles - 1)
    tok_in = pl.BlockSpec((tm, D_MODEL), lambda i: (tile(i, 0), 0))
    q_out = pl.BlockSpec((tm, D_ATTN), lambda i: (tile(i, 1), 0))
    hg_out = pl.BlockSpec((tm, D_RNN), lambda i: (tile(i, 2), 0))
    b16o = jax.ShapeDtypeStruct((n, D_ATTN), BF16)
    kv_spec = pl.BlockSpec((1, D_ATTN, tm), lambda i: (tile(i, 1) // per, 0, tile(i, 1) % per))
    kvb_spec = pl.BlockSpec((1, tm // KEY_BLOCK, D_ATTN, KEY_BLOCK),
                            lambda i: (tile(i, 1) // per, tile(i, 1) % per, 0, 0))
    kv_shape = jax.ShapeDtypeStruct((s, D_ATTN, t), F32)
    kvb_shape = jax.ShapeDtypeStruct((s, t // KEY_BLOCK, D_ATTN, KEY_BLOCK), BF16)
    cs_spec, _ = _lru_state_specs(lambda i: (tile(i, 1) // per, 0, 0))
    _, h_spec = _lru_state_specs(lambda i: (tile(i, 2) // per, 0, 0))
    w_cols = (COL_Q, COL_U, COL_GR)
    return pl.pallas_call(
        functools.partial(_proj_lru_kernel, per, n_tiles),
        grid=(n_tiles + 2,),
        in_specs=[tok_in, _const_spec(g.shape)] + [_w_in_cols(j) for j in w_cols] + [_const_spec(wkvt.shape)]
        + [cs_spec, h_spec] + [_const_spec(w.shape) for w in lru_w],
        out_specs=[q_out, kv_spec, kv_spec, kvb_spec, kvb_spec, hg_out, cs_spec, h_spec],
        out_shape=[b16o, kv_shape, kv_shape, kvb_shape, kvb_shape, b16o,
                   jax.ShapeDtypeStruct((s, CONV_WIDTH - 1, D_RNN), F32),
                   jax.ShapeDtypeStruct((s, 1, D_RNN), F32)],
        scratch_shapes=_lru_scratch(tm) + [pltpu.VMEM((2, tm, D_MODEL), BF16), pltpu.VMEM((2, tm, D_RNN), F32),
                                           pltpu.VMEM((2, tm, D_RNN), F32), pltpu.VMEM((2, tm, 2 * D_RNN), F32)],
        compiler_params=pltpu.CompilerParams(dimension_semantics=("arbitrary",),
                                             vmem_limit_bytes=VMEM_LIMIT),
        name="proj_lru",
    )(x.reshape(n, D_MODEL), g, *([w_in] * len(w_cols)), wkvt, conv_state, h0, *lru_w)


def _sb_split(z, mask):
    n = z.shape[1] // KEY_BLOCK
    tq = z.shape[0] // 2
    sp = _softplus(z)
    cats = []
    for h in range(2):
        for d in range(n):
            blk = sp[h * tq:(h + 1) * tq, d * KEY_BLOCK:(d + 1) * KEY_BLOCK]
            if d == n - 1 and mask is not None:
                blk = jnp.where(mask, blk, 0.0)
            cats.append(_hi_lo(blk))
    return jnp.concatenate(cats, axis=0)


def _hi_lo(blk):
    hi = blk.astype(BF16)
    lo = (blk - hi.astype(F32)).astype(BF16)
    return jnp.concatenate([hi, lo], axis=1)


def _sb_cumsum(cat, tri2):
    return jnp.dot(cat, tri2, preferred_element_type=F32)


def _sb_weights(z, r, carries, mask):
    n = z.shape[1] // KEY_BLOCK
    tq = z.shape[0] // 2
    rows, new_carries = [], []
    for h in range(2):
        carry = None if carries is None else carries[h]
        es = [None] * n
        for d in reversed(range(n)):
            rd = r[(h * n + d) * tq:(h * n + d + 1) * tq]
            x = z[h * tq:(h + 1) * tq, d * KEY_BLOCK:(d + 1) * KEY_BLOCK] + rd[:, :KEY_BLOCK]
            if carry is not None:
                x = x + carry
            e = jnp.exp(x)
            if d == n - 1 and mask is not None:
                e = jnp.where(mask, e, 0.0)
            es[d] = e.astype(BF16)
            tot = rd[:, KEY_BLOCK:]
            carry = tot if carry is None else carry + tot
        rows.append(es[0] if n == 1 else jnp.concatenate(es, axis=1))
        new_carries.append(carry)
    return jnp.concatenate(rows, axis=0), new_carries


def _tri2():
    j = np.arange(KEY_BLOCK)[:, None]
    s = np.arange(KEY_BLOCK)[None, :]
    half = np.concatenate([-(j >= s).astype(np.float32), -np.ones((KEY_BLOCK, LANES), np.float32)], axis=1)
    return jnp.asarray(np.concatenate([half, half], axis=0), dtype=BF16)


def _stack_heads(q, lo_half):
    zero = jnp.zeros_like(q)
    return jnp.concatenate([jnp.where(lo_half, q, zero), jnp.where(lo_half, zero, q)], axis=0)


def _sb_window(z_fn, pv_fn, carries, mask, tri2):
    z = z_fn()
    e, cs = _sb_weights(z, _sb_cumsum(_sb_split(z, mask), tri2), carries, mask)
    pv = pv_fn(e)
    tq = pv.shape[0] // 2
    return (pv[:tq], cs[0]), (pv[tq:], cs[1])


def _finish(store, j0, outs, step_fn):
    (a0, c0), (a1, c1) = outs
    store(a0, a1)
    if isinstance(j0, int) and j0 < 0:
        return lambda: None

    def alive(x0, x1):
        return jnp.max(jnp.maximum(x0, x1)) >= LOG_KEEP_FLOOR

    go = jnp.logical_and(j0 >= 0, alive(c0, c1))

    def walk():
        @pl.when(go)
        def _():
            def body(s):
                j, b0, b1, d0, d1, _ = s
                (p0, d0), (p1, d1) = step_fn(j, [d0, d1])
                return j - 1, b0 + p0, b1 + p1, d0, d1, jnp.logical_and(j >= 1, alive(d0, d1))

            _, b0, b1, _, _, _ = lax.while_loop(lambda s: s[-1], body,
                                                (jnp.int32(j0), a0, a1, c0, c1, jnp.bool_(True)))
            store(b0, b1)

    return walk


def _sb_chains(chains, tri2):
    zs = [c[0]() for c in chains]
    sums = [_sb_cumsum(_sb_split(z, c[2]), tri2) for z, c in zip(zs, chains)]
    weights = [_sb_weights(z, r, None, c[2]) for z, r, c in zip(zs, sums, chains)]
    walks = []
    for c, (e, cs) in zip(chains, weights):
        pv = c[1](e)
        tq = pv.shape[0] // 2
        walks.append(_finish(c[3], c[5], ((pv[:tq], cs[0]), (pv[tq:], cs[1])), c[4]))
    for walk in walks:
        walk()


def _attn_prompt_kernel(q_ref, kt_ref, vt_ref, tri_ref, o_ref):
    t = q_ref.shape[1]
    nq = t // Q_ROWS
    row = lax.broadcasted_iota(jnp.int32, (Q_ROWS, LANES), 0)
    lane = lax.broadcasted_iota(jnp.int32, (Q_ROWS, LANES), 1)
    lo_half = lane < HEAD_DIM
    diag_mask = lane < row
    lane_minus_row = lane - row
    tri2 = tri_ref[...]

    def window(ref, first, n):
        return jnp.concatenate([ref[0, first + d] for d in range(n)], axis=1)

    def qblocks(blocks, n_fast):
        chains = []
        for i in blocks:
            q2 = _stack_heads(q_ref[0, _rows(i, Q_ROWS), :], lo_half)
            first = i + 1 - n_fast

            def z_fn(q2=q2, first=first):
                return jnp.dot(q2, window(kt_ref, first, n_fast), preferred_element_type=F32)

            def pv_fn(e, first=first):
                return lax.dot_general(e, window(vt_ref, first, n_fast), NT_DIMS, preferred_element_type=F32)

            def store(b0, b1, i=i):
                o_ref[_rows(i, Q_ROWS), :] = jnp.where(lo_half, b0, b1).astype(BF16)

            def step(j, cs, i=i, q2=q2):
                m = lane_minus_row < (i - j) * KEY_BLOCK
                return _sb_window(lambda: jnp.dot(q2, kt_ref[0, j], preferred_element_type=F32),
                                  lambda e: lax.dot_general(e, vt_ref[0, j], NT_DIMS, preferred_element_type=F32),
                                  cs, m, tri2)

            chains.append((z_fn, pv_fn, diag_mask, store, step, i - n_fast))
        _sb_chains(chains, tri2)

    half = TAPER_ROWS
    bottom = row >= half

    def qblocks_tapered(blocks):
        assert FAST_BLOCKS == 3
        chains = []
        for i in blocks:
            q2 = _stack_heads(q_ref[0, _rows(i, Q_ROWS), :], lo_half)
            q2_top = jnp.concatenate([q2[:half], q2[Q_ROWS:Q_ROWS + half]], axis=0)
            z_new = jnp.dot(q2, window(kt_ref, i - 1, 2), preferred_element_type=F32)
            z_old = jnp.dot(q2_top, kt_ref[0, i - 2], preferred_element_type=F32)
            chains.append([i, q2, z_new, z_old])
        for c in chains:
            _, _, z_new, z_old = c
            sp_new, sp_old = _softplus(z_new), _softplus(z_old)
            cats = []
            for h in range(2):
                rows = slice(h * Q_ROWS, (h + 1) * Q_ROWS)
                cats.append(_hi_lo(sp_new[rows, :KEY_BLOCK]))
                cats.append(_hi_lo(jnp.where(diag_mask, sp_new[rows, KEY_BLOCK:], 0.0)))
            for h in range(2):
                cats.append(_hi_lo(sp_old[h * half:(h + 1) * half]))
            c.append(_sb_cumsum(jnp.concatenate(cats, axis=0), tri2))
        walks = []
        for i, q2, z_new, z_old, r in chains:
            es, carries = [], []
            for h in range(2):
                rows = slice(h * Q_ROWS, (h + 1) * Q_ROWS)
                r_prev = r[2 * h * Q_ROWS:(2 * h + 1) * Q_ROWS]
                r_diag = r[(2 * h + 1) * Q_ROWS:(2 * h + 2) * Q_ROWS]
                r_old = r[4 * Q_ROWS + h * half:4 * Q_ROWS + (h + 1) * half]
                e_diag = jnp.where(diag_mask, jnp.exp(z_new[rows, KEY_BLOCK:] + r_diag[:, :KEY_BLOCK]), 0.0)
                carry = r_diag[:, KEY_BLOCK:]
                e_prev = jnp.exp(z_new[rows, :KEY_BLOCK] + r_prev[:, :KEY_BLOCK] + carry)
                carry = carry + r_prev[:, KEY_BLOCK:]
                e_old = jnp.exp(z_old[h * half:(h + 1) * half] + r_old[:, :KEY_BLOCK] + carry[:half])
                carries.append(jnp.concatenate([carry[:half] + r_old[:, KEY_BLOCK:], carry[half:]], axis=0))
                e_old = jnp.concatenate([e_old.astype(BF16), jnp.zeros((Q_ROWS - half, KEY_BLOCK), BF16)], axis=0)
                es.append(jnp.concatenate([e_old, e_prev.astype(BF16), e_diag.astype(BF16)], axis=1))
            pv = lax.dot_general(jnp.concatenate(es, axis=0), window(vt_ref, i - 2, FAST_BLOCKS), NT_DIMS,
                                 preferred_element_type=F32)

            def store(b0, b1, i=i):
                o_ref[_rows(i, Q_ROWS), :] = jnp.where(lo_half, b0, b1).astype(BF16)

            def step(j, cs, i=i, q2=q2):
                m = jnp.logical_and(lane_minus_row < (i - j) * KEY_BLOCK, jnp.logical_or(bottom, j < i - 2))
                return _sb_window(lambda: jnp.dot(q2, kt_ref[0, j], preferred_element_type=F32),
                                  lambda e: lax.dot_general(e, vt_ref[0, j], NT_DIMS, preferred_element_type=F32),
                                  cs, m, tri2)

            walks.append(_finish(store, i - 2, ((pv[:Q_ROWS], carries[0]), (pv[Q_ROWS:], carries[1])), step))
        for walk in walks:
            walk()

    n_head = FAST_BLOCKS - 1
    for i in range(n_head):
        qblocks([i], i + 1)

    start = n_head + (nq - n_head) % Q_GROUP
    if start > n_head:
        qblocks_tapered(list(range(n_head, start)))

    def group(g, carry):
        qblocks_tapered([start + g * Q_GROUP + u for u in range(Q_GROUP)])
        return carry

    lax.fori_loop(0, (nq - start) // Q_GROUP, group, 0)


def _attn_prompt(qb, ktb, vtb, tri2):
    b, t, _ = qb.shape
    assert t % Q_ROWS == 0 and Q_ROWS == KEY_BLOCK
    qspec = pl.BlockSpec((1, t, LANES), lambda i, h: (i, 0, h))
    kspec = pl.BlockSpec((1, t // KEY_BLOCK, LANES, KEY_BLOCK), lambda i, h: (i, 0, h, 0))
    return pl.pallas_call(
        _attn_prompt_kernel,
        grid=(b, D_ATTN // LANES),
        in_specs=[qspec, kspec, kspec, _const_spec(tri2.shape)],
        out_specs=pl.BlockSpec((t, LANES), lambda i, h: (i, h)),
        out_shape=jax.ShapeDtypeStruct((b * t, D_ATTN), BF16),
        compiler_params=pltpu.CompilerParams(dimension_semantics=("arbitrary", "arbitrary"),
                                             vmem_limit_bytes=VMEM_LIMIT),
        name="attn_prompt",
    )(qb, ktb, vtb, tri2)


def _attn_sample_kernel(n_tail, q_ref, kn_ref, vn_ref, ckt_ref, cvt_ref, ck_hbm, cv_hbm, tri_ref, o_ref,
                        kbuf, vbuf):
    n_streams = q_ref.shape[0]
    tq = q_ref.shape[1]
    n_past = ck_hbm.shape[2] // KEY_BLOCK
    row = lax.broadcasted_iota(jnp.int32, (tq, LANES), 0)
    lane = lax.broadcasted_iota(jnp.int32, (tq, LANES), 1)
    lo_half = lane < HEAD_DIM
    tri2 = tri_ref[...]

    pad = jnp.zeros((KEY_BLOCK - tq, LANES), BF16)
    chains = []
    for b in range(n_streams):
        s = pl.program_id(0) * n_streams + b
        for p in range(q_ref.shape[2] // LANES):
            cols = slice(p * LANES, (p + 1) * LANES)
            q2 = _stack_heads(q_ref[b, :, cols], lo_half)

            def z_fn(q2=q2, cols=cols, b=b):
                z_new = lax.dot_general(q2, jnp.concatenate([kn_ref[b, :, cols], pad], axis=0), NT_DIMS,
                                        preferred_element_type=F32)
                if n_tail == 0:
                    return z_new
                z_past = jnp.dot(q2, ckt_ref[b, cols, :].astype(BF16), preferred_element_type=F32)
                return jnp.concatenate([z_past, z_new], axis=1)

            def pv_fn(e, cols=cols, b=b):
                pv = jnp.dot(e[:, n_tail * KEY_BLOCK:], jnp.concatenate([vn_ref[b, :, cols], pad], axis=0),
                             preferred_element_type=F32)
                if n_tail == 0:
                    return pv
                return pv + lax.dot_general(e[:, :n_tail * KEY_BLOCK], cvt_ref[b, cols, :].astype(BF16), NT_DIMS,
                                            preferred_element_type=F32)

            def store(a0, a1, cols=cols, b=b):
                o_ref[b * tq:(b + 1) * tq, cols] = jnp.where(lo_half, a0, a1).astype(BF16)

            def step(j, cs, q2=q2, p=p, s=s):
                keys = pl.ds(pl.multiple_of(j * KEY_BLOCK, KEY_BLOCK), KEY_BLOCK)
                pltpu.sync_copy(ck_hbm.at[s, pl.ds(p * LANES, LANES), keys], kbuf)
                pltpu.sync_copy(cv_hbm.at[s, pl.ds(p * LANES, LANES), keys], vbuf)
                return _sb_window(lambda: jnp.dot(q2, kbuf[...].astype(BF16), preferred_element_type=F32),
                                  lambda e: lax.dot_general(e, vbuf[...].astype(BF16), NT_DIMS,
                                                            preferred_element_type=F32),
                                  cs, None, tri2)

            chains.append((z_fn, pv_fn, lane < row, store, step, n_past - 1 - n_tail))
    _sb_chains(chains, tri2)


def _attn_sample(qb, knb, vnb, cache_kt, cache_vt, tri2):
    s, tq, _ = qb.shape
    past_len = cache_kt.shape[2]
    assert past_len % KEY_BLOCK == 0 and tq <= KEY_BLOCK and tq % 16 == 0
    n_tail = min(FAST_BLOCKS - 1, past_len // KEY_BLOCK)
    assert n_tail > 0 and past_len % (n_tail * KEY_BLOCK) == 0
    per = SAMPLE_STREAMS if s % SAMPLE_STREAMS == 0 else 1
    new = pl.BlockSpec((per, tq, D_ATTN), lambda i: (i, 0, 0))
    tail = pl.BlockSpec((per, D_ATTN, n_tail * KEY_BLOCK), lambda i: (i, 0, past_len // (n_tail * KEY_BLOCK) - 1))
    hbm = pl.BlockSpec(memory_space=pl.ANY)
    return pl.pallas_call(
        functools.partial(_attn_sample_kernel, n_tail),
        grid=(s // per,),
        in_specs=[new, new, new, tail, tail, hbm, hbm, _const_spec(tri2.shape)],
        out_specs=pl.BlockSpec((per * tq, D_ATTN), lambda i: (i, 0)),
        out_shape=jax.ShapeDtypeStruct((s * tq, D_ATTN), BF16),
        scratch_shapes=[pltpu.VMEM((LANES, KEY_BLOCK), F32), pltpu.VMEM((LANES, KEY_BLOCK), F32)],
        compiler_params=pltpu.CompilerParams(dimension_semantics=("arbitrary",),
                                             vmem_limit_bytes=VMEM_LIMIT),
        name="attn_sample",
    )(qb, knb, vnb, cache_kt, cache_vt, cache_kt, cache_vt, tri2)


FF_CHUNK = 1024
STAGE_BYTES = 1024 * 1024
STAGE_SLOTS = 3


def _stream_cast(src_hbm, dst_ref, stage_ref, sem):
    slots, rows = stage_ref.shape[0], stage_ref.shape[1]
    n = src_hbm.shape[0] // rows
    assert src_hbm.shape[0] % rows == 0

    def copy(c, slot):
        return pltpu.make_async_copy(src_hbm.at[pl.ds(c * rows, rows)], stage_ref.at[slot], sem.at[slot])

    for c in range(min(slots - 1, n)):
        copy(c, c).start()

    def body(c, carry):
        @pl.when(c + slots - 1 < n)
        def _():
            copy(c + slots - 1, (c + slots - 1) % slots).start()

        copy(c, c % slots).wait()
        dst_ref[pl.ds(pl.multiple_of(c * rows, rows), rows), :] = stage_ref[c % slots].astype(BF16)
        return carry

    lax.fori_loop(0, n, body, 0)


def _out_kernel(n_first, xa_ref, xb_ref, oa_ref, ob_ref, hga_ref, hgb_ref, g1_ref, wga0_ref, wga1_ref, wgb0_ref,
                wgb1_ref, wa_hbm, wb_hbm, wo_hbm, g2_ref, g3_ref, wup_hbm, wdn_hbm, g4_ref, ya_ref, yb_ref,
                wa_ref, wb_ref, wo_ref, wup_ref, wdn_ref, wide_stage, stage, wide_sem, sem):
    @pl.when(pl.program_id(0) == 0)
    def _():
        _stream_cast(wa_hbm, wa_ref, stage, sem)
        _stream_cast(wb_hbm, wb_ref, stage, sem)
        _stream_cast(wo_hbm, wo_ref, stage, sem)
        _stream_cast(wup_hbm, wup_ref, wide_stage, wide_sem)
        _stream_cast(wdn_hbm, wdn_ref, stage, sem)

    first = pl.program_id(0) < n_first
    tm = xa_ref.shape[0]
    halves = [pl.ds(0, tm // 2), pl.ds(tm // 2, tm // 2)]
    n_ff = D_FF // FF_CHUNK
    st = [{}, {}]

    def pick(a_ref, b_ref, h):
        return jnp.where(first, a_ref[halves[h], :], b_ref[halves[h], :])

    def norm_in(h):
        st[h]["x"] = pick(xa_ref, xb_ref, h)
        st[h]["xn"] = _rms(st[h]["x"], g1_ref[...]).astype(BF16)

    def gate(h, w0_ref, w1_ref):
        return jnp.concatenate([jnp.dot(st[h]["xn"], w_ref[...], preferred_element_type=F32)
                                for w_ref in (w0_ref, w1_ref)], axis=1)

    def attn_gate(h):
        g_a = gate(h, wga0_ref, wga1_ref)
        y_a = jnp.dot(pick(oa_ref, ob_ref, h), wa_ref[...], preferred_element_type=F32)
        st[h]["m"] = jax.nn.sigmoid(g_a) * y_a

    def rnn_gate(h):
        g_b = gate(h, wgb0_ref, wgb1_ref)
        y_b = jnp.dot(pick(hga_ref, hgb_ref, h), wb_ref[...], preferred_element_type=F32)
        st[h]["m"] = st[h]["m"] + jax.nn.sigmoid(g_b) * y_b

    def mix(h):
        st[h]["mix"] = jnp.dot(st[h]["m"].astype(BF16), wo_ref[...], preferred_element_type=F32)

    def norm_mid(h):
        st[h]["x1"] = st[h]["x"] + _rms(st[h]["mix"], g2_ref[...])
        st[h]["f"] = _rms(st[h]["x1"], g3_ref[...]).astype(BF16)

    def ffn(h, c):
        cols = slice(c * FF_CHUNK, (c + 1) * FF_CHUNK)
        up = jnp.maximum(jnp.dot(st[h]["f"], wup_ref[:, cols], preferred_element_type=F32), 0.0)
        dn = jnp.dot((up * up).astype(BF16), wdn_ref[cols, :], preferred_element_type=F32)
        st[h]["acc"] = dn if c == 0 else st[h]["acc"] + dn

    def norm_out(h):
        yb_ref[halves[h], :] = st[h]["x1"] + _rms(st[h]["acc"], g4_ref[...])

    norm_in(0)
    norm_in(1)
    attn_gate(0)
    rnn_gate(0)
    mix(0)
    attn_gate(1)
    norm_mid(0)
    rnn_gate(1)
    mix(1)
    ffn(0, 0)
    norm_mid(1)
    for c in range(1, n_ff):
        ffn(0, c)
    ffn(1, 0)
    norm_out(0)
    for c in range(1, n_ff):
        ffn(1, c)
    norm_out(1)

    @pl.when(first)
    def _():
        ya_ref[...] = yb_ref[...]


def _out(xa, xb, oa, ob, hga, hgb, g1, w_in, wa, wb, wo, g2, g3, wup, wdn, g4):
    na, nb = xa.shape[0], xb.shape[0]
    tm = TOKEN_TILE
    assert na % tm == 0 and nb % tm == 0
    n_first = na // tm
    a_tok = lambda i: (jnp.minimum(i, n_first - 1), 0)
    b_tok = lambda i: (jnp.maximum(i - n_first, 0), 0)
    w_cols = (COL_GA, COL_GA + 1, COL_GB, COL_GB + 1)
    consts = (wa, wb, wo, g2, g3, wup, wdn, g4)
    big = (wa, wb, wo, wup, wdn)
    assert all(w.dtype == F32 for w in big)
    hbm = pl.BlockSpec(memory_space=pl.ANY)
    stage_rows = STAGE_BYTES // (4 * D_MODEL)
    wide_rows = STAGE_BYTES // (4 * D_FF)
    return pl.pallas_call(
        functools.partial(_out_kernel, n_first),
        grid=((na + nb) // tm,),
        in_specs=[pl.BlockSpec((tm, D_MODEL), a_tok), pl.BlockSpec((tm, D_MODEL), b_tok),
                  pl.BlockSpec((tm, D_ATTN), a_tok), pl.BlockSpec((tm, D_ATTN), b_tok),
                  pl.BlockSpec((tm, D_RNN), a_tok), pl.BlockSpec((tm, D_RNN), b_tok), _const_spec(g1.shape)]
        + [_w_in_cols(j) for j in w_cols]
        + [hbm if any(c is w for w in big) else _const_spec(c.shape) for c in consts],
        out_specs=[pl.BlockSpec((tm, D_MODEL), a_tok), pl.BlockSpec((tm, D_MODEL), b_tok)],
        out_shape=[jax.ShapeDtypeStruct((na, D_MODEL), F32), jax.ShapeDtypeStruct((nb, D_MODEL), F32)],
        scratch_shapes=[pltpu.VMEM(w.shape, BF16) for w in big]
        + [pltpu.VMEM((STAGE_SLOTS, wide_rows, D_FF), F32), pltpu.VMEM((STAGE_SLOTS, stage_rows, D_MODEL), F32),
           pltpu.SemaphoreType.DMA((STAGE_SLOTS,)), pltpu.SemaphoreType.DMA((STAGE_SLOTS,))],
        compiler_params=pltpu.CompilerParams(dimension_semantics=("arbitrary",),
                                             vmem_limit_bytes=VMEM_LIMIT),
        name="out",
    )(xa, xb, oa, ob, hga, hgb, g1, *([w_in] * len(w_cols)), *consts)


def _block_diag(w):
    n, c, d = w.shape
    return jnp.einsum("ncd,nm->ncmd", w, jnp.eye(n, dtype=w.dtype)).reshape(n * c, n * d)


def _heads_last(kt):
    s, _, t = kt.shape
    return kt.reshape(s, N_HEADS, HEAD_DIM, t).transpose(0, 3, 1, 2)[None]


def _time_last(cache):
    s, t = cache.shape[:2]
    return cache.transpose(0, 2, 3, 1).reshape(s, D_ATTN, t)


def kernel(x_prompt, x_sample, cache_k, cache_v, state_conv, state_h, w_in, g_pre_mix, w_conv, b_conv, w_r, b_r, w_i, b_i, lam, w_a_out, w_b_out, w_o, g_post_mix, g_pre_ffn, w_up, w_down, g_post_ffn):
    assert w_in.shape[0] == 1
    row = lambda a: a[0].reshape(1, -1)
    assert w_in.shape[2] == (COL_GB + 2) * D_ATTN
    w_in_b = w_in[0].astype(BF16)
    wkv_t = w_in[0][:, COL_K * D_ATTN:(COL_V + 1) * D_ATTN].T.astype(BF16)
    wg = jnp.concatenate([_block_diag(w_r[0]), _block_diag(w_i[0])], axis=1).astype(BF16)
    bg = jnp.concatenate([row(b_r), row(b_i)], axis=1)
    lru_w = (w_conv[0], row(b_conv), wg, bg, row(lam))
    g1 = row(g_pre_mix)
    tri2 = _tri2()

    bp, tp, _ = x_prompt.shape
    bs, ts, _ = x_sample.shape
    n_p, n_s = bp * tp, bs * ts

    zc = jnp.zeros((bp, CONV_WIDTH - 1, D_RNN), F32)
    zh = jnp.zeros((bp, 1, D_RNN), F32)
    qb, kt, vt, ktb, vtb, hg_p, cp, hp = _proj_lru(x_prompt, g1, w_in_b, wkv_t, zc, zh, lru_w)
    o_p = _attn_prompt(qb.reshape(bp, tp, D_ATTN), ktb, vtb, tri2)

    s3 = lambda a: a.reshape(bs, ts, a.shape[-1])
    qs, ks, vs, ksb, vsb, hg_s, cs, hs = _proj(x_sample, g1, w_in_b, state_conv[0],
                                               state_h[0].reshape(bs, 1, D_RNN), lru_w)
    o_s = _attn_sample(s3(qs), s3(ksb), s3(vsb), _time_last(cache_k[0]), _time_last(cache_v[0]), tri2)

    ys, yp = _out(x_sample.reshape(n_s, D_MODEL), x_prompt.reshape(n_p, D_MODEL), o_s, o_p, hg_s, hg_p, g1, w_in_b,
                  w_a_out[0], w_b_out[0], w_o[0], row(g_post_mix), row(g_pre_ffn), w_up[0], w_down[0],
                  row(g_post_ffn))
    return (yp.reshape(bp, tp, D_MODEL), ys.reshape(bs, ts, D_MODEL), _heads_last(kt), _heads_last(vt),
            cp[None], hp.reshape(1, bp, D_RNN), ks.reshape(1, bs, ts, N_HEADS, HEAD_DIM),
            vs.reshape(1, bs, ts, N_HEADS, HEAD_DIM), cs[None], hs.reshape(1, bs, D_RNN))
```

```python
import functools

import jax
import jax.numpy as jnp
import numpy as np
from jax import lax
from jax.experimental import pallas as pl
from jax.experimental.pallas import tpu as pltpu

F32 = jnp.float32
BF16 = jnp.bfloat16

D_MODEL = 1024
N_HEADS = 8
HEAD_DIM = 64
D_ATTN = N_HEADS * HEAD_DIM
D_RNN = 512
N_RNN_BLOCKS = 8
CONV_WIDTH = 4
LRU_C = 8.0
D_FF = 4 * D_MODEL
EPS = 1e-6

LANES = 128
SUBLANES = 8
KEY_BLOCK = 128
Q_ROWS = 128
FAST_BLOCKS = 3
TAPER_ROWS = 32
Q_GROUP = 15
SAMPLE_STREAMS = 4
TOKEN_TILE = 512
SCAN_SLICES = 16
NORM_PIECES = 4
MXU_N = 256
LOG_KEEP_FLOOR = -104.0
VMEM_LIMIT = 58 * 1024 * 1024

NT_DIMS = (((1,), (1,)), ((), ()))
LOG2_E = 1.4426950408889634


def _rms(xf, g):
    return xf * lax.rsqrt(jnp.mean(xf * xf, axis=-1, keepdims=True) + EPS) * g


def _softplus(x):
    return jnp.maximum(x, 0.0) + jnp.log(1.0 + jnp.exp2(jnp.abs(x) * -LOG2_E))


def _sigmoid(x):
    return 0.5 * jnp.tanh(0.5 * x) + 0.5


def _const_spec(shape):
    nd = len(shape)
    return pl.BlockSpec(shape, lambda *_: (0,) * nd, pipeline_mode=pl.Buffered(1))


COL_Q, COL_K, COL_V, COL_U, COL_GR, COL_GA, COL_GB = 0, 1, 2, 3, 4, 5, 7
Q_SCALE = HEAD_DIM ** -0.5


def _w_in_cols(j):
    return pl.BlockSpec((D_MODEL, D_ATTN), lambda *_: (0, j), pipeline_mode=pl.Buffered(1))


def _rows(block_index, size):
    if isinstance(block_index, int):
        return pl.ds(block_index * size, size)
    return pl.ds(pl.multiple_of(block_index * size, size), size)


def _gelu_tanh(x):
    k = float(np.sqrt(2.0 / np.pi))
    half_x = 0.5 * x
    return half_x + half_x * jnp.tanh(x * (k + (k * 0.044715) * (x * x)))


def _conv_seed(first, cs_ref, ext_ref):
    @pl.when(first)
    def _():
        ext_ref[0:SUBLANES, :] = jnp.zeros((SUBLANES, D_RNN), F32)
        ext_ref[SUBLANES - (CONV_WIDTH - 1):SUBLANES, :] = cs_ref[0]


def _scan_seed(first, h0_ref, hc_ref):
    @pl.when(first)
    def _():
        hc_ref[...] = jnp.broadcast_to(h0_ref[0], hc_ref.shape)


def _lru_conv(u, wc_ref, bc_ref, ext_ref):
    tt = u.shape[0]
    ext_ref[SUBLANES:SUBLANES + tt, :] = u
    uc = bc_ref[...] + wc_ref[CONV_WIDTH - 1:CONV_WIDTH, :] * u
    for d in range(1, CONV_WIDTH):
        uc = uc + wc_ref[CONV_WIDTH - 1 - d:CONV_WIDTH - d, :] * ext_ref[SUBLANES - d:SUBLANES - d + tt, :]
    ext_ref[0:SUBLANES, :] = ext_ref[tt:tt + SUBLANES, :]
    return uc


def _lru_gates(uc, wg_ref, bg_ref):
    return jnp.dot(uc.astype(BF16), wg_ref[...], preferred_element_type=F32) + bg_ref[...]


def _lru_scan(g, uc, g_rnn, lam_ref, h_prev):
    tt = uc.shape[0]
    r = _sigmoid(g[:, :D_RNN])
    ig = _sigmoid(g[:, D_RNN:])
    decay = (LRU_C * _softplus(-lam_ref[...])) * r
    a = jnp.exp2(decay * -LOG2_E)
    m2 = jnp.tanh(decay) * (a * a + 1.0)
    b = jnp.where(m2 > 0.0, m2 * lax.rsqrt(m2), 0.0) * (ig * uc)

    n_groups = tt // SUBLANES
    a3 = a.reshape(n_groups, SUBLANES, D_RNN)
    b3 = b.reshape(n_groups, SUBLANES, D_RNN)
    sub = lax.broadcasted_iota(jnp.int32, a3.shape, 1)
    k = 1
    while k < SUBLANES:
        has_prev = sub >= k
        b_prev = jnp.where(has_prev, pltpu.roll(b3, k, axis=1), 0.0)
        a_prev = jnp.where(has_prev, pltpu.roll(a3, k, axis=1), 1.0)
        b3 = b3 + a3 * b_prev
        a3 = a3 * a_prev
        k *= 2

    hs = []
    for gi in range(n_groups):
        h = b3[gi] + a3[gi] * h_prev
        hs.append(h)
        h_prev = jnp.broadcast_to(h[SUBLANES - 1:SUBLANES, :], h.shape)
    return (jnp.concatenate(hs, axis=0) * _gelu_tanh(g_rnn)).astype(BF16), h_prev


def _lru_state_specs(index_map):
    past = CONV_WIDTH - 1
    return (pl.BlockSpec((1, past, D_RNN), index_map), pl.BlockSpec((1, 1, D_RNN), index_map))


def _lru_scratch(tt):
    return [pltpu.VMEM((tt + SUBLANES, D_RNN), F32), pltpu.VMEM((SUBLANES, D_RNN), F32)]


def _proj_kernel(t, x_ref, g_ref, wq_ref, wk_ref, wv_ref, wu_ref, wgr_ref, cs_ref, h0_ref, wc_ref, bc_ref, wg_ref,
                 bg_ref, lam_ref, qb_ref, k_ref, v_ref, kb_ref, vb_ref, hg_ref, cn_ref, hl_ref, ext_ref):
    n_streams = x_ref.shape[0] // t
    past = CONV_WIDTH - 1
    xn = _rms(x_ref[...], g_ref[...]).astype(BF16)
    proj = lambda w_ref: jnp.dot(xn, w_ref[...], preferred_element_type=F32)
    u, g_rnn = proj(wu_ref), proj(wgr_ref)
    qb_ref[...] = (proj(wq_ref) * Q_SCALE).astype(BF16)
    k, v = proj(wk_ref), proj(wv_ref)
    k_ref[...] = k
    v_ref[...] = v
    kb_ref[...] = k.astype(BF16)
    vb_ref[...] = v.astype(BF16)

    ucs = []
    for s in range(n_streams):
        ext = ext_ref.at[s]
        ext[0:SUBLANES, :] = jnp.zeros((SUBLANES, D_RNN), F32)
        ext[SUBLANES - past:SUBLANES, :] = cs_ref[s]
        ucs.append(_lru_conv(u[s * t:(s + 1) * t], wc_ref, bc_ref, ext))
        cn_ref[s] = ext[SUBLANES - past:SUBLANES, :]
    uc = jnp.concatenate(ucs, axis=0)
    gates = _lru_gates(uc, wg_ref, bg_ref)
    for s in range(n_streams):
        rows = slice(s * t, (s + 1) * t)
        hg, h_last = _lru_scan(gates[rows], uc[rows], g_rnn[rows], lam_ref,
                               jnp.broadcast_to(h0_ref[s], (SUBLANES, D_RNN)))
        hg_ref[rows, :] = hg
        hl_ref[s] = h_last[0:1, :]


def _proj(x, g, w_in, conv_state, h0, lru_w):
    s, t, _ = x.shape
    n = s * t
    tm = TOKEN_TILE
    assert n % tm == 0 and tm % t == 0 and t % SUBLANES == 0
    per_tile = tm // t
    tok = lambda i: (i, 0)
    row_out = pl.BlockSpec((tm, D_ATTN), tok)
    f32o = jax.ShapeDtypeStruct((n, D_ATTN), F32)
    b16o = jax.ShapeDtypeStruct((n, D_ATTN), BF16)
    st3 = pl.BlockSpec((per_tile, CONV_WIDTH - 1, D_RNN), lambda i: (i, 0, 0))
    st1 = pl.BlockSpec((per_tile, 1, D_RNN), lambda i: (i, 0, 0))
    w_cols = (COL_Q, COL_K, COL_V, COL_U, COL_GR)
    return pl.pallas_call(
        functools.partial(_proj_kernel, t),
        grid=(n // tm,),
        in_specs=[pl.BlockSpec((tm, D_MODEL), tok), _const_spec(g.shape)] + [_w_in_cols(j) for j in w_cols]
        + [st3, st1] + [_const_spec(w.shape) for w in lru_w],
        out_specs=[row_out] * 6 + [st3, st1],
        out_shape=[b16o, f32o, f32o, b16o, b16o, b16o,
                   jax.ShapeDtypeStruct((s, CONV_WIDTH - 1, D_RNN), F32),
                   jax.ShapeDtypeStruct((s, 1, D_RNN), F32)],
        scratch_shapes=[pltpu.VMEM((per_tile, t + SUBLANES, D_RNN), F32)],
        compiler_params=pltpu.CompilerParams(dimension_semantics=("arbitrary",),
                                             vmem_limit_bytes=VMEM_LIMIT),
        name="proj",
    )(x.reshape(n, D_MODEL), g, *([w_in] * len(w_cols)), conv_state, h0, *lru_w)


def _proj_lru_kernel(per, n_tiles, x_ref, g_ref, wq_ref, wu_ref, wgr_ref, wkvt_ref, cs_ref, h0_ref, wc_ref, bc_ref,
                     wg_ref, bg_ref, lam_ref, qb_ref, kt_ref, vt_ref, ktb_ref, vtb_ref, hg_ref, cn_ref, hl_ref,
                     ext_ref, hc_ref, xn_ref, uc_ref, gr_ref, gt_ref):
    i = pl.program_id(0)
    dot_tile, scan_tile = i - 1, i - 2
    dot_live = jnp.logical_and(dot_tile >= 0, dot_tile < n_tiles)
    scan_live = jnp.logical_and(scan_tile >= 0, scan_tile < n_tiles)
    _conv_seed(jnp.logical_and(dot_live, dot_tile % per == 0), cs_ref, ext_ref)
    _scan_seed(jnp.logical_and(scan_live, scan_tile % per == 0), h0_ref, hc_ref)

    @pl.when(i == 0)
    def _():
        xn_ref[...] = jnp.zeros(xn_ref.shape, BF16)
        hc_ref[...] = jnp.zeros(hc_ref.shape, F32)
        ext_ref[...] = jnp.zeros(ext_ref.shape, F32)
        for ref in (uc_ref, gr_ref, gt_ref):
            ref[...] = jnp.zeros(ref.shape, F32)

    tm = x_ref.shape[0]
    sc, dt = i % 2, (i + 1) % 2
    xn = xn_ref[dt]
    rows_per = tm // SCAN_SLICES
    u_parts = {}
    uc_b = [None]

    def scan_slice(state, p):
        rows = pl.ds(p * rows_per, rows_per)
        hg, state[0] = _lru_scan(gt_ref[sc, rows, :], uc_ref[sc, rows, :], gr_ref[sc, rows, :], lam_ref, state[0])
        hg_ref[rows, :] = hg

    def scan_done(state):
        hc_ref[...] = state[0]

        @pl.when(jnp.logical_and(scan_live, scan_tile % per == per - 1))
        def _():
            hl_ref[0] = state[0][0:1, :]

    def rms_piece(r):
        rows = pl.ds(r * (tm // NORM_PIECES), tm // NORM_PIECES)
        xn_ref[sc, rows, :] = _rms(x_ref[rows, :], g_ref[...]).astype(BF16)

    def ug_chunk(c):
        is_u = c < D_RNN // MXU_N
        cols = slice(c * MXU_N % D_RNN, c * MXU_N % D_RNN + MXU_N)
        r = jnp.dot(xn, (wu_ref if is_u else wgr_ref)[:, cols], preferred_element_type=F32)
        if is_u:
            u_parts[c] = r
        else:
            gr_ref[dt, :, cols] = r

    def conv():
        uc = _lru_conv(jnp.concatenate([u_parts[c] for c in sorted(u_parts)], axis=1), wc_ref, bc_ref, ext_ref)
        uc_ref[dt] = uc
        uc_b[0] = uc.astype(BF16)

    def q_chunk(c):
        cols = slice(c * MXU_N, (c + 1) * MXU_N)
        qb_ref[:, cols] = (jnp.dot(xn, wq_ref[:, cols], preferred_element_type=F32) * Q_SCALE).astype(BF16)

    def kv_chunk(c):
        toks = slice(c * MXU_N, (c + 1) * MXU_N)
        kv = lax.dot_general(wkvt_ref[...], xn[toks], NT_DIMS, preferred_element_type=F32)
        k, v = kv[:D_ATTN], kv[D_ATTN:]
        kt_ref[0, :, toks] = k
        vt_ref[0, :, toks] = v
        for j in range(MXU_N // KEY_BLOCK):
            cols = slice(j * KEY_BLOCK, (j + 1) * KEY_BLOCK)
            ktb_ref[0, c * (MXU_N // KEY_BLOCK) + j] = k[:, cols].astype(BF16)
            vtb_ref[0, c * (MXU_N // KEY_BLOCK) + j] = v[:, cols].astype(BF16)

    def gates_chunk(c):
        cols = slice(c * MXU_N, (c + 1) * MXU_N)
        gt_ref[dt, :, cols] = jnp.dot(uc_b[0], wg_ref[:, cols], preferred_element_type=F32) + bg_ref[:, cols]

    @pl.when(i == 0)
    def _():
        for r in range(NORM_PIECES):
            rms_piece(r)

    @pl.when(i == n_tiles + 1)
    def _():
        state = [hc_ref[...]]
        for p in range(SCAN_SLICES):
            scan_slice(state, p)
        scan_done(state)

    @pl.when(jnp.logical_and(i > 0, i <= n_tiles))
    def _():
        state = [hc_ref[...]]
        S, P = functools.partial(scan_slice, state), functools.partial
        program = [
            P(ug_chunk, 0), P(S, 0), P(S, 1), P(ug_chunk, 1), P(S, 2), P(S, 3), P(ug_chunk, 2), conv,
            P(ug_chunk, 3), P(S, 4), P(S, 5), P(q_chunk, 0), P(S, 6), P(S, 7),
            P(q_chunk, 1), P(rms_piece, 0), P(rms_piece, 1), P(S, 8),
            P(kv_chunk, 0), P(S, 9), P(S, 10), P(S, 11), P(rms_piece, 2),
            P(kv_chunk, 1), P(S, 12), P(S, 13), P(S, 14), P(rms_piece, 3),
            P(gates_chunk, 0), P(S, 15), P(gates_chunk, 1), P(gates_chunk, 2), P(gates_chunk, 3),
        ]
        assert SCAN_SLICES == 16 and NORM_PIECES == 4
        for piece in program:
            piece()
        scan_done(state)

        @pl.when(jnp.logical_and(dot_live, dot_tile % per == per - 1))
        def _():
            cn_ref[0] = ext_ref[SUBLANES - (CONV_WIDTH - 1):SUBLANES, :]


def _proj_lru(x, g, w_in, wkvt, conv_state, h0, lru_w):
    s, t, _ = x.shape
    n = s * t
    tm = TOKEN_TILE
    assert t % tm == 0 and tm % KEY_BLOCK == 0
    per = t // tm
    n_tiles = n // tm
    tile = lambda i, lag: jnp.clip(i - lag, 0, n_tiles - 1)
    tok_in = pl.BlockSpec((tm, D_MODEL), lambda i: (tile(i, 0), 0))
    q_out = pl.BlockSpec((tm, D_ATTN), lambda i: (tile(i, 1), 0))
    hg_out = pl.BlockSpec((tm, D_RNN), lambda i: (tile(i, 2), 0))
    b16o = jax.ShapeDtypeStruct((n, D_ATTN), BF16)
    kv_spec = pl.BlockSpec((1, D_ATTN, tm), lambda i: (tile(i, 1) // per, 0, tile(i, 1) % per))
    kvb_spec = pl.BlockSpec((1, tm // KEY_BLOCK, D_ATTN, KEY_BLOCK),
                            lambda i: (tile(i, 1) // per, tile(i, 1) % per, 0, 0))
    kv_shape = jax.ShapeDtypeStruct((s, D_ATTN, t), F32)
    kvb_shape = jax.ShapeDtypeStruct((s, t // KEY_BLOCK, D_ATTN, KEY_BLOCK), BF16)
    cs_spec, _ = _lru_state_specs(lambda i: (tile(i, 1) // per, 0, 0))
    _, h_spec = _lru_state_specs(lambda i: (tile(i, 2) // per, 0, 0))
    w_cols = (COL_Q, COL_U, COL_GR)
    return pl.pallas_call(
        functools.partial(_proj_lru_kernel, per, n_tiles),
        grid=(n_tiles + 2,),
        in_specs=[tok_in, _const_spec(g.shape)] + [_w_in_cols(j) for j in w_cols] + [_const_spec(wkvt.shape)]
        + [cs_spec, h_spec] + [_const_spec(w.shape) for w in lru_w],
        out_specs=[q_out, kv_spec, kv_spec, kvb_spec, kvb_spec, hg_out, cs_spec, h_spec],
        out_shape=[b16o, kv_shape, kv_shape, kvb_shape, kvb_shape, b16o,
                   jax.ShapeDtypeStruct((s, CONV_WIDTH - 1, D_RNN), F32),
                   jax.ShapeDtypeStruct((s, 1, D_RNN), F32)],
        scratch_shapes=_lru_scratch(tm) + [pltpu.VMEM((2, tm, D_MODEL), BF16), pltpu.VMEM((2, tm, D_RNN), F32),
                                           pltpu.VMEM((2, tm, D_RNN), F32), pltpu.VMEM((2, tm, 2 * D_RNN), F32)],
        compiler_params=pltpu.CompilerParams(dimension_semantics=("arbitrary",),
                                             vmem_limit_bytes=VMEM_LIMIT),
        name="proj_lru",
    )(x.reshape(n, D_MODEL), g, *([w_in] * len(w_cols)), wkvt, conv_state, h0, *lru_w)


def _sb_split(z, mask):
    n = z.shape[1] // KEY_BLOCK
    tq = z.shape[0] // 2
    sp = _softplus(z)
    cats = []
    for h in range(2):
        for d in range(n):
            blk = sp[h * tq:(h + 1) * tq, d * KEY_BLOCK:(d + 1) * KEY_BLOCK]
            if d == n - 1 and mask is not None:
                blk = jnp.where(mask, blk, 0.0)
            cats.append(_hi_lo(blk))
    return jnp.concatenate(cats, axis=0)


def _hi_lo(blk):
    hi = blk.astype(BF16)
    lo = (blk - hi.astype(F32)).astype(BF16)
    return jnp.concatenate([hi, lo], axis=1)


def _sb_cumsum(cat, tri2):
    return jnp.dot(cat, tri2, preferred_element_type=F32)


def _sb_weights(z, r, carries, mask):
    n = z.shape[1] // KEY_BLOCK
    tq = z.shape[0] // 2
    rows, new_carries = [], []
    for h in range(2):
        carry = None if carries is None else carries[h]
        es = [None] * n
        for d in reversed(range(n)):
            rd = r[(h * n + d) * tq:(h * n + d + 1) * tq]
            x = z[h * tq:(h + 1) * tq, d * KEY_BLOCK:(d + 1) * KEY_BLOCK] + rd[:, :KEY_BLOCK]
            if carry is not None:
                x = x + carry
            e = jnp.exp(x)
            if d == n - 1 and mask is not None:
                e = jnp.where(mask, e, 0.0)
            es[d] = e.astype(BF16)
            tot = rd[:, KEY_BLOCK:]
            carry = tot if carry is None else carry + tot
        rows.append(es[0] if n == 1 else jnp.concatenate(es, axis=1))
        new_carries.append(carry)
    return jnp.concatenate(rows, axis=0), new_carries


def _tri2():
    j = np.arange(KEY_BLOCK)[:, None]
    s = np.arange(KEY_BLOCK)[None, :]
    half = np.concatenate([-(j >= s).astype(np.float32), -np.ones((KEY_BLOCK, LANES), np.float32)], axis=1)
    return jnp.asarray(np.concatenate([half, half], axis=0), dtype=BF16)


def _stack_heads(q, lo_half):
    zero = jnp.zeros_like(q)
    return jnp.concatenate([jnp.where(lo_half, q, zero), jnp.where(lo_half, zero, q)], axis=0)


def _sb_window(z_fn, pv_fn, carries, mask, tri2):
    z = z_fn()
    e, cs = _sb_weights(z, _sb_cumsum(_sb_split(z, mask), tri2), carries, mask)
    pv = pv_fn(e)
    tq = pv.shape[0] // 2
    return (pv[:tq], cs[0]), (pv[tq:], cs[1])


def _finish(store, j0, outs, step_fn):
    (a0, c0), (a1, c1) = outs
    store(a0, a1)
    if isinstance(j0, int) and j0 < 0:
        return lambda: None

    def alive(x0, x1):
        return jnp.max(jnp.maximum(x0, x1)) >= LOG_KEEP_FLOOR

    go = jnp.logical_and(j0 >= 0, alive(c0, c1))

    def walk():
        @pl.when(go)
        def _():
            def body(s):
                j, b0, b1, d0, d1, _ = s
                (p0, d0), (p1, d1) = step_fn(j, [d0, d1])
                return j - 1, b0 + p0, b1 + p1, d0, d1, jnp.logical_and(j >= 1, alive(d0, d1))

            _, b0, b1, _, _, _ = lax.while_loop(lambda s: s[-1], body,
                                                (jnp.int32(j0), a0, a1, c0, c1, jnp.bool_(True)))
            store(b0, b1)

    return walk


def _sb_chains(chains, tri2):
    zs = [c[0]() for c in chains]
    sums = [_sb_cumsum(_sb_split(z, c[2]), tri2) for z, c in zip(zs, chains)]
    weights = [_sb_weights(z, r, None, c[2]) for z, r, c in zip(zs, sums, chains)]
    walks = []
    for c, (e, cs) in zip(chains, weights):
        pv = c[1](e)
        tq = pv.shape[0] // 2
        walks.append(_finish(c[3], c[5], ((pv[:tq], cs[0]), (pv[tq:], cs[1])), c[4]))
    for walk in walks:
        walk()


def _attn_prompt_kernel(q_ref, kt_ref, vt_ref, tri_ref, o_ref):
    t = q_ref.shape[1]
    nq = t // Q_ROWS
    row = lax.broadcasted_iota(jnp.int32, (Q_ROWS, LANES), 0)
    lane = lax.broadcasted_iota(jnp.int32, (Q_ROWS, LANES), 1)
    lo_half = lane < HEAD_DIM
    diag_mask = lane < row
    lane_minus_row = lane - row
    tri2 = tri_ref[...]

    def window(ref, first, n):
        return jnp.concatenate([ref[0, first + d] for d in range(n)], axis=1)

    def qblocks(blocks, n_fast):
        chains = []
        for i in blocks:
            q2 = _stack_heads(q_ref[0, _rows(i, Q_ROWS), :], lo_half)
            first = i + 1 - n_fast

            def z_fn(q2=q2, first=first):
                return jnp.dot(q2, window(kt_ref, first, n_fast), preferred_element_type=F32)

            def pv_fn(e, first=first):
                return lax.dot_general(e, window(vt_ref, first, n_fast), NT_DIMS, preferred_element_type=F32)

            def store(b0, b1, i=i):
                o_ref[_rows(i, Q_ROWS), :] = jnp.where(lo_half, b0, b1).astype(BF16)

            def step(j, cs, i=i, q2=q2):
                m = lane_minus_row < (i - j) * KEY_BLOCK
                return _sb_window(lambda: jnp.dot(q2, kt_ref[0, j], preferred_element_type=F32),
                                  lambda e: lax.dot_general(e, vt_ref[0, j], NT_DIMS, preferred_element_type=F32),
                                  cs, m, tri2)

            chains.append((z_fn, pv_fn, diag_mask, store, step, i - n_fast))
        _sb_chains(chains, tri2)

    half = TAPER_ROWS
    bottom = row >= half

    def qblocks_tapered(blocks):
        assert FAST_BLOCKS == 3
        chains = []
        for i in blocks:
            q2 = _stack_heads(q_ref[0, _rows(i, Q_ROWS), :], lo_half)
            q2_top = jnp.concatenate([q2[:half], q2[Q_ROWS:Q_ROWS + half]], axis=0)
            z_new = jnp.dot(q2, window(kt_ref, i - 1, 2), preferred_element_type=F32)
            z_old = jnp.dot(q2_top, kt_ref[0, i - 2], preferred_element_type=F32)
            chains.append([i, q2, z_new, z_old])
        for c in chains:
            _, _, z_new, z_old = c
            sp_new, sp_old = _softplus(z_new), _softplus(z_old)
            cats = []
            for h in range(2):
                rows = slice(h * Q_ROWS, (h + 1) * Q_ROWS)
                cats.append(_hi_lo(sp_new[rows, :KEY_BLOCK]))
                cats.append(_hi_lo(jnp.where(diag_mask, sp_new[rows, KEY_BLOCK:], 0.0)))
            for h in range(2):
                cats.append(_hi_lo(sp_old[h * half:(h + 1) * half]))
            c.append(_sb_cumsum(jnp.concatenate(cats, axis=0), tri2))
        walks = []
        for i, q2, z_new, z_old, r in chains:
            es, carries = [], []
            for h in range(2):
                rows = slice(h * Q_ROWS, (h + 1) * Q_ROWS)
                r_prev = r[2 * h * Q_ROWS:(2 * h + 1) * Q_ROWS]
                r_diag = r[(2 * h + 1) * Q_ROWS:(2 * h + 2) * Q_ROWS]
                r_old = r[4 * Q_ROWS + h * half:4 * Q_ROWS + (h + 1) * half]
                e_diag = jnp.where(diag_mask, jnp.exp(z_new[rows, KEY_BLOCK:] + r_diag[:, :KEY_BLOCK]), 0.0)
                carry = r_diag[:, KEY_BLOCK:]
                e_prev = jnp.exp(z_new[rows, :KEY_BLOCK] + r_prev[:, :KEY_BLOCK] + carry)
                carry = carry + r_prev[:, KEY_BLOCK:]
                e_old = jnp.exp(z_old[h * half:(h + 1) * half] + r_old[:, :KEY_BLOCK] + carry[:half])
                carries.append(jnp.concatenate([carry[:half] + r_old[:, KEY_BLOCK:], carry[half:]], axis=0))
                e_old = jnp.concatenate([e_old.astype(BF16), jnp.zeros((Q_ROWS - half, KEY_BLOCK), BF16)], axis=0)
                es.append(jnp.concatenate([e_old, e_prev.astype(BF16), e_diag.astype(BF16)], axis=1))
            pv = lax.dot_general(jnp.concatenate(es, axis=0), window(vt_ref, i - 2, FAST_BLOCKS), NT_DIMS,
                                 preferred_element_type=F32)

            def store(b0, b1, i=i):
                o_ref[_rows(i, Q_ROWS), :] = jnp.where(lo_half, b0, b1).astype(BF16)

            def step(j, cs, i=i, q2=q2):
                m = jnp.logical_and(lane_minus_row < (i - j) * KEY_BLOCK, jnp.logical_or(bottom, j < i - 2))
                return _sb_window(lambda: jnp.dot(q2, kt_ref[0, j], preferred_element_type=F32),
                                  lambda e: lax.dot_general(e, vt_ref[0, j], NT_DIMS, preferred_element_type=F32),
                                  cs, m, tri2)

            walks.append(_finish(store, i - 2, ((pv[:Q_ROWS], carries[0]), (pv[Q_ROWS:], carries[1])), step))
        for walk in walks:
            walk()

    n_head = FAST_BLOCKS - 1
    for i in range(n_head):
        qblocks([i], i + 1)

    start = n_head + (nq - n_head) % Q_GROUP
    if start > n_head:
        qblocks_tapered(list(range(n_head, start)))

    def group(g, carry):
        qblocks_tapered([start + g * Q_GROUP + u for u in range(Q_GROUP)])
        return carry

    lax.fori_loop(0, (nq - start) // Q_GROUP, group, 0)


def _attn_prompt(qb, ktb, vtb, tri2):
    b, t, _ = qb.shape
    assert t % Q_ROWS == 0 and Q_ROWS == KEY_BLOCK
    qspec = pl.BlockSpec((1, t, LANES), lambda i, h: (i, 0, h))
    kspec = pl.BlockSpec((1, t // KEY_BLOCK, LANES, KEY_BLOCK), lambda i, h: (i, 0, h, 0))
    return pl.pallas_call(
        _attn_prompt_kernel,
        grid=(b, D_ATTN // LANES),
        in_specs=[qspec, kspec, kspec, _const_spec(tri2.shape)],
        out_specs=pl.BlockSpec((t, LANES), lambda i, h: (i, h)),
        out_shape=jax.ShapeDtypeStruct((b * t, D_ATTN), BF16),
        compiler_params=pltpu.CompilerParams(dimension_semantics=("arbitrary", "arbitrary"),
                                             vmem_limit_bytes=VMEM_LIMIT),
        name="attn_prompt",
    )(qb, ktb, vtb, tri2)


def _attn_sample_kernel(n_tail, q_ref, kn_ref, vn_ref, ckt_ref, cvt_ref, ck_hbm, cv_hbm, tri_ref, o_ref,
                        kbuf, vbuf):
    n_streams = q_ref.shape[0]
    tq = q_ref.shape[1]
    n_past = ck_hbm.shape[2] // KEY_BLOCK
    row = lax.broadcasted_iota(jnp.int32, (tq, LANES), 0)
    lane = lax.broadcasted_iota(jnp.int32, (tq, LANES), 1)
    lo_half = lane < HEAD_DIM
    tri2 = tri_ref[...]

    pad = jnp.zeros((KEY_BLOCK - tq, LANES), BF16)
    chains = []
    for b in range(n_streams):
        s = pl.program_id(0) * n_streams + b
        for p in range(q_ref.shape[2] // LANES):
            cols = slice(p * LANES, (p + 1) * LANES)
            q2 = _stack_heads(q_ref[b, :, cols], lo_half)

            def z_fn(q2=q2, cols=cols, b=b):
                z_new = lax.dot_general(q2, jnp.concatenate([kn_ref[b, :, cols], pad], axis=0), NT_DIMS,
                                        preferred_element_type=F32)
                if n_tail == 0:
                    return z_new
                z_past = jnp.dot(q2, ckt_ref[b, cols, :].astype(BF16), preferred_element_type=F32)
                return jnp.concatenate([z_past, z_new], axis=1)

            def pv_fn(e, cols=cols, b=b):
                pv = jnp.dot(e[:, n_tail * KEY_BLOCK:], jnp.concatenate([vn_ref[b, :, cols], pad], axis=0),
                             preferred_element_type=F32)
                if n_tail == 0:
                    return pv
                return pv + lax.dot_general(e[:, :n_tail * KEY_BLOCK], cvt_ref[b, cols, :].astype(BF16), NT_DIMS,
                                            preferred_element_type=F32)

            def store(a0, a1, cols=cols, b=b):
                o_ref[b * tq:(b + 1) * tq, cols] = jnp.where(lo_half, a0, a1).astype(BF16)

            def step(j, cs, q2=q2, p=p, s=s):
                keys = pl.ds(pl.multiple_of(j * KEY_BLOCK, KEY_BLOCK), KEY_BLOCK)
                pltpu.sync_copy(ck_hbm.at[s, pl.ds(p * LANES, LANES), keys], kbuf)
                pltpu.sync_copy(cv_hbm.at[s, pl.ds(p * LANES, LANES), keys], vbuf)
                return _sb_window(lambda: jnp.dot(q2, kbuf[...].astype(BF16), preferred_element_type=F32),
                                  lambda e: lax.dot_general(e, vbuf[...].astype(BF16), NT_DIMS,
                                                            preferred_element_type=F32),
                                  cs, None, tri2)

            chains.append((z_fn, pv_fn, lane < row, store, step, n_past - 1 - n_tail))
    _sb_chains(chains, tri2)


def _attn_sample(qb, knb, vnb, cache_kt, cache_vt, tri2):
    s, tq, _ = qb.shape
    past_len = cache_kt.shape[2]
    assert past_len % KEY_BLOCK == 0 and tq <= KEY_BLOCK and tq % 16 == 0
    n_tail = min(FAST_BLOCKS - 1, past_len // KEY_BLOCK)
    assert n_tail > 0 and past_len % (n_tail * KEY_BLOCK) == 0
    per = SAMPLE_STREAMS if s % SAMPLE_STREAMS == 0 else 1
    new = pl.BlockSpec((per, tq, D_ATTN), lambda i: (i, 0, 0))
    tail = pl.BlockSpec((per, D_ATTN, n_tail * KEY_BLOCK), lambda i: (i, 0, past_len // (n_tail * KEY_BLOCK) - 1))
    hbm = pl.BlockSpec(memory_space=pl.ANY)
    return pl.pallas_call(
        functools.partial(_attn_sample_kernel, n_tail),
        grid=(s // per,),
        in_specs=[new, new, new, tail, tail, hbm, hbm, _const_spec(tri2.shape)],
        out_specs=pl.BlockSpec((per * tq, D_ATTN), lambda i: (i, 0)),
        out_shape=jax.ShapeDtypeStruct((s * tq, D_ATTN), BF16),
        scratch_shapes=[pltpu.VMEM((LANES, KEY_BLOCK), F32), pltpu.VMEM((LANES, KEY_BLOCK), F32)],
        compiler_params=pltpu.CompilerParams(dimension_semantics=("arbitrary",),
                                             vmem_limit_bytes=VMEM_LIMIT),
        name="attn_sample",
    )(qb, knb, vnb, cache_kt, cache_vt, cache_kt, cache_vt, tri2)


FF_CHUNK = 1024
STAGE_BYTES = 1024 * 1024
STAGE_SLOTS = 3


def _stream_cast(src_hbm, dst_ref, stage_ref, sem):
    slots, rows = stage_ref.shape[0], stage_ref.shape[1]
    n = src_hbm.shape[0] // rows
    assert src_hbm.shape[0] % rows == 0

    def copy(c, slot):
        return pltpu.make_async_copy(src_hbm.at[pl.ds(c * rows, rows)], stage_ref.at[slot], sem.at[slot])

    for c in range(min(slots - 1, n)):
        copy(c, c).start()

    def body(c, carry):
        @pl.when(c + slots - 1 < n)
        def _():
            copy(c + slots - 1, (c + slots - 1) % slots).start()

        copy(c, c % slots).wait()
        dst_ref[pl.ds(pl.multiple_of(c * rows, rows), rows), :] = stage_ref[c % slots].astype(BF16)
        return carry

    lax.fori_loop(0, n, body, 0)


def _out_kernel(n_first, xa_ref, xb_ref, oa_ref, ob_ref, hga_ref, hgb_ref, g1_ref, wga0_ref, wga1_ref, wgb0_ref,
                wgb1_ref, wa_hbm, wb_hbm, wo_hbm, g2_ref, g3_ref, wup_hbm, wdn_hbm, g4_ref, ya_ref, yb_ref,
                wa_ref, wb_ref, wo_ref, wup_ref, wdn_ref, wide_stage, stage, wide_sem, sem):
    @pl.when(pl.program_id(0) == 0)
    def _():
        _stream_cast(wa_hbm, wa_ref, stage, sem)
        _stream_cast(wb_hbm, wb_ref, stage, sem)
        _stream_cast(wo_hbm, wo_ref, stage, sem)
        _stream_cast(wup_hbm, wup_ref, wide_stage, wide_sem)
        _stream_cast(wdn_hbm, wdn_ref, stage, sem)

    first = pl.program_id(0) < n_first
    tm = xa_ref.shape[0]
    halves = [pl.ds(0, tm // 2), pl.ds(tm // 2, tm // 2)]
    n_ff = D_FF // FF_CHUNK
    st = [{}, {}]

    def pick(a_ref, b_ref, h):
        return jnp.where(first, a_ref[halves[h], :], b_ref[halves[h], :])

    def norm_in(h):
        st[h]["x"] = pick(xa_ref, xb_ref, h)
        st[h]["xn"] = _rms(st[h]["x"], g1_ref[...]).astype(BF16)

    def gate(h, w0_ref, w1_ref):
        return jnp.concatenate([jnp.dot(st[h]["xn"], w_ref[...], preferred_element_type=F32)
                                for w_ref in (w0_ref, w1_ref)], axis=1)

    def attn_gate(h):
        g_a = gate(h, wga0_ref, wga1_ref)
        y_a = jnp.dot(pick(oa_ref, ob_ref, h), wa_ref[...], preferred_element_type=F32)
        st[h]["m"] = jax.nn.sigmoid(g_a) * y_a

    def rnn_gate(h):
        g_b = gate(h, wgb0_ref, wgb1_ref)
        y_b = jnp.dot(pick(hga_ref, hgb_ref, h), wb_ref[...], preferred_element_type=F32)
        st[h]["m"] = st[h]["m"] + jax.nn.sigmoid(g_b) * y_b

    def mix(h):
        st[h]["mix"] = jnp.dot(st[h]["m"].astype(BF16), wo_ref[...], preferred_element_type=F32)

    def norm_mid(h):
        st[h]["x1"] = st[h]["x"] + _rms(st[h]["mix"], g2_ref[...])
        st[h]["f"] = _rms(st[h]["x1"], g3_ref[...]).astype(BF16)

    def ffn(h, c):
        cols = slice(c * FF_CHUNK, (c + 1) * FF_CHUNK)
        up = jnp.maximum(jnp.dot(st[h]["f"], wup_ref[:, cols], preferred_element_type=F32), 0.0)
        dn = jnp.dot((up * up).astype(BF16), wdn_ref[cols, :], preferred_element_type=F32)
        st[h]["acc"] = dn if c == 0 else st[h]["acc"] + dn

    def norm_out(h):
        yb_ref[halves[h], :] = st[h]["x1"] + _rms(st[h]["acc"], g4_ref[...])

    norm_in(0)
    norm_in(1)
    attn_gate(0)
    rnn_gate(0)
    mix(0)
    attn_gate(1)
    norm_mid(0)
    rnn_gate(1)
    mix(1)
    ffn(0, 0)
    norm_mid(1)
    for c in range(1, n_ff):
        ffn(0, c)
    ffn(1, 0)
    norm_out(0)
    for c in range(1, n_ff):
        ffn(1, c)
    norm_out(1)

    @pl.when(first)
    def _():
        ya_ref[...] = yb_ref[...]


def _out(xa, xb, oa, ob, hga, hgb, g1, w_in, wa, wb, wo, g2, g3, wup, wdn, g4):
    na, nb = xa.shape[0], xb.shape[0]
    tm = TOKEN_TILE
    assert na % tm == 0 and nb % tm == 0
    n_first = na // tm
    a_tok = lambda i: (jnp.minimum(i, n_first - 1), 0)
    b_tok = lambda i: (jnp.maximum(i - n_first, 0), 0)
    w_cols = (COL_GA, COL_GA + 1, COL_GB, COL_GB + 1)
    consts = (wa, wb, wo, g2, g3, wup, wdn, g4)
    big = (wa, wb, wo, wup, wdn)
    assert all(w.dtype == F32 for w in big)
    hbm = pl.BlockSpec(memory_space=pl.ANY)
    stage_rows = STAGE_BYTES // (4 * D_MODEL)
    wide_rows = STAGE_BYTES // (4 * D_FF)
    return pl.pallas_call(
        functools.partial(_out_kernel, n_first),
        grid=((na + nb) // tm,),
        in_specs=[pl.BlockSpec((tm, D_MODEL), a_tok), pl.BlockSpec((tm, D_MODEL), b_tok),
                  pl.BlockSpec((tm, D_ATTN), a_tok), pl.BlockSpec((tm, D_ATTN), b_tok),
                  pl.BlockSpec((tm, D_RNN), a_tok), pl.BlockSpec((tm, D_RNN), b_tok), _const_spec(g1.shape)]
        + [_w_in_cols(j) for j in w_cols]
        + [hbm if any(c is w for w in big) else _const_spec(c.shape) for c in consts],
        out_specs=[pl.BlockSpec((tm, D_MODEL), a_tok), pl.BlockSpec((tm, D_MODEL), b_tok)],
        out_shape=[jax.ShapeDtypeStruct((na, D_MODEL), F32), jax.ShapeDtypeStruct((nb, D_MODEL), F32)],
        scratch_shapes=[pltpu.VMEM(w.shape, BF16) for w in big]
        + [pltpu.VMEM((STAGE_SLOTS, wide_rows, D_FF), F32), pltpu.VMEM((STAGE_SLOTS, stage_rows, D_MODEL), F32),
           pltpu.SemaphoreType.DMA((STAGE_SLOTS,)), pltpu.SemaphoreType.DMA((STAGE_SLOTS,))],
        compiler_params=pltpu.CompilerParams(dimension_semantics=("arbitrary",),
                                             vmem_limit_bytes=VMEM_LIMIT),
        name="out",
    )(xa, xb, oa, ob, hga, hgb, g1, *([w_in] * len(w_cols)), *consts)


def _block_diag(w):
    n, c, d = w.shape
    return jnp.einsum("ncd,nm->ncmd", w, jnp.eye(n, dtype=w.dtype)).reshape(n * c, n * d)


def _heads_last(kt):
    s, _, t = kt.shape
    return kt.reshape(s, N_HEADS, HEAD_DIM, t).transpose(0, 3, 1, 2)[None]


def _time_last(cache):
    s, t = cache.shape[:2]
    return cache.transpose(0, 2, 3, 1).reshape(s, D_ATTN, t)


def kernel(x_prompt, x_sample, cache_k, cache_v, state_conv, state_h, w_in, g_pre_mix, w_conv, b_conv, w_r, b_r, w_i, b_i, lam, w_a_out, w_b_out, w_o, g_post_mix, g_pre_ffn, w_up, w_down, g_post_ffn):
    assert w_in.shape[0] == 1
    row = lambda a: a[0].reshape(1, -1)
    assert w_in.shape[2] == (COL_GB + 2) * D_ATTN
    w_in_b = w_in[0].astype(BF16)
    wkv_t = w_in[0][:, COL_K * D_ATTN:(COL_V + 1) * D_ATTN].T.astype(BF16)
    wg = jnp.concatenate([_block_diag(w_r[0]), _block_diag(w_i[0])], axis=1).astype(BF16)
    bg = jnp.concatenate([row(b_r), row(b_i)], axis=1)
    lru_w = (w_conv[0], row(b_conv), wg, bg, row(lam))
    g1 = row(g_pre_mix)
    tri2 = _tri2()

    bp, tp, _ = x_prompt.shape
    bs, ts, _ = x_sample.shape
    n_p, n_s = bp * tp, bs * ts

    zc = jnp.zeros((bp, CONV_WIDTH - 1, D_RNN), F32)
    zh = jnp.zeros((bp, 1, D_RNN), F32)
    qb, kt, vt, ktb, vtb, hg_p, cp, hp = _proj_lru(x_prompt, g1, w_in_b, wkv_t, zc, zh, lru_w)
    o_p = _attn_prompt(qb.reshape(bp, tp, D_ATTN), ktb, vtb, tri2)

    s3 = lambda a: a.reshape(bs, ts, a.shape[-1])
    qs, ks, vs, ksb, vsb, hg_s, cs, hs = _proj(x_sample, g1, w_in_b, state_conv[0],
                                               state_h[0].reshape(bs, 1, D_RNN), lru_w)
    o_s = _attn_sample(s3(qs), s3(ksb), s3(vsb), _time_last(cache_k[0]), _time_last(cache_v[0]), tri2)

    ys, yp = _out(x_sample.reshape(n_s, D_MODEL), x_prompt.reshape(n_p, D_MODEL), o_s, o_p, hg_s, hg_p, g1, w_in_b,
                  w_a_out[0], w_b_out[0], w_o[0], row(g_post_mix), row(g_pre_ffn), w_up[0], w_down[0],
                  row(g_post_ffn))
    return (yp.reshape(bp, tp, D_MODEL), ys.reshape(bs, ts, D_MODEL), _heads_last(kt), _heads_last(vt),
            cp[None], hp.reshape(1, bp, D_RNN), ks.reshape(1, bs, ts, N_HEADS, HEAD_DIM),
            vs.reshape(1, bs, ts, N_HEADS, HEAD_DIM), cs[None], hs.reshape(1, bs, D_RNN))
```
